```python
import jax, jax.numpy as jnp
from jax import lax
import numpy as np

D_MODEL = 1024
BATCH = 8
SEQ = 8192
DEPTH = 4

N_A = DEPTH // 2
N_B = DEPTH - N_A
POOL_WINDOWS = (2, 4, 8, 16)
POOL_GROUPS = len(POOL_WINDOWS)
GC = D_MODEL // POOL_GROUPS
HEAD_DIM = 64
N_HEADS = D_MODEL // HEAD_DIM
N_KV = max(1, N_HEADS // 8)
GROUP = N_HEADS // N_KV
WINDOW = 128
BLK = WINDOW
D_FF = 2816
CONV_W = 3
EPS = 1e-5

kernel_name = "yoco_pool_swa_sink_convffn"


def rmsnorm(x, g):
    xf = x.astype(jnp.float32)
    y = xf * lax.rsqrt(jnp.mean(xf * xf, axis=-1, keepdims=True) + EPS)
    return (y * g.astype(jnp.float32)).astype(x.dtype)


def pool_mixer(h, w, scale):
    B, S, D = h.shape
    c = jnp.cumsum(h.astype(jnp.float32), axis=1)
    t = jnp.arange(S)
    pooled = []
    for gi, win in enumerate(POOL_WINDOWS):
        cg = c[..., gi * GC:(gi + 1) * GC]
        lag = jnp.pad(cg, ((0, 0), (win, 0), (0, 0)))[:, :S]
        cnt = jnp.minimum(t + 1, win).astype(jnp.float32)[None, :, None]
        pooled.append((cg - lag) / cnt)
    pooled = jnp.stack(pooled, axis=2).astype(h.dtype) - h.reshape(B, S, POOL_GROUPS, GC)
    y = jnp.einsum('bsgc,gcd->bsgd', pooled, w).reshape(B, S, D)
    return y * scale


def conv_ffn(h, w_up, conv_w, conv_b, w_down):
    u = h @ w_up
    S = u.shape[1]
    up = jnp.pad(u, ((0, 0), (CONV_W - 1, 0), (0, 0)))
    u = sum(conv_w[k] * up[:, k:k + S] for k in range(CONV_W)) + conv_b
    gate, val = jnp.split(u, 2, axis=-1)
    return (jax.nn.silu(gate) * val) @ w_down


def swa_sink_attention(q, k, v, sinks):
    B, S = q.shape[:2]
    nb = S // BLK
    qb = q.reshape(B, nb, BLK, N_KV, GROUP, HEAD_DIM)
    kb = k.reshape(B, nb, BLK, N_KV, HEAD_DIM)
    vb = v.reshape(B, nb, BLK, N_KV, HEAD_DIM)
    pad = ((0, 0), (1, 0), (0, 0), (0, 0), (0, 0))
    kw = jnp.concatenate([jnp.pad(kb, pad)[:, :nb], kb], axis=2)
    vw = jnp.concatenate([jnp.pad(vb, pad)[:, :nb], vb], axis=2)
    s = jnp.einsum('bnqkgd,bnskd->bnkgqs', qb, kw).astype(jnp.float32) * (HEAD_DIM ** -0.5)
    qi = jnp.arange(BLK)[:, None]
    si = jnp.arange(2 * BLK)[None, :]
    band = (si > qi) & (si <= qi + BLK)
    valid = (jnp.arange(nb)[:, None, None] > 0) | (si >= BLK)[None]
    mask = (band[None] & valid)[None, :, None, None]
    sink = sinks.astype(jnp.float32).reshape(N_KV, GROUP)[None, None, :, :, None, None]
    s = jnp.where(mask, s, -jnp.inf)
    m = jnp.maximum(jnp.max(s, axis=-1, keepdims=True), sink)
    p = jnp.exp(s - m)
    denom = jnp.sum(p, axis=-1, keepdims=True) + jnp.exp(sink - m)
    pr = (p / denom).astype(v.dtype)
    o = jnp.einsum('bnkgqs,bnskd->bnqkgd', pr, vw)
    return o.reshape(B, S, N_HEADS * HEAD_DIM)


def _fwd_setup_inputs(seed: int = 0) -> dict:
    key = jax.random.key(seed)
    ks = jax.random.split(key, 20)
    f32 = jnp.float32
    nrm = lambda k, shp, s: jax.random.normal(k, shp, f32) * s
    KVW = 2 * N_KV * HEAD_DIM
    QW = N_HEADS * HEAD_DIM
    return {
        "x": nrm(ks[0], (BATCH, SEQ, D_MODEL), 1.0),
        "norm1_g": 1.0 + nrm(ks[1], (DEPTH, D_MODEL), 0.02),
        "norm2_g": 1.0 + nrm(ks[2], (DEPTH, D_MODEL), 0.02),
        "pool_w": nrm(ks[3], (N_A, POOL_GROUPS, GC, GC), GC ** -0.5),
        "pool_scale": 1.0 + nrm(ks[4], (N_A, D_MODEL), 0.02),
        "kv_norm_g": 1.0 + nrm(ks[5], (D_MODEL,), 0.02),
        "w_kv": nrm(ks[6], (D_MODEL, KVW), D_MODEL ** -0.5),
        "b_kv": nrm(ks[7], (KVW,), 0.02),
        "w_q": nrm(ks[8], (N_B, D_MODEL, QW), D_MODEL ** -0.5),
        "b_q": nrm(ks[9], (N_B, QW), 0.02),
        "sinks": nrm(ks[10], (N_B, N_HEADS), 1.0),
        "w_o": nrm(ks[11], (N_B, QW, D_MODEL), QW ** -0.5),
        "b_o": nrm(ks[12], (N_B, D_MODEL), 0.02),
        "ffn_up": nrm(ks[13], (DEPTH, D_MODEL, 2 * D_FF), D_MODEL ** -0.5),
        "ffn_conv_w": nrm(ks[14], (DEPTH, CONV_W, 2 * D_FF), CONV_W ** -0.5),
        "ffn_conv_b": nrm(ks[15], (DEPTH, 2 * D_FF), 0.02),
        "ffn_down": nrm(ks[16], (DEPTH, D_FF, D_MODEL), D_FF ** -0.5),
        "final_g": 1.0 + nrm(ks[17], (D_MODEL,), 0.02),
    }


def _fwd_reference(x, norm1_g, norm2_g, pool_w, pool_scale, kv_norm_g, w_kv, b_kv, w_q, b_q,
              sinks, w_o, b_o, ffn_up, ffn_conv_w, ffn_conv_b, ffn_down, final_g):
    B, S, D = x.shape
    k_sh = v_sh = None
    for l in range(DEPTH):
        h = rmsnorm(x, norm1_g[l])
        if l < N_A:
            x = x + pool_mixer(h, pool_w[l], pool_scale[l])
        else:
            j = l - N_A
            q = (h @ w_q[j] + b_q[j]).reshape(B, S, N_HEADS, HEAD_DIM)
            o = swa_sink_attention(q, k_sh, v_sh, sinks[j])
            x = x + (o @ w_o[j] + b_o[j])
        h = rmsnorm(x, norm2_g[l])
        x = x + conv_ffn(h, ffn_up[l], ffn_conv_w[l], ffn_conv_b[l], ffn_down[l])
        if l == N_A - 1:
            kv = rmsnorm(x, kv_norm_g) @ w_kv + b_kv
            k_sh, v_sh = jnp.split(kv.reshape(B, S, 2 * N_KV, HEAD_DIM), 2, axis=2)
    return rmsnorm(x, final_g)


import jax as _jax
import jax.numpy as _jnp

TWIN_FORMAT = 'train_step'
FWD_PARAMS = ['x', 'norm1_g', 'norm2_g', 'pool_w', 'pool_scale', 'kv_norm_g', 'w_kv', 'b_kv', 'w_q', 'b_q', 'sinks', 'w_o', 'b_o', 'ffn_up', 'ffn_conv_w', 'ffn_conv_b', 'ffn_down', 'final_g']
TWIN_WEIGHTS = ['norm1_g', 'norm2_g', 'pool_w', 'pool_scale', 'kv_norm_g', 'w_kv', 'b_kv', 'w_q', 'b_q', 'sinks', 'w_o', 'b_o', 'ffn_up', 'ffn_conv_w', 'ffn_conv_b', 'ffn_down', 'final_g']
TWIN_DIFF_INPUT = 'x'
TWIN_INPUTS = ['x', 'norm1_g', 'norm2_g', 'pool_w', 'pool_scale', 'kv_norm_g', 'w_kv', 'b_kv', 'w_q', 'b_q', 'sinks', 'w_o', 'b_o', 'ffn_up', 'ffn_conv_w', 'ffn_conv_b', 'ffn_down', 'final_g', 'loss_target', 'm_norm1_g', 'm_norm2_g', 'm_pool_w', 'm_pool_scale', 'm_kv_norm_g', 'm_w_kv', 'm_b_kv', 'm_w_q', 'm_b_q', 'm_sinks', 'm_w_o', 'm_b_o', 'm_ffn_up', 'm_ffn_conv_w', 'm_ffn_conv_b', 'm_ffn_down', 'm_final_g', 'v_norm1_g', 'v_norm2_g', 'v_pool_w', 'v_pool_scale', 'v_kv_norm_g', 'v_w_kv', 'v_b_kv', 'v_w_q', 'v_b_q', 'v_sinks', 'v_w_o', 'v_b_o', 'v_ffn_up', 'v_ffn_conv_w', 'v_ffn_conv_b', 'v_ffn_down', 'v_final_g']
TWIN_OUTPUTS = ['loss', 'grad_x', 'grad_norm1_g', 'grad_norm2_g', 'grad_pool_w', 'grad_pool_scale', 'grad_kv_norm_g', 'grad_w_kv', 'grad_b_kv', 'grad_w_q', 'grad_b_q', 'grad_sinks', 'grad_w_o', 'grad_b_o', 'grad_ffn_up', 'grad_ffn_conv_w', 'grad_ffn_conv_b', 'grad_ffn_down', 'grad_final_g', 'delta_norm1_g', 'delta_norm2_g', 'delta_pool_w', 'delta_pool_scale', 'delta_kv_norm_g', 'delta_w_kv', 'delta_b_kv', 'delta_w_q', 'delta_b_q', 'delta_sinks', 'delta_w_o', 'delta_b_o', 'delta_ffn_up', 'delta_ffn_conv_w', 'delta_ffn_conv_b', 'delta_ffn_down', 'delta_final_g', 'new_m_norm1_g', 'new_m_norm2_g', 'new_m_pool_w', 'new_m_pool_scale', 'new_m_kv_norm_g', 'new_m_w_kv', 'new_m_b_kv', 'new_m_w_q', 'new_m_b_q', 'new_m_sinks', 'new_m_w_o', 'new_m_b_o', 'new_m_ffn_up', 'new_m_ffn_conv_w', 'new_m_ffn_conv_b', 'new_m_ffn_down', 'new_m_final_g', 'new_v_norm1_g', 'new_v_norm2_g', 'new_v_pool_w', 'new_v_pool_scale', 'new_v_kv_norm_g', 'new_v_w_kv', 'new_v_b_kv', 'new_v_w_q', 'new_v_b_q', 'new_v_sinks', 'new_v_w_o', 'new_v_b_o', 'new_v_ffn_up', 'new_v_ffn_conv_w', 'new_v_ffn_conv_b', 'new_v_ffn_down', 'new_v_final_g']
TWIN_LEAF_KINDS = {'loss': 'loss', 'grad_x': 'grad_x', 'grad_norm1_g': 'grad_w', 'grad_norm2_g': 'grad_w', 'grad_pool_w': 'grad_w', 'grad_pool_scale': 'grad_w', 'grad_kv_norm_g': 'grad_w', 'grad_w_kv': 'grad_w', 'grad_b_kv': 'grad_w', 'grad_w_q': 'grad_w', 'grad_b_q': 'grad_w', 'grad_sinks': 'grad_w', 'grad_w_o': 'grad_w', 'grad_b_o': 'grad_w', 'grad_ffn_up': 'grad_w', 'grad_ffn_conv_w': 'grad_w', 'grad_ffn_conv_b': 'grad_w', 'grad_ffn_down': 'grad_w', 'grad_final_g': 'grad_w', 'delta_norm1_g': 'delta_w', 'delta_norm2_g': 'delta_w', 'delta_pool_w': 'delta_w', 'delta_pool_scale': 'delta_w', 'delta_kv_norm_g': 'delta_w', 'delta_w_kv': 'delta_w', 'delta_b_kv': 'delta_w', 'delta_w_q': 'delta_w', 'delta_b_q': 'delta_w', 'delta_sinks': 'delta_w', 'delta_w_o': 'delta_w', 'delta_b_o': 'delta_w', 'delta_ffn_up': 'delta_w', 'delta_ffn_conv_w': 'delta_w', 'delta_ffn_conv_b': 'delta_w', 'delta_ffn_down': 'delta_w', 'delta_final_g': 'delta_w', 'new_m_norm1_g': 'new_m', 'new_m_norm2_g': 'new_m', 'new_m_pool_w': 'new_m', 'new_m_pool_scale': 'new_m', 'new_m_kv_norm_g': 'new_m', 'new_m_w_kv': 'new_m', 'new_m_b_kv': 'new_m', 'new_m_w_q': 'new_m', 'new_m_b_q': 'new_m', 'new_m_sinks': 'new_m', 'new_m_w_o': 'new_m', 'new_m_b_o': 'new_m', 'new_m_ffn_up': 'new_m', 'new_m_ffn_conv_w': 'new_m', 'new_m_ffn_conv_b': 'new_m', 'new_m_ffn_down': 'new_m', 'new_m_final_g': 'new_m', 'new_v_norm1_g': 'new_v', 'new_v_norm2_g': 'new_v', 'new_v_pool_w': 'new_v', 'new_v_pool_scale': 'new_v', 'new_v_kv_norm_g': 'new_v', 'new_v_w_kv': 'new_v', 'new_v_b_kv': 'new_v', 'new_v_w_q': 'new_v', 'new_v_b_q': 'new_v', 'new_v_sinks': 'new_v', 'new_v_w_o': 'new_v', 'new_v_b_o': 'new_v', 'new_v_ffn_up': 'new_v', 'new_v_ffn_conv_w': 'new_v', 'new_v_ffn_conv_b': 'new_v', 'new_v_ffn_down': 'new_v', 'new_v_final_g': 'new_v'}


def _forward(args):
    return _fwd_reference(*[args[k] for k in FWD_PARAMS])


def _output_shape():
    def fwd():
        inp = _fwd_setup_inputs(0)
        return _fwd_reference(*[inp[k] for k in FWD_PARAMS])
    out = _jax.eval_shape(fwd)
    return out.shape, out.dtype

N_MICROBATCH = 1
ADAM_LR = 0.001
ADAM_B1 = 0.9
ADAM_B2 = 0.999
ADAM_EPS = 1e-08
ADAM_WD = 0.01
ADAM_STEP = 10
PER_EXAMPLE_BATCH_AXIS = {'x': 0, 'loss_target': 0}
SHARED_INPUTS = []
_WEIGHT_DTYPES = {'norm1_g': _jnp.float32, 'norm2_g': _jnp.float32, 'pool_w': _jnp.float32, 'pool_scale': _jnp.float32, 'kv_norm_g': _jnp.float32, 'w_kv': _jnp.float32, 'b_kv': _jnp.float32, 'w_q': _jnp.float32, 'b_q': _jnp.float32, 'sinks': _jnp.float32, 'w_o': _jnp.float32, 'b_o': _jnp.float32, 'ffn_up': _jnp.float32, 'ffn_conv_w': _jnp.float32, 'ffn_conv_b': _jnp.float32, 'ffn_down': _jnp.float32, 'final_g': _jnp.float32}
MOMENT_SCALE = {'norm1_g': 1.327437e-01, 'norm2_g': 1.383193e-01, 'pool_w': 1.893518e-01, 'pool_scale': 3.716467e-01, 'kv_norm_g': 5.540091e-02, 'w_kv': 1.088334e-01, 'b_kv': 4.580780e-01, 'w_q': 2.578315e-02, 'b_q': 2.546322e-02, 'sinks': 3.337795e-02, 'w_o': 2.856918e-02, 'b_o': 1.533856e-01, 'ffn_up': 5.816054e-02, 'ffn_conv_w': 5.888082e-02, 'ffn_conv_b': 5.791567e-02, 'ffn_down': 9.485320e-02, 'final_g': 6.436898e+01}


def _to_microbatches(a, axis):
    t = _jnp.moveaxis(a, axis, 0)
    t = t.reshape((N_MICROBATCH, t.shape[0] // N_MICROBATCH) + t.shape[1:])
    return _jnp.moveaxis(t, 1, axis + 1)


def setup_inputs(seed: int = 0) -> dict:
    inp = _fwd_setup_inputs(seed)
    key = _jax.random.fold_in(_jax.random.key(seed), 7919)
    shape, _ = _output_shape()
    out = dict(inp)
    out["loss_target"] = _jax.random.normal(_jax.random.fold_in(key, 0), shape, _jnp.float32)
    for i, name in enumerate(TWIN_WEIGHTS):
        w = inp[name].astype(_jnp.float32)
        if MOMENT_SCALE is None:
            s = _jnp.sqrt(_jnp.mean(_jnp.square(w)) + 1e-30)
        else:
            s = MOMENT_SCALE[name]
        km, kv = _jax.random.split(_jax.random.fold_in(key, i + 1))
        out[name] = w
        out["m_" + name] = s * _jax.random.normal(km, w.shape, _jnp.float32)
        out["v_" + name] = (s * s) * _jax.random.uniform(kv, w.shape, _jnp.float32, 0.5, 1.5)
    if N_MICROBATCH > 1:
        for name, axis in PER_EXAMPLE_BATCH_AXIS.items():
            out[name] = _to_microbatches(out[name], axis)
    return {'x': out['x'], 'norm1_g': out['norm1_g'], 'norm2_g': out['norm2_g'], 'pool_w': out['pool_w'], 'pool_scale': out['pool_scale'], 'kv_norm_g': out['kv_norm_g'], 'w_kv': out['w_kv'], 'b_kv': out['b_kv'], 'w_q': out['w_q'], 'b_q': out['b_q'], 'sinks': out['sinks'], 'w_o': out['w_o'], 'b_o': out['b_o'], 'ffn_up': out['ffn_up'], 'ffn_conv_w': out['ffn_conv_w'], 'ffn_conv_b': out['ffn_conv_b'], 'ffn_down': out['ffn_down'], 'final_g': out['final_g'], 'loss_target': out['loss_target'], 'm_norm1_g': out['m_norm1_g'], 'm_norm2_g': out['m_norm2_g'], 'm_pool_w': out['m_pool_w'], 'm_pool_scale': out['m_pool_scale'], 'm_kv_norm_g': out['m_kv_norm_g'], 'm_w_kv': out['m_w_kv'], 'm_b_kv': out['m_b_kv'], 'm_w_q': out['m_w_q'], 'm_b_q': out['m_b_q'], 'm_sinks': out['m_sinks'], 'm_w_o': out['m_w_o'], 'm_b_o': out['m_b_o'], 'm_ffn_up': out['m_ffn_up'], 'm_ffn_conv_w': out['m_ffn_conv_w'], 'm_ffn_conv_b': out['m_ffn_conv_b'], 'm_ffn_down': out['m_ffn_down'], 'm_final_g': out['m_final_g'], 'v_norm1_g': out['v_norm1_g'], 'v_norm2_g': out['v_norm2_g'], 'v_pool_w': out['v_pool_w'], 'v_pool_scale': out['v_pool_scale'], 'v_kv_norm_g': out['v_kv_norm_g'], 'v_w_kv': out['v_w_kv'], 'v_b_kv': out['v_b_kv'], 'v_w_q': out['v_w_q'], 'v_b_q': out['v_b_q'], 'v_sinks': out['v_sinks'], 'v_w_o': out['v_w_o'], 'v_b_o': out['v_b_o'], 'v_ffn_up': out['v_ffn_up'], 'v_ffn_conv_w': out['v_ffn_conv_w'], 'v_ffn_conv_b': out['v_ffn_conv_b'], 'v_ffn_down': out['v_ffn_down'], 'v_final_g': out['v_final_g']}


def _loss(weights, diff, rest, loss_target):
    with _jax.named_scope("forward"):
        args = {**rest, TWIN_DIFF_INPUT: diff, **{k: w.astype(_WEIGHT_DTYPES[k]) for k, w in weights.items()}}
        y = _forward(args)
    with _jax.named_scope("loss_head"):
        err = _jnp.square(y.astype(_jnp.float32) - loss_target)
        return 0.5 * _jnp.sum(_jnp.mean(err, axis=-1)) if err.ndim else 0.5 * err


def _adamw(w, g, m, v):
    m = ADAM_B1 * m + (1.0 - ADAM_B1) * g
    v = ADAM_B2 * v + (1.0 - ADAM_B2) * _jnp.square(g)
    m_hat = m / (1.0 - ADAM_B1 ** ADAM_STEP)
    v_hat = v / (1.0 - ADAM_B2 ** ADAM_STEP)
    delta = -ADAM_LR * (m_hat / (_jnp.sqrt(v_hat) + ADAM_EPS) + ADAM_WD * w)
    return delta, m, v


def reference(x, norm1_g, norm2_g, pool_w, pool_scale, kv_norm_g, w_kv, b_kv, w_q, b_q, sinks, w_o, b_o, ffn_up, ffn_conv_w, ffn_conv_b, ffn_down, final_g, loss_target, m_norm1_g, m_norm2_g, m_pool_w, m_pool_scale, m_kv_norm_g, m_w_kv, m_b_kv, m_w_q, m_b_q, m_sinks, m_w_o, m_b_o, m_ffn_up, m_ffn_conv_w, m_ffn_conv_b, m_ffn_down, m_final_g, v_norm1_g, v_norm2_g, v_pool_w, v_pool_scale, v_kv_norm_g, v_w_kv, v_b_kv, v_w_q, v_b_q, v_sinks, v_w_o, v_b_o, v_ffn_up, v_ffn_conv_w, v_ffn_conv_b, v_ffn_down, v_final_g):
    given = dict(x=x, norm1_g=norm1_g, norm2_g=norm2_g, pool_w=pool_w, pool_scale=pool_scale, kv_norm_g=kv_norm_g, w_kv=w_kv, b_kv=b_kv, w_q=w_q, b_q=b_q, sinks=sinks, w_o=w_o, b_o=b_o, ffn_up=ffn_up, ffn_conv_w=ffn_conv_w, ffn_conv_b=ffn_conv_b, ffn_down=ffn_down, final_g=final_g, loss_target=loss_target, m_norm1_g=m_norm1_g, m_norm2_g=m_norm2_g, m_pool_w=m_pool_w, m_pool_scale=m_pool_scale, m_kv_norm_g=m_kv_norm_g, m_w_kv=m_w_kv, m_b_kv=m_b_kv, m_w_q=m_w_q, m_b_q=m_b_q, m_sinks=m_sinks, m_w_o=m_w_o, m_b_o=m_b_o, m_ffn_up=m_ffn_up, m_ffn_conv_w=m_ffn_conv_w, m_ffn_conv_b=m_ffn_conv_b, m_ffn_down=m_ffn_down, m_final_g=m_final_g, v_norm1_g=v_norm1_g, v_norm2_g=v_norm2_g, v_pool_w=v_pool_w, v_pool_scale=v_pool_scale, v_kv_norm_g=v_kv_norm_g, v_w_kv=v_w_kv, v_b_kv=v_b_kv, v_w_q=v_w_q, v_b_q=v_b_q, v_sinks=v_sinks, v_w_o=v_w_o, v_b_o=v_b_o, v_ffn_up=v_ffn_up, v_ffn_conv_w=v_ffn_conv_w, v_ffn_conv_b=v_ffn_conv_b, v_ffn_down=v_ffn_down, v_final_g=v_final_g)
    weights = {n: given[n] for n in TWIN_WEIGHTS}
    shared = {n: given[n] for n in SHARED_INPUTS}
    per_example = {n: given[n] for n in ['x']}
    grad_fn = _jax.value_and_grad(_loss, argnums=(0, 1))

    def one_microbatch(ex, loss_target):
        ex = dict(ex)
        diff = ex.pop(TWIN_DIFF_INPUT)
        return grad_fn(weights, diff, {**shared, **ex}, loss_target)

    if N_MICROBATCH == 1:
        loss, (grad_w, grad_x) = one_microbatch(per_example, given["loss_target"])
    else:
        def body(carry, xs):
            loss_sum, grad_sum = carry
            l_k, (gw_k, gx_k) = one_microbatch(xs[0], xs[1])
            with _jax.named_scope("update"):
                return (loss_sum + l_k, _jax.tree.map(_jnp.add, grad_sum, gw_k)), gx_k

        init = (_jnp.zeros((), _jnp.float32), _jax.tree.map(_jnp.zeros_like, weights))
        (loss, grad_w), grad_x = _jax.lax.scan(body, init, (per_example, given["loss_target"]))
    with _jax.named_scope("update"):
        delta_w, new_m, new_v = {}, {}, {}
        for n in TWIN_WEIGHTS:
            delta_w[n], new_m[n], new_v[n] = _adamw(weights[n], grad_w[n], given["m_" + n], given["v_" + n])
    return (loss, grad_x, *[grad_w[n] for n in TWIN_WEIGHTS], *[delta_w[n] for n in TWIN_WEIGHTS],
            *[new_m[n] for n in TWIN_WEIGHTS], *[new_v[n] for n in TWIN_WEIGHTS])
```

```python
import functools

import jax
import jax.numpy as jnp
from jax import lax
from jax.experimental import pallas as pl
from jax.experimental.pallas import tpu as pltpu

_F32 = jnp.float32
_MXU = jnp.bfloat16

N_DEV = 8
D = 1024
DEPTH = 4
N_A = 2
POOL_WINDOWS = (2, 4, 8, 16)
GC = D // len(POOL_WINDOWS)
HALO = 16
HEAD_DIM = 64
N_HEADS = D // HEAD_DIM
GROUP = 8
N_KV = N_HEADS // GROUP
BLK = 128
PAIR = 2 * HEAD_DIM
KVD = 4 * N_KV * HEAD_DIM
CONV_W = 3
EPS = 1e-5
NEG = -1e30

ADAM_LR = 0.001
ADAM_B1 = 0.9
ADAM_B2 = 0.999
ADAM_EPS = 1e-08
ADAM_WD = 0.01
ADAM_STEP = 10

V7X_VMEM_LIMIT = 56 * 1024 * 1024
LANES = 128

_NT = (((1,), (1,)), ((), ()))
_TN = (((0,), (0,)), ((), ()))


def _params(**kw):
    return pltpu.CompilerParams(vmem_limit_bytes=V7X_VMEM_LIMIT, **kw)


def _seq(n=1):
    return _params(dimension_semantics=("arbitrary",) * n)


def _dot(a, b, dims=None):
    if dims is None:
        return jnp.dot(a, b, preferred_element_type=_F32)
    return lax.dot_general(a, b, dims, preferred_element_type=_F32)


def _rms(x):
    r = lax.rsqrt(jnp.mean(x * x, axis=-1, keepdims=True) + EPS)
    return x * r, r


def _rms_bwd(dh, xn, r, g):
    dxn = dh * g
    return r * (dxn - xn * jnp.mean(dxn * xn, axis=-1, keepdims=True))


def _colsum(a):
    return jnp.sum(a, axis=0, keepdims=True)


def _tile(n, want, mult=8):
    for t in range(min(want, n), 0, -1):
        if n % t == 0 and t % mult == 0:
            return t
    return n


def _full(shape):
    zeros = (0,) * len(shape)
    return pl.BlockSpec(shape, lambda *_: zeros)


def _pool_windows(hbuf, h, row, T):
    out = []
    for gi, win in enumerate(POOL_WINDOWS):
        cs = slice(gi * GC, (gi + 1) * GC)
        acc = hbuf[HALO:HALO + T, cs]
        for k in range(1, win):
            acc = acc + hbuf[HALO - k:HALO - k + T, cs]
        cnt = jnp.minimum(row + 1, win).astype(_F32)
        out.append((acc / cnt - h[:, cs], cnt))
    return out


def _pool_fwd(x, g, w, sc):
    S = x.shape[0]
    T = _tile(S, 512, HALO)
    n, hb = S // T, T // HALO

    def body(x_ref, xh_ref, g_ref, w_ref, sc_ref, o_ref, hbuf):
        i = pl.program_id(0)
        gv = g_ref[...]
        xv = x_ref[...]
        h = _rms(xv)[0] * gv
        hbuf[0:HALO, :] = jnp.where(i > 0, _rms(xh_ref[...])[0] * gv, 0.0)
        hbuf[HALO:, :] = h
        row = i * T + lax.broadcasted_iota(jnp.int32, (T, 1), 0)
        for gi, (p, _) in enumerate(_pool_windows(hbuf, h, row, T)):
            cs = slice(gi * GC, (gi + 1) * GC)
            z = _dot(p.astype(_MXU), w_ref[gi])
            o_ref[:, cs] = xv[:, cs] + z * sc_ref[:, cs]

    return pl.pallas_call(
        body, name="pool_fwd", grid=(n,),
        in_specs=[pl.BlockSpec((T, D), lambda i: (i, 0)),
                  pl.BlockSpec((HALO, D), lambda i: (jnp.maximum(i * hb - 1, 0), 0)),
                  _full((1, D)), _full((4, GC, GC)), _full((1, D))],
        out_specs=pl.BlockSpec((T, D), lambda i: (i, 0)),
        out_shape=jax.ShapeDtypeStruct((S, D), _F32),
        scratch_shapes=[pltpu.VMEM((T + HALO, D), _F32)],
        compiler_params=_seq(),
    )(x, x, g, w, sc)


def _pool_bwd(x, dy, g, w, sc):
    S = x.shape[0]
    T = _tile(S, 512, HALO)
    n, hb = S // T, T // HALO

    def body(x_ref, xh_ref, dy_ref, dyh_ref, g_ref, w_ref, sc_ref, dx_ref, dw_ref, dsc_ref, dg_ref,
             hbuf, qbuf, dhbuf):
        i = pl.program_id(0)

        @pl.when(i == 0)
        def _():
            dw_ref[...] = jnp.zeros_like(dw_ref)
            dsc_ref[...] = jnp.zeros_like(dsc_ref)
            dg_ref[...] = jnp.zeros_like(dg_ref)

        gv = g_ref[...]
        xv = x_ref[...]
        xn, r = _rms(xv)
        h = xn * gv
        hbuf[0:HALO, :] = jnp.where(i > 0, _rms(xh_ref[...])[0] * gv, 0.0)
        hbuf[HALO:, :] = h
        dyv = dy_ref[...]
        dz = dyv * sc_ref[...]
        dzh = jnp.where(i < n - 1, dyh_ref[...], 0.0) * sc_ref[...]
        row = i * T + lax.broadcasted_iota(jnp.int32, (T, 1), 0)
        rowh = (i + 1) * T + lax.broadcasted_iota(jnp.int32, (HALO, 1), 0)
        for gi, (p, cnt) in enumerate(_pool_windows(hbuf, h, row, T)):
            win = POOL_WINDOWS[gi]
            cs = slice(gi * GC, (gi + 1) * GC)
            pb = p.astype(_MXU)
            wg = w_ref[gi]
            dsc_ref[:, cs] += _colsum(dyv[:, cs] * _dot(pb, wg))
            dzb = dz[:, cs].astype(_MXU)
            dw_ref[gi] += _dot(pb, dzb, _TN)
            dp = _dot(dzb, wg, _NT)
            dph = _dot(dzh[:, cs].astype(_MXU), wg, _NT)
            qbuf[0:T, cs] = dp / cnt
            qbuf[T:T + HALO, cs] = dph / jnp.minimum(rowh + 1, win).astype(_F32)
            acc = qbuf[0:T, cs]
            for k in range(1, win):
                acc = acc + qbuf[k:k + T, cs]
            dhbuf[:, cs] = acc - dp
        dh = dhbuf[...]
        dg_ref[...] += _colsum(dh * xn)
        dx_ref[...] = dyv + _rms_bwd(dh, xn, r, gv)

    return pl.pallas_call(
        body, name="pool_bwd", grid=(n,),
        in_specs=[pl.BlockSpec((T, D), lambda i: (i, 0)),
                  pl.BlockSpec((HALO, D), lambda i: (jnp.maximum(i * hb - 1, 0), 0)),
                  pl.BlockSpec((T, D), lambda i: (i, 0)),
                  pl.BlockSpec((HALO, D), lambda i: (jnp.minimum((i + 1) * hb, S // HALO - 1), 0)),
                  _full((1, D)), _full((4, GC, GC)), _full((1, D))],
        out_specs=[pl.BlockSpec((T, D), lambda i: (i, 0)), _full((4, GC, GC)), _full((1, D)), _full((1, D))],
        out_shape=[jax.ShapeDtypeStruct((S, D), _F32), jax.ShapeDtypeStruct((4, GC, GC), _F32),
                   jax.ShapeDtypeStruct((1, D), _F32), jax.ShapeDtypeStruct((1, D), _F32)],
        scratch_shapes=[pltpu.VMEM((T + HALO, D), _F32), pltpu.VMEM((T + HALO, D), _F32), pltpu.VMEM((T, D), _F32)],
        compiler_params=_seq(),
    )(x, x, dy, dy, g, w, sc)


FFN_CHUNKS = 2
U_PAD = 16
DC_PAD = 8


def _load_weights(i, pairs, sems):
    @pl.when(i == 0)
    def _():
        cps = [pltpu.make_async_copy(src, dst, sems.at[k]) for k, (src, dst) in enumerate(pairs)]
        for cp in cps:
            cp.start()
        for cp in cps:
            cp.wait()


def _ffn_fwd(x, g, wup_t, cw, cb, wdn):
    S = x.shape[0]
    F2 = wup_t.shape[0]
    F = F2 // 2
    C = F // FFN_CHUNKS
    T = _tile(S, 256, 16)
    n = S // T

    def body(x_ref, g_ref, wup_hbm, cw_ref, cb_ref, wdn_hbm, o_ref, u_ref, wup, wdnv, ubg, ubv, carry, sems):
        i = pl.program_id(0)
        _load_weights(i, [(wup_hbm, wup), (wdn_hbm, wdnv)], sems)

        @pl.when(i == 0)
        def _():
            carry[...] = jnp.zeros_like(carry)

        xv = x_ref[...]
        hb = (_rms(xv)[0] * g_ref[...]).astype(_MXU)
        acc = jnp.zeros((T, D), _F32)
        for j in range(FFN_CHUNKS):
            halves = []
            for cs, ub in ((slice(j * C, (j + 1) * C), ubg), (slice(F + j * C, F + (j + 1) * C), ubv)):
                u = _dot(hb, wup[cs, :], _NT)
                u_ref[:, cs] = u.astype(u_ref.dtype)
                ub[0:8, :] = carry[:, cs]
                ub[8:, :] = u
                carry[:, cs] = u[T - 8:, :]
                halves.append(cw_ref[0:1, cs] * ub[6:6 + T, :] + cw_ref[1:2, cs] * ub[7:7 + T, :]
                              + cw_ref[2:3, cs] * u + cb_ref[:, cs])
            cg, cv = halves
            a = (cg * jax.nn.sigmoid(cg)) * cv
            acc = acc + _dot(a.astype(_MXU), wdnv[j * C:(j + 1) * C, :])
        o_ref[...] = xv + acc

    any_ = pl.BlockSpec(memory_space=pl.ANY)
    return pl.pallas_call(
        body, name="ffn_fwd", grid=(n,),
        in_specs=[pl.BlockSpec((T, D), lambda i: (i, 0)), _full((1, D)), any_, _full((CONV_W, F2)), _full((1, F2)), any_],
        out_specs=[pl.BlockSpec((T, D), lambda i: (i, 0)), pl.BlockSpec((T, F2), lambda i: (i, 0))],
        out_shape=[jax.ShapeDtypeStruct((S, D), _F32), jax.ShapeDtypeStruct((S, F2), _MXU)],
        scratch_shapes=[pltpu.VMEM((F2, D), _MXU), pltpu.VMEM((F, D), _MXU),
                        pltpu.VMEM((T + 8, C), _F32), pltpu.VMEM((T + 8, C), _F32),
                        pltpu.VMEM((8, F2), _F32), pltpu.SemaphoreType.DMA((2,))],
        compiler_params=_seq(),
    )(x, g, wup_t, cw, cb, wdn)


def _ffn_bwd(x, dy, u, g, wup_t, cw, cb, wdn):
    S = x.shape[0]
    F2 = wup_t.shape[0]
    F = F2 // 2
    C = F // FFN_CHUNKS
    T = _tile(S, 128, 16)
    n, hb = S // T, T // U_PAD

    def body(x_ref, dy_ref, u_ref, uh_ref, g_ref, wup_hbm, cw_ref, cb_ref, wdn_hbm,
             dx_ref, du_ref, a_ref, h_ref, dcw_ref, dcb_ref, dg_ref,
             wup, wdnv, ubg, ubv, dbg, dbv, carry, sems):
        i = pl.program_id(0)
        first = i == n - 1
        _load_weights(i, [(wup_hbm, wup), (wdn_hbm, wdnv)], sems)

        @pl.when(i == 0)
        def _():
            carry[...] = jnp.zeros_like(carry)
            dcw_ref[...] = jnp.zeros_like(dcw_ref)
            dcb_ref[...] = jnp.zeros_like(dcb_ref)
            dg_ref[...] = jnp.zeros_like(dg_ref)

        gv = g_ref[...]
        xv = x_ref[...]
        xn, r = _rms(xv)
        hbf = (xn * gv).astype(_MXU)
        h_ref[...] = hbf
        dyv = dy_ref[...]
        dyb = dyv.astype(_MXU)
        dh = jnp.zeros((T, D), _F32)
        for j in range(FFN_CHUNKS):
            gs, vs = slice(j * C, (j + 1) * C), slice(F + j * C, F + (j + 1) * C)
            conv, taps = [], []
            for cs, ub in ((gs, ubg), (vs, ubv)):
                ub[0:U_PAD, :] = jnp.where(first, 0.0, uh_ref[:, cs].astype(_F32))
                ub[U_PAD:, :] = u_ref[:, cs].astype(_F32)
                t3 = (ub[U_PAD - 2:U_PAD - 2 + T, :], ub[U_PAD - 1:U_PAD - 1 + T, :], ub[U_PAD:U_PAD + T, :])
                taps.append(t3)
                conv.append(cw_ref[0:1, cs] * t3[0] + cw_ref[1:2, cs] * t3[1] + cw_ref[2:3, cs] * t3[2] + cb_ref[:, cs])
            cg, cv = conv
            sg = jax.nn.sigmoid(cg)
            sl = cg * sg
            a_ref[:, gs] = (sl * cv).astype(a_ref.dtype)
            da = _dot(dyb, wdnv[gs, :], _NT)
            dcs = (da * cv * (sg * (1.0 + cg * (1.0 - sg))), da * sl)
            for cs, db, dc, t3 in ((gs, dbg, dcs[0], taps[0]), (vs, dbv, dcs[1], taps[1])):
                dcb_ref[:, cs] += _colsum(dc)
                for k in range(CONV_W):
                    dcw_ref[k:k + 1, cs] += _colsum(dc * t3[k])
                db[0:T, :] = dc
                db[T:T + DC_PAD, :] = carry[:, cs]
                carry[:, cs] = dc[0:DC_PAD, :]
                du = cw_ref[2:3, cs] * dc + cw_ref[1:2, cs] * db[1:T + 1, :] + cw_ref[0:1, cs] * db[2:T + 2, :]
                dub = du.astype(_MXU)
                du_ref[:, cs] = dub
                dh = dh + _dot(dub, wup[cs, :])
        dg_ref[...] += _colsum(dh * xn)
        dx_ref[...] = dyv + _rms_bwd(dh, xn, r, gv)

    any_ = pl.BlockSpec(memory_space=pl.ANY)
    rev = lambda i: (n - 1 - i, 0)
    return pl.pallas_call(
        body, name="ffn_bwd", grid=(n,),
        in_specs=[pl.BlockSpec((T, D), rev), pl.BlockSpec((T, D), rev), pl.BlockSpec((T, F2), rev),
                  pl.BlockSpec((U_PAD, F2), lambda i: (jnp.maximum((n - 1 - i) * hb - 1, 0), 0)),
                  _full((1, D)), any_, _full((CONV_W, F2)), _full((1, F2)), any_],
        out_specs=[pl.BlockSpec((T, D), rev), pl.BlockSpec((T, F2), rev), pl.BlockSpec((T, F), rev),
                   pl.BlockSpec((T, D), rev), _full((CONV_W, F2)), _full((1, F2)), _full((1, D))],
        out_shape=[jax.ShapeDtypeStruct((S, D), _F32), jax.ShapeDtypeStruct((S, F2), _MXU),
                   jax.ShapeDtypeStruct((S, F), _MXU), jax.ShapeDtypeStruct((S, D), _MXU),
                   jax.ShapeDtypeStruct((CONV_W, F2), _F32), jax.ShapeDtypeStruct((1, F2), _F32),
                   jax.ShapeDtypeStruct((1, D), _F32)],
        scratch_shapes=[pltpu.VMEM((F2, D), _MXU), pltpu.VMEM((F, D), _MXU),
                        pltpu.VMEM((T + U_PAD, C), _F32), pltpu.VMEM((T + U_PAD, C), _F32),
                        pltpu.VMEM((T + DC_PAD, C), _F32), pltpu.VMEM((T + DC_PAD, C), _F32),
                        pltpu.VMEM((DC_PAD, F2), _F32), pltpu.SemaphoreType.DMA((2,))],
        compiler_params=_seq(),
    )(x, dy, u, u, g, wup_t, cw, cb, wdn)


def _tn_matmul(a, b, name):
    S, M = a.shape
    N = b.shape[1]
    bm = _tile(M, 1408, LANES)
    tk = _tile(S, 512, 16)
    nk = S // tk

    def body(a_ref, b_ref, o_ref, acc):
        k = pl.program_id(1)

        @pl.when(k == 0)
        def _():
            acc[...] = jnp.zeros_like(acc)

        acc[...] += _dot(a_ref[...].astype(_MXU), b_ref[...].astype(_MXU), _TN)

        @pl.when(k == nk - 1)
        def _():
            o_ref[...] = acc[...].astype(o_ref.dtype)

    return pl.pallas_call(
        body, name=name, grid=(M // bm, nk),
        in_specs=[pl.BlockSpec((tk, bm), lambda i, k: (k, i)), pl.BlockSpec((tk, N), lambda i, k: (k, 0))],
        out_specs=pl.BlockSpec((bm, N), lambda i, k: (i, 0)),
        out_shape=jax.ShapeDtypeStruct((M, N), _MXU),
        scratch_shapes=[pltpu.VMEM((bm, N), _F32)],
        compiler_params=_seq(2),
    )(a, b)


def _kv_fwd(x, g, wkv, bkv):
    S = x.shape[0]
    T = _tile(S, 512, 16)

    def body(x_ref, g_ref, w_ref, b_ref, o_ref):
        hb = (_rms(x_ref[...])[0] * g_ref[...]).astype(_MXU)
        o_ref[...] = (_dot(hb, w_ref[...]) + b_ref[...]).astype(o_ref.dtype)

    return pl.pallas_call(
        body, name="kv_fwd", grid=(S // T,),
        in_specs=[pl.BlockSpec((T, D), lambda i: (i, 0)), _full((1, D)), _full((D, KVD)), _full((1, KVD))],
        out_specs=pl.BlockSpec((T, KVD), lambda i: (i, 0)),
        out_shape=jax.ShapeDtypeStruct((S, KVD), _MXU),
        compiler_params=_seq(),
    )(x, g, wkv, bkv)


def _kv_bwd(x, dx_in, g, wkv, cur_a, prev_a, cur_b, prev_b):
    S = x.shape[0]
    n = S // BLK

    def body(x_ref, dxi_ref, g_ref, w_ref, ca, pa, cb, pb, dx_ref, dw_ref, db_ref, dg_ref):
        i = pl.program_id(0)

        @pl.when(i == 0)
        def _():
            dw_ref[...] = jnp.zeros_like(dw_ref)
            db_ref[...] = jnp.zeros_like(db_ref)
            dg_ref[...] = jnp.zeros_like(dg_ref)

        gv = g_ref[...]
        xn, r = _rms(x_ref[...])
        dkv = ca[...] + cb[...] + jnp.where(i < n - 1, pa[...] + pb[...], 0.0)
        db_ref[...] += _colsum(dkv)
        dkb = dkv.astype(_MXU)
        dw_ref[...] += _dot((xn * gv).astype(_MXU), dkb, _TN)
        dh = _dot(dkb, w_ref[...], _NT)
        dg_ref[...] += _colsum(dh * xn)
        dx_ref[...] = dxi_ref[...] + _rms_bwd(dh, xn, r, gv)

    blk = lambda w: pl.BlockSpec((BLK, w), lambda i: (i, 0))
    nxt = pl.BlockSpec((BLK, KVD), lambda i: (jnp.minimum(i + 1, n - 1), 0))
    return pl.pallas_call(
        body, name="kv_bwd", grid=(n,),
        in_specs=[blk(D), blk(D), _full((1, D)), _full((D, KVD)), blk(KVD), nxt, blk(KVD), nxt],
        out_specs=[blk(D), _full((D, KVD)), _full((1, KVD)), _full((1, D))],
        out_shape=[jax.ShapeDtypeStruct((S, D), _F32), jax.ShapeDtypeStruct((D, KVD), _F32),
                   jax.ShapeDtypeStruct((1, KVD), _F32), jax.ShapeDtypeStruct((1, D), _F32)],
        compiler_params=_seq(),
    )(x, dx_in, g, wkv, cur_a, prev_a, cur_b, prev_b)


def _attn_mask(i):
    qi = lax.broadcasted_iota(jnp.int32, (BLK, 2 * BLK), 0)
    si = lax.broadcasted_iota(jnp.int32, (BLK, 2 * BLK), 1)
    return (si > qi) & (si <= qi + BLK) & jnp.logical_or(i > 0, si >= BLK)


def _head_probs(qm, kd, mask, sink):
    s = jnp.where(mask, _dot(qm, kd, _NT) * (HEAD_DIM ** -0.5), NEG)
    m = jnp.maximum(jnp.max(s, axis=-1, keepdims=True), sink)
    p = jnp.exp(s - m)
    es = jnp.exp(sink - m)
    inv = 1.0 / (jnp.sum(p, axis=-1, keepdims=True) + es)
    return p * inv, es * inv


def _attn_fwd(x, g, wq, bq, sinks, kvd, wo, bo):
    S = x.shape[0]
    n = S // BLK

    def body(x_ref, g_ref, wq_ref, bq_ref, sk_ref, kp_ref, kc_ref, wo_ref, bo_ref, xo_ref, q_ref, o_ref, win):
        i = pl.program_id(0)
        xv = x_ref[...]
        hb = (_rms(xv)[0] * g_ref[...]).astype(_MXU)
        q_ref[...] = (_dot(hb, wq_ref[...]) + bq_ref[...]).astype(q_ref.dtype)
        win[0:BLK, :] = kp_ref[...]
        win[BLK:, :] = kc_ref[...]
        mask = _attn_mask(i)
        low = lax.broadcasted_iota(jnp.int32, (BLK, PAIR), 1) < HEAD_DIM
        for pr in range(N_HEADS // 2):
            kh = (2 * pr) // GROUP
            kd = win[:, kh * PAIR:(kh + 1) * PAIR]
            vd = win[:, (N_KV + kh) * PAIR:(N_KV + kh + 1) * PAIR]
            q2 = q_ref[:, pr * PAIR:(pr + 1) * PAIR]
            outs = []
            for half in range(2):
                qm = jnp.where(low if half == 0 else ~low, q2, jnp.zeros_like(q2))
                pbs, _ = _head_probs(qm, kd, mask, sk_ref[2 * pr + half])
                outs.append(_dot(pbs.astype(_MXU), vd))
            o_ref[:, pr * PAIR:(pr + 1) * PAIR] = jnp.where(low, outs[0], outs[1]).astype(o_ref.dtype)
        xo_ref[...] = xv + _dot(o_ref[...], wo_ref[...]) + bo_ref[...]

    blk = lambda w: pl.BlockSpec((BLK, w), lambda i: (i, 0))
    return pl.pallas_call(
        body, name="attn_fwd", grid=(n,),
        in_specs=[blk(D), _full((1, D)), _full((D, D)), _full((1, D)),
                  pl.BlockSpec(memory_space=pltpu.SMEM),
                  pl.BlockSpec((BLK, KVD), lambda i: (jnp.maximum(i - 1, 0), 0)), blk(KVD),
                  _full((D, D)), _full((1, D))],
        out_specs=[blk(D), blk(D), blk(D)],
        out_shape=[jax.ShapeDtypeStruct((S, D), _F32), jax.ShapeDtypeStruct((S, D), _MXU),
                   jax.ShapeDtypeStruct((S, D), _MXU)],
        scratch_shapes=[pltpu.VMEM((2 * BLK, KVD), _MXU)],
        compiler_params=_seq(),
    )(x, g, wq, bq, sinks, kvd, kvd, wo, bo)


def _attn_bwd(x, dy, q, g, wq, sinks, kvd, wo):
    S = x.shape[0]
    n = S // BLK

    def body(x_ref, dy_ref, q_ref, g_ref, wq_ref, sk_ref, kp_ref, kc_ref, wo_ref,
             dx_ref, dq_ref, h_ref, dc_ref, dp_ref, dbq_ref, dbo_ref, dg_ref, dsk_ref, win, dob, dwin):
        i = pl.program_id(0)

        @pl.when(i == 0)
        def _():
            dbq_ref[...] = jnp.zeros_like(dbq_ref)
            dbo_ref[...] = jnp.zeros_like(dbo_ref)
            dg_ref[...] = jnp.zeros_like(dg_ref)
            dsk_ref[...] = jnp.zeros_like(dsk_ref)

        gv = g_ref[...]
        xv = x_ref[...]
        xn, r = _rms(xv)
        h_ref[...] = (xn * gv).astype(h_ref.dtype)
        dyv = dy_ref[...]
        dbo_ref[...] += _colsum(dyv)
        dob[...] = _dot(dyv.astype(_MXU), wo_ref[...], _NT).astype(dob.dtype)
        win[0:BLK, :] = kp_ref[...]
        win[BLK:, :] = kc_ref[...]
        dwin[...] = jnp.zeros_like(dwin)
        mask = _attn_mask(i)
        low = lax.broadcasted_iota(jnp.int32, (BLK, PAIR), 1) < HEAD_DIM
        lane = lax.broadcasted_iota(jnp.int32, (1, LANES), 1)
        dq_all = []
        for pr in range(N_HEADS // 2):
            kh = (2 * pr) // GROUP
            ks = slice(kh * PAIR, (kh + 1) * PAIR)
            vs = slice((N_KV + kh) * PAIR, (N_KV + kh + 1) * PAIR)
            kd, vd = win[:, ks], win[:, vs]
            q2 = q_ref[:, pr * PAIR:(pr + 1) * PAIR]
            do2 = dob[:, pr * PAIR:(pr + 1) * PAIR]
            dqs = []
            for half in range(2):
                sel = low if half == 0 else ~low
                qm = jnp.where(sel, q2, jnp.zeros_like(q2))
                dom = jnp.where(sel, do2, jnp.zeros_like(do2))
                pbs, ps = _head_probs(qm, kd, mask, sk_ref[2 * pr + half])
                dpr = _dot(dom, vd, _NT)
                delta = jnp.sum(pbs * dpr, axis=-1, keepdims=True)
                dsb = (pbs * (dpr - delta) * (HEAD_DIM ** -0.5)).astype(_MXU)
                dqs.append(_dot(dsb, kd))
                dwin[:, ks] += _dot(dsb, qm, _TN)
                dwin[:, vs] += _dot(pbs.astype(_MXU), dom, _TN)
                dsk_ref[...] += jnp.where(lane == 2 * pr + half, -jnp.sum(ps * delta), 0.0)
            dq_all.append(jnp.where(low, dqs[0], dqs[1]))
        dq = jnp.concatenate(dq_all, axis=1)
        dbq_ref[...] += _colsum(dq)
        dqb = dq.astype(_MXU)
        dq_ref[...] = dqb
        dp_ref[...] = dwin[0:BLK, :]
        dc_ref[...] = dwin[BLK:, :]
        dh = _dot(dqb, wq_ref[...], _NT)
        dg_ref[...] += _colsum(dh * xn)
        dx_ref[...] = dyv + _rms_bwd(dh, xn, r, gv)

    blk = lambda w: pl.BlockSpec((BLK, w), lambda i: (i, 0))
    return pl.pallas_call(
        body, name="attn_bwd", grid=(n,),
        in_specs=[blk(D), blk(D), blk(D), _full((1, D)), _full((D, D)),
                  pl.BlockSpec(memory_space=pltpu.SMEM),
                  pl.BlockSpec((BLK, KVD), lambda i: (jnp.maximum(i - 1, 0), 0)), blk(KVD), _full((D, D))],
        out_specs=[blk(D), blk(D), blk(D), blk(KVD), blk(KVD),
                   _full((1, D)), _full((1, D)), _full((1, D)), _full((1, LANES))],
        out_shape=[jax.ShapeDtypeStruct((S, D), _F32), jax.ShapeDtypeStruct((S, D), _MXU),
                   jax.ShapeDtypeStruct((S, D), _MXU), jax.ShapeDtypeStruct((S, KVD), _F32),
                   jax.ShapeDtypeStruct((S, KVD), _F32), jax.ShapeDtypeStruct((1, D), _F32),
                   jax.ShapeDtypeStruct((1, D), _F32), jax.ShapeDtypeStruct((1, D), _F32),
                   jax.ShapeDtypeStruct((1, LANES), _F32)],
        scratch_shapes=[pltpu.VMEM((2 * BLK, KVD), _MXU), pltpu.VMEM((BLK, D), _MXU),
                        pltpu.VMEM((2 * BLK, KVD), _F32)],
        compiler_params=_seq(),
    )(x, dy, q, g, wq, sinks, kvd, kvd, wo)


def _loss_bwd(x, g, tgt):
    S = x.shape[0]
    T = _tile(S, 512, 8)

    def body(x_ref, g_ref, t_ref, dx_ref, ls_ref, dg_ref):
        @pl.when(pl.program_id(0) == 0)
        def _():
            ls_ref[...] = jnp.zeros_like(ls_ref)
            dg_ref[...] = jnp.zeros_like(dg_ref)

        gv = g_ref[...]
        xn, r = _rms(x_ref[...])
        err = xn * gv - t_ref[...]
        ls_ref[...] += 0.5 * jnp.sum(jnp.mean(err * err, axis=-1, keepdims=True))
        dyv = err * (1.0 / D)
        dg_ref[...] += _colsum(dyv * xn)
        dx_ref[...] = _rms_bwd(dyv, xn, r, gv)

    return pl.pallas_call(
        body, name="loss_bwd", grid=(S // T,),
        in_specs=[pl.BlockSpec((T, D), lambda i: (i, 0)), _full((1, D)), pl.BlockSpec((T, D), lambda i: (i, 0))],
        out_specs=[pl.BlockSpec((T, D), lambda i: (i, 0)), _full((8, LANES)), _full((1, D))],
        out_shape=[jax.ShapeDtypeStruct((S, D), _F32), jax.ShapeDtypeStruct((8, LANES), _F32),
                   jax.ShapeDtypeStruct((1, D), _F32)],
        compiler_params=_seq(),
    )(x, g, tgt)


def _me():
    return 4 * lax.axis_index("x") + 2 * lax.axis_index("y") + lax.axis_index("c")


def _peer(j):
    x, y, c = lax.axis_index("x"), lax.axis_index("y"), lax.axis_index("c")
    px = 1 - x if j & 4 else x
    py = 1 - y if j & 2 else y
    pc = 1 - c if j & 1 else c
    return (px, py, pc), 4 * px + 2 * py + pc


def _exchange(name, srcs, out_shapes, plan):
    ns, no, nt = len(srcs), len(out_shapes), len(plan)

    def body(*refs):
        src_refs, out_refs = refs[:ns], refs[ns:ns + no]
        send_sems, recv_sems, local_sems = refs[ns + no:]
        me = _me()
        local, remote = [], []
        for t, (si, oi, src_of, dst_of) in enumerate(plan):
            src, out = src_refs[si], out_refs[oi]
            cp = pltpu.make_async_copy(src_of(src, me), dst_of(out, me), local_sems.at[t])
            cp.start()
            local.append(cp)
            for j in range(1, N_DEV):
                dev, pk = _peer(j)
                pltpu.make_async_remote_copy(
                    src_ref=src_of(src, pk), dst_ref=dst_of(out, me), send_sem=send_sems.at[t, j - 1],
                    recv_sem=recv_sems.at[t, j - 1], device_id=dev, device_id_type=pl.DeviceIdType.MESH).start()
                remote.append(pltpu.make_async_remote_copy(
                    src_ref=src_of(src, pk), dst_ref=dst_of(out, pk), send_sem=send_sems.at[t, j - 1],
                    recv_sem=recv_sems.at[t, j - 1], device_id=dev, device_id_type=pl.DeviceIdType.MESH))
        for cp in local:
            cp.wait()
        for cp in remote:
            cp.wait()

    any_ = pl.BlockSpec(memory_space=pl.ANY)
    return pl.pallas_call(
        body, name=name, in_specs=[any_] * ns, out_specs=[any_] * no, out_shape=out_shapes,
        scratch_shapes=[pltpu.SemaphoreType.DMA((nt, N_DEV - 1)), pltpu.SemaphoreType.DMA((nt, N_DEV - 1)),
                        pltpu.SemaphoreType.DMA((nt,))],
    )(*srcs)


def _rows(axis, size):
    def of(ref, b):
        start = b * size
        if size % 8 == 0:
            start = pl.multiple_of(start, 8)
        return ref.at[(slice(None),) * axis + (pl.ds(start, size),)]
    return of


def _all_gather(shards):
    srcs, outs, plan = [], [], []
    for si, (a, axis) in enumerate(shards):
        srcs.append(a)
        size = a.shape[1 + axis]
        full = a.shape[1:1 + axis] + (N_DEV * size,) + a.shape[2 + axis:]
        for l in range(a.shape[0]):
            plan.append((si, len(outs), (lambda l: lambda ref, b: ref.at[l])(l), _rows(axis, size)))
            outs.append(jax.ShapeDtypeStruct(full, a.dtype))
    res = list(_exchange("all_gather", srcs, outs, plan))
    out, k = [], 0
    for a, _ in shards:
        out.append(res[k:k + a.shape[0]])
        k += a.shape[0]
    return out


def _reduce_grads(groups):
    srcs, outs, plan = [], [], []
    for oi, (arrs, axis) in enumerate(groups):
        a = arrs[0]
        size = a.shape[axis] // N_DEV
        shard = a.shape[:axis] + (size,) + a.shape[axis + 1:]
        outs.append(jax.ShapeDtypeStruct((N_DEV, len(arrs)) + shard, a.dtype))
        for l, arr in enumerate(arrs):
            plan.append((len(srcs), oi, _rows(axis, size), (lambda l: lambda ref, b: ref.at[b, l])(l)))
            srcs.append(arr)
    return list(_exchange("reduce_grads", srcs, outs, plan))


def _all_reduce_small(p):
    R = p.shape[0]

    def body(p_ref, o_ref, land, send_sems, recv_sems):
        me = _me()
        land[me] = p_ref[...]
        waits = []
        for j in range(1, N_DEV):
            dev, pk = _peer(j)
            pltpu.make_async_remote_copy(
                src_ref=p_ref, dst_ref=land.at[me], send_sem=send_sems.at[j - 1], recv_sem=recv_sems.at[j - 1],
                device_id=dev, device_id_type=pl.DeviceIdType.MESH).start()
            waits.append(pltpu.make_async_remote_copy(
                src_ref=p_ref, dst_ref=land.at[pk], send_sem=send_sems.at[j - 1], recv_sem=recv_sems.at[j - 1],
                device_id=dev, device_id_type=pl.DeviceIdType.MESH))
        for cp in waits:
            cp.wait()
        tot = land[0]
        for b in range(1, N_DEV):
            tot = tot + land[b]
        o_ref[...] = tot

    vmem = pl.BlockSpec(memory_space=pltpu.VMEM)
    return pl.pallas_call(
        body, name="all_reduce_small", in_specs=[vmem], out_specs=vmem,
        out_shape=jax.ShapeDtypeStruct((R, LANES), _F32),
        scratch_shapes=[pltpu.VMEM((N_DEV, R, LANES), _F32), pltpu.SemaphoreType.DMA((N_DEV - 1,)),
                        pltpu.SemaphoreType.DMA((N_DEV - 1,))],
        compiler_params=_params(),
    )(p)


def _sum_landed(land):
    g = land[0].astype(_F32)
    for b in range(1, N_DEV):
        g = g + land[b].astype(_F32)
    return g


def _sum8(land):
    _, R, C = land.shape
    tr = _tile(R, 352, 16)

    def body(l_ref, o_ref):
        o_ref[...] = _sum_landed(l_ref)

    return pl.pallas_call(
        body, name="sum8", grid=(R // tr,),
        in_specs=[pl.BlockSpec((N_DEV, tr, C), lambda i: (0, i, 0))],
        out_specs=pl.BlockSpec((tr, C), lambda i: (i, 0)),
        out_shape=jax.ShapeDtypeStruct((R, C), _F32),
        compiler_params=_seq(),
    )(land)


def _adamw(g, w, m, v, name):
    landed = g.ndim == 3
    R, C = w.shape
    tr = _tile(R, 176 if landed else 256, 16)
    bc1 = 1.0 - ADAM_B1 ** ADAM_STEP
    bc2 = 1.0 - ADAM_B2 ** ADAM_STEP

    def body(g_ref, w_ref, m_ref, v_ref, *outs):
        gv = _sum_landed(g_ref) if landed else g_ref[...]
        if landed:
            outs[0][...] = gv
        d_ref, mo_ref, vo_ref = outs[-3:]
        mn = ADAM_B1 * m_ref[...] + (1.0 - ADAM_B1) * gv
        vn = ADAM_B2 * v_ref[...] + (1.0 - ADAM_B2) * (gv * gv)
        mo_ref[...] = mn
        vo_ref[...] = vn
        d_ref[...] = -ADAM_LR * ((mn / bc1) / (jnp.sqrt(vn / bc2) + ADAM_EPS) + ADAM_WD * w_ref[...])

    row = pl.BlockSpec((tr, C), lambda i: (i, 0))
    gspec = pl.BlockSpec((N_DEV, tr, C), lambda i: (0, i, 0)) if landed else row
    n_out = 4 if landed else 3
    return pl.pallas_call(
        body, name=name, grid=(R // tr,), in_specs=[gspec, row, row, row], out_specs=[row] * n_out,
        out_shape=[jax.ShapeDtypeStruct((R, C), _F32)] * n_out, compiler_params=_seq(),
    )(g, w, m, v)


def _pack(parts):
    flat = jnp.concatenate([p.reshape(-1).astype(_F32) for p in parts])
    n = flat.shape[0]
    rows = -(-n // (8 * LANES)) * 8
    return jnp.pad(flat, (0, rows * LANES - n)).reshape(rows, LANES)


def _unpack(packed, shapes):
    flat, out, k = packed.reshape(-1), [], 0
    for s in shapes:
        n = 1
        for d in s:
            n *= d
        out.append(flat[k:k + n].reshape(s))
        k += n
    return out


def kernel(x, norm1_g, norm2_g, pool_w, pool_scale, kv_norm_g, w_kv, b_kv, w_q, b_q, sinks, w_o, b_o, ffn_up, ffn_conv_w, ffn_conv_b, ffn_down, final_g, loss_target, m_norm1_g, m_norm2_g, m_pool_w, m_pool_scale, m_kv_norm_g, m_w_kv, m_b_kv, m_w_q, m_b_q, m_sinks, m_w_o, m_b_o, m_ffn_up, m_ffn_conv_w, m_ffn_conv_b, m_ffn_down, m_final_g, v_norm1_g, v_norm2_g, v_pool_w, v_pool_scale, v_kv_norm_g, v_w_kv, v_b_kv, v_w_q, v_b_q, v_sinks, v_w_o, v_b_o, v_ffn_up, v_ffn_conv_w, v_ffn_conv_b, v_ffn_down, v_final_g):
    S = x.shape[1]
    F2s = ffn_up.shape[2]
    F2 = N_DEV * F2s
    me = _me()
    x0 = x.reshape(S, D)
    tgt = loss_target.reshape(S, D)
    row = lambda a: a.reshape(1, -1)

    small = _pack([pool_scale, ffn_conv_w])
    (up_t, down, wq, wo, (wkv,), pw, (small_all,)) = _all_gather([
        (jnp.swapaxes(ffn_up.astype(_MXU), 1, 2), 0),
        (ffn_down.astype(_MXU), 0),
        (w_q.astype(_MXU), 0),
        (w_o.astype(_MXU), 0),
        (w_kv.astype(_MXU)[None], 0),
        (pool_w.astype(_MXU), 1),
        (small[None, None], 0),
    ])
    n_ps = pool_scale.size
    small_all = small_all.reshape(N_DEV, -1)
    pscale = jnp.transpose(small_all[:, :n_ps].reshape(N_DEV, N_A, D // N_DEV), (1, 0, 2)).reshape(N_A, D)
    conv_w = jnp.transpose(small_all[:, n_ps:n_ps + ffn_conv_w.size].reshape(N_DEV, DEPTH, CONV_W, F2s),
                           (1, 2, 0, 3)).reshape(DEPTH, CONV_W, F2)

    def dup(a):
        a4 = a.reshape(a.shape[:-1] + (2 * N_KV, 1, HEAD_DIM))
        return jnp.broadcast_to(a4, a.shape[:-1] + (2 * N_KV, 2, HEAD_DIM)).reshape(a.shape[:-1] + (KVD,))

    def fold(a):
        return a.reshape(a.shape[:-1] + (2 * N_KV, 2, HEAD_DIM)).sum(axis=-2).reshape(a.shape[:-1] + (2 * N_KV * HEAD_DIM,))

    wkv_d, bkv_d = dup(wkv), dup(row(b_kv))

    xs, us, qs, os_ = [x0], [], [], []
    xc = x0
    kvd = None
    for l in range(DEPTH):
        if l < N_A:
            xc = _pool_fwd(xc, row(norm1_g[l]), pw[l], row(pscale[l]))
        else:
            j = l - N_A
            xc, q, o = _attn_fwd(xc, row(norm1_g[l]), wq[j], row(b_q[j]), sinks[j], kvd, wo[j], row(b_o[j]))
            qs.append(q)
            os_.append(o)
        xs.append(xc)
        xc, u = _ffn_fwd(xc, row(norm2_g[l]), up_t[l], conv_w[l], row(ffn_conv_b[l]), down[l])
        us.append(u)
        xs.append(xc)
        if l == N_A - 1:
            kvd = _kv_fwd(xc, row(kv_norm_g), wkv_d, bkv_d)

    dx, loss_p, d_final = _loss_bwd(xc, row(final_g), tgt)
    d_n1, d_n2, d_cw, d_cb = [None] * DEPTH, [None] * DEPTH, [None] * DEPTH, [None] * DEPTH
    g_up, g_down, g_wq, g_wo, g_pw = [None] * DEPTH, [None] * DEPTH, [None] * 2, [None] * 2, [None] * N_A
    d_bq, d_bo, d_sk, d_ps, dkv_parts = [None] * 2, [None] * 2, [None] * 2, [None] * N_A, []
    for l in reversed(range(DEPTH)):
        x_in, x_mid, x_out = xs[2 * l], xs[2 * l + 1], xs[2 * l + 2]
        if l == N_A - 1:
            dx, d_wkv, d_bkv, d_kvg = _kv_bwd(x_out, dx, row(kv_norm_g), wkv_d, *dkv_parts)
        dy = dx
        dx, du, a, h, d_cw[l], d_cb[l], d_n2[l] = _ffn_bwd(
            x_mid, dy, us[l], row(norm2_g[l]), up_t[l], conv_w[l], row(ffn_conv_b[l]), down[l])
        g_up[l] = _tn_matmul(du, h, "tn_up")
        g_down[l] = _tn_matmul(a, dy, "tn_down")
        dy = dx
        if l < N_A:
            dx, d_pw, d_ps[l], d_n1[l] = _pool_bwd(x_in, dy, row(norm1_g[l]), pw[l], row(pscale[l]))
            g_pw[l] = d_pw.astype(_MXU)
        else:
            j = l - N_A
            dx, dq, h, d_cur, d_prev, d_bq[j], d_bo[j], d_n1[l], d_sk[j] = _attn_bwd(
                x_in, dy, qs[j], row(norm1_g[l]), wq[j], sinks[j], kvd, wo[j])
            g_wq[j] = _tn_matmul(h, dq, "tn_q")
            g_wo[j] = _tn_matmul(os_[j], dy, "tn_o")
            dkv_parts += [d_cur, d_prev]

    land_up, land_down, land_wq, land_wo, land_wkv, land_pw = _reduce_grads([
        (g_up, 0), (g_down, 0), (g_wq, 0), (g_wo, 0), ([fold(d_wkv).astype(_MXU)], 0), (g_pw, 1)])
    rep_names = ["norm1_g", "norm2_g", "kv_norm_g", "b_kv", "b_q", "sinks", "b_o", "ffn_conv_b", "final_g"]
    rep = [jnp.concatenate(d_n1), jnp.concatenate(d_n2), d_kvg, fold(d_bkv), jnp.concatenate(d_bq),
           jnp.concatenate([s[:, :N_HEADS] for s in d_sk]), jnp.concatenate(d_bo), jnp.concatenate(d_cb), d_final]
    full = [jnp.concatenate(d_ps), jnp.stack(d_cw), loss_p[0:1, 0:1]]
    given = dict(norm1_g=norm1_g, norm2_g=norm2_g, kv_norm_g=kv_norm_g, b_kv=b_kv, b_q=b_q, sinks=sinks, b_o=b_o,
                 ffn_conv_b=ffn_conv_b, final_g=final_g, pool_scale=pool_scale, ffn_conv_w=ffn_conv_w)
    tot = _unpack(_all_reduce_small(_pack(rep + full)),
                  [given[k].shape for k in rep_names] + [(N_A, D), (DEPTH, CONV_W, F2), ()])
    grad = dict(zip(rep_names, tot))
    loss = tot[-1]
    grad["pool_scale"] = lax.dynamic_slice_in_dim(tot[-3], me * (D // N_DEV), D // N_DEV, axis=1)
    grad["ffn_conv_w"] = lax.dynamic_slice_in_dim(tot[-2], me * F2s, F2s, axis=2)

    moms = dict(norm1_g=(m_norm1_g, v_norm1_g), norm2_g=(m_norm2_g, v_norm2_g), pool_w=(m_pool_w, v_pool_w),
                pool_scale=(m_pool_scale, v_pool_scale), kv_norm_g=(m_kv_norm_g, v_kv_norm_g), w_kv=(m_w_kv, v_w_kv),
                b_kv=(m_b_kv, v_b_kv), w_q=(m_w_q, v_w_q), b_q=(m_b_q, v_b_q), sinks=(m_sinks, v_sinks),
                w_o=(m_w_o, v_w_o), b_o=(m_b_o, v_b_o), ffn_up=(m_ffn_up, v_ffn_up),
                ffn_conv_w=(m_ffn_conv_w, v_ffn_conv_w), ffn_conv_b=(m_ffn_conv_b, v_ffn_conv_b),
                ffn_down=(m_ffn_down, v_ffn_down), final_g=(m_final_g, v_final_g))
    given.update(pool_w=pool_w, w_kv=w_kv, w_q=w_q, w_o=w_o, ffn_up=ffn_up, ffn_down=ffn_down)
    delta, new_m, new_v = {}, {}, {}

    small_names = rep_names + ["pool_scale", "ffn_conv_w"]
    shapes = [given[k].shape for k in small_names]
    outs = _adamw(_pack([grad[k] for k in small_names]), _pack([given[k] for k in small_names]),
                  _pack([moms[k][0] for k in small_names]), _pack([moms[k][1] for k in small_names]), "adamw_small")
    for dst, packed in zip((delta, new_m, new_v), outs):
        dst.update(zip(small_names, _unpack(packed, shapes)))

    def update(name, g, cols):
        w = given[name]
        two_d = lambda a: a.reshape(-1, cols)
        outs = _adamw(g, two_d(w), two_d(moms[name][0]), two_d(moms[name][1]), "adamw_" + name)
        if g.ndim == 3:
            grad[name] = outs[0].reshape(w.shape)
        delta[name], new_m[name], new_v[name] = (o.reshape(w.shape) for o in outs[-3:])

    g_t = _sum8(land_up.reshape(N_DEV, -1, D)).reshape(DEPTH, F2s, D)
    grad["ffn_up"] = jnp.swapaxes(g_t, 1, 2)
    update("ffn_up", grad["ffn_up"].reshape(-1, F2s), F2s)
    update("ffn_down", land_down.reshape(N_DEV, -1, D), D)
    update("w_q", land_wq.reshape(N_DEV, -1, D), D)
    update("w_o", land_wo.reshape(N_DEV, -1, D), D)
    update("w_kv", land_wkv.reshape(N_DEV, -1, w_kv.shape[1]), w_kv.shape[1])
    update("pool_w", land_pw.reshape(N_DEV, -1, GC), GC)

    names = ["norm1_g", "norm2_g", "pool_w", "pool_scale", "kv_norm_g", "w_kv", "b_kv", "w_q", "b_q", "sinks", "w_o",
             "b_o", "ffn_up", "ffn_conv_w", "ffn_conv_b", "ffn_down", "final_g"]
    return (loss, dx.reshape(x.shape), *[grad[k] for k in names], *[delta[k] for k in names],
            *[new_m[k] for k in names], *[new_v[k] for k in names])
```

```python
import functools

import jax
import jax.numpy as jnp
from jax import lax
from jax.experimental import pallas as pl
from jax.experimental.pallas import tpu as pltpu

_F32 = jnp.float32
_MXU = jnp.bfloat16

N_DEV = 8
D = 1024
DEPTH = 4
N_A = 2
POOL_WINDOWS = (2, 4, 8, 16)
GC = D // len(POOL_WINDOWS)
HALO = 16
HEAD_DIM = 64
N_HEADS = D // HEAD_DIM
GROUP = 8
N_KV = N_HEADS // GROUP
BLK = 128
PAIR = 2 * HEAD_DIM
KVD = 4 * N_KV * HEAD_DIM
CONV_W = 3
EPS = 1e-5
NEG = -1e30

ADAM_LR = 0.001
ADAM_B1 = 0.9
ADAM_B2 = 0.999
ADAM_EPS = 1e-08
ADAM_WD = 0.01
ADAM_STEP = 10

V7X_VMEM_LIMIT = 56 * 1024 * 1024
LANES = 128

_NT = (((1,), (1,)), ((), ()))
_TN = (((0,), (0,)), ((), ()))


def _params(**kw):
    return pltpu.CompilerParams(vmem_limit_bytes=V7X_VMEM_LIMIT, **kw)


def _seq(n=1):
    return _params(dimension_semantics=("arbitrary",) * n)


def _dot(a, b, dims=None):
    if dims is None:
        return jnp.dot(a, b, preferred_element_type=_F32)
    return lax.dot_general(a, b, dims, preferred_element_type=_F32)


def _rms(x):
    r = lax.rsqrt(jnp.mean(x * x, axis=-1, keepdims=True) + EPS)
    return x * r, r


def _rms_bwd(dh, xn, r, g):
    dxn = dh * g
    return r * (dxn - xn * jnp.mean(dxn * xn, axis=-1, keepdims=True))


def _colsum(a):
    return jnp.sum(a, axis=0, keepdims=True)


def _tile(n, want, mult=8):
    for t in range(min(want, n), 0, -1):
        if n % t == 0 and t % mult == 0:
            return t
    return n


def _full(shape):
    zeros = (0,) * len(shape)
    return pl.BlockSpec(shape, lambda *_: zeros)


def _pool_windows(hbuf, h, row, T):
    out = []
    for gi, win in enumerate(POOL_WINDOWS):
        cs = slice(gi * GC, (gi + 1) * GC)
        acc = hbuf[HALO:HALO + T, cs]
        for k in range(1, win):
            acc = acc + hbuf[HALO - k:HALO - k + T, cs]
        cnt = jnp.minimum(row + 1, win).astype(_F32)
        out.append((acc / cnt - h[:, cs], cnt))
    return out


def _pool_fwd(x, g, w, sc):
    S = x.shape[0]
    T = _tile(S, 512, HALO)
    n, hb = S // T, T // HALO

    def body(x_ref, xh_ref, g_ref, w_ref, sc_ref, o_ref, hbuf):
        i = pl.program_id(0)
        gv = g_ref[...]
        xv = x_ref[...]
        h = _rms(xv)[0] * gv
        hbuf[0:HALO, :] = jnp.where(i > 0, _rms(xh_ref[...])[0] * gv, 0.0)
        hbuf[HALO:, :] = h
        row = i * T + lax.broadcasted_iota(jnp.int32, (T, 1), 0)
        for gi, (p, _) in enumerate(_pool_windows(hbuf, h, row, T)):
            cs = slice(gi * GC, (gi + 1) * GC)
            z = _dot(p.astype(_MXU), w_ref[gi])
            o_ref[:, cs] = xv[:, cs] + z * sc_ref[:, cs]

    return pl.pallas_call(
        body, name="pool_fwd", grid=(n,),
        in_specs=[pl.BlockSpec((T, D), lambda i: (i, 0)),
                  pl.BlockSpec((HALO, D), lambda i: (jnp.maximum(i * hb - 1, 0), 0)),
                  _full((1, D)), _full((4, GC, GC)), _full((1, D))],
        out_specs=pl.BlockSpec((T, D), lambda i: (i, 0)),
        out_shape=jax.ShapeDtypeStruct((S, D), _F32),
        scratch_shapes=[pltpu.VMEM((T + HALO, D), _F32)],
        compiler_params=_seq(),
    )(x, x, g, w, sc)


def _pool_bwd(x, dy, g, w, sc):
    S = x.shape[0]
    T = _tile(S, 512, HALO)
    n, hb = S // T, T // HALO

    def body(x_ref, xh_ref, dy_ref, dyh_ref, g_ref, w_ref, sc_ref, dx_ref, dw_ref, dsc_ref, dg_ref,
             hbuf, qbuf, dhbuf):
        i = pl.program_id(0)

        @pl.when(i == 0)
        def _():
            dw_ref[...] = jnp.zeros_like(dw_ref)
            dsc_ref[...] = jnp.zeros_like(dsc_ref)
            dg_ref[...] = jnp.zeros_like(dg_ref)

        gv = g_ref[...]
        xv = x_ref[...]
        xn, r = _rms(xv)
        h = xn * gv
        hbuf[0:HALO, :] = jnp.where(i > 0, _rms(xh_ref[...])[0] * gv, 0.0)
        hbuf[HALO:, :] = h
        dyv = dy_ref[...]
        dz = dyv * sc_ref[...]
        dzh = jnp.where(i < n - 1, dyh_ref[...], 0.0) * sc_ref[...]
        row = i * T + lax.broadcasted_iota(jnp.int32, (T, 1), 0)
        rowh = (i + 1) * T + lax.broadcasted_iota(jnp.int32, (HALO, 1), 0)
        for gi, (p, cnt) in enumerate(_pool_windows(hbuf, h, row, T)):
            win = POOL_WINDOWS[gi]
            cs = slice(gi * GC, (gi + 1) * GC)
            pb = p.astype(_MXU)
            wg = w_ref[gi]
            dsc_ref[:, cs] += _colsum(dyv[:, cs] * _dot(pb, wg))
            dzb = dz[:, cs].astype(_MXU)
            dw_ref[gi] += _dot(pb, dzb, _TN)
            dp = _dot(dzb, wg, _NT)
            dph = _dot(dzh[:, cs].astype(_MXU), wg, _NT)
            qbuf[0:T, cs] = dp / cnt
            qbuf[T:T + HALO, cs] = dph / jnp.minimum(rowh + 1, win).astype(_F32)
            acc = qbuf[0:T, cs]
            for k in range(1, win):
                acc = acc + qbuf[k:k + T, cs]
            dhbuf[:, cs] = acc - dp
        dh = dhbuf[...]
        dg_ref[...] += _colsum(dh * xn)
        dx_ref[...] = dyv + _rms_bwd(dh, xn, r, gv)

    return pl.pallas_call(
        body, name="pool_bwd", grid=(n,),
        in_specs=[pl.BlockSpec((T, D), lambda i: (i, 0)),
                  pl.BlockSpec((HALO, D), lambda i: (jnp.maximum(i * hb - 1, 0), 0)),
                  pl.BlockSpec((T, D), lambda i: (i, 0)),
                  pl.BlockSpec((HALO, D), lambda i: (jnp.minimum((i + 1) * hb, S // HALO - 1), 0)),
                  _full((1, D)), _full((4, GC, GC)), _full((1, D))],
        out_specs=[pl.BlockSpec((T, D), lambda i: (i, 0)), _full((4, GC, GC)), _full((1, D)), _full((1, D))],
        out_shape=[jax.ShapeDtypeStruct((S, D), _F32), jax.ShapeDtypeStruct((4, GC, GC), _F32),
                   jax.ShapeDtypeStruct((1, D), _F32), jax.ShapeDtypeStruct((1, D), _F32)],
        scratch_shapes=[pltpu.VMEM((T + HALO, D), _F32), pltpu.VMEM((T + HALO, D), _F32), pltpu.VMEM((T, D), _F32)],
        compiler_params=_seq(),
    )(x, x, dy, dy, g, w, sc)


FFN_FWD_TILE, FFN_FWD_CHUNKS = 256, 2
FFN_BWD_TILE, FFN_BWD_CHUNKS = 128, 2
EDGE = 8


def _shift_down(v, k, prev):
    r = pltpu.roll(v, k, axis=0)
    i8 = lax.broadcasted_iota(jnp.int32, (EDGE, v.shape[1]), 0)
    head = jnp.where(i8 >= k, r[0:EDGE, :], pltpu.roll(prev, k, axis=0))
    return jnp.concatenate([head, r[EDGE:, :]], axis=0)


def _shift_up(v, k, nxt):
    T = v.shape[0]
    r = pltpu.roll(v, T - k, axis=0)
    i8 = lax.broadcasted_iota(jnp.int32, (EDGE, v.shape[1]), 0)
    tail = jnp.where(i8 < EDGE - k, r[T - EDGE:, :], pltpu.roll(nxt, EDGE - k, axis=0))
    return jnp.concatenate([r[:T - EDGE, :], tail], axis=0)


def _load_weights(i, pairs, sems):
    @pl.when(i == 0)
    def _():
        cps = [pltpu.make_async_copy(src, dst, sems.at[k]) for k, (src, dst) in enumerate(pairs)]
        for cp in cps:
            cp.start()
        for cp in cps:
            cp.wait()


def _ffn_fwd(x, g, wup_t, cw, cb, wdn):
    S = x.shape[0]
    F2 = wup_t.shape[0]
    F = F2 // 2
    C = F // FFN_FWD_CHUNKS
    T = _tile(S, FFN_FWD_TILE, 16)
    n = S // T

    def body(x_ref, g_ref, wup_hbm, cw_ref, cb_ref, wdn_hbm, o_ref, u_ref, c_ref, wup, wdnv, carry, sems):
        i = pl.program_id(0)
        _load_weights(i, [(wup_hbm, wup), (wdn_hbm, wdnv)], sems)

        @pl.when(i == 0)
        def _():
            carry[...] = jnp.zeros_like(carry)

        xv = x_ref[...]
        hb = (_rms(xv)[0] * g_ref[...]).astype(_MXU)
        acc = jnp.zeros((T, D), _F32)
        for j in range(FFN_FWD_CHUNKS):
            halves = []
            for cs in (slice(j * C, (j + 1) * C), slice(F + j * C, F + (j + 1) * C)):
                u = _dot(hb, wup[cs, :], _NT)
                u_ref[:, cs] = u.astype(u_ref.dtype)
                prev = carry[:, cs]
                carry[:, cs] = u[T - EDGE:, :]
                c = (cw_ref[0:1, cs] * _shift_down(u, 2, prev) + cw_ref[1:2, cs] * _shift_down(u, 1, prev)
                     + cw_ref[2:3, cs] * u + cb_ref[:, cs])
                c_ref[:, cs] = c.astype(c_ref.dtype)
                halves.append(c)
            cg, cv = halves
            a = (cg * jax.nn.sigmoid(cg)) * cv
            acc = acc + _dot(a.astype(_MXU), wdnv[j * C:(j + 1) * C, :])
        o_ref[...] = xv + acc

    any_ = pl.BlockSpec(memory_space=pl.ANY)
    wide = pl.BlockSpec((T, F2), lambda i: (i, 0))
    return pl.pallas_call(
        body, name="ffn_fwd", grid=(n,),
        in_specs=[pl.BlockSpec((T, D), lambda i: (i, 0)), _full((1, D)), any_, _full((CONV_W, F2)), _full((1, F2)), any_],
        out_specs=[pl.BlockSpec((T, D), lambda i: (i, 0)), wide, wide],
        out_shape=[jax.ShapeDtypeStruct((S, D), _F32), jax.ShapeDtypeStruct((S, F2), _MXU),
                   jax.ShapeDtypeStruct((S, F2), _MXU)],
        scratch_shapes=[pltpu.VMEM((F2, D), _MXU), pltpu.VMEM((F, D), _MXU),
                        pltpu.VMEM((EDGE, F2), _F32), pltpu.SemaphoreType.DMA((2,))],
        compiler_params=_seq(),
    )(x, g, wup_t, cw, cb, wdn)


def _ffn_bwd(x, dy, u, c, g, wup_t, cw, wdn):
    S = x.shape[0]
    F2 = wup_t.shape[0]
    F = F2 // 2
    C = F // FFN_BWD_CHUNKS
    T = _tile(S, FFN_BWD_TILE, 16)
    n = S // T

    def body(x_ref, dy_ref, u_ref, c_ref, g_ref, wup_hbm, cw_ref, wdn_hbm,
             dx_ref, du_ref, a_ref, h_ref, dcw_ref, dcb_ref, dg_ref, wup, wdnv, carry, sems):
        i = pl.program_id(0)
        _load_weights(i, [(wup_hbm, wup), (wdn_hbm, wdnv)], sems)

        @pl.when(i == 0)
        def _():
            carry[...] = jnp.zeros_like(carry)
            dcw_ref[...] = jnp.zeros_like(dcw_ref)
            dcb_ref[...] = jnp.zeros_like(dcb_ref)
            dg_ref[...] = jnp.zeros_like(dg_ref)

        gv = g_ref[...]
        xv = x_ref[...]
        xn, r = _rms(xv)
        hbf = (xn * gv).astype(_MXU)
        h_ref[...] = hbf
        dyv = dy_ref[...]
        dyb = dyv.astype(_MXU)
        dh = jnp.zeros((T, D), _F32)
        for j in range(FFN_BWD_CHUNKS):
            gs, vs = slice(j * C, (j + 1) * C), slice(F + j * C, F + (j + 1) * C)
            cg, cv = c_ref[:, gs].astype(_F32), c_ref[:, vs].astype(_F32)
            sg = jax.nn.sigmoid(cg)
            sl = cg * sg
            a_ref[:, gs] = (sl * cv).astype(a_ref.dtype)
            da = _dot(dyb, wdnv[gs, :], _NT)
            for cs, dc in ((gs, da * cv * (sg * (1.0 + cg * (1.0 - sg)))), (vs, da * sl)):
                nxt = carry[:, cs]
                carry[:, cs] = dc[0:EDGE, :]
                dc1, dc2 = _shift_up(dc, 1, nxt), _shift_up(dc, 2, nxt)
                uf = u_ref[:, cs].astype(_F32)
                dcb_ref[:, cs] += _colsum(dc)
                for k, d in enumerate((dc2, dc1, dc)):
                    dcw_ref[k:k + 1, cs] += _colsum(d * uf)
                du = cw_ref[2:3, cs] * dc + cw_ref[1:2, cs] * dc1 + cw_ref[0:1, cs] * dc2
                dub = du.astype(_MXU)
                du_ref[:, cs] = dub
                dh = dh + _dot(dub, wup[cs, :])
        dg_ref[...] += _colsum(dh * xn)
        dx_ref[...] = dyv + _rms_bwd(dh, xn, r, gv)

    any_ = pl.BlockSpec(memory_space=pl.ANY)
    rev = lambda i: (n - 1 - i, 0)
    return pl.pallas_call(
        body, name="ffn_bwd", grid=(n,),
        in_specs=[pl.BlockSpec((T, D), rev), pl.BlockSpec((T, D), rev), pl.BlockSpec((T, F2), rev),
                  pl.BlockSpec((T, F2), rev), _full((1, D)), any_, _full((CONV_W, F2)), any_],
        out_specs=[pl.BlockSpec((T, D), rev), pl.BlockSpec((T, F2), rev), pl.BlockSpec((T, F), rev),
                   pl.BlockSpec((T, D), rev), _full((CONV_W, F2)), _full((1, F2)), _full((1, D))],
        out_shape=[jax.ShapeDtypeStruct((S, D), _F32), jax.ShapeDtypeStruct((S, F2), _MXU),
                   jax.ShapeDtypeStruct((S, F), _MXU), jax.ShapeDtypeStruct((S, D), _MXU),
                   jax.ShapeDtypeStruct((CONV_W, F2), _F32), jax.ShapeDtypeStruct((1, F2), _F32),
                   jax.ShapeDtypeStruct((1, D), _F32)],
        scratch_shapes=[pltpu.VMEM((F2, D), _MXU), pltpu.VMEM((F, D), _MXU),
                        pltpu.VMEM((EDGE, F2), _F32), pltpu.SemaphoreType.DMA((2,))],
        compiler_params=_seq(),
    )(x, dy, u, c, g, wup_t, cw, wdn)


def _tn_matmul(a, b, name):
    S, M = a.shape
    N = b.shape[1]
    bm = _tile(M, 1408, LANES)
    tk = _tile(S, 512, 16)
    nk = S // tk

    def body(a_ref, b_ref, o_ref, acc):
        k = pl.program_id(1)

        @pl.when(k == 0)
        def _():
            acc[...] = jnp.zeros_like(acc)

        acc[...] += _dot(a_ref[...].astype(_MXU), b_ref[...].astype(_MXU), _TN)

        @pl.when(k == nk - 1)
        def _():
            o_ref[...] = acc[...].astype(o_ref.dtype)

    return pl.pallas_call(
        body, name=name, grid=(M // bm, nk),
        in_specs=[pl.BlockSpec((tk, bm), lambda i, k: (k, i)), pl.BlockSpec((tk, N), lambda i, k: (k, 0))],
        out_specs=pl.BlockSpec((bm, N), lambda i, k: (i, 0)),
        out_shape=jax.ShapeDtypeStruct((M, N), _MXU),
        scratch_shapes=[pltpu.VMEM((bm, N), _F32)],
        compiler_params=_seq(2),
    )(a, b)


def _kv_fwd(x, g, wkv, bkv):
    S = x.shape[0]
    T = _tile(S, 512, 16)

    def body(x_ref, g_ref, w_ref, b_ref, o_ref):
        hb = (_rms(x_ref[...])[0] * g_ref[...]).astype(_MXU)
        o_ref[...] = (_dot(hb, w_ref[...]) + b_ref[...]).astype(o_ref.dtype)

    return pl.pallas_call(
        body, name="kv_fwd", grid=(S // T,),
        in_specs=[pl.BlockSpec((T, D), lambda i: (i, 0)), _full((1, D)), _full((D, KVD)), _full((1, KVD))],
        out_specs=pl.BlockSpec((T, KVD), lambda i: (i, 0)),
        out_shape=jax.ShapeDtypeStruct((S, KVD), _MXU),
        compiler_params=_seq(),
    )(x, g, wkv, bkv)


def _kv_bwd(x, dx_in, g, wkv, cur_a, prev_a, cur_b, prev_b):
    S = x.shape[0]
    n = S // BLK

    def body(x_ref, dxi_ref, g_ref, w_ref, ca, pa, cb, pb, dx_ref, dw_ref, db_ref, dg_ref):
        i = pl.program_id(0)

        @pl.when(i == 0)
        def _():
            dw_ref[...] = jnp.zeros_like(dw_ref)
            db_ref[...] = jnp.zeros_like(db_ref)
            dg_ref[...] = jnp.zeros_like(dg_ref)

        gv = g_ref[...]
        xn, r = _rms(x_ref[...])
        dkv = ca[...] + cb[...] + jnp.where(i < n - 1, pa[...] + pb[...], 0.0)
        db_ref[...] += _colsum(dkv)
        dkb = dkv.astype(_MXU)
        dw_ref[...] += _dot((xn * gv).astype(_MXU), dkb, _TN)
        dh = _dot(dkb, w_ref[...], _NT)
        dg_ref[...] += _colsum(dh * xn)
        dx_ref[...] = dxi_ref[...] + _rms_bwd(dh, xn, r, gv)

    blk = lambda w: pl.BlockSpec((BLK, w), lambda i: (i, 0))
    nxt = pl.BlockSpec((BLK, KVD), lambda i: (jnp.minimum(i + 1, n - 1), 0))
    return pl.pallas_call(
        body, name="kv_bwd", grid=(n,),
        in_specs=[blk(D), blk(D), _full((1, D)), _full((D, KVD)), blk(KVD), nxt, blk(KVD), nxt],
        out_specs=[blk(D), _full((D, KVD)), _full((1, KVD)), _full((1, D))],
        out_shape=[jax.ShapeDtypeStruct((S, D), _F32), jax.ShapeDtypeStruct((D, KVD), _F32),
                   jax.ShapeDtypeStruct((1, KVD), _F32), jax.ShapeDtypeStruct((1, D), _F32)],
        compiler_params=_seq(),
    )(x, dx_in, g, wkv, cur_a, prev_a, cur_b, prev_b)


STACK = GROUP * BLK


def _attn_mask(i, rows):
    qi = lax.broadcasted_iota(jnp.int32, (rows, 2 * BLK), 0) & (BLK - 1)
    si = lax.broadcasted_iota(jnp.int32, (rows, 2 * BLK), 1)
    return (si > qi) & (si <= qi + BLK) & jnp.logical_or(i > 0, si >= BLK)


def _low_half():
    return lax.broadcasted_iota(jnp.int32, (BLK, PAIR), 1) < HEAD_DIM


def _stack_heads(ref, kh, dst):
    low = _low_half()
    for pp in range(GROUP // 2):
        pr = kh * (GROUP // 2) + pp
        v2 = ref[:, pr * PAIR:(pr + 1) * PAIR]
        zero = jnp.zeros_like(v2)
        dst[2 * pp * BLK:(2 * pp + 1) * BLK, :] = jnp.where(low, v2, zero)
        dst[(2 * pp + 1) * BLK:(2 * pp + 2) * BLK, :] = jnp.where(low, zero, v2)


def _unstack_heads(st, pp):
    return jnp.where(_low_half(), st[2 * pp * BLK:(2 * pp + 1) * BLK, :], st[(2 * pp + 1) * BLK:(2 * pp + 2) * BLK, :])


def _sink_col(sk_ref, kh):
    return jnp.concatenate([jnp.full((BLK, 1), sk_ref[kh * GROUP + h], _F32) for h in range(GROUP)], axis=0)


def _head_probs(qm, kd, mask, sink):
    s = jnp.where(mask, _dot(qm, kd, _NT) * (HEAD_DIM ** -0.5), NEG)
    m = jnp.maximum(jnp.max(s, axis=-1, keepdims=True), sink)
    p = jnp.exp(s - m)
    es = jnp.exp(sink - m)
    inv = 1.0 / (jnp.sum(p, axis=-1, keepdims=True) + es)
    return p * inv, es * inv


def _attn_fwd(x, g, wq, bq, sinks, kvd, wo, bo):
    S = x.shape[0]
    n = S // BLK

    def body(x_ref, g_ref, wq_ref, bq_ref, sk_ref, kp_ref, kc_ref, wo_ref, bo_ref, xo_ref, q_ref, o_ref, win):
        i = pl.program_id(0)
        xv = x_ref[...]
        hb = (_rms(xv)[0] * g_ref[...]).astype(_MXU)
        q_ref[...] = (_dot(hb, wq_ref[...]) + bq_ref[...]).astype(q_ref.dtype)
        win[0:BLK, :] = kp_ref[...]
        win[BLK:, :] = kc_ref[...]
        mask = _attn_mask(i, BLK)
        low = _low_half()
        for pr in range(N_HEADS // 2):
            kh = (2 * pr) // GROUP
            kd = win[:, kh * PAIR:(kh + 1) * PAIR]
            vd = win[:, (N_KV + kh) * PAIR:(N_KV + kh + 1) * PAIR]
            q2 = q_ref[:, pr * PAIR:(pr + 1) * PAIR]
            outs = []
            for half in range(2):
                qm = jnp.where(low if half == 0 else ~low, q2, jnp.zeros_like(q2))
                pbs, _ = _head_probs(qm, kd, mask, sk_ref[2 * pr + half])
                outs.append(_dot(pbs.astype(_MXU), vd))
            o_ref[:, pr * PAIR:(pr + 1) * PAIR] = jnp.where(low, outs[0], outs[1]).astype(o_ref.dtype)
        xo_ref[...] = xv + _dot(o_ref[...], wo_ref[...]) + bo_ref[...]

    blk = lambda w: pl.BlockSpec((BLK, w), lambda i: (i, 0))
    return pl.pallas_call(
        body, name="attn_fwd", grid=(n,),
        in_specs=[blk(D), _full((1, D)), _full((D, D)), _full((1, D)),
                  pl.BlockSpec(memory_space=pltpu.SMEM),
                  pl.BlockSpec((BLK, KVD), lambda i: (jnp.maximum(i - 1, 0), 0)), blk(KVD),
                  _full((D, D)), _full((1, D))],
        out_specs=[blk(D), blk(D), blk(D)],
        out_shape=[jax.ShapeDtypeStruct((S, D), _F32), jax.ShapeDtypeStruct((S, D), _MXU),
                   jax.ShapeDtypeStruct((S, D), _MXU)],
        scratch_shapes=[pltpu.VMEM((2 * BLK, KVD), _MXU)],
        compiler_params=_seq(),
    )(x, g, wq, bq, sinks, kvd, kvd, wo, bo)


def _attn_bwd(x, dy, q, g, wq, sinks, kvd, wo):
    S = x.shape[0]
    n = S // BLK

    def body(x_ref, dy_ref, q_ref, g_ref, wq_ref, sk_ref, kp_ref, kc_ref, wo_ref,
             dx_ref, dq_ref, h_ref, dc_ref, dp_ref, dbq_ref, dbo_ref, dg_ref, dsk_ref, win, dob, qs, dos):
        i = pl.program_id(0)

        @pl.when(i == 0)
        def _():
            dbq_ref[...] = jnp.zeros_like(dbq_ref)
            dbo_ref[...] = jnp.zeros_like(dbo_ref)
            dg_ref[...] = jnp.zeros_like(dg_ref)
            dsk_ref[...] = jnp.zeros_like(dsk_ref)

        gv = g_ref[...]
        xv = x_ref[...]
        xn, r = _rms(xv)
        h_ref[...] = (xn * gv).astype(h_ref.dtype)
        dyv = dy_ref[...]
        dbo_ref[...] += _colsum(dyv)
        dob[...] = _dot(dyv.astype(_MXU), wo_ref[...], _NT).astype(dob.dtype)
        win[0:BLK, :] = kp_ref[...]
        win[BLK:, :] = kc_ref[...]
        mask = _attn_mask(i, STACK)
        lane = lax.broadcasted_iota(jnp.int32, (1, LANES), 1)
        dq_all = []
        for kh in range(N_KV):
            ks = slice(kh * PAIR, (kh + 1) * PAIR)
            vs = slice((N_KV + kh) * PAIR, (N_KV + kh + 1) * PAIR)
            kd, vd = win[:, ks], win[:, vs]
            _stack_heads(q_ref, kh, qs)
            _stack_heads(dob, kh, dos)
            pbs, ps = _head_probs(qs[...], kd, mask, _sink_col(sk_ref, kh))
            dpr = _dot(dos[...], vd, _NT)
            delta = jnp.sum(pbs * dpr, axis=-1, keepdims=True)
            dsb = (pbs * (dpr - delta) * (HEAD_DIM ** -0.5)).astype(_MXU)
            dqst = _dot(dsb, kd)
            dk = _dot(dsb, qs[...], _TN)
            dv = _dot(pbs.astype(_MXU), dos[...], _TN)
            dp_ref[:, ks], dc_ref[:, ks] = dk[0:BLK, :], dk[BLK:, :]
            dp_ref[:, vs], dc_ref[:, vs] = dv[0:BLK, :], dv[BLK:, :]
            sd = ps * delta
            for hh in range(GROUP):
                dsk_ref[...] -= jnp.where(lane == kh * GROUP + hh, _colsum(sd[hh * BLK:(hh + 1) * BLK, :]), 0.0)
            dq_all += [_unstack_heads(dqst, pp) for pp in range(GROUP // 2)]
        dq = jnp.concatenate(dq_all, axis=1)
        dbq_ref[...] += _colsum(dq)
        dqb = dq.astype(_MXU)
        dq_ref[...] = dqb
        dh = _dot(dqb, wq_ref[...], _NT)
        dg_ref[...] += _colsum(dh * xn)
        dx_ref[...] = dyv + _rms_bwd(dh, xn, r, gv)

    blk = lambda w: pl.BlockSpec((BLK, w), lambda i: (i, 0))
    return pl.pallas_call(
        body, name="attn_bwd", grid=(n,),
        in_specs=[blk(D), blk(D), blk(D), _full((1, D)), _full((D, D)),
                  pl.BlockSpec(memory_space=pltpu.SMEM),
                  pl.BlockSpec((BLK, KVD), lambda i: (jnp.maximum(i - 1, 0), 0)), blk(KVD), _full((D, D))],
        out_specs=[blk(D), blk(D), blk(D), blk(KVD), blk(KVD),
                   _full((1, D)), _full((1, D)), _full((1, D)), _full((1, LANES))],
        out_shape=[jax.ShapeDtypeStruct((S, D), _F32), jax.ShapeDtypeStruct((S, D), _MXU),
                   jax.ShapeDtypeStruct((S, D), _MXU), jax.ShapeDtypeStruct((S, KVD), _F32),
                   jax.ShapeDtypeStruct((S, KVD), _F32), jax.ShapeDtypeStruct((1, D), _F32),
                   jax.ShapeDtypeStruct((1, D), _F32), jax.ShapeDtypeStruct((1, D), _F32),
                   jax.ShapeDtypeStruct((1, LANES), _F32)],
        scratch_shapes=[pltpu.VMEM((2 * BLK, KVD), _MXU), pltpu.VMEM((BLK, D), _MXU),
                        pltpu.VMEM((STACK, PAIR), _MXU), pltpu.VMEM((STACK, PAIR), _MXU)],
        compiler_params=_seq(),
    )(x, dy, q, g, wq, sinks, kvd, kvd, wo)


def _loss_bwd(x, g, tgt):
    S = x.shape[0]
    T = _tile(S, 512, 8)

    def body(x_ref, g_ref, t_ref, dx_ref, ls_ref, dg_ref):
        @pl.when(pl.program_id(0) == 0)
        def _():
            ls_ref[...] = jnp.zeros_like(ls_ref)
            dg_ref[...] = jnp.zeros_like(dg_ref)

        gv = g_ref[...]
        xn, r = _rms(x_ref[...])
        err = xn * gv - t_ref[...]
        ls_ref[...] += 0.5 * jnp.sum(jnp.mean(err * err, axis=-1, keepdims=True))
        dyv = err * (1.0 / D)
        dg_ref[...] += _colsum(dyv * xn)
        dx_ref[...] = _rms_bwd(dyv, xn, r, gv)

    return pl.pallas_call(
        body, name="loss_bwd", grid=(S // T,),
        in_specs=[pl.BlockSpec((T, D), lambda i: (i, 0)), _full((1, D)), pl.BlockSpec((T, D), lambda i: (i, 0))],
        out_specs=[pl.BlockSpec((T, D), lambda i: (i, 0)), _full((8, LANES)), _full((1, D))],
        out_shape=[jax.ShapeDtypeStruct((S, D), _F32), jax.ShapeDtypeStruct((8, LANES), _F32),
                   jax.ShapeDtypeStruct((1, D), _F32)],
        compiler_params=_seq(),
    )(x, g, tgt)


def _me():
    return 4 * lax.axis_index("x") + 2 * lax.axis_index("y") + lax.axis_index("c")


def _peer(j):
    x, y, c = lax.axis_index("x"), lax.axis_index("y"), lax.axis_index("c")
    px = 1 - x if j & 4 else x
    py = 1 - y if j & 2 else y
    pc = 1 - c if j & 1 else c
    return (px, py, pc), 4 * px + 2 * py + pc


def _exchange(name, srcs, out_shapes, plan):
    ns, no, nt = len(srcs), len(out_shapes), len(plan)

    def body(*refs):
        src_refs, out_refs = refs[:ns], refs[ns:ns + no]
        send_sems, recv_sems, local_sems = refs[ns + no:]
        me = _me()
        local, remote = [], []
        for t, (si, oi, src_of, dst_of) in enumerate(plan):
            src, out = src_refs[si], out_refs[oi]
            cp = pltpu.make_async_copy(src_of(src, me), dst_of(out, me), local_sems.at[t])
            cp.start()
            local.append(cp)
            for j in range(1, N_DEV):
                dev, pk = _peer(j)
                pltpu.make_async_remote_copy(
                    src_ref=src_of(src, pk), dst_ref=dst_of(out, me), send_sem=send_sems.at[t, j - 1],
                    recv_sem=recv_sems.at[t, j - 1], device_id=dev, device_id_type=pl.DeviceIdType.MESH).start()
                remote.append(pltpu.make_async_remote_copy(
                    src_ref=src_of(src, pk), dst_ref=dst_of(out, pk), send_sem=send_sems.at[t, j - 1],
                    recv_sem=recv_sems.at[t, j - 1], device_id=dev, device_id_type=pl.DeviceIdType.MESH))
        for cp in local:
            cp.wait()
        for cp in remote:
            cp.wait()

    any_ = pl.BlockSpec(memory_space=pl.ANY)
    return pl.pallas_call(
        body, name=name, in_specs=[any_] * ns, out_specs=[any_] * no, out_shape=out_shapes,
        scratch_shapes=[pltpu.SemaphoreType.DMA((nt, N_DEV - 1)), pltpu.SemaphoreType.DMA((nt, N_DEV - 1)),
                        pltpu.SemaphoreType.DMA((nt,))],
    )(*srcs)


def _rows(axis, size):
    def of(ref, b):
        start = b * size
        if size % 8 == 0:
            start = pl.multiple_of(start, 8)
        return ref.at[(slice(None),) * axis + (pl.ds(start, size),)]
    return of


def _all_gather(shards):
    srcs, outs, plan = [], [], []
    for si, (a, axis) in enumerate(shards):
        srcs.append(a)
        size = a.shape[1 + axis]
        full = a.shape[1:1 + axis] + (N_DEV * size,) + a.shape[2 + axis:]
        for l in range(a.shape[0]):
            plan.append((si, len(outs), (lambda l: lambda ref, b: ref.at[l])(l), _rows(axis, size)))
            outs.append(jax.ShapeDtypeStruct(full, a.dtype))
    res = list(_exchange("all_gather", srcs, outs, plan))
    out, k = [], 0
    for a, _ in shards:
        out.append(res[k:k + a.shape[0]])
        k += a.shape[0]
    return out


def _reduce_grads(groups):
    srcs, outs, plan = [], [], []
    for oi, (arrs, axis) in enumerate(groups):
        a = arrs[0]
        size = a.shape[axis] // N_DEV
        shard = a.shape[:axis] + (size,) + a.shape[axis + 1:]
        outs.append(jax.ShapeDtypeStruct((N_DEV, len(arrs)) + shard, a.dtype))
        for l, arr in enumerate(arrs):
            plan.append((len(srcs), oi, _rows(axis, size), (lambda l: lambda ref, b: ref.at[b, l])(l)))
            srcs.append(arr)
    return list(_exchange("reduce_grads", srcs, outs, plan))


def _all_reduce_small(p):
    R = p.shape[0]

    def body(p_ref, o_ref, land, send_sems, recv_sems):
        me = _me()
        land[me] = p_ref[...]
        waits = []
        for j in range(1, N_DEV):
            dev, pk = _peer(j)
            pltpu.make_async_remote_copy(
                src_ref=p_ref, dst_ref=land.at[me], send_sem=send_sems.at[j - 1], recv_sem=recv_sems.at[j - 1],
                device_id=dev, device_id_type=pl.DeviceIdType.MESH).start()
            waits.append(pltpu.make_async_remote_copy(
                src_ref=p_ref, dst_ref=land.at[pk], send_sem=send_sems.at[j - 1], recv_sem=recv_sems.at[j - 1],
                device_id=dev, device_id_type=pl.DeviceIdType.MESH))
        for cp in waits:
            cp.wait()
        tot = land[0]
        for b in range(1, N_DEV):
            tot = tot + land[b]
        o_ref[...] = tot

    vmem = pl.BlockSpec(memory_space=pltpu.VMEM)
    return pl.pallas_call(
        body, name="all_reduce_small", in_specs=[vmem], out_specs=vmem,
        out_shape=jax.ShapeDtypeStruct((R, LANES), _F32),
        scratch_shapes=[pltpu.VMEM((N_DEV, R, LANES), _F32), pltpu.SemaphoreType.DMA((N_DEV - 1,)),
                        pltpu.SemaphoreType.DMA((N_DEV - 1,))],
        compiler_params=_params(),
    )(p)


def _sum_landed(land):
    g = land[0].astype(_F32)
    for b in range(1, N_DEV):
        g = g + land[b].astype(_F32)
    return g


def _sum8(land):
    _, R, C = land.shape
    tr = _tile(R, 352, 16)

    def body(l_ref, o_ref):
        o_ref[...] = _sum_landed(l_ref)

    return pl.pallas_call(
        body, name="sum8", grid=(R // tr,),
        in_specs=[pl.BlockSpec((N_DEV, tr, C), lambda i: (0, i, 0))],
        out_specs=pl.BlockSpec((tr, C), lambda i: (i, 0)),
        out_shape=jax.ShapeDtypeStruct((R, C), _F32),
        compiler_params=_seq(),
    )(land)


def _adamw(g, w, m, v, name):
    landed = g.ndim == 3
    R, C = w.shape
    tr = _tile(R, 176 if landed else 256, 16)
    bc1 = 1.0 - ADAM_B1 ** ADAM_STEP
    bc2 = 1.0 - ADAM_B2 ** ADAM_STEP

    def body(g_ref, w_ref, m_ref, v_ref, *outs):
        gv = _sum_landed(g_ref) if landed else g_ref[...]
        if landed:
            outs[0][...] = gv
        d_ref, mo_ref, vo_ref = outs[-3:]
        mn = ADAM_B1 * m_ref[...] + (1.0 - ADAM_B1) * gv
        vn = ADAM_B2 * v_ref[...] + (1.0 - ADAM_B2) * (gv * gv)
        mo_ref[...] = mn
        vo_ref[...] = vn
        d_ref[...] = -ADAM_LR * ((mn / bc1) / (jnp.sqrt(vn / bc2) + ADAM_EPS) + ADAM_WD * w_ref[...])

    row = pl.BlockSpec((tr, C), lambda i: (i, 0))
    gspec = pl.BlockSpec((N_DEV, tr, C), lambda i: (0, i, 0)) if landed else row
    n_out = 4 if landed else 3
    return pl.pallas_call(
        body, name=name, grid=(R // tr,), in_specs=[gspec, row, row, row], out_specs=[row] * n_out,
        out_shape=[jax.ShapeDtypeStruct((R, C), _F32)] * n_out, compiler_params=_seq(),
    )(g, w, m, v)


def _pack(parts):
    flat = jnp.concatenate([p.reshape(-1).astype(_F32) for p in parts])
    n = flat.shape[0]
    rows = -(-n // (8 * LANES)) * 8
    return jnp.pad(flat, (0, rows * LANES - n)).reshape(rows, LANES)


def _unpack(packed, shapes):
    flat, out, k = packed.reshape(-1), [], 0
    for s in shapes:
        n = 1
        for d in s:
            n *= d
        out.append(flat[k:k + n].reshape(s))
        k += n
    return out


def kernel(x, norm1_g, norm2_g, pool_w, pool_scale, kv_norm_g, w_kv, b_kv, w_q, b_q, sinks, w_o, b_o, ffn_up, ffn_conv_w, ffn_conv_b, ffn_down, final_g, loss_target, m_norm1_g, m_norm2_g, m_pool_w, m_pool_scale, m_kv_norm_g, m_w_kv, m_b_kv, m_w_q, m_b_q, m_sinks, m_w_o, m_b_o, m_ffn_up, m_ffn_conv_w, m_ffn_conv_b, m_ffn_down, m_final_g, v_norm1_g, v_norm2_g, v_pool_w, v_pool_scale, v_kv_norm_g, v_w_kv, v_b_kv, v_w_q, v_b_q, v_sinks, v_w_o, v_b_o, v_ffn_up, v_ffn_conv_w, v_ffn_conv_b, v_ffn_down, v_final_g):
    S = x.shape[1]
    F2s = ffn_up.shape[2]
    F2 = N_DEV * F2s
    me = _me()
    x0 = x.reshape(S, D)
    tgt = loss_target.reshape(S, D)
    row = lambda a: a.reshape(1, -1)

    small = _pack([pool_scale, ffn_conv_w])
    (up_t, down, wq, wo, (wkv,), pw, (small_all,)) = _all_gather([
        (jnp.swapaxes(ffn_up.astype(_MXU), 1, 2), 0),
        (ffn_down.astype(_MXU), 0),
        (w_q.astype(_MXU), 0),
        (w_o.astype(_MXU), 0),
        (w_kv.astype(_MXU)[None], 0),
        (pool_w.astype(_MXU), 1),
        (small[None, None], 0),
    ])
    n_ps = pool_scale.size
    small_all = small_all.reshape(N_DEV, -1)
    pscale = jnp.transpose(small_all[:, :n_ps].reshape(N_DEV, N_A, D // N_DEV), (1, 0, 2)).reshape(N_A, D)
    conv_w = jnp.transpose(small_all[:, n_ps:n_ps + ffn_conv_w.size].reshape(N_DEV, DEPTH, CONV_W, F2s),
                           (1, 2, 0, 3)).reshape(DEPTH, CONV_W, F2)

    def dup(a):
        a4 = a.reshape(a.shape[:-1] + (2 * N_KV, 1, HEAD_DIM))
        return jnp.broadcast_to(a4, a.shape[:-1] + (2 * N_KV, 2, HEAD_DIM)).reshape(a.shape[:-1] + (KVD,))

    def fold(a):
        return a.reshape(a.shape[:-1] + (2 * N_KV, 2, HEAD_DIM)).sum(axis=-2).reshape(a.shape[:-1] + (2 * N_KV * HEAD_DIM,))

    wkv_d, bkv_d = dup(wkv), dup(row(b_kv))

    xs, us, qs, os_ = [x0], [], [], []
    xc = x0
    kvd = None
    for l in range(DEPTH):
        if l < N_A:
            xc = _pool_fwd(xc, row(norm1_g[l]), pw[l], row(pscale[l]))
        else:
            j = l - N_A
            xc, q, o = _attn_fwd(xc, row(norm1_g[l]), wq[j], row(b_q[j]), sinks[j], kvd, wo[j], row(b_o[j]))
            qs.append(q)
            os_.append(o)
        xs.append(xc)
        xc, u, c = _ffn_fwd(xc, row(norm2_g[l]), up_t[l], conv_w[l], row(ffn_conv_b[l]), down[l])
        us.append((u, c))
        xs.append(xc)
        if l == N_A - 1:
            kvd = _kv_fwd(xc, row(kv_norm_g), wkv_d, bkv_d)

    dx, loss_p, d_final = _loss_bwd(xc, row(final_g), tgt)
    d_n1, d_n2, d_cw, d_cb = [None] * DEPTH, [None] * DEPTH, [None] * DEPTH, [None] * DEPTH
    g_up, g_down, g_wq, g_wo, g_pw = [None] * DEPTH, [None] * DEPTH, [None] * 2, [None] * 2, [None] * N_A
    d_bq, d_bo, d_sk, d_ps, dkv_parts = [None] * 2, [None] * 2, [None] * 2, [None] * N_A, []
    for l in reversed(range(DEPTH)):
        x_in, x_mid, x_out = xs[2 * l], xs[2 * l + 1], xs[2 * l + 2]
        if l == N_A - 1:
            dx, d_wkv, d_bkv, d_kvg = _kv_bwd(x_out, dx, row(kv_norm_g), wkv_d, *dkv_parts)
        dy = dx
        dx, du, a, h, d_cw[l], d_cb[l], d_n2[l] = _ffn_bwd(
            x_mid, dy, *us[l], row(norm2_g[l]), up_t[l], conv_w[l], down[l])
        g_up[l] = _tn_matmul(du, h, "tn_up")
        g_down[l] = _tn_matmul(a, dy, "tn_down")
        dy = dx
        if l < N_A:
            dx, d_pw, d_ps[l], d_n1[l] = _pool_bwd(x_in, dy, row(norm1_g[l]), pw[l], row(pscale[l]))
            g_pw[l] = d_pw.astype(_MXU)
        else:
            j = l - N_A
            dx, dq, h, d_cur, d_prev, d_bq[j], d_bo[j], d_n1[l], d_sk[j] = _attn_bwd(
                x_in, dy, qs[j], row(norm1_g[l]), wq[j], sinks[j], kvd, wo[j])
            g_wq[j] = _tn_matmul(h, dq, "tn_q")
            g_wo[j] = _tn_matmul(os_[j], dy, "tn_o")
            dkv_parts += [d_cur, d_prev]

    land_up, land_down, land_wq, land_wo, land_wkv, land_pw = _reduce_grads([
        (g_up, 0), (g_down, 0), (g_wq, 0), (g_wo, 0), ([fold(d_wkv).astype(_MXU)], 0), (g_pw, 1)])
    rep_names = ["norm1_g", "norm2_g", "kv_norm_g", "b_kv", "b_q", "sinks", "b_o", "ffn_conv_b", "final_g"]
    rep = [jnp.concatenate(d_n1), jnp.concatenate(d_n2), d_kvg, fold(d_bkv), jnp.concatenate(d_bq),
           jnp.concatenate([s[:, :N_HEADS] for s in d_sk]), jnp.concatenate(d_bo), jnp.concatenate(d_cb), d_final]
    full = [jnp.concatenate(d_ps), jnp.stack(d_cw), loss_p[0:1, 0:1]]
    given = dict(norm1_g=norm1_g, norm2_g=norm2_g, kv_norm_g=kv_norm_g, b_kv=b_kv, b_q=b_q, sinks=sinks, b_o=b_o,
                 ffn_conv_b=ffn_conv_b, final_g=final_g, pool_scale=pool_scale, ffn_conv_w=ffn_conv_w)
    tot = _unpack(_all_reduce_small(_pack(rep + full)),
                  [given[k].shape for k in rep_names] + [(N_A, D), (DEPTH, CONV_W, F2), ()])
    grad = dict(zip(rep_names, tot))
    loss = tot[-1]
    grad["pool_scale"] = lax.dynamic_slice_in_dim(tot[-3], me * (D // N_DEV), D // N_DEV, axis=1)
    grad["ffn_conv_w"] = lax.dynamic_slice_in_dim(tot[-2], me * F2s, F2s, axis=2)

    moms = dict(norm1_g=(m_norm1_g, v_norm1_g), norm2_g=(m_norm2_g, v_norm2_g), pool_w=(m_pool_w, v_pool_w),
                pool_scale=(m_pool_scale, v_pool_scale), kv_norm_g=(m_kv_norm_g, v_kv_norm_g), w_kv=(m_w_kv, v_w_kv),
                b_kv=(m_b_kv, v_b_kv), w_q=(m_w_q, v_w_q), b_q=(m_b_q, v_b_q), sinks=(m_sinks, v_sinks),
                w_o=(m_w_o, v_w_o), b_o=(m_b_o, v_b_o), ffn_up=(m_ffn_up, v_ffn_up),
                ffn_conv_w=(m_ffn_conv_w, v_ffn_conv_w), ffn_conv_b=(m_ffn_conv_b, v_ffn_conv_b),
                ffn_down=(m_ffn_down, v_ffn_down), final_g=(m_final_g, v_final_g))
    given.update(pool_w=pool_w, w_kv=w_kv, w_q=w_q, w_o=w_o, ffn_up=ffn_up, ffn_down=ffn_down)
    delta, new_m, new_v = {}, {}, {}

    small_names = rep_names + ["pool_scale", "ffn_conv_w"]
    shapes = [given[k].shape for k in small_names]
    outs = _adamw(_pack([grad[k] for k in small_names]), _pack([given[k] for k in small_names]),
                  _pack([moms[k][0] for k in small_names]), _pack([moms[k][1] for k in small_names]), "adamw_small")
    for dst, packed in zip((delta, new_m, new_v), outs):
        dst.update(zip(small_names, _unpack(packed, shapes)))

    def update(name, g, cols):
        w = given[name]
        two_d = lambda a: a.reshape(-1, cols)
        outs = _adamw(g, two_d(w), two_d(moms[name][0]), two_d(moms[name][1]), "adamw_" + name)
        if g.ndim == 3:
            grad[name] = outs[0].reshape(w.shape)
        delta[name], new_m[name], new_v[name] = (o.reshape(w.shape) for o in outs[-3:])

    g_t = _sum8(land_up.reshape(N_DEV, -1, D)).reshape(DEPTH, F2s, D)
    grad["ffn_up"] = jnp.swapaxes(g_t, 1, 2)
    update("ffn_up", grad["ffn_up"].reshape(-1, F2s), F2s)
    update("ffn_down", land_down.reshape(N_DEV, -1, D), D)
    update("w_q", land_wq.reshape(N_DEV, -1, D), D)
    update("w_o", land_wo.reshape(N_DEV, -1, D), D)
    update("w_kv", land_wkv.reshape(N_DEV, -1, w_kv.shape[1]), w_kv.shape[1])
    update("pool_w", land_pw.reshape(N_DEV, -1, GC), GC)

    names = ["norm1_g", "norm2_g", "pool_w", "pool_scale", "kv_norm_g", "w_kv", "b_kv", "w_q", "b_q", "sinks", "w_o",
             "b_o", "ffn_up", "ffn_conv_w", "ffn_conv_b", "ffn_down", "final_g"]
    return (loss, dx.reshape(x.shape), *[grad[k] for k in names], *[delta[k] for k in names],
            *[new_m[k] for k in names], *[new_v[k] for k in names])
```

```python
import functools

import jax
import jax.numpy as jnp
from jax import lax
from jax.experimental import pallas as pl
from jax.experimental.pallas import tpu as pltpu

_F32 = jnp.float32
_MXU = jnp.bfloat16

N_DEV = 8
D = 1024
DEPTH = 4
N_A = 2
POOL_WINDOWS = (2, 4, 8, 16)
GC = D // len(POOL_WINDOWS)
HALO = 16
HEAD_DIM = 64
N_HEADS = D // HEAD_DIM
GROUP = 8
N_KV = N_HEADS // GROUP
BLK = 128
PAIR = 2 * HEAD_DIM
KVD = 4 * N_KV * HEAD_DIM
CONV_W = 3
EPS = 1e-5
NEG = -1e30

ADAM_LR = 0.001
ADAM_B1 = 0.9
ADAM_B2 = 0.999
ADAM_EPS = 1e-08
ADAM_WD = 0.01
ADAM_STEP = 10

V7X_VMEM_LIMIT = 56 * 1024 * 1024
LANES = 128

_NT = (((1,), (1,)), ((), ()))
_TN = (((0,), (0,)), ((), ()))


def _params(**kw):
    return pltpu.CompilerParams(vmem_limit_bytes=V7X_VMEM_LIMIT, **kw)


def _seq(n=1):
    return _params(dimension_semantics=("arbitrary",) * n)


def _dot(a, b, dims=None):
    if dims is None:
        return jnp.dot(a, b, preferred_element_type=_F32)
    return lax.dot_general(a, b, dims, preferred_element_type=_F32)


def _rms(x):
    r = lax.rsqrt(jnp.mean(x * x, axis=-1, keepdims=True) + EPS)
    return x * r, r


def _rms_bwd(dh, xn, r, g):
    dxn = dh * g
    return r * (dxn - xn * jnp.mean(dxn * xn, axis=-1, keepdims=True))


def _colsum(a):
    return jnp.sum(a, axis=0, keepdims=True)


def _tile(n, want, mult=8):
    for t in range(min(want, n), 0, -1):
        if n % t == 0 and t % mult == 0:
            return t
    return n


def _full(shape):
    zeros = (0,) * len(shape)
    return pl.BlockSpec(shape, lambda *_: zeros)


def _pool_windows(hbuf, h, row, T):
    out = []
    for gi, win in enumerate(POOL_WINDOWS):
        cs = slice(gi * GC, (gi + 1) * GC)
        acc = hbuf[HALO:HALO + T, cs]
        for k in range(1, win):
            acc = acc + hbuf[HALO - k:HALO - k + T, cs]
        cnt = jnp.minimum(row + 1, win).astype(_F32)
        out.append((acc / cnt - h[:, cs], cnt))
    return out


def _pool_fwd(x, g, w, sc):
    S = x.shape[0]
    T = _tile(S, 512, HALO)
    n, hb = S // T, T // HALO

    def body(x_ref, xh_ref, g_ref, w_ref, sc_ref, o_ref, hbuf):
        i = pl.program_id(0)
        gv = g_ref[...]
        xv = x_ref[...]
        h = _rms(xv)[0] * gv
        hbuf[0:HALO, :] = jnp.where(i > 0, _rms(xh_ref[...])[0] * gv, 0.0)
        hbuf[HALO:, :] = h
        row = i * T + lax.broadcasted_iota(jnp.int32, (T, 1), 0)
        for gi, (p, _) in enumerate(_pool_windows(hbuf, h, row, T)):
            cs = slice(gi * GC, (gi + 1) * GC)
            z = _dot(p.astype(_MXU), w_ref[gi])
            o_ref[:, cs] = xv[:, cs] + z * sc_ref[:, cs]

    return pl.pallas_call(
        body, name="pool_fwd", grid=(n,),
        in_specs=[pl.BlockSpec((T, D), lambda i: (i, 0)),
                  pl.BlockSpec((HALO, D), lambda i: (jnp.maximum(i * hb - 1, 0), 0)),
                  _full((1, D)), _full((4, GC, GC)), _full((1, D))],
        out_specs=pl.BlockSpec((T, D), lambda i: (i, 0)),
        out_shape=jax.ShapeDtypeStruct((S, D), _F32),
        scratch_shapes=[pltpu.VMEM((T + HALO, D), _F32)],
        compiler_params=_seq(),
    )(x, x, g, w, sc)


def _pool_bwd(x, dy, g, w, sc):
    S = x.shape[0]
    T = _tile(S, 512, HALO)
    n, hb = S // T, T // HALO

    def body(x_ref, xh_ref, dy_ref, dyh_ref, g_ref, w_ref, sc_ref, dx_ref, dw_ref, dsc_ref, dg_ref,
             hbuf, qbuf, dhbuf):
        i = pl.program_id(0)

        @pl.when(i == 0)
        def _():
            dw_ref[...] = jnp.zeros_like(dw_ref)
            dsc_ref[...] = jnp.zeros_like(dsc_ref)
            dg_ref[...] = jnp.zeros_like(dg_ref)

        gv = g_ref[...]
        xv = x_ref[...]
        xn, r = _rms(xv)
        h = xn * gv
        hbuf[0:HALO, :] = jnp.where(i > 0, _rms(xh_ref[...])[0] * gv, 0.0)
        hbuf[HALO:, :] = h
        dyv = dy_ref[...]
        dz = dyv * sc_ref[...]
        dzh = jnp.where(i < n - 1, dyh_ref[...], 0.0) * sc_ref[...]
        row = i * T + lax.broadcasted_iota(jnp.int32, (T, 1), 0)
        rowh = (i + 1) * T + lax.broadcasted_iota(jnp.int32, (HALO, 1), 0)
        for gi, (p, cnt) in enumerate(_pool_windows(hbuf, h, row, T)):
            win = POOL_WINDOWS[gi]
            cs = slice(gi * GC, (gi + 1) * GC)
            pb = p.astype(_MXU)
            wg = w_ref[gi]
            dsc_ref[:, cs] += _colsum(dyv[:, cs] * _dot(pb, wg))
            dzb = dz[:, cs].astype(_MXU)
            dw_ref[gi] += _dot(pb, dzb, _TN)
            dp = _dot(dzb, wg, _NT)
            dph = _dot(dzh[:, cs].astype(_MXU), wg, _NT)
            qbuf[0:T, cs] = dp / cnt
            qbuf[T:T + HALO, cs] = dph / jnp.minimum(rowh + 1, win).astype(_F32)
            acc = qbuf[0:T, cs]
            for k in range(1, win):
                acc = acc + qbuf[k:k + T, cs]
            dhbuf[:, cs] = acc - dp
        dh = dhbuf[...]
        dg_ref[...] += _colsum(dh * xn)
        dx_ref[...] = dyv + _rms_bwd(dh, xn, r, gv)

    return pl.pallas_call(
        body, name="pool_bwd", grid=(n,),
        in_specs=[pl.BlockSpec((T, D), lambda i: (i, 0)),
                  pl.BlockSpec((HALO, D), lambda i: (jnp.maximum(i * hb - 1, 0), 0)),
                  pl.BlockSpec((T, D), lambda i: (i, 0)),
                  pl.BlockSpec((HALO, D), lambda i: (jnp.minimum((i + 1) * hb, S // HALO - 1), 0)),
                  _full((1, D)), _full((4, GC, GC)), _full((1, D))],
        out_specs=[pl.BlockSpec((T, D), lambda i: (i, 0)), _full((4, GC, GC)), _full((1, D)), _full((1, D))],
        out_shape=[jax.ShapeDtypeStruct((S, D), _F32), jax.ShapeDtypeStruct((4, GC, GC), _F32),
                   jax.ShapeDtypeStruct((1, D), _F32), jax.ShapeDtypeStruct((1, D), _F32)],
        scratch_shapes=[pltpu.VMEM((T + HALO, D), _F32), pltpu.VMEM((T + HALO, D), _F32), pltpu.VMEM((T, D), _F32)],
        compiler_params=_seq(),
    )(x, x, dy, dy, g, w, sc)


FFN_FWD_TILE, FFN_FWD_CHUNKS = 256, 2
FFN_BWD_TILE, FFN_BWD_CHUNKS = 128, 2
EDGE = 8


def _shift_down(v, k, prev):
    r = pltpu.roll(v, k, axis=0)
    i8 = lax.broadcasted_iota(jnp.int32, (EDGE, v.shape[1]), 0)
    head = jnp.where(i8 >= k, r[0:EDGE, :], pltpu.roll(prev, k, axis=0))
    return jnp.concatenate([head, r[EDGE:, :]], axis=0)


def _shift_up(v, k, nxt):
    T = v.shape[0]
    r = pltpu.roll(v, T - k, axis=0)
    i8 = lax.broadcasted_iota(jnp.int32, (EDGE, v.shape[1]), 0)
    tail = jnp.where(i8 < EDGE - k, r[T - EDGE:, :], pltpu.roll(nxt, EDGE - k, axis=0))
    return jnp.concatenate([r[:T - EDGE, :], tail], axis=0)


def _load_weights(i, pairs, sems):
    @pl.when(i == 0)
    def _():
        cps = [pltpu.make_async_copy(src, dst, sems.at[k]) for k, (src, dst) in enumerate(pairs)]
        for cp in cps:
            cp.start()
        for cp in cps:
            cp.wait()


def _ffn_fwd(x, g, wup_t, cw, cb, wdn):
    S = x.shape[0]
    F2 = wup_t.shape[0]
    F = F2 // 2
    C = F // FFN_FWD_CHUNKS
    T = _tile(S, FFN_FWD_TILE, 16)
    n = S // T

    def body(x_ref, g_ref, wup_hbm, cw_ref, cb_ref, wdn_hbm, o_ref, u_ref, c_ref, wup, wdnv, carry, sems):
        i = pl.program_id(0)
        _load_weights(i, [(wup_hbm, wup), (wdn_hbm, wdnv)], sems)

        @pl.when(i == 0)
        def _():
            carry[...] = jnp.zeros_like(carry)

        xv = x_ref[...]
        hb = (_rms(xv)[0] * g_ref[...]).astype(_MXU)
        acc = jnp.zeros((T, D), _F32)
        for j in range(FFN_FWD_CHUNKS):
            halves = []
            for cs in (slice(j * C, (j + 1) * C), slice(F + j * C, F + (j + 1) * C)):
                u = _dot(hb, wup[cs, :], _NT)
                u_ref[:, cs] = u.astype(u_ref.dtype)
                prev = carry[:, cs]
                carry[:, cs] = u[T - EDGE:, :]
                c = (cw_ref[0:1, cs] * _shift_down(u, 2, prev) + cw_ref[1:2, cs] * _shift_down(u, 1, prev)
                     + cw_ref[2:3, cs] * u + cb_ref[:, cs])
                c_ref[:, cs] = c.astype(c_ref.dtype)
                halves.append(c)
            cg, cv = halves
            a = (cg * jax.nn.sigmoid(cg)) * cv
            acc = acc + _dot(a.astype(_MXU), wdnv[j * C:(j + 1) * C, :])
        o_ref[...] = xv + acc

    any_ = pl.BlockSpec(memory_space=pl.ANY)
    wide = pl.BlockSpec((T, F2), lambda i: (i, 0))
    return pl.pallas_call(
        body, name="ffn_fwd", grid=(n,),
        in_specs=[pl.BlockSpec((T, D), lambda i: (i, 0)), _full((1, D)), any_, _full((CONV_W, F2)), _full((1, F2)), any_],
        out_specs=[pl.BlockSpec((T, D), lambda i: (i, 0)), wide, wide],
        out_shape=[jax.ShapeDtypeStruct((S, D), _F32), jax.ShapeDtypeStruct((S, F2), _MXU),
                   jax.ShapeDtypeStruct((S, F2), _MXU)],
        scratch_shapes=[pltpu.VMEM((F2, D), _MXU), pltpu.VMEM((F, D), _MXU),
                        pltpu.VMEM((EDGE, F2), _F32), pltpu.SemaphoreType.DMA((2,))],
        compiler_params=_seq(),
    )(x, g, wup_t, cw, cb, wdn)


def _ffn_bwd(x, dy, u, c, g, wup_t, cw, wdn):
    S = x.shape[0]
    F2 = wup_t.shape[0]
    F = F2 // 2
    C = F // FFN_BWD_CHUNKS
    T = _tile(S, FFN_BWD_TILE, 16)
    n = S // T

    def body(x_ref, dy_ref, u_ref, c_ref, g_ref, wup_hbm, cw_ref, wdn_hbm,
             dx_ref, du_ref, a_ref, h_ref, dcw_ref, dcb_ref, dg_ref, wup, wdnv, carry, sems):
        i = pl.program_id(0)
        _load_weights(i, [(wup_hbm, wup), (wdn_hbm, wdnv)], sems)

        @pl.when(i == 0)
        def _():
            carry[...] = jnp.zeros_like(carry)
            dcw_ref[...] = jnp.zeros_like(dcw_ref)
            dcb_ref[...] = jnp.zeros_like(dcb_ref)
            dg_ref[...] = jnp.zeros_like(dg_ref)

        gv = g_ref[...]
        xv = x_ref[...]
        xn, r = _rms(xv)
        hbf = (xn * gv).astype(_MXU)
        h_ref[...] = hbf
        dyv = dy_ref[...]
        dyb = dyv.astype(_MXU)
        dh = jnp.zeros((T, D), _F32)
        for j in range(FFN_BWD_CHUNKS):
            gs, vs = slice(j * C, (j + 1) * C), slice(F + j * C, F + (j + 1) * C)
            cg, cv = c_ref[:, gs].astype(_F32), c_ref[:, vs].astype(_F32)
            sg = jax.nn.sigmoid(cg)
            sl = cg * sg
            a_ref[:, gs] = (sl * cv).astype(a_ref.dtype)
            da = _dot(dyb, wdnv[gs, :], _NT)
            for cs, dc in ((gs, da * cv * (sg * (1.0 + cg * (1.0 - sg)))), (vs, da * sl)):
                nxt = carry[:, cs]
                carry[:, cs] = dc[0:EDGE, :]
                dc1, dc2 = _shift_up(dc, 1, nxt), _shift_up(dc, 2, nxt)
                uf = u_ref[:, cs].astype(_F32)
                dcb_ref[:, cs] += _colsum(dc)
                for k, d in enumerate((dc2, dc1, dc)):
                    dcw_ref[k:k + 1, cs] += _colsum(d * uf)
                du = cw_ref[2:3, cs] * dc + cw_ref[1:2, cs] * dc1 + cw_ref[0:1, cs] * dc2
                dub = du.astype(_MXU)
                du_ref[:, cs] = dub
                dh = dh + _dot(dub, wup[cs, :])
        dg_ref[...] += _colsum(dh * xn)
        dx_ref[...] = dyv + _rms_bwd(dh, xn, r, gv)

    any_ = pl.BlockSpec(memory_space=pl.ANY)
    rev = lambda i: (n - 1 - i, 0)
    return pl.pallas_call(
        body, name="ffn_bwd", grid=(n,),
        in_specs=[pl.BlockSpec((T, D), rev), pl.BlockSpec((T, D), rev), pl.BlockSpec((T, F2), rev),
                  pl.BlockSpec((T, F2), rev), _full((1, D)), any_, _full((CONV_W, F2)), any_],
        out_specs=[pl.BlockSpec((T, D), rev), pl.BlockSpec((T, F2), rev), pl.BlockSpec((T, F), rev),
                   pl.BlockSpec((T, D), rev), _full((CONV_W, F2)), _full((1, F2)), _full((1, D))],
        out_shape=[jax.ShapeDtypeStruct((S, D), _F32), jax.ShapeDtypeStruct((S, F2), _MXU),
                   jax.ShapeDtypeStruct((S, F), _MXU), jax.ShapeDtypeStruct((S, D), _MXU),
                   jax.ShapeDtypeStruct((CONV_W, F2), _F32), jax.ShapeDtypeStruct((1, F2), _F32),
                   jax.ShapeDtypeStruct((1, D), _F32)],
        scratch_shapes=[pltpu.VMEM((F2, D), _MXU), pltpu.VMEM((F, D), _MXU),
                        pltpu.VMEM((EDGE, F2), _F32), pltpu.SemaphoreType.DMA((2,))],
        compiler_params=_seq(),
    )(x, dy, u, c, g, wup_t, cw, wdn)


def _tn_matmul(a, b, name):
    S, M = a.shape
    N = b.shape[1]
    bm = _tile(M, 1408, LANES)
    tk = _tile(S, 512, 16)
    nk = S // tk

    def body(a_ref, b_ref, o_ref, acc):
        k = pl.program_id(1)

        @pl.when(k == 0)
        def _():
            acc[...] = jnp.zeros_like(acc)

        acc[...] += _dot(a_ref[...].astype(_MXU), b_ref[...].astype(_MXU), _TN)

        @pl.when(k == nk - 1)
        def _():
            o_ref[...] = acc[...].astype(o_ref.dtype)

    return pl.pallas_call(
        body, name=name, grid=(M // bm, nk),
        in_specs=[pl.BlockSpec((tk, bm), lambda i, k: (k, i)), pl.BlockSpec((tk, N), lambda i, k: (k, 0))],
        out_specs=pl.BlockSpec((bm, N), lambda i, k: (i, 0)),
        out_shape=jax.ShapeDtypeStruct((M, N), _MXU),
        scratch_shapes=[pltpu.VMEM((bm, N), _F32)],
        compiler_params=_seq(2),
    )(a, b)


def _kv_fwd(x, g, wkv, bkv):
    S = x.shape[0]
    T = _tile(S, 512, 16)

    def body(x_ref, g_ref, w_ref, b_ref, o_ref):
        hb = (_rms(x_ref[...])[0] * g_ref[...]).astype(_MXU)
        o_ref[...] = (_dot(hb, w_ref[...]) + b_ref[...]).astype(o_ref.dtype)

    return pl.pallas_call(
        body, name="kv_fwd", grid=(S // T,),
        in_specs=[pl.BlockSpec((T, D), lambda i: (i, 0)), _full((1, D)), _full((D, KVD)), _full((1, KVD))],
        out_specs=pl.BlockSpec((T, KVD), lambda i: (i, 0)),
        out_shape=jax.ShapeDtypeStruct((S, KVD), _MXU),
        compiler_params=_seq(),
    )(x, g, wkv, bkv)


def _kv_bwd(x, dx_in, g, wkv, cur_a, prev_a, cur_b, prev_b):
    S = x.shape[0]
    n = S // BLK

    def body(x_ref, dxi_ref, g_ref, w_ref, ca, pa, cb, pb, dx_ref, dw_ref, db_ref, dg_ref):
        i = pl.program_id(0)

        @pl.when(i == 0)
        def _():
            dw_ref[...] = jnp.zeros_like(dw_ref)
            db_ref[...] = jnp.zeros_like(db_ref)
            dg_ref[...] = jnp.zeros_like(dg_ref)

        gv = g_ref[...]
        xn, r = _rms(x_ref[...])
        dkv = ca[...] + cb[...] + jnp.where(i < n - 1, pa[...] + pb[...], 0.0)
        db_ref[...] += _colsum(dkv)
        dkb = dkv.astype(_MXU)
        dw_ref[...] += _dot((xn * gv).astype(_MXU), dkb, _TN)
        dh = _dot(dkb, w_ref[...], _NT)
        dg_ref[...] += _colsum(dh * xn)
        dx_ref[...] = dxi_ref[...] + _rms_bwd(dh, xn, r, gv)

    blk = lambda w: pl.BlockSpec((BLK, w), lambda i: (i, 0))
    nxt = pl.BlockSpec((BLK, KVD), lambda i: (jnp.minimum(i + 1, n - 1), 0))
    return pl.pallas_call(
        body, name="kv_bwd", grid=(n,),
        in_specs=[blk(D), blk(D), _full((1, D)), _full((D, KVD)), blk(KVD), nxt, blk(KVD), nxt],
        out_specs=[blk(D), _full((D, KVD)), _full((1, KVD)), _full((1, D))],
        out_shape=[jax.ShapeDtypeStruct((S, D), _F32), jax.ShapeDtypeStruct((D, KVD), _F32),
                   jax.ShapeDtypeStruct((1, KVD), _F32), jax.ShapeDtypeStruct((1, D), _F32)],
        compiler_params=_seq(),
    )(x, dx_in, g, wkv, cur_a, prev_a, cur_b, prev_b)


STACK = GROUP * BLK


def _attn_mask(i, rows):
    qi = lax.broadcasted_iota(jnp.int32, (rows, 2 * BLK), 0) & (BLK - 1)
    si = lax.broadcasted_iota(jnp.int32, (rows, 2 * BLK), 1)
    return (si > qi) & (si <= qi + BLK) & jnp.logical_or(i > 0, si >= BLK)


def _low_half():
    return lax.broadcasted_iota(jnp.int32, (BLK, PAIR), 1) < HEAD_DIM


def _stack_heads(ref, kh, dst):
    low = _low_half()
    for pp in range(GROUP // 2):
        pr = kh * (GROUP // 2) + pp
        v2 = ref[:, pr * PAIR:(pr + 1) * PAIR]
        zero = jnp.zeros_like(v2)
        dst[2 * pp * BLK:(2 * pp + 1) * BLK, :] = jnp.where(low, v2, zero)
        dst[(2 * pp + 1) * BLK:(2 * pp + 2) * BLK, :] = jnp.where(low, zero, v2)


def _unstack_heads(st, pp):
    return jnp.where(_low_half(), st[2 * pp * BLK:(2 * pp + 1) * BLK, :], st[(2 * pp + 1) * BLK:(2 * pp + 2) * BLK, :])


def _sink_col(sk_ref, kh):
    return jnp.concatenate([jnp.full((BLK, 1), sk_ref[kh * GROUP + h], _F32) for h in range(GROUP)], axis=0)


def _head_probs(qm, kd, mask, sink):
    s = jnp.where(mask, _dot(qm, kd, _NT) * (HEAD_DIM ** -0.5), NEG)
    m = jnp.maximum(jnp.max(s, axis=-1, keepdims=True), sink)
    p = jnp.exp(s - m)
    es = jnp.exp(sink - m)
    inv = 1.0 / (jnp.sum(p, axis=-1, keepdims=True) + es)
    return p * inv, es * inv


def _attn_fwd(x, g, wq, bq, sinks, kvd, wo, bo):
    S = x.shape[0]
    n = S // BLK

    def body(x_ref, g_ref, wq_ref, bq_ref, sk_ref, kp_ref, kc_ref, wo_ref, bo_ref, xo_ref, q_ref, o_ref, win):
        i = pl.program_id(0)
        xv = x_ref[...]
        hb = (_rms(xv)[0] * g_ref[...]).astype(_MXU)
        q_ref[...] = (_dot(hb, wq_ref[...]) + bq_ref[...]).astype(q_ref.dtype)
        win[0:BLK, :] = kp_ref[...]
        win[BLK:, :] = kc_ref[...]
        mask = _attn_mask(i, BLK)
        low = _low_half()
        for pr in range(N_HEADS // 2):
            kh = (2 * pr) // GROUP
            kd = win[:, kh * PAIR:(kh + 1) * PAIR]
            vd = win[:, (N_KV + kh) * PAIR:(N_KV + kh + 1) * PAIR]
            q2 = q_ref[:, pr * PAIR:(pr + 1) * PAIR]
            outs = []
            for half in range(2):
                qm = jnp.where(low if half == 0 else ~low, q2, jnp.zeros_like(q2))
                pbs, _ = _head_probs(qm, kd, mask, sk_ref[2 * pr + half])
                outs.append(_dot(pbs.astype(_MXU), vd))
            o_ref[:, pr * PAIR:(pr + 1) * PAIR] = jnp.where(low, outs[0], outs[1]).astype(o_ref.dtype)
        xo_ref[...] = xv + _dot(o_ref[...], wo_ref[...]) + bo_ref[...]

    blk = lambda w: pl.BlockSpec((BLK, w), lambda i: (i, 0))
    return pl.pallas_call(
        body, name="attn_fwd", grid=(n,),
        in_specs=[blk(D), _full((1, D)), _full((D, D)), _full((1, D)),
                  pl.BlockSpec(memory_space=pltpu.SMEM),
                  pl.BlockSpec((BLK, KVD), lambda i: (jnp.maximum(i - 1, 0), 0)), blk(KVD),
                  _full((D, D)), _full((1, D))],
        out_specs=[blk(D), blk(D), blk(D)],
        out_shape=[jax.ShapeDtypeStruct((S, D), _F32), jax.ShapeDtypeStruct((S, D), _MXU),
                   jax.ShapeDtypeStruct((S, D), _MXU)],
        scratch_shapes=[pltpu.VMEM((2 * BLK, KVD), _MXU)],
        compiler_params=_seq(),
    )(x, g, wq, bq, sinks, kvd, kvd, wo, bo)


def _attn_bwd(x, dy, q, g, wq, sinks, kvd, wo):
    S = x.shape[0]
    n = S // BLK

    def body(x_ref, dy_ref, q_ref, g_ref, wq_ref, sk_ref, kp_ref, kc_ref, wo_ref,
             dx_ref, dq_ref, h_ref, dc_ref, dp_ref, dbq_ref, dbo_ref, dg_ref, dsk_ref, win, dob, qs, dos):
        i = pl.program_id(0)

        @pl.when(i == 0)
        def _():
            dbq_ref[...] = jnp.zeros_like(dbq_ref)
            dbo_ref[...] = jnp.zeros_like(dbo_ref)
            dg_ref[...] = jnp.zeros_like(dg_ref)
            dsk_ref[...] = jnp.zeros_like(dsk_ref)

        gv = g_ref[...]
        xv = x_ref[...]
        xn, r = _rms(xv)
        h_ref[...] = (xn * gv).astype(h_ref.dtype)
        dyv = dy_ref[...]
        dbo_ref[...] += _colsum(dyv)
        dob[...] = _dot(dyv.astype(_MXU), wo_ref[...], _NT).astype(dob.dtype)
        win[0:BLK, :] = kp_ref[...]
        win[BLK:, :] = kc_ref[...]
        mask = _attn_mask(i, STACK)
        lane = lax.broadcasted_iota(jnp.int32, (1, LANES), 1)
        dq_all = []
        for kh in range(N_KV):
            ks = slice(kh * PAIR, (kh + 1) * PAIR)
            vs = slice((N_KV + kh) * PAIR, (N_KV + kh + 1) * PAIR)
            kd, vd = win[:, ks], win[:, vs]
            _stack_heads(q_ref, kh, qs)
            _stack_heads(dob, kh, dos)
            pbs, ps = _head_probs(qs[...], kd, mask, _sink_col(sk_ref, kh))
            dpr = _dot(dos[...], vd, _NT)
            delta = jnp.sum(pbs * dpr, axis=-1, keepdims=True)
            dsb = (pbs * (dpr - delta) * (HEAD_DIM ** -0.5)).astype(_MXU)
            dqst = _dot(dsb, kd)
            dk = _dot(dsb, qs[...], _TN)
            dv = _dot(pbs.astype(_MXU), dos[...], _TN)
            dp_ref[:, ks], dc_ref[:, ks] = dk[0:BLK, :], dk[BLK:, :]
            dp_ref[:, vs], dc_ref[:, vs] = dv[0:BLK, :], dv[BLK:, :]
            sd = ps * delta
            for hh in range(GROUP):
                dsk_ref[...] -= jnp.where(lane == kh * GROUP + hh, _colsum(sd[hh * BLK:(hh + 1) * BLK, :]), 0.0)
            dq_all += [_unstack_heads(dqst, pp) for pp in range(GROUP // 2)]
        dq = jnp.concatenate(dq_all, axis=1)
        dbq_ref[...] += _colsum(dq)
        dqb = dq.astype(_MXU)
        dq_ref[...] = dqb
        dh = _dot(dqb, wq_ref[...], _NT)
        dg_ref[...] += _colsum(dh * xn)
        dx_ref[...] = dyv + _rms_bwd(dh, xn, r, gv)

    blk = lambda w: pl.BlockSpec((BLK, w), lambda i: (i, 0))
    return pl.pallas_call(
        body, name="attn_bwd", grid=(n,),
        in_specs=[blk(D), blk(D), blk(D), _full((1, D)), _full((D, D)),
                  pl.BlockSpec(memory_space=pltpu.SMEM),
                  pl.BlockSpec((BLK, KVD), lambda i: (jnp.maximum(i - 1, 0), 0)), blk(KVD), _full((D, D))],
        out_specs=[blk(D), blk(D), blk(D), blk(KVD), blk(KVD),
                   _full((1, D)), _full((1, D)), _full((1, D)), _full((1, LANES))],
        out_shape=[jax.ShapeDtypeStruct((S, D), _F32), jax.ShapeDtypeStruct((S, D), _MXU),
                   jax.ShapeDtypeStruct((S, D), _MXU), jax.ShapeDtypeStruct((S, KVD), _F32),
                   jax.ShapeDtypeStruct((S, KVD), _F32), jax.ShapeDtypeStruct((1, D), _F32),
                   jax.ShapeDtypeStruct((1, D), _F32), jax.ShapeDtypeStruct((1, D), _F32),
                   jax.ShapeDtypeStruct((1, LANES), _F32)],
        scratch_shapes=[pltpu.VMEM((2 * BLK, KVD), _MXU), pltpu.VMEM((BLK, D), _MXU),
                        pltpu.VMEM((STACK, PAIR), _MXU), pltpu.VMEM((STACK, PAIR), _MXU)],
        compiler_params=_seq(),
    )(x, dy, q, g, wq, sinks, kvd, kvd, wo)


def _loss_bwd(x, g, tgt):
    S = x.shape[0]
    T = _tile(S, 512, 8)

    def body(x_ref, g_ref, t_ref, dx_ref, ls_ref, dg_ref):
        @pl.when(pl.program_id(0) == 0)
        def _():
            ls_ref[...] = jnp.zeros_like(ls_ref)
            dg_ref[...] = jnp.zeros_like(dg_ref)

        gv = g_ref[...]
        xn, r = _rms(x_ref[...])
        err = xn * gv - t_ref[...]
        ls_ref[...] += 0.5 * jnp.sum(jnp.mean(err * err, axis=-1, keepdims=True))
        dyv = err * (1.0 / D)
        dg_ref[...] += _colsum(dyv * xn)
        dx_ref[...] = _rms_bwd(dyv, xn, r, gv)

    return pl.pallas_call(
        body, name="loss_bwd", grid=(S // T,),
        in_specs=[pl.BlockSpec((T, D), lambda i: (i, 0)), _full((1, D)), pl.BlockSpec((T, D), lambda i: (i, 0))],
        out_specs=[pl.BlockSpec((T, D), lambda i: (i, 0)), _full((8, LANES)), _full((1, D))],
        out_shape=[jax.ShapeDtypeStruct((S, D), _F32), jax.ShapeDtypeStruct((8, LANES), _F32),
                   jax.ShapeDtypeStruct((1, D), _F32)],
        compiler_params=_seq(),
    )(x, g, tgt)


def _me():
    return 4 * lax.axis_index("x") + 2 * lax.axis_index("y") + lax.axis_index("c")


def _peer(j):
    x, y, c = lax.axis_index("x"), lax.axis_index("y"), lax.axis_index("c")
    px = 1 - x if j & 4 else x
    py = 1 - y if j & 2 else y
    pc = 1 - c if j & 1 else c
    return (px, py, pc), 4 * px + 2 * py + pc


_HBM = pl.BlockSpec(memory_space=pltpu.HBM)
_SEMS = pl.BlockSpec(memory_space=pltpu.SEMAPHORE)
_EFFECT = pltpu.SideEffectType.DATAFLOW_SIDE_EFFECTING


def _in_hbm(a):
    return pltpu.with_memory_space_constraint(a, pltpu.HBM)


def _start_copies(name, groups):
    flat = []
    for srcs, zones, _ in groups:
        flat += [_in_hbm(a) for a in srcs] + [_in_hbm(lax.empty(z.shape, z.dtype)) for z in zones]
    n_in, n_g = len(flat), len(groups)

    def body(*refs):
        sems = refs[n_in:n_in + 2 * n_g]
        me, k = _me(), 0
        for gi, (srcs, zones, plan) in enumerate(groups):
            src_refs, zone_refs = refs[k:k + len(srcs)], refs[k + len(srcs):k + len(srcs) + len(zones)]
            k += len(srcs) + len(zones)
            for t, (si, zi, src_of, dst_of) in enumerate(plan):
                for j in range(1, N_DEV):
                    dev, pk = _peer(j)
                    pltpu.make_async_remote_copy(
                        src_ref=src_of(src_refs[si], pk), dst_ref=dst_of(zone_refs[zi], me),
                        send_sem=sems[2 * gi].at[t * (N_DEV - 1) + j - 1], recv_sem=sems[2 * gi + 1].at[t * (N_DEV - 1) + j - 1],
                        device_id=dev, device_id_type=pl.DeviceIdType.MESH).start()
        refs[-1][...] = jnp.zeros_like(refs[-1])

    sem_shapes = []
    for _, _, plan in groups:
        sem_shapes += [pltpu.SemaphoreType.DMA((len(plan) * (N_DEV - 1),))] * 2
    outs = pl.pallas_call(
        body, name=name,
        out_shape=(*sem_shapes, *[pltpu.HBM(a.shape, a.dtype) for a in flat], jax.ShapeDtypeStruct((8, LANES), _F32)),
        in_specs=[_HBM] * n_in,
        out_specs=(*[_SEMS] * (2 * n_g), *[_HBM] * n_in, pl.BlockSpec(memory_space=pltpu.VMEM)),
        input_output_aliases={k: 2 * n_g + k for k in range(n_in)},
        compiler_params=pltpu.CompilerParams(has_side_effects=_EFFECT),
    )(*flat)
    handles, k = [], 2 * n_g
    for gi, (srcs, zones, plan) in enumerate(groups):
        ns, nz = len(srcs), len(zones)
        handles.append((outs[2 * gi], outs[2 * gi + 1], list(outs[k:k + ns]), list(outs[k + ns:k + ns + nz]), plan))
        k += ns + nz
    return handles, outs[-1]


def _wait_copies(name, handles, after):
    flat = []
    for _, _, srcs, zones, _ in handles:
        flat += srcs + zones
    n_in, n_g = len(flat), len(handles)

    def body(*refs):
        sems = refs[n_in:n_in + 2 * n_g]
        outs = refs[n_in + 2 * n_g + 1:2 * n_in + 2 * n_g + 1]
        local_sems = refs[-1]
        me, k, n_local, local, remote = _me(), 0, 0, [], []
        for gi, (_, _, srcs, zones, plan) in enumerate(handles):
            ns, nz = len(srcs), len(zones)
            src_refs, zone_refs, zone_outs = refs[k:k + ns], refs[k + ns:k + ns + nz], outs[k + ns:k + ns + nz]
            k += ns + nz
            for t, (si, zi, src_of, dst_of) in enumerate(plan):
                cp = pltpu.make_async_copy(src_of(src_refs[si], me), dst_of(zone_outs[zi], me), local_sems.at[n_local])
                cp.start()
                local.append(cp)
                n_local += 1
                for j in range(1, N_DEV):
                    dev, pk = _peer(j)
                    remote.append(pltpu.make_async_remote_copy(
                        src_ref=src_of(src_refs[si], pk), dst_ref=dst_of(zone_refs[zi], pk),
                        send_sem=sems[2 * gi].at[t * (N_DEV - 1) + j - 1], recv_sem=sems[2 * gi + 1].at[t * (N_DEV - 1) + j - 1],
                        device_id=dev, device_id_type=pl.DeviceIdType.MESH))
        for cp in remote:
            cp.wait_send()
            cp.wait_recv()
        for cp in local:
            cp.wait()

    sem_args = []
    for send, recv, _, _, _ in handles:
        sem_args += [send, recv]
    outs = pl.pallas_call(
        body, name=name, out_shape=tuple(pltpu.HBM(a.shape, a.dtype) for a in flat),
        in_specs=[_HBM] * n_in + [_SEMS] * (2 * n_g) + [pl.BlockSpec(memory_space=pl.ANY)],
        out_specs=tuple([_HBM] * n_in), input_output_aliases={k: k for k in range(n_in)},
        scratch_shapes=[pltpu.SemaphoreType.DMA((sum(len(h[4]) for h in handles),))],
        compiler_params=pltpu.CompilerParams(has_side_effects=_EFFECT),
    )(*flat, *sem_args, after)
    res, k = [], 0
    for _, _, srcs, zones, _ in handles:
        res.append(list(outs[k + len(srcs):k + len(srcs) + len(zones)]))
        k += len(srcs) + len(zones)
    return res


def _rows(axis, size):
    def of(ref, b):
        start = b * size
        if size % 8 == 0:
            start = pl.multiple_of(start, 8)
        return ref.at[(slice(None),) * axis + (pl.ds(start, size),)]
    return of


def _whole(ref, b):
    return ref


def _slot(ref, b):
    return ref.at[b]


def _gather_group(shards):
    zones, plan = [], []
    for k, (a, axis) in enumerate(shards):
        zones.append(jax.ShapeDtypeStruct(a.shape[:axis] + (N_DEV * a.shape[axis],) + a.shape[axis + 1:], a.dtype))
        plan.append((k, k, _whole, _rows(axis, a.shape[axis])))
    return [a for a, _ in shards], zones, plan


def _scatter_group(grads):
    zones, plan = [], []
    for k, (a, axis) in enumerate(grads):
        size = a.shape[axis] // N_DEV
        zones.append(jax.ShapeDtypeStruct((N_DEV,) + a.shape[:axis] + (size,) + a.shape[axis + 1:], a.dtype))
        plan.append((k, k, _rows(axis, size), _slot))
    return [a for a, _ in grads], zones, plan


def _all_reduce_small(p):
    R = p.shape[0]

    def body(p_ref, o_ref, land, send_sems, recv_sems):
        me = _me()
        land[me] = p_ref[...]
        waits = []
        for j in range(1, N_DEV):
            dev, pk = _peer(j)
            pltpu.make_async_remote_copy(
                src_ref=p_ref, dst_ref=land.at[me], send_sem=send_sems.at[j - 1], recv_sem=recv_sems.at[j - 1],
                device_id=dev, device_id_type=pl.DeviceIdType.MESH).start()
            waits.append(pltpu.make_async_remote_copy(
                src_ref=p_ref, dst_ref=land.at[pk], send_sem=send_sems.at[j - 1], recv_sem=recv_sems.at[j - 1],
                device_id=dev, device_id_type=pl.DeviceIdType.MESH))
        for cp in waits:
            cp.wait()
        tot = land[0]
        for b in range(1, N_DEV):
            tot = tot + land[b]
        o_ref[...] = tot

    vmem = pl.BlockSpec(memory_space=pltpu.VMEM)
    return pl.pallas_call(
        body, name="all_reduce_small", in_specs=[vmem], out_specs=vmem,
        out_shape=jax.ShapeDtypeStruct((R, LANES), _F32),
        scratch_shapes=[pltpu.VMEM((N_DEV, R, LANES), _F32), pltpu.SemaphoreType.DMA((N_DEV - 1,)),
                        pltpu.SemaphoreType.DMA((N_DEV - 1,))],
        compiler_params=_params(),
    )(p)


def _sum_landed(land):
    g = land[0].astype(_F32)
    for b in range(1, N_DEV):
        g = g + land[b].astype(_F32)
    return g


def _landed_specs(n_layers, tr, C, nr):
    def spec(k):
        return pl.BlockSpec((N_DEV, tr, C), lambda l, i: (0, jnp.where(l == k, i, jnp.where(l < k, 0, nr - 1)), 0))
    return [spec(k) for k in range(n_layers)]


def _per_layer(l, zone_refs, fn):
    for k, ref in enumerate(zone_refs):
        @pl.when(l == k)
        def _(ref=ref):
            fn(_sum_landed(ref))


def _sum8(zones):
    L = len(zones)
    _, R, C = zones[0].shape
    tr = _tile(R, 352, 16)
    nr = R // tr

    def body(*refs):
        o_ref = refs[L]

        def put(g):
            o_ref[...] = g

        _per_layer(pl.program_id(0), refs[:L], put)

    return pl.pallas_call(
        body, name="sum8", grid=(L, nr), in_specs=_landed_specs(L, tr, C, nr),
        out_specs=pl.BlockSpec((tr, C), lambda l, i: (l * nr + i, 0)),
        out_shape=jax.ShapeDtypeStruct((L * R, C), _F32), compiler_params=_seq(2),
    )(*zones)


def _adam_update(gv, w_ref, m_ref, v_ref, d_ref, mo_ref, vo_ref):
    mn = ADAM_B1 * m_ref[...] + (1.0 - ADAM_B1) * gv
    vn = ADAM_B2 * v_ref[...] + (1.0 - ADAM_B2) * (gv * gv)
    mo_ref[...] = mn
    vo_ref[...] = vn
    d_ref[...] = -ADAM_LR * ((mn / (1.0 - ADAM_B1 ** ADAM_STEP)) / (jnp.sqrt(vn / (1.0 - ADAM_B2 ** ADAM_STEP)) + ADAM_EPS)
                             + ADAM_WD * w_ref[...])


def _adamw(g, w, m, v, name):
    R, C = w.shape
    tr = _tile(R, 256, 16)

    def body(g_ref, w_ref, m_ref, v_ref, d_ref, mo_ref, vo_ref):
        _adam_update(g_ref[...], w_ref, m_ref, v_ref, d_ref, mo_ref, vo_ref)

    row = pl.BlockSpec((tr, C), lambda i: (i, 0))
    return pl.pallas_call(
        body, name=name, grid=(R // tr,), in_specs=[row] * 4, out_specs=[row] * 3,
        out_shape=[jax.ShapeDtypeStruct((R, C), _F32)] * 3, compiler_params=_seq(),
    )(g, w, m, v)


def _adamw_landed(zones, w, m, v, name):
    L = len(zones)
    _, R, C = zones[0].shape
    tr = _tile(R, 176, 16)
    nr = R // tr

    def body(*refs):
        w_ref, m_ref, v_ref, g_ref, d_ref, mo_ref, vo_ref = refs[L:]

        def update(g):
            g_ref[...] = g
            _adam_update(g, w_ref, m_ref, v_ref, d_ref, mo_ref, vo_ref)

        _per_layer(pl.program_id(0), refs[:L], update)

    row = pl.BlockSpec((tr, C), lambda l, i: (l * nr + i, 0))
    return pl.pallas_call(
        body, name=name, grid=(L, nr), in_specs=_landed_specs(L, tr, C, nr) + [row] * 3, out_specs=[row] * 4,
        out_shape=[jax.ShapeDtypeStruct((L * R, C), _F32)] * 4, compiler_params=_seq(2),
    )(*zones, w, m, v)


def _pack(parts):
    flat = jnp.concatenate([p.reshape(-1).astype(_F32) for p in parts])
    n = flat.shape[0]
    rows = -(-n // (8 * LANES)) * 8
    return jnp.pad(flat, (0, rows * LANES - n)).reshape(rows, LANES)


def _unpack(packed, shapes):
    flat, out, k = packed.reshape(-1), [], 0
    for s in shapes:
        n = 1
        for d in s:
            n *= d
        out.append(flat[k:k + n].reshape(s))
        k += n
    return out


def kernel(x, norm1_g, norm2_g, pool_w, pool_scale, kv_norm_g, w_kv, b_kv, w_q, b_q, sinks, w_o, b_o, ffn_up, ffn_conv_w, ffn_conv_b, ffn_down, final_g, loss_target, m_norm1_g, m_norm2_g, m_pool_w, m_pool_scale, m_kv_norm_g, m_w_kv, m_b_kv, m_w_q, m_b_q, m_sinks, m_w_o, m_b_o, m_ffn_up, m_ffn_conv_w, m_ffn_conv_b, m_ffn_down, m_final_g, v_norm1_g, v_norm2_g, v_pool_w, v_pool_scale, v_kv_norm_g, v_w_kv, v_b_kv, v_w_q, v_b_q, v_sinks, v_w_o, v_b_o, v_ffn_up, v_ffn_conv_w, v_ffn_conv_b, v_ffn_down, v_final_g):
    S = x.shape[1]
    F2s = ffn_up.shape[2]
    F2 = N_DEV * F2s
    me = _me()
    x0 = x.reshape(S, D)
    tgt = loss_target.reshape(S, D)
    row = lambda a: a.reshape(1, -1)

    small = _pack([pool_scale, ffn_conv_w])
    wire = lambda a: a.astype(_MXU)
    ffn_w = lambda l: [(wire(ffn_up[l]).T, 0), (wire(ffn_down[l]), 0)]
    attn_w = lambda j: [(wire(w_q[j]), 0), (wire(w_o[j]), 0)]
    gathers, token = _start_copies("gather_start", [_gather_group(g) for g in (
        [(wire(pool_w[0]), 1), (small[None], 0)], ffn_w(0), [(wire(pool_w[1]), 1)] + ffn_w(1),
        [(wire(w_kv), 0)] + attn_w(0), ffn_w(2), attn_w(1), ffn_w(3))])

    def gathered(k, after):
        return _wait_copies("gather_wait_%d" % k, [gathers[k]], after)[0]

    pw, up_t, down, wq, wo = [None] * N_A, [None] * DEPTH, [None] * DEPTH, [None] * 2, [None] * 2
    pw[0], small_all = gathered(0, token)
    n_ps = pool_scale.size
    small_all = small_all.reshape(N_DEV, -1)
    pscale = jnp.transpose(small_all[:, :n_ps].reshape(N_DEV, N_A, D // N_DEV), (1, 0, 2)).reshape(N_A, D)
    conv_w = jnp.transpose(small_all[:, n_ps:n_ps + ffn_conv_w.size].reshape(N_DEV, DEPTH, CONV_W, F2s),
                           (1, 2, 0, 3)).reshape(DEPTH, CONV_W, F2)

    def dup(a):
        a4 = a.reshape(a.shape[:-1] + (2 * N_KV, 1, HEAD_DIM))
        return jnp.broadcast_to(a4, a.shape[:-1] + (2 * N_KV, 2, HEAD_DIM)).reshape(a.shape[:-1] + (KVD,))

    def fold(a):
        return a.reshape(a.shape[:-1] + (2 * N_KV, 2, HEAD_DIM)).sum(axis=-2).reshape(a.shape[:-1] + (2 * N_KV * HEAD_DIM,))

    xs, us, qs, os_ = [x0], [], [], []
    xc = x0
    kvd = None
    for l in range(DEPTH):
        if l == 1:
            pw[1], up_t[1], down[1] = gathered(2, xc)
        if l == 3:
            wq[1], wo[1] = gathered(5, xc)
        if l < N_A:
            xc = _pool_fwd(xc, row(norm1_g[l]), pw[l], row(pscale[l]))
        else:
            j = l - N_A
            xc, q, o = _attn_fwd(xc, row(norm1_g[l]), wq[j], row(b_q[j]), sinks[j], kvd, wo[j], row(b_o[j]))
            qs.append(q)
            os_.append(o)
        xs.append(xc)
        if l != 1:
            up_t[l], down[l] = gathered((1, None, 4, 6)[l], xc)
        xc, u, c = _ffn_fwd(xc, row(norm2_g[l]), up_t[l], conv_w[l], row(ffn_conv_b[l]), down[l])
        us.append((u, c))
        xs.append(xc)
        if l == N_A - 1:
            wkv, wq[0], wo[0] = gathered(3, xc)
            wkv_d, bkv_d = dup(wkv), dup(row(b_kv))
            kvd = _kv_fwd(xc, row(kv_norm_g), wkv_d, bkv_d)

    dx, loss_p, d_final = _loss_bwd(xc, row(final_g), tgt)
    d_n1, d_n2, d_cw, d_cb = [None] * DEPTH, [None] * DEPTH, [None] * DEPTH, [None] * DEPTH
    d_bq, d_bo, d_sk, d_ps, dkv_parts = [None] * 2, [None] * 2, [None] * 2, [None] * N_A, []
    scatters = [None] * DEPTH
    for l in reversed(range(DEPTH)):
        x_in, x_mid, x_out = xs[2 * l], xs[2 * l + 1], xs[2 * l + 2]
        mixer_grads = []
        if l == N_A - 1:
            dx, d_wkv, d_bkv, d_kvg = _kv_bwd(x_out, dx, row(kv_norm_g), wkv_d, *dkv_parts)
            mixer_grads.append((fold(d_wkv).astype(_MXU), 0))
        dy = dx
        dx, du, a, h, d_cw[l], d_cb[l], d_n2[l] = _ffn_bwd(
            x_mid, dy, *us[l], row(norm2_g[l]), up_t[l], conv_w[l], down[l])
        g_up = _tn_matmul(du, h, "tn_up")
        g_down = _tn_matmul(a, dy, "tn_down")
        dy = dx
        if l < N_A:
            dx, d_pw, d_ps[l], d_n1[l] = _pool_bwd(x_in, dy, row(norm1_g[l]), pw[l], row(pscale[l]))
            mixer_grads.append((d_pw.astype(_MXU), 1))
        else:
            j = l - N_A
            dx, dq, h, d_cur, d_prev, d_bq[j], d_bo[j], d_n1[l], d_sk[j] = _attn_bwd(
                x_in, dy, qs[j], row(norm1_g[l]), wq[j], sinks[j], kvd, wo[j])
            mixer_grads += [(_tn_matmul(h, dq, "tn_q"), 0), (_tn_matmul(os_[j], dy, "tn_o"), 0)]
            dkv_parts += [d_cur, d_prev]
        (scatters[l],), _ = _start_copies("scatter_start_%d" % l, [_scatter_group([(g_up, 0), (g_down, 0)] + mixer_grads)])

    rep_names = ["norm1_g", "norm2_g", "kv_norm_g", "b_kv", "b_q", "sinks", "b_o", "ffn_conv_b", "final_g"]
    rep = [jnp.concatenate(d_n1), jnp.concatenate(d_n2), d_kvg, fold(d_bkv), jnp.concatenate(d_bq),
           jnp.concatenate([s[:, :N_HEADS] for s in d_sk]), jnp.concatenate(d_bo), jnp.concatenate(d_cb), d_final]
    full = [jnp.concatenate(d_ps), jnp.stack(d_cw), loss_p[0:1, 0:1]]
    given = dict(norm1_g=norm1_g, norm2_g=norm2_g, kv_norm_g=kv_norm_g, b_kv=b_kv, b_q=b_q, sinks=sinks, b_o=b_o,
                 ffn_conv_b=ffn_conv_b, final_g=final_g, pool_scale=pool_scale, ffn_conv_w=ffn_conv_w)
    tot = _unpack(_all_reduce_small(_pack(rep + full)),
                  [given[k].shape for k in rep_names] + [(N_A, D), (DEPTH, CONV_W, F2), ()])
    grad = dict(zip(rep_names, tot))
    loss = tot[-1]
    grad["pool_scale"] = lax.dynamic_slice_in_dim(tot[-3], me * (D // N_DEV), D // N_DEV, axis=1)
    grad["ffn_conv_w"] = lax.dynamic_slice_in_dim(tot[-2], me * F2s, F2s, axis=2)

    moms = dict(norm1_g=(m_norm1_g, v_norm1_g), norm2_g=(m_norm2_g, v_norm2_g), pool_w=(m_pool_w, v_pool_w),
                pool_scale=(m_pool_scale, v_pool_scale), kv_norm_g=(m_kv_norm_g, v_kv_norm_g), w_kv=(m_w_kv, v_w_kv),
                b_kv=(m_b_kv, v_b_kv), w_q=(m_w_q, v_w_q), b_q=(m_b_q, v_b_q), sinks=(m_sinks, v_sinks),
                w_o=(m_w_o, v_w_o), b_o=(m_b_o, v_b_o), ffn_up=(m_ffn_up, v_ffn_up),
                ffn_conv_w=(m_ffn_conv_w, v_ffn_conv_w), ffn_conv_b=(m_ffn_conv_b, v_ffn_conv_b),
                ffn_down=(m_ffn_down, v_ffn_down), final_g=(m_final_g, v_final_g))
    given.update(pool_w=pool_w, w_kv=w_kv, w_q=w_q, w_o=w_o, ffn_up=ffn_up, ffn_down=ffn_down)
    delta, new_m, new_v = {}, {}, {}

    small_names = rep_names + ["pool_scale", "ffn_conv_w"]
    shapes = [given[k].shape for k in small_names]
    outs = _adamw(_pack([grad[k] for k in small_names]), _pack([given[k] for k in small_names]),
                  _pack([moms[k][0] for k in small_names]), _pack([moms[k][1] for k in small_names]), "adamw_small")
    for dst, packed in zip((delta, new_m, new_v), outs):
        dst.update(zip(small_names, _unpack(packed, shapes)))

    zones = _wait_copies("scatter_wait", scatters, outs[0])
    by_name = dict(ffn_up=[z[0] for z in zones], ffn_down=[z[1] for z in zones],
                   w_q=[zones[N_A][2], zones[N_A + 1][2]], w_o=[zones[N_A][3], zones[N_A + 1][3]],
                   w_kv=[zones[N_A - 1][2]], pool_w=[zones[0][2], zones[N_A - 1][3]])

    def update(name, g, cols):
        w = given[name]
        two_d = lambda a: a.reshape(-1, cols)
        if g is None:
            landed = [z.reshape(N_DEV, -1, cols) for z in by_name[name]]
            outs = _adamw_landed(landed, two_d(w), two_d(moms[name][0]), two_d(moms[name][1]), "adamw_" + name)
            grad[name] = outs[0].reshape(w.shape)
        else:
            outs = _adamw(g, two_d(w), two_d(moms[name][0]), two_d(moms[name][1]), "adamw_" + name)
        delta[name], new_m[name], new_v[name] = (o.reshape(w.shape) for o in outs[-3:])

    grad["ffn_up"] = jnp.swapaxes(_sum8(by_name["ffn_up"]).reshape(DEPTH, F2s, D), 1, 2)
    update("ffn_up", grad["ffn_up"].reshape(-1, F2s), F2s)
    update("ffn_down", None, D)
    update("w_q", None, D)
    update("w_o", None, D)
    update("w_kv", None, w_kv.shape[1])
    update("pool_w", None, GC)

    names = ["norm1_g", "norm2_g", "pool_w", "pool_scale", "kv_norm_g", "w_kv", "b_kv", "w_q", "b_q", "sinks", "w_o",
             "b_o", "ffn_up", "ffn_conv_w", "ffn_conv_b", "ffn_down", "final_g"]
    return (loss, dx.reshape(x.shape), *[grad[k] for k in names], *[delta[k] for k in names],
            *[new_m[k] for k in names], *[new_v[k] for k in names])
```

```python
import functools

import jax
import jax.numpy as jnp
from jax import lax
from jax.experimental import pallas as pl
from jax.experimental.pallas import tpu as pltpu

_F32 = jnp.float32
_MXU = jnp.bfloat16

N_DEV = 8
D = 1024
DEPTH = 4
N_A = 2
POOL_WINDOWS = (2, 4, 8, 16)
GC = D // len(POOL_WINDOWS)
HALO = 16
HEAD_DIM = 64
N_HEADS = D // HEAD_DIM
GROUP = 8
N_KV = N_HEADS // GROUP
BLK = 128
PAIR = 2 * HEAD_DIM
KVD = 4 * N_KV * HEAD_DIM
CONV_W = 3
EPS = 1e-5
NEG = -1e30

ADAM_LR = 0.001
ADAM_B1 = 0.9
ADAM_B2 = 0.999
ADAM_EPS = 1e-08
ADAM_WD = 0.01
ADAM_STEP = 10

V7X_VMEM_LIMIT = 56 * 1024 * 1024
LANES = 128

_NT = (((1,), (1,)), ((), ()))
_TN = (((0,), (0,)), ((), ()))


def _params(**kw):
    return pltpu.CompilerParams(vmem_limit_bytes=V7X_VMEM_LIMIT, **kw)


def _seq(n=1):
    return _params(dimension_semantics=("arbitrary",) * n)


def _dot(a, b, dims=None):
    if dims is None:
        return jnp.dot(a, b, preferred_element_type=_F32)
    return lax.dot_general(a, b, dims, preferred_element_type=_F32)


def _rms(x):
    r = lax.rsqrt(jnp.mean(x * x, axis=-1, keepdims=True) + EPS)
    return x * r, r


def _rms_bwd(dh, xn, r, g):
    dxn = dh * g
    return r * (dxn - xn * jnp.mean(dxn * xn, axis=-1, keepdims=True))


def _colsum(a):
    return jnp.sum(a, axis=0, keepdims=True)


def _tile(n, want, mult=8):
    for t in range(min(want, n), 0, -1):
        if n % t == 0 and t % mult == 0:
            return t
    return n


def _full(shape):
    zeros = (0,) * len(shape)
    return pl.BlockSpec(shape, lambda *_: zeros)


def _pool_windows(hbuf, h, row, T):
    out = []
    for gi, win in enumerate(POOL_WINDOWS):
        cs = slice(gi * GC, (gi + 1) * GC)
        acc = hbuf[HALO:HALO + T, cs]
        for k in range(1, win):
            acc = acc + hbuf[HALO - k:HALO - k + T, cs]
        cnt = jnp.minimum(row + 1, win).astype(_F32)
        out.append((acc / cnt - h[:, cs], cnt))
    return out


def _pool_fwd(x, g, w, sc):
    S = x.shape[0]
    T = _tile(S, 512, HALO)
    n, hb = S // T, T // HALO

    def body(x_ref, xh_ref, g_ref, w_ref, sc_ref, o_ref, hbuf):
        i = pl.program_id(0)
        gv = g_ref[...]
        xv = x_ref[...]
        h = _rms(xv)[0] * gv
        hbuf[0:HALO, :] = jnp.where(i > 0, _rms(xh_ref[...])[0] * gv, 0.0)
        hbuf[HALO:, :] = h
        row = i * T + lax.broadcasted_iota(jnp.int32, (T, 1), 0)
        for gi, (p, _) in enumerate(_pool_windows(hbuf, h, row, T)):
            cs = slice(gi * GC, (gi + 1) * GC)
            z = _dot(p.astype(_MXU), w_ref[gi])
            o_ref[:, cs] = xv[:, cs] + z * sc_ref[:, cs]

    return pl.pallas_call(
        body, name="pool_fwd", grid=(n,),
        in_specs=[pl.BlockSpec((T, D), lambda i: (i, 0)),
                  pl.BlockSpec((HALO, D), lambda i: (jnp.maximum(i * hb - 1, 0), 0)),
                  _full((1, D)), _full((4, GC, GC)), _full((1, D))],
        out_specs=pl.BlockSpec((T, D), lambda i: (i, 0)),
        out_shape=jax.ShapeDtypeStruct((S, D), _F32),
        scratch_shapes=[pltpu.VMEM((T + HALO, D), _F32)],
        compiler_params=_seq(),
    )(x, x, g, w, sc)


def _pool_bwd(x, dy, g, w, sc):
    S = x.shape[0]
    T = _tile(S, 512, HALO)
    n, hb = S // T, T // HALO

    def body(x_ref, xh_ref, dy_ref, dyh_ref, g_ref, w_ref, sc_ref, dx_ref, dw_ref, dsc_ref, dg_ref,
             hbuf, qbuf, dhbuf):
        i = pl.program_id(0)

        @pl.when(i == 0)
        def _():
            dw_ref[...] = jnp.zeros_like(dw_ref)
            dsc_ref[...] = jnp.zeros_like(dsc_ref)
            dg_ref[...] = jnp.zeros_like(dg_ref)

        gv = g_ref[...]
        xv = x_ref[...]
        xn, r = _rms(xv)
        h = xn * gv
        hbuf[0:HALO, :] = jnp.where(i > 0, _rms(xh_ref[...])[0] * gv, 0.0)
        hbuf[HALO:, :] = h
        dyv = dy_ref[...]
        dz = dyv * sc_ref[...]
        dzh = jnp.where(i < n - 1, dyh_ref[...], 0.0) * sc_ref[...]
        row = i * T + lax.broadcasted_iota(jnp.int32, (T, 1), 0)
        rowh = (i + 1) * T + lax.broadcasted_iota(jnp.int32, (HALO, 1), 0)
        for gi, (p, cnt) in enumerate(_pool_windows(hbuf, h, row, T)):
            win = POOL_WINDOWS[gi]
            cs = slice(gi * GC, (gi + 1) * GC)
            pb = p.astype(_MXU)
            wg = w_ref[gi]
            dsc_ref[:, cs] += _colsum(dyv[:, cs] * _dot(pb, wg))
            dzb = dz[:, cs].astype(_MXU)
            dw_ref[gi] += _dot(pb, dzb, _TN)
            dp = _dot(dzb, wg, _NT)
            dph = _dot(dzh[:, cs].astype(_MXU), wg, _NT)
            qbuf[0:T, cs] = dp / cnt
            qbuf[T:T + HALO, cs] = dph / jnp.minimum(rowh + 1, win).astype(_F32)
            acc = qbuf[0:T, cs]
            for k in range(1, win):
                acc = acc + qbuf[k:k + T, cs]
            dhbuf[:, cs] = acc - dp
        dh = dhbuf[...]
        dg_ref[...] += _colsum(dh * xn)
        dx_ref[...] = dyv + _rms_bwd(dh, xn, r, gv)

    return pl.pallas_call(
        body, name="pool_bwd", grid=(n,),
        in_specs=[pl.BlockSpec((T, D), lambda i: (i, 0)),
                  pl.BlockSpec((HALO, D), lambda i: (jnp.maximum(i * hb - 1, 0), 0)),
                  pl.BlockSpec((T, D), lambda i: (i, 0)),
                  pl.BlockSpec((HALO, D), lambda i: (jnp.minimum((i + 1) * hb, S // HALO - 1), 0)),
                  _full((1, D)), _full((4, GC, GC)), _full((1, D))],
        out_specs=[pl.BlockSpec((T, D), lambda i: (i, 0)), _full((4, GC, GC)), _full((1, D)), _full((1, D))],
        out_shape=[jax.ShapeDtypeStruct((S, D), _F32), jax.ShapeDtypeStruct((4, GC, GC), _F32),
                   jax.ShapeDtypeStruct((1, D), _F32), jax.ShapeDtypeStruct((1, D), _F32)],
        scratch_shapes=[pltpu.VMEM((T + HALO, D), _F32), pltpu.VMEM((T + HALO, D), _F32), pltpu.VMEM((T, D), _F32)],
        compiler_params=_seq(),
    )(x, x, dy, dy, g, w, sc)


FFN_FWD_TILE, FFN_FWD_CHUNKS = 256, 2
FFN_BWD_TILE, FFN_BWD_CHUNKS = 128, 2
EDGE = 8


def _shift_down(v, k, prev):
    r = pltpu.roll(v, k, axis=0)
    i8 = lax.broadcasted_iota(jnp.int32, (EDGE, v.shape[1]), 0)
    head = jnp.where(i8 >= k, r[0:EDGE, :], pltpu.roll(prev, k, axis=0))
    return jnp.concatenate([head, r[EDGE:, :]], axis=0)


def _shift_up(v, k, nxt):
    T = v.shape[0]
    r = pltpu.roll(v, T - k, axis=0)
    i8 = lax.broadcasted_iota(jnp.int32, (EDGE, v.shape[1]), 0)
    tail = jnp.where(i8 < EDGE - k, r[T - EDGE:, :], pltpu.roll(nxt, EDGE - k, axis=0))
    return jnp.concatenate([r[:T - EDGE, :], tail], axis=0)


def _load_weights(i, pairs, sems):
    @pl.when(i == 0)
    def _():
        cps = [pltpu.make_async_copy(src, dst, sems.at[k]) for k, (src, dst) in enumerate(pairs)]
        for cp in cps:
            cp.start()
        for cp in cps:
            cp.wait()


def _ffn_fwd(x, g, wup_t, cw, cb, wdn):
    S = x.shape[0]
    F2 = wup_t.shape[0]
    F = F2 // 2
    C = F // FFN_FWD_CHUNKS
    T = _tile(S, FFN_FWD_TILE, 16)
    n = S // T

    def body(x_ref, g_ref, wup_hbm, cw_ref, cb_ref, wdn_hbm, o_ref, u_ref, c_ref, wup, wdnv, carry, sems):
        i = pl.program_id(0)
        _load_weights(i, [(wup_hbm, wup), (wdn_hbm, wdnv)], sems)

        @pl.when(i == 0)
        def _():
            carry[...] = jnp.zeros_like(carry)

        xv = x_ref[...]
        hb = (_rms(xv)[0] * g_ref[...]).astype(_MXU)
        acc = jnp.zeros((T, D), _F32)
        for j in range(FFN_FWD_CHUNKS):
            halves = []
            for cs in (slice(j * C, (j + 1) * C), slice(F + j * C, F + (j + 1) * C)):
                u = _dot(hb, wup[cs, :], _NT)
                u_ref[:, cs] = u.astype(u_ref.dtype)
                prev = carry[:, cs]
                carry[:, cs] = u[T - EDGE:, :]
                c = (cw_ref[0:1, cs] * _shift_down(u, 2, prev) + cw_ref[1:2, cs] * _shift_down(u, 1, prev)
                     + cw_ref[2:3, cs] * u + cb_ref[:, cs])
                c_ref[:, cs] = c.astype(c_ref.dtype)
                halves.append(c)
            cg, cv = halves
            a = (cg * jax.nn.sigmoid(cg)) * cv
            acc = acc + _dot(a.astype(_MXU), wdnv[j * C:(j + 1) * C, :])
        o_ref[...] = xv + acc

    any_ = pl.BlockSpec(memory_space=pl.ANY)
    wide = pl.BlockSpec((T, F2), lambda i: (i, 0))
    return pl.pallas_call(
        body, name="ffn_fwd", grid=(n,),
        in_specs=[pl.BlockSpec((T, D), lambda i: (i, 0)), _full((1, D)), any_, _full((CONV_W, F2)), _full((1, F2)), any_],
        out_specs=[pl.BlockSpec((T, D), lambda i: (i, 0)), wide, wide],
        out_shape=[jax.ShapeDtypeStruct((S, D), _F32), jax.ShapeDtypeStruct((S, F2), _MXU),
                   jax.ShapeDtypeStruct((S, F2), _MXU)],
        scratch_shapes=[pltpu.VMEM((F2, D), _MXU), pltpu.VMEM((F, D), _MXU),
                        pltpu.VMEM((EDGE, F2), _F32), pltpu.SemaphoreType.DMA((2,))],
        compiler_params=_seq(),
    )(x, g, wup_t, cw, cb, wdn)


def _ffn_bwd(x, dy, u, c, g, wup_t, cw, wdn):
    S = x.shape[0]
    F2 = wup_t.shape[0]
    F = F2 // 2
    C = F // FFN_BWD_CHUNKS
    T = _tile(S, FFN_BWD_TILE, 16)
    n = S // T

    def body(x_ref, dy_ref, u_ref, c_ref, g_ref, wup_hbm, cw_ref, wdn_hbm,
             dx_ref, du_ref, a_ref, h_ref, dcw_ref, dcb_ref, dg_ref, wup, wdnv, carry, sems):
        i = pl.program_id(0)
        _load_weights(i, [(wup_hbm, wup), (wdn_hbm, wdnv)], sems)

        @pl.when(i == 0)
        def _():
            carry[...] = jnp.zeros_like(carry)
            dcw_ref[...] = jnp.zeros_like(dcw_ref)
            dcb_ref[...] = jnp.zeros_like(dcb_ref)
            dg_ref[...] = jnp.zeros_like(dg_ref)

        gv = g_ref[...]
        xv = x_ref[...]
        xn, r = _rms(xv)
        hbf = (xn * gv).astype(_MXU)
        h_ref[...] = hbf
        dyv = dy_ref[...]
        dyb = dyv.astype(_MXU)
        dh = jnp.zeros((T, D), _F32)
        for j in range(FFN_BWD_CHUNKS):
            gs, vs = slice(j * C, (j + 1) * C), slice(F + j * C, F + (j + 1) * C)
            cg, cv = c_ref[:, gs].astype(_F32), c_ref[:, vs].astype(_F32)
            sg = jax.nn.sigmoid(cg)
            sl = cg * sg
            a_ref[:, gs] = (sl * cv).astype(a_ref.dtype)
            da = _dot(dyb, wdnv[gs, :], _NT)
            for cs, dc in ((gs, da * cv * (sg * (1.0 + cg * (1.0 - sg)))), (vs, da * sl)):
                nxt = carry[:, cs]
                carry[:, cs] = dc[0:EDGE, :]
                dc1, dc2 = _shift_up(dc, 1, nxt), _shift_up(dc, 2, nxt)
                uf = u_ref[:, cs].astype(_F32)
                dcb_ref[:, cs] += _colsum(dc)
                for k, d in enumerate((dc2, dc1, dc)):
                    dcw_ref[k:k + 1, cs] += _colsum(d * uf)
                du = cw_ref[2:3, cs] * dc + cw_ref[1:2, cs] * dc1 + cw_ref[0:1, cs] * dc2
                dub = du.astype(_MXU)
                du_ref[:, cs] = dub
                dh = dh + _dot(dub, wup[cs, :])
        dg_ref[...] += _colsum(dh * xn)
        dx_ref[...] = dyv + _rms_bwd(dh, xn, r, gv)

    any_ = pl.BlockSpec(memory_space=pl.ANY)
    rev = lambda i: (n - 1 - i, 0)
    return pl.pallas_call(
        body, name="ffn_bwd", grid=(n,),
        in_specs=[pl.BlockSpec((T, D), rev), pl.BlockSpec((T, D), rev), pl.BlockSpec((T, F2), rev),
                  pl.BlockSpec((T, F2), rev), _full((1, D)), any_, _full((CONV_W, F2)), any_],
        out_specs=[pl.BlockSpec((T, D), rev), pl.BlockSpec((T, F2), rev), pl.BlockSpec((T, F), rev),
                   pl.BlockSpec((T, D), rev), _full((CONV_W, F2)), _full((1, F2)), _full((1, D))],
        out_shape=[jax.ShapeDtypeStruct((S, D), _F32), jax.ShapeDtypeStruct((S, F2), _MXU),
                   jax.ShapeDtypeStruct((S, F), _MXU), jax.ShapeDtypeStruct((S, D), _MXU),
                   jax.ShapeDtypeStruct((CONV_W, F2), _F32), jax.ShapeDtypeStruct((1, F2), _F32),
                   jax.ShapeDtypeStruct((1, D), _F32)],
        scratch_shapes=[pltpu.VMEM((F2, D), _MXU), pltpu.VMEM((F, D), _MXU),
                        pltpu.VMEM((EDGE, F2), _F32), pltpu.SemaphoreType.DMA((2,))],
        compiler_params=_seq(),
    )(x, dy, u, c, g, wup_t, cw, wdn)


def _tn_matmul(a, b, name):
    S, M = a.shape
    N = b.shape[1]
    bm = _tile(M, 1408, LANES)
    tk = _tile(S, 512, 16)
    nk = S // tk

    def body(a_ref, b_ref, o_ref, acc):
        k = pl.program_id(1)

        @pl.when(k == 0)
        def _():
            acc[...] = jnp.zeros_like(acc)

        acc[...] += _dot(a_ref[...].astype(_MXU), b_ref[...].astype(_MXU), _TN)

        @pl.when(k == nk - 1)
        def _():
            o_ref[...] = acc[...].astype(o_ref.dtype)

    return pl.pallas_call(
        body, name=name, grid=(M // bm, nk),
        in_specs=[pl.BlockSpec((tk, bm), lambda i, k: (k, i)), pl.BlockSpec((tk, N), lambda i, k: (k, 0))],
        out_specs=pl.BlockSpec((bm, N), lambda i, k: (i, 0)),
        out_shape=jax.ShapeDtypeStruct((M, N), _MXU),
        scratch_shapes=[pltpu.VMEM((bm, N), _F32)],
        compiler_params=_seq(2),
    )(a, b)


def _kv_fwd(x, g, wkv, bkv):
    S = x.shape[0]
    T = _tile(S, 512, 16)

    def body(x_ref, g_ref, w_ref, b_ref, o_ref):
        hb = (_rms(x_ref[...])[0] * g_ref[...]).astype(_MXU)
        o_ref[...] = (_dot(hb, w_ref[...]) + b_ref[...]).astype(o_ref.dtype)

    return pl.pallas_call(
        body, name="kv_fwd", grid=(S // T,),
        in_specs=[pl.BlockSpec((T, D), lambda i: (i, 0)), _full((1, D)), _full((D, KVD)), _full((1, KVD))],
        out_specs=pl.BlockSpec((T, KVD), lambda i: (i, 0)),
        out_shape=jax.ShapeDtypeStruct((S, KVD), _MXU),
        compiler_params=_seq(),
    )(x, g, wkv, bkv)


def _kv_bwd(x, dx_in, g, wkv, cur_a, prev_a, cur_b, prev_b):
    S = x.shape[0]
    n = S // BLK

    def body(x_ref, dxi_ref, g_ref, w_ref, ca, pa, cb, pb, dx_ref, dw_ref, db_ref, dg_ref):
        i = pl.program_id(0)

        @pl.when(i == 0)
        def _():
            dw_ref[...] = jnp.zeros_like(dw_ref)
            db_ref[...] = jnp.zeros_like(db_ref)
            dg_ref[...] = jnp.zeros_like(dg_ref)

        gv = g_ref[...]
        xn, r = _rms(x_ref[...])
        dkv = ca[...] + cb[...] + jnp.where(i < n - 1, pa[...] + pb[...], 0.0)
        db_ref[...] += _colsum(dkv)
        dkb = dkv.astype(_MXU)
        dw_ref[...] += _dot((xn * gv).astype(_MXU), dkb, _TN)
        dh = _dot(dkb, w_ref[...], _NT)
        dg_ref[...] += _colsum(dh * xn)
        dx_ref[...] = dxi_ref[...] + _rms_bwd(dh, xn, r, gv)

    blk = lambda w: pl.BlockSpec((BLK, w), lambda i: (i, 0))
    nxt = pl.BlockSpec((BLK, KVD), lambda i: (jnp.minimum(i + 1, n - 1), 0))
    return pl.pallas_call(
        body, name="kv_bwd", grid=(n,),
        in_specs=[blk(D), blk(D), _full((1, D)), _full((D, KVD)), blk(KVD), nxt, blk(KVD), nxt],
        out_specs=[blk(D), _full((D, KVD)), _full((1, KVD)), _full((1, D))],
        out_shape=[jax.ShapeDtypeStruct((S, D), _F32), jax.ShapeDtypeStruct((D, KVD), _F32),
                   jax.ShapeDtypeStruct((1, KVD), _F32), jax.ShapeDtypeStruct((1, D), _F32)],
        compiler_params=_seq(),
    )(x, dx_in, g, wkv, cur_a, prev_a, cur_b, prev_b)


STACK = GROUP * BLK


def _attn_mask(i, rows):
    qi = lax.broadcasted_iota(jnp.int32, (rows, 2 * BLK), 0) & (BLK - 1)
    si = lax.broadcasted_iota(jnp.int32, (rows, 2 * BLK), 1)
    return (si > qi) & (si <= qi + BLK) & jnp.logical_or(i > 0, si >= BLK)


def _low_half():
    return lax.broadcasted_iota(jnp.int32, (BLK, PAIR), 1) < HEAD_DIM


def _stack_heads(ref, kh, dst):
    low = _low_half()
    for pp in range(GROUP // 2):
        pr = kh * (GROUP // 2) + pp
        v2 = ref[:, pr * PAIR:(pr + 1) * PAIR]
        zero = jnp.zeros_like(v2)
        dst[2 * pp * BLK:(2 * pp + 1) * BLK, :] = jnp.where(low, v2, zero)
        dst[(2 * pp + 1) * BLK:(2 * pp + 2) * BLK, :] = jnp.where(low, zero, v2)


def _unstack_heads(st, pp):
    return jnp.where(_low_half(), st[2 * pp * BLK:(2 * pp + 1) * BLK, :], st[(2 * pp + 1) * BLK:(2 * pp + 2) * BLK, :])


def _sink_col(sk_ref, kh):
    return jnp.concatenate([jnp.full((BLK, 1), sk_ref[kh * GROUP + h], _F32) for h in range(GROUP)], axis=0)


def _head_probs(qm, kd, mask, sink):
    s = jnp.where(mask, _dot(qm, kd, _NT) * (HEAD_DIM ** -0.5), NEG)
    m = jnp.maximum(jnp.max(s, axis=-1, keepdims=True), sink)
    p = jnp.exp(s - m)
    es = jnp.exp(sink - m)
    inv = 1.0 / (jnp.sum(p, axis=-1, keepdims=True) + es)
    return p * inv, es * inv


def _attn_fwd(x, g, wq, bq, sinks, kvd, wo, bo):
    S = x.shape[0]
    n = S // BLK

    def body(x_ref, g_ref, wq_ref, bq_ref, sk_ref, kp_ref, kc_ref, wo_ref, bo_ref, xo_ref, q_ref, o_ref, win):
        i = pl.program_id(0)
        xv = x_ref[...]
        hb = (_rms(xv)[0] * g_ref[...]).astype(_MXU)
        q_ref[...] = (_dot(hb, wq_ref[...]) + bq_ref[...]).astype(q_ref.dtype)
        win[0:BLK, :] = kp_ref[...]
        win[BLK:, :] = kc_ref[...]
        mask = _attn_mask(i, BLK)
        low = _low_half()
        for pr in range(N_HEADS // 2):
            kh = (2 * pr) // GROUP
            kd = win[:, kh * PAIR:(kh + 1) * PAIR]
            vd = win[:, (N_KV + kh) * PAIR:(N_KV + kh + 1) * PAIR]
            q2 = q_ref[:, pr * PAIR:(pr + 1) * PAIR]
            outs = []
            for half in range(2):
                qm = jnp.where(low if half == 0 else ~low, q2, jnp.zeros_like(q2))
                pbs, _ = _head_probs(qm, kd, mask, sk_ref[2 * pr + half])
                outs.append(_dot(pbs.astype(_MXU), vd))
            o_ref[:, pr * PAIR:(pr + 1) * PAIR] = jnp.where(low, outs[0], outs[1]).astype(o_ref.dtype)
        xo_ref[...] = xv + _dot(o_ref[...], wo_ref[...]) + bo_ref[...]

    blk = lambda w: pl.BlockSpec((BLK, w), lambda i: (i, 0))
    return pl.pallas_call(
        body, name="attn_fwd", grid=(n,),
        in_specs=[blk(D), _full((1, D)), _full((D, D)), _full((1, D)),
                  pl.BlockSpec(memory_space=pltpu.SMEM),
                  pl.BlockSpec((BLK, KVD), lambda i: (jnp.maximum(i - 1, 0), 0)), blk(KVD),
                  _full((D, D)), _full((1, D))],
        out_specs=[blk(D), blk(D), blk(D)],
        out_shape=[jax.ShapeDtypeStruct((S, D), _F32), jax.ShapeDtypeStruct((S, D), _MXU),
                   jax.ShapeDtypeStruct((S, D), _MXU)],
        scratch_shapes=[pltpu.VMEM((2 * BLK, KVD), _MXU)],
        compiler_params=_seq(),
    )(x, g, wq, bq, sinks, kvd, kvd, wo, bo)


def _attn_bwd(x, dy, q, g, wq, sinks, kvd, wo):
    S = x.shape[0]
    n = S // BLK

    def body(x_ref, dy_ref, q_ref, g_ref, wq_ref, sk_ref, kp_ref, kc_ref, wo_ref,
             dx_ref, dq_ref, h_ref, dc_ref, dp_ref, dbq_ref, dbo_ref, dg_ref, dsk_ref, win, dob, qs, dos):
        i = pl.program_id(0)

        @pl.when(i == 0)
        def _():
            dbq_ref[...] = jnp.zeros_like(dbq_ref)
            dbo_ref[...] = jnp.zeros_like(dbo_ref)
            dg_ref[...] = jnp.zeros_like(dg_ref)
            dsk_ref[...] = jnp.zeros_like(dsk_ref)

        gv = g_ref[...]
        xv = x_ref[...]
        xn, r = _rms(xv)
        h_ref[...] = (xn * gv).astype(h_ref.dtype)
        dyv = dy_ref[...]
        dbo_ref[...] += _colsum(dyv)
        dob[...] = _dot(dyv.astype(_MXU), wo_ref[...], _NT).astype(dob.dtype)
        win[0:BLK, :] = kp_ref[...]
        win[BLK:, :] = kc_ref[...]
        mask = _attn_mask(i, STACK)
        lane = lax.broadcasted_iota(jnp.int32, (1, LANES), 1)
        dq_all = []
        for kh in range(N_KV):
            ks = slice(kh * PAIR, (kh + 1) * PAIR)
            vs = slice((N_KV + kh) * PAIR, (N_KV + kh + 1) * PAIR)
            kd, vd = win[:, ks], win[:, vs]
            _stack_heads(q_ref, kh, qs)
            _stack_heads(dob, kh, dos)
            pbs, ps = _head_probs(qs[...], kd, mask, _sink_col(sk_ref, kh))
            dpr = _dot(dos[...], vd, _NT)
            delta = jnp.sum(pbs * dpr, axis=-1, keepdims=True)
            dsb = (pbs * (dpr - delta) * (HEAD_DIM ** -0.5)).astype(_MXU)
            dqst = _dot(dsb, kd)
            dk = _dot(dsb, qs[...], _TN)
            dv = _dot(pbs.astype(_MXU), dos[...], _TN)
            dp_ref[:, ks], dc_ref[:, ks] = dk[0:BLK, :], dk[BLK:, :]
            dp_ref[:, vs], dc_ref[:, vs] = dv[0:BLK, :], dv[BLK:, :]
            sd = ps * delta
            for hh in range(GROUP):
                dsk_ref[...] -= jnp.where(lane == kh * GROUP + hh, _colsum(sd[hh * BLK:(hh + 1) * BLK, :]), 0.0)
            dq_all += [_unstack_heads(dqst, pp) for pp in range(GROUP // 2)]
        dq = jnp.concatenate(dq_all, axis=1)
        dbq_ref[...] += _colsum(dq)
        dqb = dq.astype(_MXU)
        dq_ref[...] = dqb
        dh = _dot(dqb, wq_ref[...], _NT)
        dg_ref[...] += _colsum(dh * xn)
        dx_ref[...] = dyv + _rms_bwd(dh, xn, r, gv)

    blk = lambda w: pl.BlockSpec((BLK, w), lambda i: (i, 0))
    return pl.pallas_call(
        body, name="attn_bwd", grid=(n,),
        in_specs=[blk(D), blk(D), blk(D), _full((1, D)), _full((D, D)),
                  pl.BlockSpec(memory_space=pltpu.SMEM),
                  pl.BlockSpec((BLK, KVD), lambda i: (jnp.maximum(i - 1, 0), 0)), blk(KVD), _full((D, D))],
        out_specs=[blk(D), blk(D), blk(D), blk(KVD), blk(KVD),
                   _full((1, D)), _full((1, D)), _full((1, D)), _full((1, LANES))],
        out_shape=[jax.ShapeDtypeStruct((S, D), _F32), jax.ShapeDtypeStruct((S, D), _MXU),
                   jax.ShapeDtypeStruct((S, D), _MXU), jax.ShapeDtypeStruct((S, KVD), _F32),
                   jax.ShapeDtypeStruct((S, KVD), _F32), jax.ShapeDtypeStruct((1, D), _F32),
                   jax.ShapeDtypeStruct((1, D), _F32), jax.ShapeDtypeStruct((1, D), _F32),
                   jax.ShapeDtypeStruct((1, LANES), _F32)],
        scratch_shapes=[pltpu.VMEM((2 * BLK, KVD), _MXU), pltpu.VMEM((BLK, D), _MXU),
                        pltpu.VMEM((STACK, PAIR), _MXU), pltpu.VMEM((STACK, PAIR), _MXU)],
        compiler_params=_seq(),
    )(x, dy, q, g, wq, sinks, kvd, kvd, wo)


def _loss_bwd(x, g, tgt):
    S = x.shape[0]
    T = _tile(S, 512, 8)

    def body(x_ref, g_ref, t_ref, dx_ref, ls_ref, dg_ref):
        @pl.when(pl.program_id(0) == 0)
        def _():
            ls_ref[...] = jnp.zeros_like(ls_ref)
            dg_ref[...] = jnp.zeros_like(dg_ref)

        gv = g_ref[...]
        xn, r = _rms(x_ref[...])
        err = xn * gv - t_ref[...]
        ls_ref[...] += 0.5 * jnp.sum(jnp.mean(err * err, axis=-1, keepdims=True))
        dyv = err * (1.0 / D)
        dg_ref[...] += _colsum(dyv * xn)
        dx_ref[...] = _rms_bwd(dyv, xn, r, gv)

    return pl.pallas_call(
        body, name="loss_bwd", grid=(S // T,),
        in_specs=[pl.BlockSpec((T, D), lambda i: (i, 0)), _full((1, D)), pl.BlockSpec((T, D), lambda i: (i, 0))],
        out_specs=[pl.BlockSpec((T, D), lambda i: (i, 0)), _full((8, LANES)), _full((1, D))],
        out_shape=[jax.ShapeDtypeStruct((S, D), _F32), jax.ShapeDtypeStruct((8, LANES), _F32),
                   jax.ShapeDtypeStruct((1, D), _F32)],
        compiler_params=_seq(),
    )(x, g, tgt)


def _me():
    return 4 * lax.axis_index("x") + 2 * lax.axis_index("y") + lax.axis_index("c")


def _peer(j):
    x, y, c = lax.axis_index("x"), lax.axis_index("y"), lax.axis_index("c")
    px = 1 - x if j & 4 else x
    py = 1 - y if j & 2 else y
    pc = 1 - c if j & 1 else c
    return (px, py, pc), 4 * px + 2 * py + pc


_HBM = pl.BlockSpec(memory_space=pltpu.HBM)
_SEMS = pl.BlockSpec(memory_space=pltpu.SEMAPHORE)
_EFFECT = pltpu.SideEffectType.DATAFLOW_SIDE_EFFECTING


def _in_hbm(a):
    return pltpu.with_memory_space_constraint(a, pltpu.HBM)


def _start_copies(name, groups):
    flat = []
    for srcs, zones, _ in groups:
        flat += [_in_hbm(a) for a in srcs] + [_in_hbm(lax.empty(z.shape, z.dtype)) for z in zones]
    n_in, n_g = len(flat), len(groups)

    def body(*refs):
        sems = refs[n_in:n_in + 2 * n_g]
        me, k = _me(), 0
        for gi, (srcs, zones, plan) in enumerate(groups):
            src_refs, zone_refs = refs[k:k + len(srcs)], refs[k + len(srcs):k + len(srcs) + len(zones)]
            k += len(srcs) + len(zones)
            for t, (si, zi, src_of, dst_of) in enumerate(plan):
                for j in range(1, N_DEV):
                    dev, pk = _peer(j)
                    pltpu.make_async_remote_copy(
                        src_ref=src_of(src_refs[si], pk), dst_ref=dst_of(zone_refs[zi], me),
                        send_sem=sems[2 * gi].at[t * (N_DEV - 1) + j - 1], recv_sem=sems[2 * gi + 1].at[t * (N_DEV - 1) + j - 1],
                        device_id=dev, device_id_type=pl.DeviceIdType.MESH).start()
        refs[-1][...] = jnp.zeros_like(refs[-1])

    sem_shapes = []
    for _, _, plan in groups:
        sem_shapes += [pltpu.SemaphoreType.DMA((len(plan) * (N_DEV - 1),))] * 2
    outs = pl.pallas_call(
        body, name=name,
        out_shape=(*sem_shapes, *[pltpu.HBM(a.shape, a.dtype) for a in flat], jax.ShapeDtypeStruct((8, LANES), _F32)),
        in_specs=[_HBM] * n_in,
        out_specs=(*[_SEMS] * (2 * n_g), *[_HBM] * n_in, pl.BlockSpec(memory_space=pltpu.VMEM)),
        input_output_aliases={k: 2 * n_g + k for k in range(n_in)},
        compiler_params=pltpu.CompilerParams(has_side_effects=_EFFECT),
    )(*flat)
    handles, k = [], 2 * n_g
    for gi, (srcs, zones, plan) in enumerate(groups):
        ns, nz = len(srcs), len(zones)
        handles.append((outs[2 * gi], outs[2 * gi + 1], list(outs[k:k + ns]), list(outs[k + ns:k + ns + nz]), plan))
        k += ns + nz
    return handles, outs[-1]


def _wait_copies(name, handles, after):
    flat = []
    for _, _, srcs, zones, _ in handles:
        flat += srcs + zones
    n_in, n_g = len(flat), len(handles)

    def body(*refs):
        sems = refs[n_in:n_in + 2 * n_g]
        outs = refs[n_in + 2 * n_g + 1:2 * n_in + 2 * n_g + 1]
        local_sems = refs[-1]
        me, k, n_local, local, remote = _me(), 0, 0, [], []
        for gi, (_, _, srcs, zones, plan) in enumerate(handles):
            ns, nz = len(srcs), len(zones)
            src_refs, zone_refs, zone_outs = refs[k:k + ns], refs[k + ns:k + ns + nz], outs[k + ns:k + ns + nz]
            k += ns + nz
            for t, (si, zi, src_of, dst_of) in enumerate(plan):
                cp = pltpu.make_async_copy(src_of(src_refs[si], me), dst_of(zone_outs[zi], me), local_sems.at[n_local])
                cp.start()
                local.append(cp)
                n_local += 1
                for j in range(1, N_DEV):
                    dev, pk = _peer(j)
                    remote.append(pltpu.make_async_remote_copy(
                        src_ref=src_of(src_refs[si], pk), dst_ref=dst_of(zone_refs[zi], pk),
                        send_sem=sems[2 * gi].at[t * (N_DEV - 1) + j - 1], recv_sem=sems[2 * gi + 1].at[t * (N_DEV - 1) + j - 1],
                        device_id=dev, device_id_type=pl.DeviceIdType.MESH))
        for cp in remote:
            cp.wait_send()
            cp.wait_recv()
        for cp in local:
            cp.wait()

    sem_args = []
    for send, recv, _, _, _ in handles:
        sem_args += [send, recv]
    outs = pl.pallas_call(
        body, name=name, out_shape=tuple(pltpu.HBM(a.shape, a.dtype) for a in flat),
        in_specs=[_HBM] * n_in + [_SEMS] * (2 * n_g) + [pl.BlockSpec(memory_space=pl.ANY)],
        out_specs=tuple([_HBM] * n_in), input_output_aliases={k: k for k in range(n_in)},
        scratch_shapes=[pltpu.SemaphoreType.DMA((sum(len(h[4]) for h in handles),))],
        compiler_params=pltpu.CompilerParams(has_side_effects=_EFFECT),
    )(*flat, *sem_args, after)
    res, k = [], 0
    for _, _, srcs, zones, _ in handles:
        res.append(list(outs[k + len(srcs):k + len(srcs) + len(zones)]))
        k += len(srcs) + len(zones)
    return res


def _rows(axis, size):
    def of(ref, b):
        start = b * size
        if size % 8 == 0:
            start = pl.multiple_of(start, 8)
        return ref.at[(slice(None),) * axis + (pl.ds(start, size),)]
    return of


def _whole(ref, b):
    return ref


def _slot(ref, b):
    return ref.at[b]


def _gather_group(shards):
    zones, plan = [], []
    for k, (a, axis) in enumerate(shards):
        zones.append(jax.ShapeDtypeStruct(a.shape[:axis] + (N_DEV * a.shape[axis],) + a.shape[axis + 1:], a.dtype))
        plan.append((k, k, _whole, _rows(axis, a.shape[axis])))
    return [a for a, _ in shards], zones, plan


def _scatter_group(grads):
    zones, plan = [], []
    for k, (a, axis) in enumerate(grads):
        size = a.shape[axis] // N_DEV
        zones.append(jax.ShapeDtypeStruct((N_DEV,) + a.shape[:axis] + (size,) + a.shape[axis + 1:], a.dtype))
        plan.append((k, k, _rows(axis, size), _slot))
    return [a for a, _ in grads], zones, plan


def _all_reduce_small(p):
    R = p.shape[0]

    def body(p_ref, o_ref, land, send_sems, recv_sems):
        me = _me()
        land[me] = p_ref[...]
        waits = []
        for j in range(1, N_DEV):
            dev, pk = _peer(j)
            pltpu.make_async_remote_copy(
                src_ref=p_ref, dst_ref=land.at[me], send_sem=send_sems.at[j - 1], recv_sem=recv_sems.at[j - 1],
                device_id=dev, device_id_type=pl.DeviceIdType.MESH).start()
            waits.append(pltpu.make_async_remote_copy(
                src_ref=p_ref, dst_ref=land.at[pk], send_sem=send_sems.at[j - 1], recv_sem=recv_sems.at[j - 1],
                device_id=dev, device_id_type=pl.DeviceIdType.MESH))
        for cp in waits:
            cp.wait()
        tot = land[0]
        for b in range(1, N_DEV):
            tot = tot + land[b]
        o_ref[...] = tot

    vmem = pl.BlockSpec(memory_space=pltpu.VMEM)
    return pl.pallas_call(
        body, name="all_reduce_small", in_specs=[vmem], out_specs=vmem,
        out_shape=jax.ShapeDtypeStruct((R, LANES), _F32),
        scratch_shapes=[pltpu.VMEM((N_DEV, R, LANES), _F32), pltpu.SemaphoreType.DMA((N_DEV - 1,)),
                        pltpu.SemaphoreType.DMA((N_DEV - 1,))],
        compiler_params=_params(),
    )(p)


def _sum_landed(land):
    g = land[0].astype(_F32)
    for b in range(1, N_DEV):
        g = g + land[b].astype(_F32)
    return g


def _landed_specs(n_layers, tr, C, nr):
    def spec(k):
        return pl.BlockSpec((N_DEV, tr, C), lambda l, i: (0, jnp.where(l == k, i, jnp.where(l < k, 0, nr - 1)), 0))
    return [spec(k) for k in range(n_layers)]


def _per_layer(l, zone_refs, fn):
    for k, ref in enumerate(zone_refs):
        @pl.when(l == k)
        def _(ref=ref):
            fn(_sum_landed(ref))


def _sum8(zones):
    L = len(zones)
    _, R, C = zones[0].shape
    tr = _tile(R, 352, 16)
    nr = R // tr

    def body(*refs):
        o_ref = refs[L]

        def put(g):
            o_ref[...] = g

        _per_layer(pl.program_id(0), refs[:L], put)

    return pl.pallas_call(
        body, name="sum8", grid=(L, nr), in_specs=_landed_specs(L, tr, C, nr),
        out_specs=pl.BlockSpec((tr, C), lambda l, i: (l * nr + i, 0)),
        out_shape=jax.ShapeDtypeStruct((L * R, C), _F32), compiler_params=_seq(2),
    )(*zones)


def _adam_update(gv, w_ref, m_ref, v_ref, d_ref, mo_ref, vo_ref):
    mn = ADAM_B1 * m_ref[...] + (1.0 - ADAM_B1) * gv
    vn = ADAM_B2 * v_ref[...] + (1.0 - ADAM_B2) * (gv * gv)
    mo_ref[...] = mn
    vo_ref[...] = vn
    d_ref[...] = -ADAM_LR * ((mn / (1.0 - ADAM_B1 ** ADAM_STEP)) / (jnp.sqrt(vn / (1.0 - ADAM_B2 ** ADAM_STEP)) + ADAM_EPS)
                             + ADAM_WD * w_ref[...])


def _adamw(g, w, m, v, name):
    R, C = w.shape
    tr = _tile(R, 256, 16)

    def body(g_ref, w_ref, m_ref, v_ref, d_ref, mo_ref, vo_ref):
        _adam_update(g_ref[...], w_ref, m_ref, v_ref, d_ref, mo_ref, vo_ref)

    row = pl.BlockSpec((tr, C), lambda i: (i, 0))
    return pl.pallas_call(
        body, name=name, grid=(R // tr,), in_specs=[row] * 4, out_specs=[row] * 3,
        out_shape=[jax.ShapeDtypeStruct((R, C), _F32)] * 3, compiler_params=_seq(),
    )(g, w, m, v)


def _adamw_landed(zones, w, m, v, name):
    L = len(zones)
    _, R, C = zones[0].shape
    tr = _tile(R, 176, 16)
    nr = R // tr

    def body(*refs):
        w_ref, m_ref, v_ref, g_ref, d_ref, mo_ref, vo_ref = refs[L:]

        def update(g):
            g_ref[...] = g
            _adam_update(g, w_ref, m_ref, v_ref, d_ref, mo_ref, vo_ref)

        _per_layer(pl.program_id(0), refs[:L], update)

    row = pl.BlockSpec((tr, C), lambda l, i: (l * nr + i, 0))
    return pl.pallas_call(
        body, name=name, grid=(L, nr), in_specs=_landed_specs(L, tr, C, nr) + [row] * 3, out_specs=[row] * 4,
        out_shape=[jax.ShapeDtypeStruct((L * R, C), _F32)] * 4, compiler_params=_seq(2),
    )(*zones, w, m, v)


def _pack(parts):
    flat = jnp.concatenate([p.reshape(-1).astype(_F32) for p in parts])
    n = flat.shape[0]
    rows = -(-n // (8 * LANES)) * 8
    return jnp.pad(flat, (0, rows * LANES - n)).reshape(rows, LANES)


def _unpack(packed, shapes):
    flat, out, k = packed.reshape(-1), [], 0
    for s in shapes:
        n = 1
        for d in s:
            n *= d
        out.append(flat[k:k + n].reshape(s))
        k += n
    return out


def kernel(x, norm1_g, norm2_g, pool_w, pool_scale, kv_norm_g, w_kv, b_kv, w_q, b_q, sinks, w_o, b_o, ffn_up, ffn_conv_w, ffn_conv_b, ffn_down, final_g, loss_target, m_norm1_g, m_norm2_g, m_pool_w, m_pool_scale, m_kv_norm_g, m_w_kv, m_b_kv, m_w_q, m_b_q, m_sinks, m_w_o, m_b_o, m_ffn_up, m_ffn_conv_w, m_ffn_conv_b, m_ffn_down, m_final_g, v_norm1_g, v_norm2_g, v_pool_w, v_pool_scale, v_kv_norm_g, v_w_kv, v_b_kv, v_w_q, v_b_q, v_sinks, v_w_o, v_b_o, v_ffn_up, v_ffn_conv_w, v_ffn_conv_b, v_ffn_down, v_final_g):
    S = x.shape[1]
    F2s = ffn_up.shape[2]
    F2 = N_DEV * F2s
    me = _me()
    x0 = x.reshape(S, D)
    tgt = loss_target.reshape(S, D)
    row = lambda a: a.reshape(1, -1)

    small = _pack([pool_scale, ffn_conv_w])
    wire = lambda a: a.astype(_MXU)
    ffn_w = lambda l: [(wire(ffn_up[l]).T, 0), (wire(ffn_down[l]), 0)]
    attn_w = lambda j: [(wire(w_q[j]), 0), (wire(w_o[j]), 0)]
    gathers, token = _start_copies("gather_start", [_gather_group(g) for g in (
        [(wire(pool_w[0]), 1), (small[None], 0)], ffn_w(0), [(wire(pool_w[1]), 1)] + ffn_w(1),
        [(wire(w_kv), 0)] + attn_w(0), ffn_w(2), attn_w(1), ffn_w(3))])

    def gathered(k, after):
        return _wait_copies("gather_wait_%d" % k, [gathers[k]], after)[0]

    pw, up_t, down, wq, wo = [None] * N_A, [None] * DEPTH, [None] * DEPTH, [None] * 2, [None] * 2
    pw[0], small_all = gathered(0, token)
    n_ps = pool_scale.size
    small_all = small_all.reshape(N_DEV, -1)
    pscale = jnp.transpose(small_all[:, :n_ps].reshape(N_DEV, N_A, D // N_DEV), (1, 0, 2)).reshape(N_A, D)
    conv_w = jnp.transpose(small_all[:, n_ps:n_ps + ffn_conv_w.size].reshape(N_DEV, DEPTH, CONV_W, F2s),
                           (1, 2, 0, 3)).reshape(DEPTH, CONV_W, F2)

    def dup(a):
        a4 = a.reshape(a.shape[:-1] + (2 * N_KV, 1, HEAD_DIM))
        return jnp.broadcast_to(a4, a.shape[:-1] + (2 * N_KV, 2, HEAD_DIM)).reshape(a.shape[:-1] + (KVD,))

    def fold(a):
        return a.reshape(a.shape[:-1] + (2 * N_KV, 2, HEAD_DIM)).sum(axis=-2).reshape(a.shape[:-1] + (2 * N_KV * HEAD_DIM,))

    xs, us, qs, os_ = [x0], [], [], []
    xc = x0
    kvd = None
    for l in range(DEPTH):
        if l == 1:
            pw[1], up_t[1], down[1] = gathered(2, xc)
        if l == 3:
            wq[1], wo[1] = gathered(5, xc)
        if l < N_A:
            xc = _pool_fwd(xc, row(norm1_g[l]), pw[l], row(pscale[l]))
        else:
            j = l - N_A
            xc, q, o = _attn_fwd(xc, row(norm1_g[l]), wq[j], row(b_q[j]), sinks[j], kvd, wo[j], row(b_o[j]))
            qs.append(q)
            os_.append(o)
        xs.append(xc)
        if l != 1:
            up_t[l], down[l] = gathered((1, None, 4, 6)[l], xc)
        xc, u, c = _ffn_fwd(xc, row(norm2_g[l]), up_t[l], conv_w[l], row(ffn_conv_b[l]), down[l])
        us.append((u, c))
        xs.append(xc)
        if l == N_A - 1:
            wkv, wq[0], wo[0] = gathered(3, xc)
            wkv_d, bkv_d = dup(wkv), dup(row(b_kv))
            kvd = _kv_fwd(xc, row(kv_norm_g), wkv_d, bkv_d)

    dx, loss_p, d_final = _loss_bwd(xc, row(final_g), tgt)
    d_n1, d_n2, d_cw, d_cb = [None] * DEPTH, [None] * DEPTH, [None] * DEPTH, [None] * DEPTH
    d_bq, d_bo, d_sk, d_ps, dkv_parts = [None] * 2, [None] * 2, [None] * 2, [None] * N_A, []
    scatters = [None] * (2 * DEPTH)
    token = None

    def after(gain):
        return gain if token is None else gain + token[0:1, 0:1]

    for l in reversed(range(DEPTH)):
        x_in, x_mid, x_out = xs[2 * l], xs[2 * l + 1], xs[2 * l + 2]
        mixer_grads = []
        if l == N_A - 1:
            dx, d_wkv, d_bkv, d_kvg = _kv_bwd(x_out, dx, after(row(kv_norm_g)), wkv_d, *dkv_parts)
            mixer_grads.append((fold(d_wkv).astype(_MXU), 0))
        dy = dx
        dx, du, a, h, d_cw[l], d_cb[l], d_n2[l] = _ffn_bwd(
            x_mid, dy, *us[l], after(row(norm2_g[l])), up_t[l], conv_w[l], down[l])
        (scatters[2 * l],), token = _start_copies("scatter_ffn_%d" % l, [_scatter_group(
            [(_tn_matmul(du, h, "tn_up"), 0), (_tn_matmul(a, dy, "tn_down"), 0)])])
        dy = dx
        if l < N_A:
            dx, d_pw, d_ps[l], d_n1[l] = _pool_bwd(x_in, dy, after(row(norm1_g[l])), pw[l], row(pscale[l]))
            mixer_grads.append((d_pw.astype(_MXU), 1))
        else:
            j = l - N_A
            dx, dq, h, d_cur, d_prev, d_bq[j], d_bo[j], d_n1[l], d_sk[j] = _attn_bwd(
                x_in, dy, qs[j], after(row(norm1_g[l])), wq[j], sinks[j], kvd, wo[j])
            mixer_grads += [(_tn_matmul(h, dq, "tn_q"), 0), (_tn_matmul(os_[j], dy, "tn_o"), 0)]
            dkv_parts += [d_cur, d_prev]
        (scatters[2 * l + 1],), token = _start_copies("scatter_mixer_%d" % l, [_scatter_group(mixer_grads)])

    rep_names = ["norm1_g", "norm2_g", "kv_norm_g", "b_kv", "b_q", "sinks", "b_o", "ffn_conv_b", "final_g"]
    rep = [jnp.concatenate(d_n1), jnp.concatenate(d_n2), d_kvg, fold(d_bkv), jnp.concatenate(d_bq),
           jnp.concatenate([s[:, :N_HEADS] for s in d_sk]), jnp.concatenate(d_bo), jnp.concatenate(d_cb), d_final]
    full = [jnp.concatenate(d_ps), jnp.stack(d_cw), loss_p[0:1, 0:1]]
    given = dict(norm1_g=norm1_g, norm2_g=norm2_g, kv_norm_g=kv_norm_g, b_kv=b_kv, b_q=b_q, sinks=sinks, b_o=b_o,
                 ffn_conv_b=ffn_conv_b, final_g=final_g, pool_scale=pool_scale, ffn_conv_w=ffn_conv_w)
    tot = _unpack(_all_reduce_small(_pack(rep + full)),
                  [given[k].shape for k in rep_names] + [(N_A, D), (DEPTH, CONV_W, F2), ()])
    grad = dict(zip(rep_names, tot))
    loss = tot[-1]
    grad["pool_scale"] = lax.dynamic_slice_in_dim(tot[-3], me * (D // N_DEV), D // N_DEV, axis=1)
    grad["ffn_conv_w"] = lax.dynamic_slice_in_dim(tot[-2], me * F2s, F2s, axis=2)

    moms = dict(norm1_g=(m_norm1_g, v_norm1_g), norm2_g=(m_norm2_g, v_norm2_g), pool_w=(m_pool_w, v_pool_w),
                pool_scale=(m_pool_scale, v_pool_scale), kv_norm_g=(m_kv_norm_g, v_kv_norm_g), w_kv=(m_w_kv, v_w_kv),
                b_kv=(m_b_kv, v_b_kv), w_q=(m_w_q, v_w_q), b_q=(m_b_q, v_b_q), sinks=(m_sinks, v_sinks),
                w_o=(m_w_o, v_w_o), b_o=(m_b_o, v_b_o), ffn_up=(m_ffn_up, v_ffn_up),
                ffn_conv_w=(m_ffn_conv_w, v_ffn_conv_w), ffn_conv_b=(m_ffn_conv_b, v_ffn_conv_b),
                ffn_down=(m_ffn_down, v_ffn_down), final_g=(m_final_g, v_final_g))
    given.update(pool_w=pool_w, w_kv=w_kv, w_q=w_q, w_o=w_o, ffn_up=ffn_up, ffn_down=ffn_down)
    delta, new_m, new_v = {}, {}, {}

    small_names = rep_names + ["pool_scale", "ffn_conv_w"]
    shapes = [given[k].shape for k in small_names]
    outs = _adamw(_pack([grad[k] for k in small_names]), _pack([given[k] for k in small_names]),
                  _pack([moms[k][0] for k in small_names]), _pack([moms[k][1] for k in small_names]), "adamw_small")
    for dst, packed in zip((delta, new_m, new_v), outs):
        dst.update(zip(small_names, _unpack(packed, shapes)))

    zones = _wait_copies("scatter_wait", scatters, outs[0])
    ffn_z, mix_z = zones[0::2], zones[1::2]
    by_name = dict(ffn_up=[z[0] for z in ffn_z], ffn_down=[z[1] for z in ffn_z],
                   w_q=[mix_z[N_A][0], mix_z[N_A + 1][0]], w_o=[mix_z[N_A][1], mix_z[N_A + 1][1]],
                   w_kv=[mix_z[N_A - 1][0]], pool_w=[mix_z[0][0], mix_z[N_A - 1][1]])

    def update(name, g, cols):
        w = given[name]
        two_d = lambda a: a.reshape(-1, cols)
        if g is None:
            landed = [z.reshape(N_DEV, -1, cols) for z in by_name[name]]
            outs = _adamw_landed(landed, two_d(w), two_d(moms[name][0]), two_d(moms[name][1]), "adamw_" + name)
            grad[name] = outs[0].reshape(w.shape)
        else:
            outs = _adamw(g, two_d(w), two_d(moms[name][0]), two_d(moms[name][1]), "adamw_" + name)
        delta[name], new_m[name], new_v[name] = (o.reshape(w.shape) for o in outs[-3:])

    grad["ffn_up"] = jnp.swapaxes(_sum8(by_name["ffn_up"]).reshape(DEPTH, F2s, D), 1, 2)
    update("ffn_up", grad["ffn_up"].reshape(-1, F2s), F2s)
    update("ffn_down", None, D)
    update("w_q", None, D)
    update("w_o", None, D)
    update("w_kv", None, w_kv.shape[1])
    update("pool_w", None, GC)

    names = ["norm1_g", "norm2_g", "pool_w", "pool_scale", "kv_norm_g", "w_kv", "b_kv", "w_q", "b_q", "sinks", "w_o",
             "b_o", "ffn_up", "ffn_conv_w", "ffn_conv_b", "ffn_down", "final_g"]
    return (loss, dx.reshape(x.shape), *[grad[k] for k in names], *[delta[k] for k in names],
            *[new_m[k] for k in names], *[new_v[k] for k in names])
```

```python
import functools

import jax
import jax.numpy as jnp
from jax import lax
from jax.experimental import pallas as pl
from jax.experimental.pallas import tpu as pltpu

_F32 = jnp.float32
_MXU = jnp.bfloat16

N_DEV = 8
D = 1024
DEPTH = 4
N_A = 2
POOL_WINDOWS = (2, 4, 8, 16)
GC = D // len(POOL_WINDOWS)
HALO = 16
HEAD_DIM = 64
N_HEADS = D // HEAD_DIM
GROUP = 8
N_KV = N_HEADS // GROUP
BLK = 128
PAIR = 2 * HEAD_DIM
KVD = 4 * N_KV * HEAD_DIM
CONV_W = 3
EPS = 1e-5
NEG = -1e30

ADAM_LR = 0.001
ADAM_B1 = 0.9
ADAM_B2 = 0.999
ADAM_EPS = 1e-08
ADAM_WD = 0.01
ADAM_STEP = 10

V7X_VMEM_LIMIT = 56 * 1024 * 1024
LANES = 128

_NT = (((1,), (1,)), ((), ()))
_TN = (((0,), (0,)), ((), ()))


def _params(**kw):
    return pltpu.CompilerParams(vmem_limit_bytes=V7X_VMEM_LIMIT, **kw)


def _seq(n=1):
    return _params(dimension_semantics=("arbitrary",) * n)


def _dot(a, b, dims=None):
    if dims is None:
        return jnp.dot(a, b, preferred_element_type=_F32)
    return lax.dot_general(a, b, dims, preferred_element_type=_F32)


def _rms(x):
    r = lax.rsqrt(jnp.mean(x * x, axis=-1, keepdims=True) + EPS)
    return x * r, r


def _rms_bwd(dh, xn, r, g):
    dxn = dh * g
    return r * (dxn - xn * jnp.mean(dxn * xn, axis=-1, keepdims=True))


def _colsum(a):
    return jnp.sum(a, axis=0, keepdims=True)


def _tile(n, want, mult=8):
    for t in range(min(want, n), 0, -1):
        if n % t == 0 and t % mult == 0:
            return t
    return n


def _full(shape):
    zeros = (0,) * len(shape)
    return pl.BlockSpec(shape, lambda *_: zeros)


def _pool_windows(hbuf, h, row, T):
    out = []
    for gi, win in enumerate(POOL_WINDOWS):
        cs = slice(gi * GC, (gi + 1) * GC)
        acc = hbuf[HALO:HALO + T, cs]
        for k in range(1, win):
            acc = acc + hbuf[HALO - k:HALO - k + T, cs]
        cnt = jnp.minimum(row + 1, win).astype(_F32)
        out.append((acc / cnt - h[:, cs], cnt))
    return out


def _pool_fwd(x, g, w, sc):
    S = x.shape[0]
    T = _tile(S, 512, HALO)
    n, hb = S // T, T // HALO

    def body(x_ref, xh_ref, g_ref, w_ref, sc_ref, o_ref, hbuf):
        i = pl.program_id(0)
        gv = g_ref[...]
        xv = x_ref[...]
        h = _rms(xv)[0] * gv
        hbuf[0:HALO, :] = jnp.where(i > 0, _rms(xh_ref[...])[0] * gv, 0.0)
        hbuf[HALO:, :] = h
        row = i * T + lax.broadcasted_iota(jnp.int32, (T, 1), 0)
        for gi, (p, _) in enumerate(_pool_windows(hbuf, h, row, T)):
            cs = slice(gi * GC, (gi + 1) * GC)
            z = _dot(p.astype(_MXU), w_ref[gi])
            o_ref[:, cs] = xv[:, cs] + z * sc_ref[:, cs]

    return pl.pallas_call(
        body, name="pool_fwd", grid=(n,),
        in_specs=[pl.BlockSpec((T, D), lambda i: (i, 0)),
                  pl.BlockSpec((HALO, D), lambda i: (jnp.maximum(i * hb - 1, 0), 0)),
                  _full((1, D)), _full((4, GC, GC)), _full((1, D))],
        out_specs=pl.BlockSpec((T, D), lambda i: (i, 0)),
        out_shape=jax.ShapeDtypeStruct((S, D), _F32),
        scratch_shapes=[pltpu.VMEM((T + HALO, D), _F32)],
        compiler_params=_seq(),
    )(x, x, g, w, sc)


def _pool_bwd(x, dy, g, w, sc):
    S = x.shape[0]
    T = _tile(S, 512, HALO)
    n, hb = S // T, T // HALO

    def body(x_ref, xh_ref, dy_ref, dyh_ref, g_ref, w_ref, sc_ref, dx_ref, dw_ref, dsc_ref, dg_ref,
             hbuf, qbuf, dhbuf):
        i = pl.program_id(0)

        @pl.when(i == 0)
        def _():
            dw_ref[...] = jnp.zeros_like(dw_ref)
            dsc_ref[...] = jnp.zeros_like(dsc_ref)
            dg_ref[...] = jnp.zeros_like(dg_ref)

        gv = g_ref[...]
        xv = x_ref[...]
        xn, r = _rms(xv)
        h = xn * gv
        hbuf[0:HALO, :] = jnp.where(i > 0, _rms(xh_ref[...])[0] * gv, 0.0)
        hbuf[HALO:, :] = h
        dyv = dy_ref[...]
        dz = dyv * sc_ref[...]
        dzh = jnp.where(i < n - 1, dyh_ref[...], 0.0) * sc_ref[...]
        row = i * T + lax.broadcasted_iota(jnp.int32, (T, 1), 0)
        rowh = (i + 1) * T + lax.broadcasted_iota(jnp.int32, (HALO, 1), 0)
        for gi, (p, cnt) in enumerate(_pool_windows(hbuf, h, row, T)):
            win = POOL_WINDOWS[gi]
            cs = slice(gi * GC, (gi + 1) * GC)
            pb = p.astype(_MXU)
            wg = w_ref[gi]
            dsc_ref[:, cs] += _colsum(dyv[:, cs] * _dot(pb, wg))
            dzb = dz[:, cs].astype(_MXU)
            dw_ref[gi] += _dot(pb, dzb, _TN)
            dp = _dot(dzb, wg, _NT)
            dph = _dot(dzh[:, cs].astype(_MXU), wg, _NT)
            qbuf[0:T, cs] = dp / cnt
            qbuf[T:T + HALO, cs] = dph / jnp.minimum(rowh + 1, win).astype(_F32)
            acc = qbuf[0:T, cs]
            for k in range(1, win):
                acc = acc + qbuf[k:k + T, cs]
            dhbuf[:, cs] = acc - dp
        dh = dhbuf[...]
        dg_ref[...] += _colsum(dh * xn)
        dx_ref[...] = dyv + _rms_bwd(dh, xn, r, gv)

    return pl.pallas_call(
        body, name="pool_bwd", grid=(n,),
        in_specs=[pl.BlockSpec((T, D), lambda i: (i, 0)),
                  pl.BlockSpec((HALO, D), lambda i: (jnp.maximum(i * hb - 1, 0), 0)),
                  pl.BlockSpec((T, D), lambda i: (i, 0)),
                  pl.BlockSpec((HALO, D), lambda i: (jnp.minimum((i + 1) * hb, S // HALO - 1), 0)),
                  _full((1, D)), _full((4, GC, GC)), _full((1, D))],
        out_specs=[pl.BlockSpec((T, D), lambda i: (i, 0)), _full((4, GC, GC)), _full((1, D)), _full((1, D))],
        out_shape=[jax.ShapeDtypeStruct((S, D), _F32), jax.ShapeDtypeStruct((4, GC, GC), _F32),
                   jax.ShapeDtypeStruct((1, D), _F32), jax.ShapeDtypeStruct((1, D), _F32)],
        scratch_shapes=[pltpu.VMEM((T + HALO, D), _F32), pltpu.VMEM((T + HALO, D), _F32), pltpu.VMEM((T, D), _F32)],
        compiler_params=_seq(),
    )(x, x, dy, dy, g, w, sc)


FFN_FWD_TILE, FFN_FWD_CHUNKS = 256, 2
FFN_BWD_TILE, FFN_BWD_CHUNKS = 128, 2
EDGE = 8


def _shift_down(v, k, prev):
    r = pltpu.roll(v, k, axis=0)
    i8 = lax.broadcasted_iota(jnp.int32, (EDGE, v.shape[1]), 0)
    head = jnp.where(i8 >= k, r[0:EDGE, :], pltpu.roll(prev, k, axis=0))
    return jnp.concatenate([head, r[EDGE:, :]], axis=0)


def _shift_up(v, k, nxt):
    T = v.shape[0]
    r = pltpu.roll(v, T - k, axis=0)
    i8 = lax.broadcasted_iota(jnp.int32, (EDGE, v.shape[1]), 0)
    tail = jnp.where(i8 < EDGE - k, r[T - EDGE:, :], pltpu.roll(nxt, EDGE - k, axis=0))
    return jnp.concatenate([r[:T - EDGE, :], tail], axis=0)


def _load_weights(i, pairs, sems):
    @pl.when(i == 0)
    def _():
        cps = [pltpu.make_async_copy(src, dst, sems.at[k]) for k, (src, dst) in enumerate(pairs)]
        for cp in cps:
            cp.start()
        for cp in cps:
            cp.wait()


def _ffn_fwd(x, g, wup_t, cw, cb, wdn):
    S = x.shape[0]
    F2 = wup_t.shape[0]
    F = F2 // 2
    C = F // FFN_FWD_CHUNKS
    T = _tile(S, FFN_FWD_TILE, 16)
    n = S // T

    def body(x_ref, g_ref, wup_hbm, cw_ref, cb_ref, wdn_hbm, o_ref, u_ref, c_ref, wup, wdnv, carry, sems):
        i = pl.program_id(0)
        _load_weights(i, [(wup_hbm, wup), (wdn_hbm, wdnv)], sems)

        @pl.when(i == 0)
        def _():
            carry[...] = jnp.zeros_like(carry)

        xv = x_ref[...]
        hb = (_rms(xv)[0] * g_ref[...]).astype(_MXU)
        acc = jnp.zeros((T, D), _F32)
        for j in range(FFN_FWD_CHUNKS):
            halves = []
            for cs in (slice(j * C, (j + 1) * C), slice(F + j * C, F + (j + 1) * C)):
                u = _dot(hb, wup[cs, :], _NT)
                u_ref[:, cs] = u.astype(u_ref.dtype)
                prev = carry[:, cs]
                carry[:, cs] = u[T - EDGE:, :]
                c = (cw_ref[0:1, cs] * _shift_down(u, 2, prev) + cw_ref[1:2, cs] * _shift_down(u, 1, prev)
                     + cw_ref[2:3, cs] * u + cb_ref[:, cs])
                c_ref[:, cs] = c.astype(c_ref.dtype)
                halves.append(c)
            cg, cv = halves
            a = (cg * jax.nn.sigmoid(cg)) * cv
            acc = acc + _dot(a.astype(_MXU), wdnv[j * C:(j + 1) * C, :])
        o_ref[...] = xv + acc

    any_ = pl.BlockSpec(memory_space=pl.ANY)
    wide = pl.BlockSpec((T, F2), lambda i: (i, 0))
    return pl.pallas_call(
        body, name="ffn_fwd", grid=(n,),
        in_specs=[pl.BlockSpec((T, D), lambda i: (i, 0)), _full((1, D)), any_, _full((CONV_W, F2)), _full((1, F2)), any_],
        out_specs=[pl.BlockSpec((T, D), lambda i: (i, 0)), wide, wide],
        out_shape=[jax.ShapeDtypeStruct((S, D), _F32), jax.ShapeDtypeStruct((S, F2), _MXU),
                   jax.ShapeDtypeStruct((S, F2), _MXU)],
        scratch_shapes=[pltpu.VMEM((F2, D), _MXU), pltpu.VMEM((F, D), _MXU),
                        pltpu.VMEM((EDGE, F2), _F32), pltpu.SemaphoreType.DMA((2,))],
        compiler_params=_seq(),
    )(x, g, wup_t, cw, cb, wdn)


def _ffn_bwd(x, dy, u, c, g, wup_t, cw, wdn):
    S = x.shape[0]
    F2 = wup_t.shape[0]
    F = F2 // 2
    C = F // FFN_BWD_CHUNKS
    T = _tile(S, FFN_BWD_TILE, 16)
    n = S // T

    def body(x_ref, dy_ref, u_ref, c_ref, g_ref, wup_hbm, cw_ref, wdn_hbm,
             dx_ref, du_ref, a_ref, h_ref, dcw_ref, dcb_ref, dg_ref, wup, wdnv, carry, sems):
        i = pl.program_id(0)
        _load_weights(i, [(wup_hbm, wup), (wdn_hbm, wdnv)], sems)

        @pl.when(i == 0)
        def _():
            carry[...] = jnp.zeros_like(carry)
            dcw_ref[...] = jnp.zeros_like(dcw_ref)
            dcb_ref[...] = jnp.zeros_like(dcb_ref)
            dg_ref[...] = jnp.zeros_like(dg_ref)

        gv = g_ref[...]
        xv = x_ref[...]
        xn, r = _rms(xv)
        hbf = (xn * gv).astype(_MXU)
        h_ref[...] = hbf
        dyv = dy_ref[...]
        dyb = dyv.astype(_MXU)
        dh = jnp.zeros((T, D), _F32)
        for j in range(FFN_BWD_CHUNKS):
            gs, vs = slice(j * C, (j + 1) * C), slice(F + j * C, F + (j + 1) * C)
            cg, cv = c_ref[:, gs].astype(_F32), c_ref[:, vs].astype(_F32)
            sg = jax.nn.sigmoid(cg)
            sl = cg * sg
            a_ref[:, gs] = (sl * cv).astype(a_ref.dtype)
            da = _dot(dyb, wdnv[gs, :], _NT)
            for cs, dc in ((gs, da * cv * (sg * (1.0 + cg * (1.0 - sg)))), (vs, da * sl)):
                nxt = carry[:, cs]
                carry[:, cs] = dc[0:EDGE, :]
                dc1, dc2 = _shift_up(dc, 1, nxt), _shift_up(dc, 2, nxt)
                uf = u_ref[:, cs].astype(_F32)
                dcb_ref[:, cs] += _colsum(dc)
                for k, d in enumerate((dc2, dc1, dc)):
                    dcw_ref[k:k + 1, cs] += _colsum(d * uf)
                du = cw_ref[2:3, cs] * dc + cw_ref[1:2, cs] * dc1 + cw_ref[0:1, cs] * dc2
                dub = du.astype(_MXU)
                du_ref[:, cs] = dub
                dh = dh + _dot(dub, wup[cs, :])
        dg_ref[...] += _colsum(dh * xn)
        dx_ref[...] = dyv + _rms_bwd(dh, xn, r, gv)

    any_ = pl.BlockSpec(memory_space=pl.ANY)
    rev = lambda i: (n - 1 - i, 0)
    return pl.pallas_call(
        body, name="ffn_bwd", grid=(n,),
        in_specs=[pl.BlockSpec((T, D), rev), pl.BlockSpec((T, D), rev), pl.BlockSpec((T, F2), rev),
                  pl.BlockSpec((T, F2), rev), _full((1, D)), any_, _full((CONV_W, F2)), any_],
        out_specs=[pl.BlockSpec((T, D), rev), pl.BlockSpec((T, F2), rev), pl.BlockSpec((T, F), rev),
                   pl.BlockSpec((T, D), rev), _full((CONV_W, F2)), _full((1, F2)), _full((1, D))],
        out_shape=[jax.ShapeDtypeStruct((S, D), _F32), jax.ShapeDtypeStruct((S, F2), _MXU),
                   jax.ShapeDtypeStruct((S, F), _MXU), jax.ShapeDtypeStruct((S, D), _MXU),
                   jax.ShapeDtypeStruct((CONV_W, F2), _F32), jax.ShapeDtypeStruct((1, F2), _F32),
                   jax.ShapeDtypeStruct((1, D), _F32)],
        scratch_shapes=[pltpu.VMEM((F2, D), _MXU), pltpu.VMEM((F, D), _MXU),
                        pltpu.VMEM((EDGE, F2), _F32), pltpu.SemaphoreType.DMA((2,))],
        compiler_params=_seq(),
    )(x, dy, u, c, g, wup_t, cw, wdn)


def _tn_matmul(a, b, name):
    S, M = a.shape
    N = b.shape[1]
    bm = _tile(M, 1408, LANES)
    tk = _tile(S, 512, 16)
    nk = S // tk

    def body(a_ref, b_ref, o_ref, acc):
        k = pl.program_id(1)

        @pl.when(k == 0)
        def _():
            acc[...] = jnp.zeros_like(acc)

        acc[...] += _dot(a_ref[...].astype(_MXU), b_ref[...].astype(_MXU), _TN)

        @pl.when(k == nk - 1)
        def _():
            o_ref[...] = acc[...].astype(o_ref.dtype)

    return pl.pallas_call(
        body, name=name, grid=(M // bm, nk),
        in_specs=[pl.BlockSpec((tk, bm), lambda i, k: (k, i)), pl.BlockSpec((tk, N), lambda i, k: (k, 0))],
        out_specs=pl.BlockSpec((bm, N), lambda i, k: (i, 0)),
        out_shape=jax.ShapeDtypeStruct((M, N), _MXU),
        scratch_shapes=[pltpu.VMEM((bm, N), _F32)],
        compiler_params=_seq(2),
    )(a, b)


def _kv_fwd(x, g, wkv, bkv):
    S = x.shape[0]
    T = _tile(S, 512, 16)

    def body(x_ref, g_ref, w_ref, b_ref, o_ref):
        hb = (_rms(x_ref[...])[0] * g_ref[...]).astype(_MXU)
        o_ref[...] = (_dot(hb, w_ref[...]) + b_ref[...]).astype(o_ref.dtype)

    return pl.pallas_call(
        body, name="kv_fwd", grid=(S // T,),
        in_specs=[pl.BlockSpec((T, D), lambda i: (i, 0)), _full((1, D)), _full((D, KVD)), _full((1, KVD))],
        out_specs=pl.BlockSpec((T, KVD), lambda i: (i, 0)),
        out_shape=jax.ShapeDtypeStruct((S, KVD), _MXU),
        compiler_params=_seq(),
    )(x, g, wkv, bkv)


def _kv_bwd(x, dx_in, g, wkv, cur_a, prev_a, cur_b, prev_b):
    S = x.shape[0]
    n = S // BLK

    def body(x_ref, dxi_ref, g_ref, w_ref, ca, pa, cb, pb, dx_ref, dw_ref, db_ref, dg_ref):
        i = pl.program_id(0)

        @pl.when(i == 0)
        def _():
            dw_ref[...] = jnp.zeros_like(dw_ref)
            db_ref[...] = jnp.zeros_like(db_ref)
            dg_ref[...] = jnp.zeros_like(dg_ref)

        gv = g_ref[...]
        xn, r = _rms(x_ref[...])
        dkv = ca[...] + cb[...] + jnp.where(i < n - 1, pa[...] + pb[...], 0.0)
        db_ref[...] += _colsum(dkv)
        dkb = dkv.astype(_MXU)
        dw_ref[...] += _dot((xn * gv).astype(_MXU), dkb, _TN)
        dh = _dot(dkb, w_ref[...], _NT)
        dg_ref[...] += _colsum(dh * xn)
        dx_ref[...] = dxi_ref[...] + _rms_bwd(dh, xn, r, gv)

    blk = lambda w: pl.BlockSpec((BLK, w), lambda i: (i, 0))
    nxt = pl.BlockSpec((BLK, KVD), lambda i: (jnp.minimum(i + 1, n - 1), 0))
    return pl.pallas_call(
        body, name="kv_bwd", grid=(n,),
        in_specs=[blk(D), blk(D), _full((1, D)), _full((D, KVD)), blk(KVD), nxt, blk(KVD), nxt],
        out_specs=[blk(D), _full((D, KVD)), _full((1, KVD)), _full((1, D))],
        out_shape=[jax.ShapeDtypeStruct((S, D), _F32), jax.ShapeDtypeStruct((D, KVD), _F32),
                   jax.ShapeDtypeStruct((1, KVD), _F32), jax.ShapeDtypeStruct((1, D), _F32)],
        compiler_params=_seq(),
    )(x, dx_in, g, wkv, cur_a, prev_a, cur_b, prev_b)


STACK = GROUP * BLK


def _attn_mask(i, rows):
    qi = lax.broadcasted_iota(jnp.int32, (rows, 2 * BLK), 0) & (BLK - 1)
    si = lax.broadcasted_iota(jnp.int32, (rows, 2 * BLK), 1)
    return (si > qi) & (si <= qi + BLK) & jnp.logical_or(i > 0, si >= BLK)


def _low_half():
    return lax.broadcasted_iota(jnp.int32, (BLK, PAIR), 1) < HEAD_DIM


def _stack_heads(ref, kh, dst):
    low = _low_half()
    for pp in range(GROUP // 2):
        pr = kh * (GROUP // 2) + pp
        v2 = ref[:, pr * PAIR:(pr + 1) * PAIR]
        zero = jnp.zeros_like(v2)
        dst[2 * pp * BLK:(2 * pp + 1) * BLK, :] = jnp.where(low, v2, zero)
        dst[(2 * pp + 1) * BLK:(2 * pp + 2) * BLK, :] = jnp.where(low, zero, v2)


def _unstack_heads(st, pp):
    return jnp.where(_low_half(), st[2 * pp * BLK:(2 * pp + 1) * BLK, :], st[(2 * pp + 1) * BLK:(2 * pp + 2) * BLK, :])


def _sink_col(sk_ref, kh):
    return jnp.concatenate([jnp.full((BLK, 1), sk_ref[kh * GROUP + h], _F32) for h in range(GROUP)], axis=0)


def _head_probs(qm, kd, mask, sink):
    s = jnp.where(mask, _dot(qm, kd, _NT) * (HEAD_DIM ** -0.5), NEG)
    m = jnp.maximum(jnp.max(s, axis=-1, keepdims=True), sink)
    p = jnp.exp(s - m)
    es = jnp.exp(sink - m)
    inv = 1.0 / (jnp.sum(p, axis=-1, keepdims=True) + es)
    return p * inv, es * inv


def _attn_fwd(x, g, wq, bq, sinks, kvd, wo, bo):
    S = x.shape[0]
    n = S // BLK

    def body(x_ref, g_ref, wq_ref, bq_ref, sk_ref, kp_ref, kc_ref, wo_ref, bo_ref, xo_ref, q_ref, o_ref, win):
        i = pl.program_id(0)
        xv = x_ref[...]
        hb = (_rms(xv)[0] * g_ref[...]).astype(_MXU)
        q_ref[...] = (_dot(hb, wq_ref[...]) + bq_ref[...]).astype(q_ref.dtype)
        win[0:BLK, :] = kp_ref[...]
        win[BLK:, :] = kc_ref[...]
        mask = _attn_mask(i, BLK)
        low = _low_half()
        for pr in range(N_HEADS // 2):
            kh = (2 * pr) // GROUP
            kd = win[:, kh * PAIR:(kh + 1) * PAIR]
            vd = win[:, (N_KV + kh) * PAIR:(N_KV + kh + 1) * PAIR]
            q2 = q_ref[:, pr * PAIR:(pr + 1) * PAIR]
            outs = []
            for half in range(2):
                qm = jnp.where(low if half == 0 else ~low, q2, jnp.zeros_like(q2))
                pbs, _ = _head_probs(qm, kd, mask, sk_ref[2 * pr + half])
                outs.append(_dot(pbs.astype(_MXU), vd))
            o_ref[:, pr * PAIR:(pr + 1) * PAIR] = jnp.where(low, outs[0], outs[1]).astype(o_ref.dtype)
        xo_ref[...] = xv + _dot(o_ref[...], wo_ref[...]) + bo_ref[...]

    blk = lambda w: pl.BlockSpec((BLK, w), lambda i: (i, 0))
    return pl.pallas_call(
        body, name="attn_fwd", grid=(n,),
        in_specs=[blk(D), _full((1, D)), _full((D, D)), _full((1, D)),
                  pl.BlockSpec(memory_space=pltpu.SMEM),
                  pl.BlockSpec((BLK, KVD), lambda i: (jnp.maximum(i - 1, 0), 0)), blk(KVD),
                  _full((D, D)), _full((1, D))],
        out_specs=[blk(D), blk(D), blk(D)],
        out_shape=[jax.ShapeDtypeStruct((S, D), _F32), jax.ShapeDtypeStruct((S, D), _MXU),
                   jax.ShapeDtypeStruct((S, D), _MXU)],
        scratch_shapes=[pltpu.VMEM((2 * BLK, KVD), _MXU)],
        compiler_params=_seq(),
    )(x, g, wq, bq, sinks, kvd, kvd, wo, bo)


def _attn_bwd(x, dy, q, g, wq, sinks, kvd, wo):
    S = x.shape[0]
    n = S // BLK

    def body(x_ref, dy_ref, q_ref, g_ref, wq_ref, sk_ref, kp_ref, kc_ref, wo_ref,
             dx_ref, dq_ref, h_ref, dc_ref, dp_ref, dbq_ref, dbo_ref, dg_ref, dsk_ref, win, dob, qs, dos):
        i = pl.program_id(0)

        @pl.when(i == 0)
        def _():
            dbq_ref[...] = jnp.zeros_like(dbq_ref)
            dbo_ref[...] = jnp.zeros_like(dbo_ref)
            dg_ref[...] = jnp.zeros_like(dg_ref)
            dsk_ref[...] = jnp.zeros_like(dsk_ref)

        gv = g_ref[...]
        xv = x_ref[...]
        xn, r = _rms(xv)
        h_ref[...] = (xn * gv).astype(h_ref.dtype)
        dyv = dy_ref[...]
        dbo_ref[...] += _colsum(dyv)
        dob[...] = _dot(dyv.astype(_MXU), wo_ref[...], _NT).astype(dob.dtype)
        win[0:BLK, :] = kp_ref[...]
        win[BLK:, :] = kc_ref[...]
        mask = _attn_mask(i, STACK)
        lane = lax.broadcasted_iota(jnp.int32, (1, LANES), 1)
        dq_all = []
        for kh in range(N_KV):
            ks = slice(kh * PAIR, (kh + 1) * PAIR)
            vs = slice((N_KV + kh) * PAIR, (N_KV + kh + 1) * PAIR)
            kd, vd = win[:, ks], win[:, vs]
            _stack_heads(q_ref, kh, qs)
            _stack_heads(dob, kh, dos)
            pbs, ps = _head_probs(qs[...], kd, mask, _sink_col(sk_ref, kh))
            dpr = _dot(dos[...], vd, _NT)
            delta = jnp.sum(pbs * dpr, axis=-1, keepdims=True)
            dsb = (pbs * (dpr - delta) * (HEAD_DIM ** -0.5)).astype(_MXU)
            dqst = _dot(dsb, kd)
            dk = _dot(dsb, qs[...], _TN)
            dv = _dot(pbs.astype(_MXU), dos[...], _TN)
            dp_ref[:, ks], dc_ref[:, ks] = dk[0:BLK, :], dk[BLK:, :]
            dp_ref[:, vs], dc_ref[:, vs] = dv[0:BLK, :], dv[BLK:, :]
            sd = ps * delta
            for hh in range(GROUP):
                dsk_ref[...] -= jnp.where(lane == kh * GROUP + hh, _colsum(sd[hh * BLK:(hh + 1) * BLK, :]), 0.0)
            dq_all += [_unstack_heads(dqst, pp) for pp in range(GROUP // 2)]
        dq = jnp.concatenate(dq_all, axis=1)
        dbq_ref[...] += _colsum(dq)
        dqb = dq.astype(_MXU)
        dq_ref[...] = dqb
        dh = _dot(dqb, wq_ref[...], _NT)
        dg_ref[...] += _colsum(dh * xn)
        dx_ref[...] = dyv + _rms_bwd(dh, xn, r, gv)

    blk = lambda w: pl.BlockSpec((BLK, w), lambda i: (i, 0))
    return pl.pallas_call(
        body, name="attn_bwd", grid=(n,),
        in_specs=[blk(D), blk(D), blk(D), _full((1, D)), _full((D, D)),
                  pl.BlockSpec(memory_space=pltpu.SMEM),
                  pl.BlockSpec((BLK, KVD), lambda i: (jnp.maximum(i - 1, 0), 0)), blk(KVD), _full((D, D))],
        out_specs=[blk(D), blk(D), blk(D), blk(KVD), blk(KVD),
                   _full((1, D)), _full((1, D)), _full((1, D)), _full((1, LANES))],
        out_shape=[jax.ShapeDtypeStruct((S, D), _F32), jax.ShapeDtypeStruct((S, D), _MXU),
                   jax.ShapeDtypeStruct((S, D), _MXU), jax.ShapeDtypeStruct((S, KVD), _F32),
                   jax.ShapeDtypeStruct((S, KVD), _F32), jax.ShapeDtypeStruct((1, D), _F32),
                   jax.ShapeDtypeStruct((1, D), _F32), jax.ShapeDtypeStruct((1, D), _F32),
                   jax.ShapeDtypeStruct((1, LANES), _F32)],
        scratch_shapes=[pltpu.VMEM((2 * BLK, KVD), _MXU), pltpu.VMEM((BLK, D), _MXU),
                        pltpu.VMEM((STACK, PAIR), _MXU), pltpu.VMEM((STACK, PAIR), _MXU)],
        compiler_params=_seq(),
    )(x, dy, q, g, wq, sinks, kvd, kvd, wo)


def _loss_bwd(x, g, tgt):
    S = x.shape[0]
    T = _tile(S, 512, 8)

    def body(x_ref, g_ref, t_ref, dx_ref, ls_ref, dg_ref):
        @pl.when(pl.program_id(0) == 0)
        def _():
            ls_ref[...] = jnp.zeros_like(ls_ref)
            dg_ref[...] = jnp.zeros_like(dg_ref)

        gv = g_ref[...]
        xn, r = _rms(x_ref[...])
        err = xn * gv - t_ref[...]
        ls_ref[...] += 0.5 * jnp.sum(jnp.mean(err * err, axis=-1, keepdims=True))
        dyv = err * (1.0 / D)
        dg_ref[...] += _colsum(dyv * xn)
        dx_ref[...] = _rms_bwd(dyv, xn, r, gv)

    return pl.pallas_call(
        body, name="loss_bwd", grid=(S // T,),
        in_specs=[pl.BlockSpec((T, D), lambda i: (i, 0)), _full((1, D)), pl.BlockSpec((T, D), lambda i: (i, 0))],
        out_specs=[pl.BlockSpec((T, D), lambda i: (i, 0)), _full((8, LANES)), _full((1, D))],
        out_shape=[jax.ShapeDtypeStruct((S, D), _F32), jax.ShapeDtypeStruct((8, LANES), _F32),
                   jax.ShapeDtypeStruct((1, D), _F32)],
        compiler_params=_seq(),
    )(x, g, tgt)


def _me():
    return 4 * lax.axis_index("x") + 2 * lax.axis_index("y") + lax.axis_index("c")


def _peer(j):
    x, y, c = lax.axis_index("x"), lax.axis_index("y"), lax.axis_index("c")
    px = 1 - x if j & 4 else x
    py = 1 - y if j & 2 else y
    pc = 1 - c if j & 1 else c
    return (px, py, pc), 4 * px + 2 * py + pc


_HBM = pl.BlockSpec(memory_space=pltpu.HBM)
_SEMS = pl.BlockSpec(memory_space=pltpu.SEMAPHORE)
_EFFECT = pltpu.SideEffectType.DATAFLOW_SIDE_EFFECTING


def _in_hbm(a):
    return pltpu.with_memory_space_constraint(a, pltpu.HBM)


def _start_copies(name, groups):
    flat = []
    for srcs, zones, _ in groups:
        flat += [_in_hbm(a) for a in srcs] + [_in_hbm(lax.empty(z.shape, z.dtype)) for z in zones]
    n_in, n_g = len(flat), len(groups)

    def body(*refs):
        sems = refs[n_in:n_in + 2 * n_g]
        me, k = _me(), 0
        for gi, (srcs, zones, plan) in enumerate(groups):
            src_refs, zone_refs = refs[k:k + len(srcs)], refs[k + len(srcs):k + len(srcs) + len(zones)]
            k += len(srcs) + len(zones)
            for t, (si, zi, src_of, dst_of) in enumerate(plan):
                for j in range(1, N_DEV):
                    dev, pk = _peer(j)
                    pltpu.make_async_remote_copy(
                        src_ref=src_of(src_refs[si], pk), dst_ref=dst_of(zone_refs[zi], me),
                        send_sem=sems[2 * gi].at[t * (N_DEV - 1) + j - 1], recv_sem=sems[2 * gi + 1].at[t * (N_DEV - 1) + j - 1],
                        device_id=dev, device_id_type=pl.DeviceIdType.MESH).start()
                pltpu.make_async_copy(src_of(src_refs[si], me), dst_of(zone_refs[zi], me),
                                      sems[2 * gi].at[len(plan) * (N_DEV - 1) + t]).start()
        refs[-1][...] = jnp.zeros_like(refs[-1])

    sem_shapes = []
    for _, _, plan in groups:
        sem_shapes += [pltpu.SemaphoreType.DMA((len(plan) * N_DEV,)), pltpu.SemaphoreType.DMA((len(plan) * (N_DEV - 1),))]
    outs = pl.pallas_call(
        body, name=name,
        out_shape=(*sem_shapes, *[pltpu.HBM(a.shape, a.dtype) for a in flat], jax.ShapeDtypeStruct((8, LANES), _F32)),
        in_specs=[_HBM] * n_in,
        out_specs=(*[_SEMS] * (2 * n_g), *[_HBM] * n_in, pl.BlockSpec(memory_space=pltpu.VMEM)),
        input_output_aliases={k: 2 * n_g + k for k in range(n_in)},
        compiler_params=pltpu.CompilerParams(has_side_effects=_EFFECT),
    )(*flat)
    handles, k = [], 2 * n_g
    for gi, (srcs, zones, plan) in enumerate(groups):
        ns, nz = len(srcs), len(zones)
        handles.append((outs[2 * gi], outs[2 * gi + 1], list(outs[k:k + ns]), list(outs[k + ns:k + ns + nz]), plan))
        k += ns + nz
    return handles, outs[-1]


def _wait_copies(name, handles, after):
    flat = []
    for _, _, srcs, zones, _ in handles:
        flat += srcs + zones
    n_in, n_g = len(flat), len(handles)

    def body(*refs):
        sems = refs[n_in:n_in + 2 * n_g]
        me, k, local, remote = _me(), 0, [], []
        for gi, (_, _, srcs, zones, plan) in enumerate(handles):
            ns, nz = len(srcs), len(zones)
            src_refs, zone_refs = refs[k:k + ns], refs[k + ns:k + ns + nz]
            k += ns + nz
            for t, (si, zi, src_of, dst_of) in enumerate(plan):
                local.append(pltpu.make_async_copy(src_of(src_refs[si], me), dst_of(zone_refs[zi], me),
                                                   sems[2 * gi].at[len(plan) * (N_DEV - 1) + t]))
                for j in range(1, N_DEV):
                    dev, pk = _peer(j)
                    remote.append(pltpu.make_async_remote_copy(
                        src_ref=src_of(src_refs[si], pk), dst_ref=dst_of(zone_refs[zi], pk),
                        send_sem=sems[2 * gi].at[t * (N_DEV - 1) + j - 1], recv_sem=sems[2 * gi + 1].at[t * (N_DEV - 1) + j - 1],
                        device_id=dev, device_id_type=pl.DeviceIdType.MESH))
        for cp in remote:
            cp.wait_send()
            cp.wait_recv()
        for cp in local:
            cp.wait()

    sem_args = []
    for send, recv, _, _, _ in handles:
        sem_args += [send, recv]
    outs = pl.pallas_call(
        body, name=name, out_shape=tuple(pltpu.HBM(a.shape, a.dtype) for a in flat),
        in_specs=[_HBM] * n_in + [_SEMS] * (2 * n_g) + [pl.BlockSpec(memory_space=pl.ANY)],
        out_specs=tuple([_HBM] * n_in), input_output_aliases={k: k for k in range(n_in)},
        compiler_params=pltpu.CompilerParams(has_side_effects=_EFFECT),
    )(*flat, *sem_args, after)
    res, k = [], 0
    for _, _, srcs, zones, _ in handles:
        res.append(list(outs[k + len(srcs):k + len(srcs) + len(zones)]))
        k += len(srcs) + len(zones)
    return res


def _rows(axis, size):
    def of(ref, b):
        start = b * size
        if size % 8 == 0:
            start = pl.multiple_of(start, 8)
        return ref.at[(slice(None),) * axis + (pl.ds(start, size),)]
    return of


def _whole(ref, b):
    return ref


def _slot(ref, b):
    return ref.at[b]


def _gather_group(shards):
    zones, plan = [], []
    for k, (a, axis) in enumerate(shards):
        zones.append(jax.ShapeDtypeStruct(a.shape[:axis] + (N_DEV * a.shape[axis],) + a.shape[axis + 1:], a.dtype))
        plan.append((k, k, _whole, _rows(axis, a.shape[axis])))
    return [a for a, _ in shards], zones, plan


def _scatter_group(grads):
    zones, plan = [], []
    for k, (a, axis) in enumerate(grads):
        size = a.shape[axis] // N_DEV
        zones.append(jax.ShapeDtypeStruct((N_DEV,) + a.shape[:axis] + (size,) + a.shape[axis + 1:], a.dtype))
        plan.append((k, k, _rows(axis, size), _slot))
    return [a for a, _ in grads], zones, plan


def _all_reduce_small(p):
    R = p.shape[0]

    def body(p_ref, o_ref, land, send_sems, recv_sems):
        me = _me()
        land[me] = p_ref[...]
        waits = []
        for j in range(1, N_DEV):
            dev, pk = _peer(j)
            pltpu.make_async_remote_copy(
                src_ref=p_ref, dst_ref=land.at[me], send_sem=send_sems.at[j - 1], recv_sem=recv_sems.at[j - 1],
                device_id=dev, device_id_type=pl.DeviceIdType.MESH).start()
            waits.append(pltpu.make_async_remote_copy(
                src_ref=p_ref, dst_ref=land.at[pk], send_sem=send_sems.at[j - 1], recv_sem=recv_sems.at[j - 1],
                device_id=dev, device_id_type=pl.DeviceIdType.MESH))
        for cp in waits:
            cp.wait()
        tot = land[0]
        for b in range(1, N_DEV):
            tot = tot + land[b]
        o_ref[...] = tot

    vmem = pl.BlockSpec(memory_space=pltpu.VMEM)
    return pl.pallas_call(
        body, name="all_reduce_small", in_specs=[vmem], out_specs=vmem,
        out_shape=jax.ShapeDtypeStruct((R, LANES), _F32),
        scratch_shapes=[pltpu.VMEM((N_DEV, R, LANES), _F32), pltpu.SemaphoreType.DMA((N_DEV - 1,)),
                        pltpu.SemaphoreType.DMA((N_DEV - 1,))],
        compiler_params=_params(),
    )(p)


def _sum_landed(land):
    g = land[0].astype(_F32)
    for b in range(1, N_DEV):
        g = g + land[b].astype(_F32)
    return g


def _landed_specs(n_layers, tr, C, nr):
    def spec(k):
        return pl.BlockSpec((N_DEV, tr, C), lambda l, i: (0, jnp.where(l == k, i, jnp.where(l < k, 0, nr - 1)), 0))
    return [spec(k) for k in range(n_layers)]


def _per_layer(l, zone_refs, fn):
    for k, ref in enumerate(zone_refs):
        @pl.when(l == k)
        def _(ref=ref):
            fn(_sum_landed(ref))


def _sum8(zones):
    L = len(zones)
    _, R, C = zones[0].shape
    tr = _tile(R, 352, 16)
    nr = R // tr

    def body(*refs):
        o_ref = refs[L]

        def put(g):
            o_ref[...] = g

        _per_layer(pl.program_id(0), refs[:L], put)

    return pl.pallas_call(
        body, name="sum8", grid=(L, nr), in_specs=_landed_specs(L, tr, C, nr),
        out_specs=pl.BlockSpec((tr, C), lambda l, i: (l * nr + i, 0)),
        out_shape=jax.ShapeDtypeStruct((L * R, C), _F32), compiler_params=_seq(2),
    )(*zones)


def _adam_update(gv, w_ref, m_ref, v_ref, d_ref, mo_ref, vo_ref):
    mn = ADAM_B1 * m_ref[...] + (1.0 - ADAM_B1) * gv
    vn = ADAM_B2 * v_ref[...] + (1.0 - ADAM_B2) * (gv * gv)
    mo_ref[...] = mn
    vo_ref[...] = vn
    d_ref[...] = -ADAM_LR * ((mn / (1.0 - ADAM_B1 ** ADAM_STEP)) / (jnp.sqrt(vn / (1.0 - ADAM_B2 ** ADAM_STEP)) + ADAM_EPS)
                             + ADAM_WD * w_ref[...])


def _adamw(g, w, m, v, name):
    R, C = w.shape
    tr = _tile(R, 256, 16)

    def body(g_ref, w_ref, m_ref, v_ref, d_ref, mo_ref, vo_ref):
        _adam_update(g_ref[...], w_ref, m_ref, v_ref, d_ref, mo_ref, vo_ref)

    row = pl.BlockSpec((tr, C), lambda i: (i, 0))
    return pl.pallas_call(
        body, name=name, grid=(R // tr,), in_specs=[row] * 4, out_specs=[row] * 3,
        out_shape=[jax.ShapeDtypeStruct((R, C), _F32)] * 3, compiler_params=_seq(),
    )(g, w, m, v)


def _adamw_landed(zones, w, m, v, name):
    L = len(zones)
    _, R, C = zones[0].shape
    tr = _tile(R, 176, 16)
    nr = R // tr

    def body(*refs):
        w_ref, m_ref, v_ref, g_ref, d_ref, mo_ref, vo_ref = refs[L:]

        def update(g):
            g_ref[...] = g
            _adam_update(g, w_ref, m_ref, v_ref, d_ref, mo_ref, vo_ref)

        _per_layer(pl.program_id(0), refs[:L], update)

    row = pl.BlockSpec((tr, C), lambda l, i: (l * nr + i, 0))
    return pl.pallas_call(
        body, name=name, grid=(L, nr), in_specs=_landed_specs(L, tr, C, nr) + [row] * 3, out_specs=[row] * 4,
        out_shape=[jax.ShapeDtypeStruct((L * R, C), _F32)] * 4, compiler_params=_seq(2),
    )(*zones, w, m, v)


def _pack(parts):
    flat = jnp.concatenate([p.reshape(-1).astype(_F32) for p in parts])
    n = flat.shape[0]
    rows = -(-n // (8 * LANES)) * 8
    return jnp.pad(flat, (0, rows * LANES - n)).reshape(rows, LANES)


def _unpack(packed, shapes):
    flat, out, k = packed.reshape(-1), [], 0
    for s in shapes:
        n = 1
        for d in s:
            n *= d
        out.append(flat[k:k + n].reshape(s))
        k += n
    return out


def kernel(x, norm1_g, norm2_g, pool_w, pool_scale, kv_norm_g, w_kv, b_kv, w_q, b_q, sinks, w_o, b_o, ffn_up, ffn_conv_w, ffn_conv_b, ffn_down, final_g, loss_target, m_norm1_g, m_norm2_g, m_pool_w, m_pool_scale, m_kv_norm_g, m_w_kv, m_b_kv, m_w_q, m_b_q, m_sinks, m_w_o, m_b_o, m_ffn_up, m_ffn_conv_w, m_ffn_conv_b, m_ffn_down, m_final_g, v_norm1_g, v_norm2_g, v_pool_w, v_pool_scale, v_kv_norm_g, v_w_kv, v_b_kv, v_w_q, v_b_q, v_sinks, v_w_o, v_b_o, v_ffn_up, v_ffn_conv_w, v_ffn_conv_b, v_ffn_down, v_final_g):
    S = x.shape[1]
    F2s = ffn_up.shape[2]
    F2 = N_DEV * F2s
    me = _me()
    x0 = x.reshape(S, D)
    tgt = loss_target.reshape(S, D)
    row = lambda a: a.reshape(1, -1)

    small = _pack([pool_scale, ffn_conv_w])
    wire = lambda a: a.astype(_MXU)
    ffn_w = lambda l: [(wire(ffn_up[l]).T, 0), (wire(ffn_down[l]), 0)]
    attn_w = lambda j: [(wire(w_q[j]), 0), (wire(w_o[j]), 0)]
    gathers, token = _start_copies("gather_start", [_gather_group(g) for g in (
        [(wire(pool_w[0]), 1), (small[None], 0)], ffn_w(0), [(wire(pool_w[1]), 1)] + ffn_w(1),
        [(wire(w_kv), 0)] + attn_w(0), ffn_w(2), attn_w(1), ffn_w(3))])

    def gathered(k, after):
        return _wait_copies("gather_wait_%d" % k, [gathers[k]], after)[0]

    pw, up_t, down, wq, wo = [None] * N_A, [None] * DEPTH, [None] * DEPTH, [None] * 2, [None] * 2
    pw[0], small_all = gathered(0, token)
    n_ps = pool_scale.size
    small_all = small_all.reshape(N_DEV, -1)
    pscale = jnp.transpose(small_all[:, :n_ps].reshape(N_DEV, N_A, D // N_DEV), (1, 0, 2)).reshape(N_A, D)
    conv_w = jnp.transpose(small_all[:, n_ps:n_ps + ffn_conv_w.size].reshape(N_DEV, DEPTH, CONV_W, F2s),
                           (1, 2, 0, 3)).reshape(DEPTH, CONV_W, F2)

    def dup(a):
        a4 = a.reshape(a.shape[:-1] + (2 * N_KV, 1, HEAD_DIM))
        return jnp.broadcast_to(a4, a.shape[:-1] + (2 * N_KV, 2, HEAD_DIM)).reshape(a.shape[:-1] + (KVD,))

    def fold(a):
        return a.reshape(a.shape[:-1] + (2 * N_KV, 2, HEAD_DIM)).sum(axis=-2).reshape(a.shape[:-1] + (2 * N_KV * HEAD_DIM,))

    xs, us, qs, os_ = [x0], [], [], []
    xc = x0
    kvd = None
    for l in range(DEPTH):
        if l == 1:
            pw[1], up_t[1], down[1] = gathered(2, xc)
        if l == 3:
            wq[1], wo[1] = gathered(5, xc)
        if l < N_A:
            xc = _pool_fwd(xc, row(norm1_g[l]), pw[l], row(pscale[l]))
        else:
            j = l - N_A
            xc, q, o = _attn_fwd(xc, row(norm1_g[l]), wq[j], row(b_q[j]), sinks[j], kvd, wo[j], row(b_o[j]))
            qs.append(q)
            os_.append(o)
        xs.append(xc)
        if l != 1:
            up_t[l], down[l] = gathered((1, None, 4, 6)[l], xc)
        xc, u, c = _ffn_fwd(xc, row(norm2_g[l]), up_t[l], conv_w[l], row(ffn_conv_b[l]), down[l])
        us.append((u, c))
        xs.append(xc)
        if l == N_A - 1:
            wkv, wq[0], wo[0] = gathered(3, xc)
            wkv_d, bkv_d = dup(wkv), dup(row(b_kv))
            kvd = _kv_fwd(xc, row(kv_norm_g), wkv_d, bkv_d)

    dx, loss_p, d_final = _loss_bwd(xc, row(final_g), tgt)
    d_n1, d_n2, d_cw, d_cb = [None] * DEPTH, [None] * DEPTH, [None] * DEPTH, [None] * DEPTH
    d_bq, d_bo, d_sk, d_ps, dkv_parts = [None] * 2, [None] * 2, [None] * 2, [None] * N_A, []
    scatters = [None] * (2 * DEPTH)
    token = None

    def after(gain):
        return gain if token is None else gain + token[0:1, 0:1]

    for l in reversed(range(DEPTH)):
        x_in, x_mid, x_out = xs[2 * l], xs[2 * l + 1], xs[2 * l + 2]
        mixer_grads = []
        if l == N_A - 1:
            dx, d_wkv, d_bkv, d_kvg = _kv_bwd(x_out, dx, after(row(kv_norm_g)), wkv_d, *dkv_parts)
            mixer_grads.append((fold(d_wkv).astype(_MXU), 0))
        dy = dx
        dx, du, a, h, d_cw[l], d_cb[l], d_n2[l] = _ffn_bwd(
            x_mid, dy, *us[l], after(row(norm2_g[l])), up_t[l], conv_w[l], down[l])
        (scatters[2 * l],), token = _start_copies("scatter_ffn_%d" % l, [_scatter_group(
            [(_tn_matmul(du, h, "tn_up"), 0), (_tn_matmul(a, dy, "tn_down"), 0)])])
        dy = dx
        if l < N_A:
            dx, d_pw, d_ps[l], d_n1[l] = _pool_bwd(x_in, dy, after(row(norm1_g[l])), pw[l], row(pscale[l]))
            mixer_grads.append((d_pw.astype(_MXU), 1))
        else:
            j = l - N_A
            dx, dq, h, d_cur, d_prev, d_bq[j], d_bo[j], d_n1[l], d_sk[j] = _attn_bwd(
                x_in, dy, qs[j], after(row(norm1_g[l])), wq[j], sinks[j], kvd, wo[j])
            mixer_grads += [(_tn_matmul(h, dq, "tn_q"), 0), (_tn_matmul(os_[j], dy, "tn_o"), 0)]
            dkv_parts += [d_cur, d_prev]
        (scatters[2 * l + 1],), token = _start_copies("scatter_mixer_%d" % l, [_scatter_group(mixer_grads)])

    rep_names = ["norm1_g", "norm2_g", "kv_norm_g", "b_kv", "b_q", "sinks", "b_o", "ffn_conv_b", "final_g"]
    rep = [jnp.concatenate(d_n1), jnp.concatenate(d_n2), d_kvg, fold(d_bkv), jnp.concatenate(d_bq),
           jnp.concatenate([s[:, :N_HEADS] for s in d_sk]), jnp.concatenate(d_bo), jnp.concatenate(d_cb), d_final]
    full = [jnp.concatenate(d_ps), jnp.stack(d_cw), loss_p[0:1, 0:1]]
    given = dict(norm1_g=norm1_g, norm2_g=norm2_g, kv_norm_g=kv_norm_g, b_kv=b_kv, b_q=b_q, sinks=sinks, b_o=b_o,
                 ffn_conv_b=ffn_conv_b, final_g=final_g, pool_scale=pool_scale, ffn_conv_w=ffn_conv_w)
    tot = _unpack(_all_reduce_small(_pack(rep + full)),
                  [given[k].shape for k in rep_names] + [(N_A, D), (DEPTH, CONV_W, F2), ()])
    grad = dict(zip(rep_names, tot))
    loss = tot[-1]
    grad["pool_scale"] = lax.dynamic_slice_in_dim(tot[-3], me * (D // N_DEV), D // N_DEV, axis=1)
    grad["ffn_conv_w"] = lax.dynamic_slice_in_dim(tot[-2], me * F2s, F2s, axis=2)

    moms = dict(norm1_g=(m_norm1_g, v_norm1_g), norm2_g=(m_norm2_g, v_norm2_g), pool_w=(m_pool_w, v_pool_w),
                pool_scale=(m_pool_scale, v_pool_scale), kv_norm_g=(m_kv_norm_g, v_kv_norm_g), w_kv=(m_w_kv, v_w_kv),
                b_kv=(m_b_kv, v_b_kv), w_q=(m_w_q, v_w_q), b_q=(m_b_q, v_b_q), sinks=(m_sinks, v_sinks),
                w_o=(m_w_o, v_w_o), b_o=(m_b_o, v_b_o), ffn_up=(m_ffn_up, v_ffn_up),
                ffn_conv_w=(m_ffn_conv_w, v_ffn_conv_w), ffn_conv_b=(m_ffn_conv_b, v_ffn_conv_b),
                ffn_down=(m_ffn_down, v_ffn_down), final_g=(m_final_g, v_final_g))
    given.update(pool_w=pool_w, w_kv=w_kv, w_q=w_q, w_o=w_o, ffn_up=ffn_up, ffn_down=ffn_down)
    delta, new_m, new_v = {}, {}, {}

    small_names = rep_names + ["pool_scale", "ffn_conv_w"]
    shapes = [given[k].shape for k in small_names]
    outs = _adamw(_pack([grad[k] for k in small_names]), _pack([given[k] for k in small_names]),
                  _pack([moms[k][0] for k in small_names]), _pack([moms[k][1] for k in small_names]), "adamw_small")
    for dst, packed in zip((delta, new_m, new_v), outs):
        dst.update(zip(small_names, _unpack(packed, shapes)))

    zones = _wait_copies("scatter_wait", scatters, outs[0])
    ffn_z, mix_z = zones[0::2], zones[1::2]
    by_name = dict(ffn_up=[z[0] for z in ffn_z], ffn_down=[z[1] for z in ffn_z],
                   w_q=[mix_z[N_A][0], mix_z[N_A + 1][0]], w_o=[mix_z[N_A][1], mix_z[N_A + 1][1]],
                   w_kv=[mix_z[N_A - 1][0]], pool_w=[mix_z[0][0], mix_z[N_A - 1][1]])

    def update(name, g, cols):
        w = given[name]
        two_d = lambda a: a.reshape(-1, cols)
        if g is None:
            landed = [z.reshape(N_DEV, -1, cols) for z in by_name[name]]
            outs = _adamw_landed(landed, two_d(w), two_d(moms[name][0]), two_d(moms[name][1]), "adamw_" + name)
            grad[name] = outs[0].reshape(w.shape)
        else:
            outs = _adamw(g, two_d(w), two_d(moms[name][0]), two_d(moms[name][1]), "adamw_" + name)
        delta[name], new_m[name], new_v[name] = (o.reshape(w.shape) for o in outs[-3:])

    grad["ffn_up"] = jnp.swapaxes(_sum8(by_name["ffn_up"]).reshape(DEPTH, F2s, D), 1, 2)
    update("ffn_up", grad["ffn_up"].reshape(-1, F2s), F2s)
    update("ffn_down", None, D)
    update("w_q", None, D)
    update("w_o", None, D)
    update("w_kv", None, w_kv.shape[1])
    update("pool_w", None, GC)

    names = ["norm1_g", "norm2_g", "pool_w", "pool_scale", "kv_norm_g", "w_kv", "b_kv", "w_q", "b_q", "sinks", "w_o",
             "b_o", "ffn_up", "ffn_conv_w", "ffn_conv_b", "ffn_down", "final_g"]
    return (loss, dx.reshape(x.shape), *[grad[k] for k in names], *[delta[k] for k in names],
            *[new_m[k] for k in names], *[new_v[k] for k in names])
```

```python
import functools

import jax
import jax.numpy as jnp
from jax import lax
from jax.experimental import pallas as pl
from jax.experimental.pallas import tpu as pltpu

_F32 = jnp.float32
_MXU = jnp.bfloat16

N_DEV = 8
D = 1024
DEPTH = 4
N_A = 2
POOL_WINDOWS = (2, 4, 8, 16)
GC = D // len(POOL_WINDOWS)
HALO = 16
HEAD_DIM = 64
N_HEADS = D // HEAD_DIM
GROUP = 8
N_KV = N_HEADS // GROUP
BLK = 128
PAIR = 2 * HEAD_DIM
KVD = 4 * N_KV * HEAD_DIM
CONV_W = 3
EPS = 1e-5
NEG = -1e30

ADAM_LR = 0.001
ADAM_B1 = 0.9
ADAM_B2 = 0.999
ADAM_EPS = 1e-08
ADAM_WD = 0.01
ADAM_STEP = 10

V7X_VMEM_LIMIT = 56 * 1024 * 1024
LANES = 128

_NT = (((1,), (1,)), ((), ()))
_TN = (((0,), (0,)), ((), ()))


def _params(**kw):
    return pltpu.CompilerParams(vmem_limit_bytes=V7X_VMEM_LIMIT, **kw)


def _seq(n=1):
    return _params(dimension_semantics=("arbitrary",) * n)


def _dot(a, b, dims=None):
    if dims is None:
        return jnp.dot(a, b, preferred_element_type=_F32)
    return lax.dot_general(a, b, dims, preferred_element_type=_F32)


def _rms(x):
    r = lax.rsqrt(jnp.mean(x * x, axis=-1, keepdims=True) + EPS)
    return x * r, r


def _rms_bwd(dh, xn, r, g):
    dxn = dh * g
    return r * (dxn - xn * jnp.mean(dxn * xn, axis=-1, keepdims=True))


def _colsum(a):
    return jnp.sum(a, axis=0, keepdims=True)


def _tile(n, want, mult=8):
    for t in range(min(want, n), 0, -1):
        if n % t == 0 and t % mult == 0:
            return t
    return n


def _full(shape):
    zeros = (0,) * len(shape)
    return pl.BlockSpec(shape, lambda *_: zeros)


def _pool_windows(hbuf, h, row, T):
    out = []
    for gi, win in enumerate(POOL_WINDOWS):
        cs = slice(gi * GC, (gi + 1) * GC)
        acc = hbuf[HALO:HALO + T, cs]
        for k in range(1, win):
            acc = acc + hbuf[HALO - k:HALO - k + T, cs]
        cnt = jnp.minimum(row + 1, win).astype(_F32)
        out.append((acc / cnt - h[:, cs], cnt))
    return out


def _pool_fwd(x, g, w, sc):
    S = x.shape[0]
    T = _tile(S, 512, HALO)
    n, hb = S // T, T // HALO

    def body(x_ref, xh_ref, g_ref, w_ref, sc_ref, o_ref, hbuf):
        i = pl.program_id(0)
        gv = g_ref[...]
        xv = x_ref[...]
        h = _rms(xv)[0] * gv
        hbuf[0:HALO, :] = jnp.where(i > 0, _rms(xh_ref[...])[0] * gv, 0.0)
        hbuf[HALO:, :] = h
        row = i * T + lax.broadcasted_iota(jnp.int32, (T, 1), 0)
        for gi, (p, _) in enumerate(_pool_windows(hbuf, h, row, T)):
            cs = slice(gi * GC, (gi + 1) * GC)
            z = _dot(p.astype(_MXU), w_ref[gi])
            o_ref[:, cs] = xv[:, cs] + z * sc_ref[:, cs]

    return pl.pallas_call(
        body, name="pool_fwd", grid=(n,),
        in_specs=[pl.BlockSpec((T, D), lambda i: (i, 0)),
                  pl.BlockSpec((HALO, D), lambda i: (jnp.maximum(i * hb - 1, 0), 0)),
                  _full((1, D)), _full((4, GC, GC)), _full((1, D))],
        out_specs=pl.BlockSpec((T, D), lambda i: (i, 0)),
        out_shape=jax.ShapeDtypeStruct((S, D), _F32),
        scratch_shapes=[pltpu.VMEM((T + HALO, D), _F32)],
        compiler_params=_seq(),
    )(x, x, g, w, sc)


def _pool_bwd(x, dy, g, w, sc):
    S = x.shape[0]
    T = _tile(S, 512, HALO)
    n, hb = S // T, T // HALO

    def body(x_ref, xh_ref, dy_ref, dyh_ref, g_ref, w_ref, sc_ref, dx_ref, dw_ref, dsc_ref, dg_ref,
             hbuf, qbuf, dhbuf):
        i = pl.program_id(0)

        @pl.when(i == 0)
        def _():
            dw_ref[...] = jnp.zeros_like(dw_ref)
            dsc_ref[...] = jnp.zeros_like(dsc_ref)
            dg_ref[...] = jnp.zeros_like(dg_ref)

        gv = g_ref[...]
        xv = x_ref[...]
        xn, r = _rms(xv)
        h = xn * gv
        hbuf[0:HALO, :] = jnp.where(i > 0, _rms(xh_ref[...])[0] * gv, 0.0)
        hbuf[HALO:, :] = h
        dyv = dy_ref[...]
        dz = dyv * sc_ref[...]
        dzh = jnp.where(i < n - 1, dyh_ref[...], 0.0) * sc_ref[...]
        row = i * T + lax.broadcasted_iota(jnp.int32, (T, 1), 0)
        rowh = (i + 1) * T + lax.broadcasted_iota(jnp.int32, (HALO, 1), 0)
        for gi, (p, cnt) in enumerate(_pool_windows(hbuf, h, row, T)):
            win = POOL_WINDOWS[gi]
            cs = slice(gi * GC, (gi + 1) * GC)
            pb = p.astype(_MXU)
            wg = w_ref[gi]
            dsc_ref[:, cs] += _colsum(dyv[:, cs] * _dot(pb, wg))
            dzb = dz[:, cs].astype(_MXU)
            dw_ref[gi] += _dot(pb, dzb, _TN)
            dp = _dot(dzb, wg, _NT)
            dph = _dot(dzh[:, cs].astype(_MXU), wg, _NT)
            qbuf[0:T, cs] = dp / cnt
            qbuf[T:T + HALO, cs] = dph / jnp.minimum(rowh + 1, win).astype(_F32)
            acc = qbuf[0:T, cs]
            for k in range(1, win):
                acc = acc + qbuf[k:k + T, cs]
            dhbuf[:, cs] = acc - dp
        dh = dhbuf[...]
        dg_ref[...] += _colsum(dh * xn)
        dx_ref[...] = dyv + _rms_bwd(dh, xn, r, gv)

    return pl.pallas_call(
        body, name="pool_bwd", grid=(n,),
        in_specs=[pl.BlockSpec((T, D), lambda i: (i, 0)),
                  pl.BlockSpec((HALO, D), lambda i: (jnp.maximum(i * hb - 1, 0), 0)),
                  pl.BlockSpec((T, D), lambda i: (i, 0)),
                  pl.BlockSpec((HALO, D), lambda i: (jnp.minimum((i + 1) * hb, S // HALO - 1), 0)),
                  _full((1, D)), _full((4, GC, GC)), _full((1, D))],
        out_specs=[pl.BlockSpec((T, D), lambda i: (i, 0)), _full((4, GC, GC)), _full((1, D)), _full((1, D))],
        out_shape=[jax.ShapeDtypeStruct((S, D), _F32), jax.ShapeDtypeStruct((4, GC, GC), _F32),
                   jax.ShapeDtypeStruct((1, D), _F32), jax.ShapeDtypeStruct((1, D), _F32)],
        scratch_shapes=[pltpu.VMEM((T + HALO, D), _F32), pltpu.VMEM((T + HALO, D), _F32), pltpu.VMEM((T, D), _F32)],
        compiler_params=_seq(),
    )(x, x, dy, dy, g, w, sc)


FFN_FWD_TILE, FFN_FWD_CHUNKS = 256, 2
FFN_BWD_TILE, FFN_BWD_CHUNKS = 128, 2
EDGE = 8


def _shift_down(v, k, prev):
    r = pltpu.roll(v, k, axis=0)
    i8 = lax.broadcasted_iota(jnp.int32, (EDGE, v.shape[1]), 0)
    head = jnp.where(i8 >= k, r[0:EDGE, :], pltpu.roll(prev, k, axis=0))
    return jnp.concatenate([head, r[EDGE:, :]], axis=0)


def _shift_up(v, k, nxt):
    T = v.shape[0]
    r = pltpu.roll(v, T - k, axis=0)
    i8 = lax.broadcasted_iota(jnp.int32, (EDGE, v.shape[1]), 0)
    tail = jnp.where(i8 < EDGE - k, r[T - EDGE:, :], pltpu.roll(nxt, EDGE - k, axis=0))
    return jnp.concatenate([r[:T - EDGE, :], tail], axis=0)


def _load_weights(i, pairs, sems):
    @pl.when(i == 0)
    def _():
        cps = [pltpu.make_async_copy(src, dst, sems.at[k]) for k, (src, dst) in enumerate(pairs)]
        for cp in cps:
            cp.start()
        for cp in cps:
            cp.wait()


def _ffn_fwd(x, g, wup_t, cw, cb, wdn):
    S = x.shape[0]
    F2 = wup_t.shape[0]
    F = F2 // 2
    C = F // FFN_FWD_CHUNKS
    T = _tile(S, FFN_FWD_TILE, 16)
    n = S // T

    def body(x_ref, g_ref, wup_hbm, cw_ref, cb_ref, wdn_hbm, o_ref, u_ref, c_ref, wup, wdnv, carry, sems):
        i = pl.program_id(0)
        _load_weights(i, [(wup_hbm, wup), (wdn_hbm, wdnv)], sems)

        @pl.when(i == 0)
        def _():
            carry[...] = jnp.zeros_like(carry)

        xv = x_ref[...]
        hb = (_rms(xv)[0] * g_ref[...]).astype(_MXU)
        acc = jnp.zeros((T, D), _F32)
        for j in range(FFN_FWD_CHUNKS):
            halves = []
            for cs in (slice(j * C, (j + 1) * C), slice(F + j * C, F + (j + 1) * C)):
                u = _dot(hb, wup[cs, :], _NT)
                u_ref[:, cs] = u.astype(u_ref.dtype)
                prev = carry[:, cs]
                carry[:, cs] = u[T - EDGE:, :]
                c = (cw_ref[0:1, cs] * _shift_down(u, 2, prev) + cw_ref[1:2, cs] * _shift_down(u, 1, prev)
                     + cw_ref[2:3, cs] * u + cb_ref[:, cs])
                c_ref[:, cs] = c.astype(c_ref.dtype)
                halves.append(c)
            cg, cv = halves
            a = (cg * jax.nn.sigmoid(cg)) * cv
            acc = acc + _dot(a.astype(_MXU), wdnv[j * C:(j + 1) * C, :])
        o_ref[...] = xv + acc

    any_ = pl.BlockSpec(memory_space=pl.ANY)
    wide = pl.BlockSpec((T, F2), lambda i: (i, 0))
    return pl.pallas_call(
        body, name="ffn_fwd", grid=(n,),
        in_specs=[pl.BlockSpec((T, D), lambda i: (i, 0)), _full((1, D)), any_, _full((CONV_W, F2)), _full((1, F2)), any_],
        out_specs=[pl.BlockSpec((T, D), lambda i: (i, 0)), wide, wide],
        out_shape=[jax.ShapeDtypeStruct((S, D), _F32), jax.ShapeDtypeStruct((S, F2), _MXU),
                   jax.ShapeDtypeStruct((S, F2), _MXU)],
        scratch_shapes=[pltpu.VMEM((F2, D), _MXU), pltpu.VMEM((F, D), _MXU),
                        pltpu.VMEM((EDGE, F2), _F32), pltpu.SemaphoreType.DMA((2,))],
        compiler_params=_seq(),
    )(x, g, wup_t, cw, cb, wdn)


def _ffn_bwd(x, dy, u, c, g, wup_t, cw, wdn):
    S = x.shape[0]
    F2 = wup_t.shape[0]
    F = F2 // 2
    C = F // FFN_BWD_CHUNKS
    T = _tile(S, FFN_BWD_TILE, 16)
    n = S // T

    def body(x_ref, dy_ref, u_ref, c_ref, g_ref, wup_hbm, cw_ref, wdn_hbm,
             dx_ref, du_ref, a_ref, h_ref, dcw_ref, dcb_ref, dg_ref, wup, wdnv, carry, sems):
        i = pl.program_id(0)
        _load_weights(i, [(wup_hbm, wup), (wdn_hbm, wdnv)], sems)

        @pl.when(i == 0)
        def _():
            carry[...] = jnp.zeros_like(carry)
            dcw_ref[...] = jnp.zeros_like(dcw_ref)
            dcb_ref[...] = jnp.zeros_like(dcb_ref)
            dg_ref[...] = jnp.zeros_like(dg_ref)

        gv = g_ref[...]
        xv = x_ref[...]
        xn, r = _rms(xv)
        hbf = (xn * gv).astype(_MXU)
        h_ref[...] = hbf
        dyv = dy_ref[...]
        dyb = dyv.astype(_MXU)
        dh = jnp.zeros((T, D), _F32)
        for j in range(FFN_BWD_CHUNKS):
            gs, vs = slice(j * C, (j + 1) * C), slice(F + j * C, F + (j + 1) * C)
            cg, cv = c_ref[:, gs].astype(_F32), c_ref[:, vs].astype(_F32)
            sg = jax.nn.sigmoid(cg)
            sl = cg * sg
            a_ref[:, gs] = (sl * cv).astype(a_ref.dtype)
            da = _dot(dyb, wdnv[gs, :], _NT)
            for cs, dc in ((gs, da * cv * (sg * (1.0 + cg * (1.0 - sg)))), (vs, da * sl)):
                nxt = carry[:, cs]
                carry[:, cs] = dc[0:EDGE, :]
                dc1, dc2 = _shift_up(dc, 1, nxt), _shift_up(dc, 2, nxt)
                uf = u_ref[:, cs].astype(_F32)
                dcb_ref[:, cs] += _colsum(dc)
                for k, d in enumerate((dc2, dc1, dc)):
                    dcw_ref[k:k + 1, cs] += _colsum(d * uf)
                du = cw_ref[2:3, cs] * dc + cw_ref[1:2, cs] * dc1 + cw_ref[0:1, cs] * dc2
                dub = du.astype(_MXU)
                du_ref[:, cs] = dub
                dh = dh + _dot(dub, wup[cs, :])
        dg_ref[...] += _colsum(dh * xn)
        dx_ref[...] = dyv + _rms_bwd(dh, xn, r, gv)

    any_ = pl.BlockSpec(memory_space=pl.ANY)
    rev = lambda i: (n - 1 - i, 0)
    return pl.pallas_call(
        body, name="ffn_bwd", grid=(n,),
        in_specs=[pl.BlockSpec((T, D), rev), pl.BlockSpec((T, D), rev), pl.BlockSpec((T, F2), rev),
                  pl.BlockSpec((T, F2), rev), _full((1, D)), any_, _full((CONV_W, F2)), any_],
        out_specs=[pl.BlockSpec((T, D), rev), pl.BlockSpec((T, F2), rev), pl.BlockSpec((T, F), rev),
                   pl.BlockSpec((T, D), rev), _full((CONV_W, F2)), _full((1, F2)), _full((1, D))],
        out_shape=[jax.ShapeDtypeStruct((S, D), _F32), jax.ShapeDtypeStruct((S, F2), _MXU),
                   jax.ShapeDtypeStruct((S, F), _MXU), jax.ShapeDtypeStruct((S, D), _MXU),
                   jax.ShapeDtypeStruct((CONV_W, F2), _F32), jax.ShapeDtypeStruct((1, F2), _F32),
                   jax.ShapeDtypeStruct((1, D), _F32)],
        scratch_shapes=[pltpu.VMEM((F2, D), _MXU), pltpu.VMEM((F, D), _MXU),
                        pltpu.VMEM((EDGE, F2), _F32), pltpu.SemaphoreType.DMA((2,))],
        compiler_params=_seq(),
    )(x, dy, u, c, g, wup_t, cw, wdn)


def _tn_matmul(a, b, name):
    S, M = a.shape
    N = b.shape[1]
    bm = _tile(M, 1408, LANES)
    tk = _tile(S, 512, 16)
    nk = S // tk

    def body(a_ref, b_ref, o_ref, acc):
        k = pl.program_id(1)

        @pl.when(k == 0)
        def _():
            acc[...] = jnp.zeros_like(acc)

        acc[...] += _dot(a_ref[...].astype(_MXU), b_ref[...].astype(_MXU), _TN)

        @pl.when(k == nk - 1)
        def _():
            o_ref[...] = acc[...].astype(o_ref.dtype)

    return pl.pallas_call(
        body, name=name, grid=(M // bm, nk),
        in_specs=[pl.BlockSpec((tk, bm), lambda i, k: (k, i)), pl.BlockSpec((tk, N), lambda i, k: (k, 0))],
        out_specs=pl.BlockSpec((bm, N), lambda i, k: (i, 0)),
        out_shape=jax.ShapeDtypeStruct((M, N), _MXU),
        scratch_shapes=[pltpu.VMEM((bm, N), _F32)],
        compiler_params=_seq(2),
    )(a, b)


def _kv_fwd(x, g, wkv, bkv):
    S = x.shape[0]
    T = _tile(S, 512, 16)

    def body(x_ref, g_ref, w_ref, b_ref, o_ref):
        hb = (_rms(x_ref[...])[0] * g_ref[...]).astype(_MXU)
        o_ref[...] = (_dot(hb, w_ref[...]) + b_ref[...]).astype(o_ref.dtype)

    return pl.pallas_call(
        body, name="kv_fwd", grid=(S // T,),
        in_specs=[pl.BlockSpec((T, D), lambda i: (i, 0)), _full((1, D)), _full((D, KVD)), _full((1, KVD))],
        out_specs=pl.BlockSpec((T, KVD), lambda i: (i, 0)),
        out_shape=jax.ShapeDtypeStruct((S, KVD), _MXU),
        compiler_params=_seq(),
    )(x, g, wkv, bkv)


def _kv_bwd(x, dx_in, g, wkv, cur_a, prev_a, cur_b, prev_b):
    S = x.shape[0]
    n = S // BLK

    def body(x_ref, dxi_ref, g_ref, w_ref, ca, pa, cb, pb, dx_ref, dw_ref, db_ref, dg_ref):
        i = pl.program_id(0)

        @pl.when(i == 0)
        def _():
            dw_ref[...] = jnp.zeros_like(dw_ref)
            db_ref[...] = jnp.zeros_like(db_ref)
            dg_ref[...] = jnp.zeros_like(dg_ref)

        gv = g_ref[...]
        xn, r = _rms(x_ref[...])
        dkv = ca[...] + cb[...] + jnp.where(i < n - 1, pa[...] + pb[...], 0.0)
        db_ref[...] += _colsum(dkv)
        dkb = dkv.astype(_MXU)
        dw_ref[...] += _dot((xn * gv).astype(_MXU), dkb, _TN)
        dh = _dot(dkb, w_ref[...], _NT)
        dg_ref[...] += _colsum(dh * xn)
        dx_ref[...] = dxi_ref[...] + _rms_bwd(dh, xn, r, gv)

    blk = lambda w: pl.BlockSpec((BLK, w), lambda i: (i, 0))
    nxt = pl.BlockSpec((BLK, KVD), lambda i: (jnp.minimum(i + 1, n - 1), 0))
    return pl.pallas_call(
        body, name="kv_bwd", grid=(n,),
        in_specs=[blk(D), blk(D), _full((1, D)), _full((D, KVD)), blk(KVD), nxt, blk(KVD), nxt],
        out_specs=[blk(D), _full((D, KVD)), _full((1, KVD)), _full((1, D))],
        out_shape=[jax.ShapeDtypeStruct((S, D), _F32), jax.ShapeDtypeStruct((D, KVD), _F32),
                   jax.ShapeDtypeStruct((1, KVD), _F32), jax.ShapeDtypeStruct((1, D), _F32)],
        compiler_params=_seq(),
    )(x, dx_in, g, wkv, cur_a, prev_a, cur_b, prev_b)


STACK = GROUP * BLK


def _attn_mask(i, rows):
    qi = lax.broadcasted_iota(jnp.int32, (rows, 2 * BLK), 0) & (BLK - 1)
    si = lax.broadcasted_iota(jnp.int32, (rows, 2 * BLK), 1)
    return (si > qi) & (si <= qi + BLK) & jnp.logical_or(i > 0, si >= BLK)


def _low_half():
    return lax.broadcasted_iota(jnp.int32, (BLK, PAIR), 1) < HEAD_DIM


def _stack_heads(ref, kh, dst):
    low = _low_half()
    for pp in range(GROUP // 2):
        pr = kh * (GROUP // 2) + pp
        v2 = ref[:, pr * PAIR:(pr + 1) * PAIR]
        zero = jnp.zeros_like(v2)
        dst[2 * pp * BLK:(2 * pp + 1) * BLK, :] = jnp.where(low, v2, zero)
        dst[(2 * pp + 1) * BLK:(2 * pp + 2) * BLK, :] = jnp.where(low, zero, v2)


def _unstack_heads(st, pp):
    return jnp.where(_low_half(), st[2 * pp * BLK:(2 * pp + 1) * BLK, :], st[(2 * pp + 1) * BLK:(2 * pp + 2) * BLK, :])


def _sink_col(sk_ref, kh):
    return jnp.concatenate([jnp.full((BLK, 1), sk_ref[kh * GROUP + h], _F32) for h in range(GROUP)], axis=0)


def _head_probs(qm, kd, mask, sink):
    s = jnp.where(mask, _dot(qm, kd, _NT) * (HEAD_DIM ** -0.5), NEG)
    m = jnp.maximum(jnp.max(s, axis=-1, keepdims=True), sink)
    p = jnp.exp(s - m)
    es = jnp.exp(sink - m)
    inv = 1.0 / (jnp.sum(p, axis=-1, keepdims=True) + es)
    return p * inv, es * inv


def _attn_fwd(x, g, wq, bq, sinks, kvd, wo, bo):
    S = x.shape[0]
    n = S // BLK

    def body(x_ref, g_ref, wq_ref, bq_ref, sk_ref, kp_ref, kc_ref, wo_ref, bo_ref, xo_ref, q_ref, o_ref, win):
        i = pl.program_id(0)
        xv = x_ref[...]
        hb = (_rms(xv)[0] * g_ref[...]).astype(_MXU)
        q_ref[...] = (_dot(hb, wq_ref[...]) + bq_ref[...]).astype(q_ref.dtype)
        win[0:BLK, :] = kp_ref[...]
        win[BLK:, :] = kc_ref[...]
        mask = _attn_mask(i, BLK)
        low = _low_half()
        for pr in range(N_HEADS // 2):
            kh = (2 * pr) // GROUP
            kd = win[:, kh * PAIR:(kh + 1) * PAIR]
            vd = win[:, (N_KV + kh) * PAIR:(N_KV + kh + 1) * PAIR]
            q2 = q_ref[:, pr * PAIR:(pr + 1) * PAIR]
            outs = []
            for half in range(2):
                qm = jnp.where(low if half == 0 else ~low, q2, jnp.zeros_like(q2))
                pbs, _ = _head_probs(qm, kd, mask, sk_ref[2 * pr + half])
                outs.append(_dot(pbs.astype(_MXU), vd))
            o_ref[:, pr * PAIR:(pr + 1) * PAIR] = jnp.where(low, outs[0], outs[1]).astype(o_ref.dtype)
        xo_ref[...] = xv + _dot(o_ref[...], wo_ref[...]) + bo_ref[...]

    blk = lambda w: pl.BlockSpec((BLK, w), lambda i: (i, 0))
    return pl.pallas_call(
        body, name="attn_fwd", grid=(n,),
        in_specs=[blk(D), _full((1, D)), _full((D, D)), _full((1, D)),
                  pl.BlockSpec(memory_space=pltpu.SMEM),
                  pl.BlockSpec((BLK, KVD), lambda i: (jnp.maximum(i - 1, 0), 0)), blk(KVD),
                  _full((D, D)), _full((1, D))],
        out_specs=[blk(D), blk(D), blk(D)],
        out_shape=[jax.ShapeDtypeStruct((S, D), _F32), jax.ShapeDtypeStruct((S, D), _MXU),
                   jax.ShapeDtypeStruct((S, D), _MXU)],
        scratch_shapes=[pltpu.VMEM((2 * BLK, KVD), _MXU)],
        compiler_params=_seq(),
    )(x, g, wq, bq, sinks, kvd, kvd, wo, bo)


def _attn_bwd(x, dy, q, o, g, wq, sinks, kvd, wo):
    S = x.shape[0]
    n = S // BLK
    all_rows = N_HEADS * BLK

    def body(x_ref, dy_ref, q_ref, o_ref, g_ref, wq_ref, sk_ref, kp_ref, kc_ref, wo_ref,
             dx_ref, dq_ref, h_ref, dc_ref, dp_ref, dbq_ref, dbo_ref, dg_ref, dsk_ref, win, dob, qs, dos, pall, dsall):
        i = pl.program_id(0)

        @pl.when(i == 0)
        def _():
            dbq_ref[...] = jnp.zeros_like(dbq_ref)
            dbo_ref[...] = jnp.zeros_like(dbo_ref)
            dg_ref[...] = jnp.zeros_like(dg_ref)
            dsk_ref[...] = jnp.zeros_like(dsk_ref)

        gv = g_ref[...]
        xv = x_ref[...]
        xn, r = _rms(xv)
        h_ref[...] = (xn * gv).astype(h_ref.dtype)
        dyv = dy_ref[...]
        dbo_ref[...] += _colsum(dyv)
        dob[...] = _dot(dyv.astype(_MXU), wo_ref[...], _NT).astype(dob.dtype)
        win[0:BLK, :] = kp_ref[...]
        win[BLK:, :] = kc_ref[...]
        mask = _attn_mask(i, BLK)
        low = _low_half()
        lane = lax.broadcasted_iota(jnp.int32, (1, LANES), 1)
        for pr in range(N_HEADS // 2):
            kh = (2 * pr) // GROUP
            kd = win[:, kh * PAIR:(kh + 1) * PAIR]
            vd = win[:, (N_KV + kh) * PAIR:(N_KV + kh + 1) * PAIR]
            q2 = q_ref[:, pr * PAIR:(pr + 1) * PAIR]
            do2 = dob[:, pr * PAIR:(pr + 1) * PAIR]
            od = do2.astype(_F32) * o_ref[:, pr * PAIR:(pr + 1) * PAIR].astype(_F32)
            for half in range(2):
                hd = 2 * pr + half
                rows = slice(hd * BLK, (hd + 1) * BLK)
                sel = low if half == 0 else ~low
                qm = jnp.where(sel, q2, jnp.zeros_like(q2))
                dom = jnp.where(sel, do2, jnp.zeros_like(do2))
                qs[rows, :] = qm
                dos[rows, :] = dom
                pbs, ps = _head_probs(qm, kd, mask, sk_ref[hd])
                pall[rows, :] = pbs.astype(_MXU)
                delta = jnp.sum(jnp.where(sel, od, 0.0), axis=-1, keepdims=True)
                dsall[rows, :] = (pbs * (_dot(dom, vd, _NT) - delta) * (HEAD_DIM ** -0.5)).astype(_MXU)
                dsk_ref[...] -= jnp.where(lane == hd, _colsum(ps * delta), 0.0)
        dq_all = []
        for kh in range(N_KV):
            ks = slice(kh * PAIR, (kh + 1) * PAIR)
            vs = slice((N_KV + kh) * PAIR, (N_KV + kh + 1) * PAIR)
            rows = slice(kh * STACK, (kh + 1) * STACK)
            dqst = _dot(dsall[rows, :], win[:, ks])
            dk = _dot(dsall[rows, :], qs[rows, :], _TN)
            dv = _dot(pall[rows, :], dos[rows, :], _TN)
            dp_ref[:, ks], dc_ref[:, ks] = dk[0:BLK, :], dk[BLK:, :]
            dp_ref[:, vs], dc_ref[:, vs] = dv[0:BLK, :], dv[BLK:, :]
            dq_all += [_unstack_heads(dqst, pp) for pp in range(GROUP // 2)]
        dq = jnp.concatenate(dq_all, axis=1)
        dbq_ref[...] += _colsum(dq)
        dqb = dq.astype(_MXU)
        dq_ref[...] = dqb
        dh = _dot(dqb, wq_ref[...], _NT)
        dg_ref[...] += _colsum(dh * xn)
        dx_ref[...] = dyv + _rms_bwd(dh, xn, r, gv)

    blk = lambda w: pl.BlockSpec((BLK, w), lambda i: (i, 0))
    return pl.pallas_call(
        body, name="attn_bwd", grid=(n,),
        in_specs=[blk(D), blk(D), blk(D), blk(D), _full((1, D)), _full((D, D)),
                  pl.BlockSpec(memory_space=pltpu.SMEM),
                  pl.BlockSpec((BLK, KVD), lambda i: (jnp.maximum(i - 1, 0), 0)), blk(KVD), _full((D, D))],
        out_specs=[blk(D), blk(D), blk(D), blk(KVD), blk(KVD),
                   _full((1, D)), _full((1, D)), _full((1, D)), _full((1, LANES))],
        out_shape=[jax.ShapeDtypeStruct((S, D), _F32), jax.ShapeDtypeStruct((S, D), _MXU),
                   jax.ShapeDtypeStruct((S, D), _MXU), jax.ShapeDtypeStruct((S, KVD), _F32),
                   jax.ShapeDtypeStruct((S, KVD), _F32), jax.ShapeDtypeStruct((1, D), _F32),
                   jax.ShapeDtypeStruct((1, D), _F32), jax.ShapeDtypeStruct((1, D), _F32),
                   jax.ShapeDtypeStruct((1, LANES), _F32)],
        scratch_shapes=[pltpu.VMEM((2 * BLK, KVD), _MXU), pltpu.VMEM((BLK, D), _MXU),
                        pltpu.VMEM((all_rows, PAIR), _MXU), pltpu.VMEM((all_rows, PAIR), _MXU),
                        pltpu.VMEM((all_rows, 2 * BLK), _MXU), pltpu.VMEM((all_rows, 2 * BLK), _MXU)],
        compiler_params=_seq(),
    )(x, dy, q, o, g, wq, sinks, kvd, kvd, wo)


def _loss_bwd(x, g, tgt):
    S = x.shape[0]
    T = _tile(S, 512, 8)

    def body(x_ref, g_ref, t_ref, dx_ref, ls_ref, dg_ref):
        @pl.when(pl.program_id(0) == 0)
        def _():
            ls_ref[...] = jnp.zeros_like(ls_ref)
            dg_ref[...] = jnp.zeros_like(dg_ref)

        gv = g_ref[...]
        xn, r = _rms(x_ref[...])
        err = xn * gv - t_ref[...]
        ls_ref[...] += 0.5 * jnp.sum(jnp.mean(err * err, axis=-1, keepdims=True))
        dyv = err * (1.0 / D)
        dg_ref[...] += _colsum(dyv * xn)
        dx_ref[...] = _rms_bwd(dyv, xn, r, gv)

    return pl.pallas_call(
        body, name="loss_bwd", grid=(S // T,),
        in_specs=[pl.BlockSpec((T, D), lambda i: (i, 0)), _full((1, D)), pl.BlockSpec((T, D), lambda i: (i, 0))],
        out_specs=[pl.BlockSpec((T, D), lambda i: (i, 0)), _full((8, LANES)), _full((1, D))],
        out_shape=[jax.ShapeDtypeStruct((S, D), _F32), jax.ShapeDtypeStruct((8, LANES), _F32),
                   jax.ShapeDtypeStruct((1, D), _F32)],
        compiler_params=_seq(),
    )(x, g, tgt)


def _me():
    return 4 * lax.axis_index("x") + 2 * lax.axis_index("y") + lax.axis_index("c")


def _peer(j):
    x, y, c = lax.axis_index("x"), lax.axis_index("y"), lax.axis_index("c")
    px = 1 - x if j & 4 else x
    py = 1 - y if j & 2 else y
    pc = 1 - c if j & 1 else c
    return (px, py, pc), 4 * px + 2 * py + pc


_HBM = pl.BlockSpec(memory_space=pltpu.HBM)
_SEMS = pl.BlockSpec(memory_space=pltpu.SEMAPHORE)
_EFFECT = pltpu.SideEffectType.DATAFLOW_SIDE_EFFECTING


def _in_hbm(a):
    return pltpu.with_memory_space_constraint(a, pltpu.HBM)


def _start_copies(name, groups):
    flat = []
    for srcs, zones, _ in groups:
        flat += [_in_hbm(a) for a in srcs] + [_in_hbm(lax.empty(z.shape, z.dtype)) for z in zones]
    n_in, n_g = len(flat), len(groups)

    def body(*refs):
        sems = refs[n_in:n_in + 2 * n_g]
        me, k = _me(), 0
        for gi, (srcs, zones, plan) in enumerate(groups):
            src_refs, zone_refs = refs[k:k + len(srcs)], refs[k + len(srcs):k + len(srcs) + len(zones)]
            k += len(srcs) + len(zones)
            for t, (si, zi, src_of, dst_of) in enumerate(plan):
                for j in range(1, N_DEV):
                    dev, pk = _peer(j)
                    pltpu.make_async_remote_copy(
                        src_ref=src_of(src_refs[si], pk), dst_ref=dst_of(zone_refs[zi], me),
                        send_sem=sems[2 * gi].at[t * (N_DEV - 1) + j - 1], recv_sem=sems[2 * gi + 1].at[t * (N_DEV - 1) + j - 1],
                        device_id=dev, device_id_type=pl.DeviceIdType.MESH).start()
                pltpu.make_async_copy(src_of(src_refs[si], me), dst_of(zone_refs[zi], me),
                                      sems[2 * gi].at[len(plan) * (N_DEV - 1) + t]).start()
        refs[-1][...] = jnp.zeros_like(refs[-1])

    sem_shapes = []
    for _, _, plan in groups:
        sem_shapes += [pltpu.SemaphoreType.DMA((len(plan) * N_DEV,)), pltpu.SemaphoreType.DMA((len(plan) * (N_DEV - 1),))]
    outs = pl.pallas_call(
        body, name=name,
        out_shape=(*sem_shapes, *[pltpu.HBM(a.shape, a.dtype) for a in flat], jax.ShapeDtypeStruct((8, LANES), _F32)),
        in_specs=[_HBM] * n_in,
        out_specs=(*[_SEMS] * (2 * n_g), *[_HBM] * n_in, pl.BlockSpec(memory_space=pltpu.VMEM)),
        input_output_aliases={k: 2 * n_g + k for k in range(n_in)},
        compiler_params=pltpu.CompilerParams(has_side_effects=_EFFECT),
    )(*flat)
    handles, k = [], 2 * n_g
    for gi, (srcs, zones, plan) in enumerate(groups):
        ns, nz = len(srcs), len(zones)
        handles.append((outs[2 * gi], outs[2 * gi + 1], list(outs[k:k + ns]), list(outs[k + ns:k + ns + nz]), plan))
        k += ns + nz
    return handles, outs[-1]


def _wait_copies(name, handles, after):
    flat = []
    for _, _, srcs, zones, _ in handles:
        flat += srcs + zones
    n_in, n_g = len(flat), len(handles)

    def body(*refs):
        sems = refs[n_in:n_in + 2 * n_g]
        me, k, local, remote = _me(), 0, [], []
        for gi, (_, _, srcs, zones, plan) in enumerate(handles):
            ns, nz = len(srcs), len(zones)
            src_refs, zone_refs = refs[k:k + ns], refs[k + ns:k + ns + nz]
            k += ns + nz
            for t, (si, zi, src_of, dst_of) in enumerate(plan):
                local.append(pltpu.make_async_copy(src_of(src_refs[si], me), dst_of(zone_refs[zi], me),
                                                   sems[2 * gi].at[len(plan) * (N_DEV - 1) + t]))
                for j in range(1, N_DEV):
                    dev, pk = _peer(j)
                    remote.append(pltpu.make_async_remote_copy(
                        src_ref=src_of(src_refs[si], pk), dst_ref=dst_of(zone_refs[zi], pk),
                        send_sem=sems[2 * gi].at[t * (N_DEV - 1) + j - 1], recv_sem=sems[2 * gi + 1].at[t * (N_DEV - 1) + j - 1],
                        device_id=dev, device_id_type=pl.DeviceIdType.MESH))
        for cp in remote:
            cp.wait_send()
            cp.wait_recv()
        for cp in local:
            cp.wait()

    sem_args = []
    for send, recv, _, _, _ in handles:
        sem_args += [send, recv]
    outs = pl.pallas_call(
        body, name=name, out_shape=tuple(pltpu.HBM(a.shape, a.dtype) for a in flat),
        in_specs=[_HBM] * n_in + [_SEMS] * (2 * n_g) + [pl.BlockSpec(memory_space=pl.ANY)],
        out_specs=tuple([_HBM] * n_in), input_output_aliases={k: k for k in range(n_in)},
        compiler_params=pltpu.CompilerParams(has_side_effects=_EFFECT),
    )(*flat, *sem_args, after)
    res, k = [], 0
    for _, _, srcs, zones, _ in handles:
        res.append(list(outs[k + len(srcs):k + len(srcs) + len(zones)]))
        k += len(srcs) + len(zones)
    return res


def _rows(axis, size):
    def of(ref, b):
        start = b * size
        if size % 8 == 0:
            start = pl.multiple_of(start, 8)
        return ref.at[(slice(None),) * axis + (pl.ds(start, size),)]
    return of


def _whole(ref, b):
    return ref


def _slot(ref, b):
    return ref.at[b]


def _gather_group(shards):
    zones, plan = [], []
    for k, (a, axis) in enumerate(shards):
        zones.append(jax.ShapeDtypeStruct(a.shape[:axis] + (N_DEV * a.shape[axis],) + a.shape[axis + 1:], a.dtype))
        plan.append((k, k, _whole, _rows(axis, a.shape[axis])))
    return [a for a, _ in shards], zones, plan


def _scatter_group(grads):
    zones, plan = [], []
    for k, (a, axis) in enumerate(grads):
        size = a.shape[axis] // N_DEV
        zones.append(jax.ShapeDtypeStruct((N_DEV,) + a.shape[:axis] + (size,) + a.shape[axis + 1:], a.dtype))
        plan.append((k, k, _rows(axis, size), _slot))
    return [a for a, _ in grads], zones, plan


def _all_reduce_small(p):
    R = p.shape[0]

    def body(p_ref, o_ref, land, send_sems, recv_sems):
        me = _me()
        land[me] = p_ref[...]
        waits = []
        for j in range(1, N_DEV):
            dev, pk = _peer(j)
            pltpu.make_async_remote_copy(
                src_ref=p_ref, dst_ref=land.at[me], send_sem=send_sems.at[j - 1], recv_sem=recv_sems.at[j - 1],
                device_id=dev, device_id_type=pl.DeviceIdType.MESH).start()
            waits.append(pltpu.make_async_remote_copy(
                src_ref=p_ref, dst_ref=land.at[pk], send_sem=send_sems.at[j - 1], recv_sem=recv_sems.at[j - 1],
                device_id=dev, device_id_type=pl.DeviceIdType.MESH))
        for cp in waits:
            cp.wait()
        tot = land[0]
        for b in range(1, N_DEV):
            tot = tot + land[b]
        o_ref[...] = tot

    vmem = pl.BlockSpec(memory_space=pltpu.VMEM)
    return pl.pallas_call(
        body, name="all_reduce_small", in_specs=[vmem], out_specs=vmem,
        out_shape=jax.ShapeDtypeStruct((R, LANES), _F32),
        scratch_shapes=[pltpu.VMEM((N_DEV, R, LANES), _F32), pltpu.SemaphoreType.DMA((N_DEV - 1,)),
                        pltpu.SemaphoreType.DMA((N_DEV - 1,))],
        compiler_params=_params(),
    )(p)


def _sum_landed(land):
    g = land[0].astype(_F32)
    for b in range(1, N_DEV):
        g = g + land[b].astype(_F32)
    return g


def _landed_specs(n_layers, tr, C, nr):
    def spec(k):
        return pl.BlockSpec((N_DEV, tr, C), lambda l, i: (0, jnp.where(l == k, i, jnp.where(l < k, 0, nr - 1)), 0))
    return [spec(k) for k in range(n_layers)]


def _per_layer(l, zone_refs, fn):
    for k, ref in enumerate(zone_refs):
        @pl.when(l == k)
        def _(ref=ref):
            fn(_sum_landed(ref))


def _sum8(zones):
    L = len(zones)
    _, R, C = zones[0].shape
    tr = _tile(R, 352, 16)
    nr = R // tr

    def body(*refs):
        o_ref = refs[L]

        def put(g):
            o_ref[...] = g

        _per_layer(pl.program_id(0), refs[:L], put)

    return pl.pallas_call(
        body, name="sum8", grid=(L, nr), in_specs=_landed_specs(L, tr, C, nr),
        out_specs=pl.BlockSpec((tr, C), lambda l, i: (l * nr + i, 0)),
        out_shape=jax.ShapeDtypeStruct((L * R, C), _F32), compiler_params=_seq(2),
    )(*zones)


def _adam_update(gv, w_ref, m_ref, v_ref, d_ref, mo_ref, vo_ref):
    mn = ADAM_B1 * m_ref[...] + (1.0 - ADAM_B1) * gv
    vn = ADAM_B2 * v_ref[...] + (1.0 - ADAM_B2) * (gv * gv)
    mo_ref[...] = mn
    vo_ref[...] = vn
    d_ref[...] = -ADAM_LR * ((mn / (1.0 - ADAM_B1 ** ADAM_STEP)) / (jnp.sqrt(vn / (1.0 - ADAM_B2 ** ADAM_STEP)) + ADAM_EPS)
                             + ADAM_WD * w_ref[...])


def _adamw(g, w, m, v, name):
    R, C = w.shape
    tr = _tile(R, 256, 16)

    def body(g_ref, w_ref, m_ref, v_ref, d_ref, mo_ref, vo_ref):
        _adam_update(g_ref[...], w_ref, m_ref, v_ref, d_ref, mo_ref, vo_ref)

    row = pl.BlockSpec((tr, C), lambda i: (i, 0))
    return pl.pallas_call(
        body, name=name, grid=(R // tr,), in_specs=[row] * 4, out_specs=[row] * 3,
        out_shape=[jax.ShapeDtypeStruct((R, C), _F32)] * 3, compiler_params=_seq(),
    )(g, w, m, v)


def _adamw_landed(zones, w, m, v, name):
    L = len(zones)
    _, R, C = zones[0].shape
    tr = _tile(R, 176, 16)
    nr = R // tr

    def body(*refs):
        w_ref, m_ref, v_ref, g_ref, d_ref, mo_ref, vo_ref = refs[L:]

        def update(g):
            g_ref[...] = g
            _adam_update(g, w_ref, m_ref, v_ref, d_ref, mo_ref, vo_ref)

        _per_layer(pl.program_id(0), refs[:L], update)

    row = pl.BlockSpec((tr, C), lambda l, i: (l * nr + i, 0))
    return pl.pallas_call(
        body, name=name, grid=(L, nr), in_specs=_landed_specs(L, tr, C, nr) + [row] * 3, out_specs=[row] * 4,
        out_shape=[jax.ShapeDtypeStruct((L * R, C), _F32)] * 4, compiler_params=_seq(2),
    )(*zones, w, m, v)


def _pack(parts):
    flat = jnp.concatenate([p.reshape(-1).astype(_F32) for p in parts])
    n = flat.shape[0]
    rows = -(-n // (8 * LANES)) * 8
    return jnp.pad(flat, (0, rows * LANES - n)).reshape(rows, LANES)


def _unpack(packed, shapes):
    flat, out, k = packed.reshape(-1), [], 0
    for s in shapes:
        n = 1
        for d in s:
            n *= d
        out.append(flat[k:k + n].reshape(s))
        k += n
    return out


def kernel(x, norm1_g, norm2_g, pool_w, pool_scale, kv_norm_g, w_kv, b_kv, w_q, b_q, sinks, w_o, b_o, ffn_up, ffn_conv_w, ffn_conv_b, ffn_down, final_g, loss_target, m_norm1_g, m_norm2_g, m_pool_w, m_pool_scale, m_kv_norm_g, m_w_kv, m_b_kv, m_w_q, m_b_q, m_sinks, m_w_o, m_b_o, m_ffn_up, m_ffn_conv_w, m_ffn_conv_b, m_ffn_down, m_final_g, v_norm1_g, v_norm2_g, v_pool_w, v_pool_scale, v_kv_norm_g, v_w_kv, v_b_kv, v_w_q, v_b_q, v_sinks, v_w_o, v_b_o, v_ffn_up, v_ffn_conv_w, v_ffn_conv_b, v_ffn_down, v_final_g):
    S = x.shape[1]
    F2s = ffn_up.shape[2]
    F2 = N_DEV * F2s
    me = _me()
    x0 = x.reshape(S, D)
    tgt = loss_target.reshape(S, D)
    row = lambda a: a.reshape(1, -1)

    small = _pack([pool_scale, ffn_conv_w])
    wire = lambda a: a.astype(_MXU)
    ffn_w = lambda l: [(wire(ffn_up[l]).T, 0), (wire(ffn_down[l]), 0)]
    attn_w = lambda j: [(wire(w_q[j]), 0), (wire(w_o[j]), 0)]
    gathers, token = _start_copies("gather_start", [_gather_group(g) for g in (
        [(wire(pool_w[0]), 1), (small[None], 0)], ffn_w(0), [(wire(pool_w[1]), 1)] + ffn_w(1),
        [(wire(w_kv), 0)] + attn_w(0), ffn_w(2), attn_w(1), ffn_w(3))])

    def gathered(k, after):
        return _wait_copies("gather_wait_%d" % k, [gathers[k]], after)[0]

    pw, up_t, down, wq, wo = [None] * N_A, [None] * DEPTH, [None] * DEPTH, [None] * 2, [None] * 2
    pw[0], small_all = gathered(0, token)
    n_ps = pool_scale.size
    small_all = small_all.reshape(N_DEV, -1)
    pscale = jnp.transpose(small_all[:, :n_ps].reshape(N_DEV, N_A, D // N_DEV), (1, 0, 2)).reshape(N_A, D)
    conv_w = jnp.transpose(small_all[:, n_ps:n_ps + ffn_conv_w.size].reshape(N_DEV, DEPTH, CONV_W, F2s),
                           (1, 2, 0, 3)).reshape(DEPTH, CONV_W, F2)

    def dup(a):
        a4 = a.reshape(a.shape[:-1] + (2 * N_KV, 1, HEAD_DIM))
        return jnp.broadcast_to(a4, a.shape[:-1] + (2 * N_KV, 2, HEAD_DIM)).reshape(a.shape[:-1] + (KVD,))

    def fold(a):
        return a.reshape(a.shape[:-1] + (2 * N_KV, 2, HEAD_DIM)).sum(axis=-2).reshape(a.shape[:-1] + (2 * N_KV * HEAD_DIM,))

    xs, us, qs, os_ = [x0], [], [], []
    xc = x0
    kvd = None
    for l in range(DEPTH):
        if l == 1:
            pw[1], up_t[1], down[1] = gathered(2, xc)
        if l == 3:
            wq[1], wo[1] = gathered(5, xc)
        if l < N_A:
            xc = _pool_fwd(xc, row(norm1_g[l]), pw[l], row(pscale[l]))
        else:
            j = l - N_A
            xc, q, o = _attn_fwd(xc, row(norm1_g[l]), wq[j], row(b_q[j]), sinks[j], kvd, wo[j], row(b_o[j]))
            qs.append(q)
            os_.append(o)
        xs.append(xc)
        if l != 1:
            up_t[l], down[l] = gathered((1, None, 4, 6)[l], xc)
        xc, u, c = _ffn_fwd(xc, row(norm2_g[l]), up_t[l], conv_w[l], row(ffn_conv_b[l]), down[l])
        us.append((u, c))
        xs.append(xc)
        if l == N_A - 1:
            wkv, wq[0], wo[0] = gathered(3, xc)
            wkv_d, bkv_d = dup(wkv), dup(row(b_kv))
            kvd = _kv_fwd(xc, row(kv_norm_g), wkv_d, bkv_d)

    dx, loss_p, d_final = _loss_bwd(xc, row(final_g), tgt)
    d_n1, d_n2, d_cw, d_cb = [None] * DEPTH, [None] * DEPTH, [None] * DEPTH, [None] * DEPTH
    d_bq, d_bo, d_sk, d_ps, dkv_parts = [None] * 2, [None] * 2, [None] * 2, [None] * N_A, []
    scatters = [None] * (2 * DEPTH)
    token = None

    def after(gain):
        return gain if token is None else gain + token[0:1, 0:1]

    rep_names = ["norm1_g", "norm2_g", "kv_norm_g", "b_kv", "b_q", "sinks", "b_o", "ffn_conv_b", "final_g"]

    def small_parts(n1_rest, ps_rest):
        zero = jnp.zeros((1, D), _F32)
        return [jnp.concatenate([zero] + n1_rest), jnp.concatenate(d_n2), d_kvg, fold(d_bkv), jnp.concatenate(d_bq),
                jnp.concatenate([s[:, :N_HEADS] for s in d_sk]), jnp.concatenate(d_bo), jnp.concatenate(d_cb), d_final,
                jnp.concatenate([zero] + ps_rest), jnp.stack(d_cw), loss_p[0:1, 0:1]]

    for l in reversed(range(DEPTH)):
        x_in, x_mid, x_out = xs[2 * l], xs[2 * l + 1], xs[2 * l + 2]
        mixer_grads = []
        if l == N_A - 1:
            dx, d_wkv, d_bkv, d_kvg = _kv_bwd(x_out, dx, after(row(kv_norm_g)), wkv_d, *dkv_parts)
            mixer_grads.append((fold(d_wkv).astype(_MXU), 0))
        dy = dx
        dx, du, a, h, d_cw[l], d_cb[l], d_n2[l] = _ffn_bwd(
            x_mid, dy, *us[l], after(row(norm2_g[l])), up_t[l], conv_w[l], down[l])
        groups = [_scatter_group([(_tn_matmul(du, h, "tn_up"), 0), (_tn_matmul(a, dy, "tn_down"), 0)])]
        if l == 0:
            groups.insert(0, _gather_group([(_pack(small_parts(d_n1[1:], d_ps[1:]))[None], 0)]))
        handles, token = _start_copies("scatter_ffn_%d" % l, groups)
        scatters[2 * l], small_handle = handles[-1], handles[0]
        dy = dx
        if l < N_A:
            dx, d_pw, d_ps[l], d_n1[l] = _pool_bwd(x_in, dy, after(row(norm1_g[l])), pw[l], row(pscale[l]))
            mixer_grads.append((d_pw.astype(_MXU), 1))
        else:
            j = l - N_A
            dx, dq, h, d_cur, d_prev, d_bq[j], d_bo[j], d_n1[l], d_sk[j] = _attn_bwd(
                x_in, dy, qs[j], os_[j], after(row(norm1_g[l])), wq[j], sinks[j], kvd, wo[j])
            mixer_grads += [(_tn_matmul(h, dq, "tn_q"), 0), (_tn_matmul(os_[j], dy, "tn_o"), 0)]
            dkv_parts += [d_cur, d_prev]
        (scatters[2 * l + 1],), token = _start_copies("scatter_mixer_%d" % l, [_scatter_group(mixer_grads)])

    late = _all_reduce_small(_pack([d_n1[0], d_ps[0]])).reshape(-1)
    (early,), = _wait_copies("small_wait", [small_handle], late)
    given = dict(norm1_g=norm1_g, norm2_g=norm2_g, kv_norm_g=kv_norm_g, b_kv=b_kv, b_q=b_q, sinks=sinks, b_o=b_o,
                 ffn_conv_b=ffn_conv_b, final_g=final_g, pool_scale=pool_scale, ffn_conv_w=ffn_conv_w)
    tot = _unpack(_sum8([early]), [given[k].shape for k in rep_names] + [(N_A, D), (DEPTH, CONV_W, F2), ()])
    tot[0] = tot[0].at[0].add(late[:D])
    tot[-3] = tot[-3].at[0].add(late[D:2 * D])
    grad = dict(zip(rep_names, tot))
    loss = tot[-1]
    grad["pool_scale"] = lax.dynamic_slice_in_dim(tot[-3], me * (D // N_DEV), D // N_DEV, axis=1)
    grad["ffn_conv_w"] = lax.dynamic_slice_in_dim(tot[-2], me * F2s, F2s, axis=2)

    moms = dict(norm1_g=(m_norm1_g, v_norm1_g), norm2_g=(m_norm2_g, v_norm2_g), pool_w=(m_pool_w, v_pool_w),
                pool_scale=(m_pool_scale, v_pool_scale), kv_norm_g=(m_kv_norm_g, v_kv_norm_g), w_kv=(m_w_kv, v_w_kv),
                b_kv=(m_b_kv, v_b_kv), w_q=(m_w_q, v_w_q), b_q=(m_b_q, v_b_q), sinks=(m_sinks, v_sinks),
                w_o=(m_w_o, v_w_o), b_o=(m_b_o, v_b_o), ffn_up=(m_ffn_up, v_ffn_up),
                ffn_conv_w=(m_ffn_conv_w, v_ffn_conv_w), ffn_conv_b=(m_ffn_conv_b, v_ffn_conv_b),
                ffn_down=(m_ffn_down, v_ffn_down), final_g=(m_final_g, v_final_g))
    given.update(pool_w=pool_w, w_kv=w_kv, w_q=w_q, w_o=w_o, ffn_up=ffn_up, ffn_down=ffn_down)
    delta, new_m, new_v = {}, {}, {}

    small_names = rep_names + ["pool_scale", "ffn_conv_w"]
    shapes = [given[k].shape for k in small_names]
    outs = _adamw(_pack([grad[k] for k in small_names]), _pack([given[k] for k in small_names]),
                  _pack([moms[k][0] for k in small_names]), _pack([moms[k][1] for k in small_names]), "adamw_small")
    for dst, packed in zip((delta, new_m, new_v), outs):
        dst.update(zip(small_names, _unpack(packed, shapes)))

    zones = _wait_copies("scatter_wait", scatters, outs[0])
    ffn_z, mix_z = zones[0::2], zones[1::2]
    by_name = dict(ffn_up=[z[0] for z in ffn_z], ffn_down=[z[1] for z in ffn_z],
                   w_q=[mix_z[N_A][0], mix_z[N_A + 1][0]], w_o=[mix_z[N_A][1], mix_z[N_A + 1][1]],
                   w_kv=[mix_z[N_A - 1][0]], pool_w=[mix_z[0][0], mix_z[N_A - 1][1]])

    def update(name, g, cols):
        w = given[name]
        two_d = lambda a: a.reshape(-1, cols)
        if g is None:
            landed = [z.reshape(N_DEV, -1, cols) for z in by_name[name]]
            outs = _adamw_landed(landed, two_d(w), two_d(moms[name][0]), two_d(moms[name][1]), "adamw_" + name)
            grad[name] = outs[0].reshape(w.shape)
        else:
            outs = _adamw(g, two_d(w), two_d(moms[name][0]), two_d(moms[name][1]), "adamw_" + name)
        delta[name], new_m[name], new_v[name] = (o.reshape(w.shape) for o in outs[-3:])

    grad["ffn_up"] = jnp.swapaxes(_sum8(by_name["ffn_up"]).reshape(DEPTH, F2s, D), 1, 2)
    update("ffn_up", grad["ffn_up"].reshape(-1, F2s), F2s)
    update("ffn_down", None, D)
    update("w_q", None, D)
    update("w_o", None, D)
    update("w_kv", None, w_kv.shape[1])
    update("pool_w", None, GC)

    names = ["norm1_g", "norm2_g", "pool_w", "pool_scale", "kv_norm_g", "w_kv", "b_kv", "w_q", "b_q", "sinks", "w_o",
             "b_o", "ffn_up", "ffn_conv_w", "ffn_conv_b", "ffn_down", "final_g"]
    return (loss, dx.reshape(x.shape), *[grad[k] for k in names], *[delta[k] for k in names],
            *[new_m[k] for k in names], *[new_v[k] for k in names])
```

```python
import functools

import jax
import jax.numpy as jnp
from jax import lax
from jax.experimental import pallas as pl
from jax.experimental.pallas import tpu as pltpu

_F32 = jnp.float32
_MXU = jnp.bfloat16

N_DEV = 8
D = 1024
DEPTH = 4
N_A = 2
POOL_WINDOWS = (2, 4, 8, 16)
GC = D // len(POOL_WINDOWS)
HALO = 16
HEAD_DIM = 64
N_HEADS = D // HEAD_DIM
GROUP = 8
N_KV = N_HEADS // GROUP
BLK = 128
PAIR = 2 * HEAD_DIM
KVD = 4 * N_KV * HEAD_DIM
CONV_W = 3
EPS = 1e-5
NEG = -1e30

ADAM_LR = 0.001
ADAM_B1 = 0.9
ADAM_B2 = 0.999
ADAM_EPS = 1e-08
ADAM_WD = 0.01
ADAM_STEP = 10

V7X_VMEM_LIMIT = 56 * 1024 * 1024
LANES = 128

_NT = (((1,), (1,)), ((), ()))
_TN = (((0,), (0,)), ((), ()))


def _params(**kw):
    return pltpu.CompilerParams(vmem_limit_bytes=V7X_VMEM_LIMIT, **kw)


def _seq(n=1):
    return _params(dimension_semantics=("arbitrary",) * n)


def _dot(a, b, dims=None):
    if dims is None:
        return jnp.dot(a, b, preferred_element_type=_F32)
    return lax.dot_general(a, b, dims, preferred_element_type=_F32)


def _rms(x):
    r = lax.rsqrt(jnp.mean(x * x, axis=-1, keepdims=True) + EPS)
    return x * r, r


def _rms_bwd(dh, xn, r, g):
    dxn = dh * g
    return r * (dxn - xn * jnp.mean(dxn * xn, axis=-1, keepdims=True))


def _colsum(a):
    return jnp.sum(a, axis=0, keepdims=True)


def _tile(n, want, mult=8):
    for t in range(min(want, n), 0, -1):
        if n % t == 0 and t % mult == 0:
            return t
    return n


def _full(shape):
    zeros = (0,) * len(shape)
    return pl.BlockSpec(shape, lambda *_: zeros)


def _pool_windows(hbuf, h, row, T):
    out = []
    for gi, win in enumerate(POOL_WINDOWS):
        cs = slice(gi * GC, (gi + 1) * GC)
        acc = hbuf[HALO:HALO + T, cs]
        for k in range(1, win):
            acc = acc + hbuf[HALO - k:HALO - k + T, cs]
        cnt = jnp.minimum(row + 1, win).astype(_F32)
        out.append((acc / cnt - h[:, cs], cnt))
    return out


def _pool_fwd(x, g, w, sc):
    S = x.shape[0]
    T = _tile(S, 512, HALO)
    n, hb = S // T, T // HALO

    def body(x_ref, xh_ref, g_ref, w_ref, sc_ref, o_ref, hbuf):
        i = pl.program_id(0)
        gv = g_ref[...]
        xv = x_ref[...]
        h = _rms(xv)[0] * gv
        hbuf[0:HALO, :] = jnp.where(i > 0, _rms(xh_ref[...])[0] * gv, 0.0)
        hbuf[HALO:, :] = h
        row = i * T + lax.broadcasted_iota(jnp.int32, (T, 1), 0)
        for gi, (p, _) in enumerate(_pool_windows(hbuf, h, row, T)):
            cs = slice(gi * GC, (gi + 1) * GC)
            z = _dot(p.astype(_MXU), w_ref[gi])
            o_ref[:, cs] = xv[:, cs] + z * sc_ref[:, cs]

    return pl.pallas_call(
        body, name="pool_fwd", grid=(n,),
        in_specs=[pl.BlockSpec((T, D), lambda i: (i, 0)),
                  pl.BlockSpec((HALO, D), lambda i: (jnp.maximum(i * hb - 1, 0), 0)),
                  _full((1, D)), _full((4, GC, GC)), _full((1, D))],
        out_specs=pl.BlockSpec((T, D), lambda i: (i, 0)),
        out_shape=jax.ShapeDtypeStruct((S, D), _F32),
        scratch_shapes=[pltpu.VMEM((T + HALO, D), _F32)],
        compiler_params=_seq(),
    )(x, x, g, w, sc)


def _pool_bwd(x, dy, g, w, sc):
    S = x.shape[0]
    T = _tile(S, 512, HALO)
    n, hb = S // T, T // HALO

    def body(x_ref, xh_ref, dy_ref, dyh_ref, g_ref, w_ref, sc_ref, dx_ref, dw_ref, dsc_ref, dg_ref,
             hbuf, qbuf, dhbuf):
        i = pl.program_id(0)

        @pl.when(i == 0)
        def _():
            dw_ref[...] = jnp.zeros_like(dw_ref)
            dsc_ref[...] = jnp.zeros_like(dsc_ref)
            dg_ref[...] = jnp.zeros_like(dg_ref)

        gv = g_ref[...]
        xv = x_ref[...]
        xn, r = _rms(xv)
        h = xn * gv
        hbuf[0:HALO, :] = jnp.where(i > 0, _rms(xh_ref[...])[0] * gv, 0.0)
        hbuf[HALO:, :] = h
        dyv = dy_ref[...]
        dz = dyv * sc_ref[...]
        dzh = jnp.where(i < n - 1, dyh_ref[...], 0.0) * sc_ref[...]
        row = i * T + lax.broadcasted_iota(jnp.int32, (T, 1), 0)
        rowh = (i + 1) * T + lax.broadcasted_iota(jnp.int32, (HALO, 1), 0)
        for gi, (p, cnt) in enumerate(_pool_windows(hbuf, h, row, T)):
            win = POOL_WINDOWS[gi]
            cs = slice(gi * GC, (gi + 1) * GC)
            pb = p.astype(_MXU)
            wg = w_ref[gi]
            dsc_ref[:, cs] += _colsum(dyv[:, cs] * _dot(pb, wg))
            dzb = dz[:, cs].astype(_MXU)
            dw_ref[gi] += _dot(pb, dzb, _TN)
            dp = _dot(dzb, wg, _NT)
            dph = _dot(dzh[:, cs].astype(_MXU), wg, _NT)
            qbuf[0:T, cs] = dp / cnt
            qbuf[T:T + HALO, cs] = dph / jnp.minimum(rowh + 1, win).astype(_F32)
            acc = qbuf[0:T, cs]
            for k in range(1, win):
                acc = acc + qbuf[k:k + T, cs]
            dhbuf[:, cs] = acc - dp
        dh = dhbuf[...]
        dg_ref[...] += _colsum(dh * xn)
        dx_ref[...] = dyv + _rms_bwd(dh, xn, r, gv)

    return pl.pallas_call(
        body, name="pool_bwd", grid=(n,),
        in_specs=[pl.BlockSpec((T, D), lambda i: (i, 0)),
                  pl.BlockSpec((HALO, D), lambda i: (jnp.maximum(i * hb - 1, 0), 0)),
                  pl.BlockSpec((T, D), lambda i: (i, 0)),
                  pl.BlockSpec((HALO, D), lambda i: (jnp.minimum((i + 1) * hb, S // HALO - 1), 0)),
                  _full((1, D)), _full((4, GC, GC)), _full((1, D))],
        out_specs=[pl.BlockSpec((T, D), lambda i: (i, 0)), _full((4, GC, GC)), _full((1, D)), _full((1, D))],
        out_shape=[jax.ShapeDtypeStruct((S, D), _F32), jax.ShapeDtypeStruct((4, GC, GC), _F32),
                   jax.ShapeDtypeStruct((1, D), _F32), jax.ShapeDtypeStruct((1, D), _F32)],
        scratch_shapes=[pltpu.VMEM((T + HALO, D), _F32), pltpu.VMEM((T + HALO, D), _F32), pltpu.VMEM((T, D), _F32)],
        compiler_params=_seq(),
    )(x, x, dy, dy, g, w, sc)


FFN_FWD_TILE, FFN_FWD_CHUNKS = 256, 2
FFN_BWD_TILE, FFN_BWD_CHUNKS = 128, 2
EDGE = 8


def _shift_down(v, k, prev):
    r = pltpu.roll(v, k, axis=0)
    i8 = lax.broadcasted_iota(jnp.int32, (EDGE, v.shape[1]), 0)
    head = jnp.where(i8 >= k, r[0:EDGE, :], pltpu.roll(prev, k, axis=0))
    return jnp.concatenate([head, r[EDGE:, :]], axis=0)


def _shift_up(v, k, nxt):
    T = v.shape[0]
    r = pltpu.roll(v, T - k, axis=0)
    i8 = lax.broadcasted_iota(jnp.int32, (EDGE, v.shape[1]), 0)
    tail = jnp.where(i8 < EDGE - k, r[T - EDGE:, :], pltpu.roll(nxt, EDGE - k, axis=0))
    return jnp.concatenate([r[:T - EDGE, :], tail], axis=0)


def _load_weights(i, pairs, sems):
    @pl.when(i == 0)
    def _():
        cps = [pltpu.make_async_copy(src, dst, sems.at[k]) for k, (src, dst) in enumerate(pairs)]
        for cp in cps:
            cp.start()
        for cp in cps:
            cp.wait()


def _ffn_fwd(x, g, wup_t, cw, cb, wdn):
    S = x.shape[0]
    F2 = wup_t.shape[0]
    F = F2 // 2
    C = F // FFN_FWD_CHUNKS
    T = _tile(S, FFN_FWD_TILE, 16)
    n = S // T

    def body(x_ref, g_ref, wup_hbm, cw_ref, cb_ref, wdn_hbm, o_ref, u_ref, c_ref, wup, wdnv, carry, sems):
        i = pl.program_id(0)
        _load_weights(i, [(wup_hbm, wup), (wdn_hbm, wdnv)], sems)

        @pl.when(i == 0)
        def _():
            carry[...] = jnp.zeros_like(carry)

        xv = x_ref[...]
        hb = (_rms(xv)[0] * g_ref[...]).astype(_MXU)
        acc = jnp.zeros((T, D), _F32)
        for j in range(FFN_FWD_CHUNKS):
            halves = []
            for cs in (slice(j * C, (j + 1) * C), slice(F + j * C, F + (j + 1) * C)):
                u = _dot(hb, wup[cs, :], _NT)
                u_ref[:, cs] = u.astype(u_ref.dtype)
                prev = carry[:, cs]
                carry[:, cs] = u[T - EDGE:, :]
                c = (cw_ref[0:1, cs] * _shift_down(u, 2, prev) + cw_ref[1:2, cs] * _shift_down(u, 1, prev)
                     + cw_ref[2:3, cs] * u + cb_ref[:, cs])
                c_ref[:, cs] = c.astype(c_ref.dtype)
                halves.append(c)
            cg, cv = halves
            a = (cg * jax.nn.sigmoid(cg)) * cv
            acc = acc + _dot(a.astype(_MXU), wdnv[j * C:(j + 1) * C, :])
        o_ref[...] = xv + acc

    any_ = pl.BlockSpec(memory_space=pl.ANY)
    wide = pl.BlockSpec((T, F2), lambda i: (i, 0))
    return pl.pallas_call(
        body, name="ffn_fwd", grid=(n,),
        in_specs=[pl.BlockSpec((T, D), lambda i: (i, 0)), _full((1, D)), any_, _full((CONV_W, F2)), _full((1, F2)), any_],
        out_specs=[pl.BlockSpec((T, D), lambda i: (i, 0)), wide, wide],
        out_shape=[jax.ShapeDtypeStruct((S, D), _F32), jax.ShapeDtypeStruct((S, F2), _MXU),
                   jax.ShapeDtypeStruct((S, F2), _MXU)],
        scratch_shapes=[pltpu.VMEM((F2, D), _MXU), pltpu.VMEM((F, D), _MXU),
                        pltpu.VMEM((EDGE, F2), _F32), pltpu.SemaphoreType.DMA((2,))],
        compiler_params=_seq(),
    )(x, g, wup_t, cw, cb, wdn)


def _ffn_bwd(x, dy, u, c, g, wup_t, cw, wdn):
    S = x.shape[0]
    F2 = wup_t.shape[0]
    F = F2 // 2
    C = F // FFN_BWD_CHUNKS
    T = _tile(S, FFN_BWD_TILE, 16)
    n = S // T

    def body(x_ref, dy_ref, u_ref, c_ref, g_ref, wup_hbm, cw_ref, wdn_hbm,
             dx_ref, du_ref, a_ref, h_ref, dcw_ref, dcb_ref, dg_ref, wup, wdnv, carry, sems):
        i = pl.program_id(0)
        _load_weights(i, [(wup_hbm, wup), (wdn_hbm, wdnv)], sems)

        @pl.when(i == 0)
        def _():
            carry[...] = jnp.zeros_like(carry)
            dcw_ref[...] = jnp.zeros_like(dcw_ref)
            dcb_ref[...] = jnp.zeros_like(dcb_ref)
            dg_ref[...] = jnp.zeros_like(dg_ref)

        gv = g_ref[...]
        xv = x_ref[...]
        xn, r = _rms(xv)
        hbf = (xn * gv).astype(_MXU)
        h_ref[...] = hbf
        dyv = dy_ref[...]
        dyb = dyv.astype(_MXU)
        dh = jnp.zeros((T, D), _F32)
        for j in range(FFN_BWD_CHUNKS):
            gs, vs = slice(j * C, (j + 1) * C), slice(F + j * C, F + (j + 1) * C)
            cg, cv = c_ref[:, gs].astype(_F32), c_ref[:, vs].astype(_F32)
            sg = jax.nn.sigmoid(cg)
            sl = cg * sg
            a_ref[:, gs] = (sl * cv).astype(a_ref.dtype)
            da = _dot(dyb, wdnv[gs, :], _NT)
            for cs, dc in ((gs, da * cv * (sg * (1.0 + cg * (1.0 - sg)))), (vs, da * sl)):
                nxt = carry[:, cs]
                carry[:, cs] = dc[0:EDGE, :]
                dc1, dc2 = _shift_up(dc, 1, nxt), _shift_up(dc, 2, nxt)
                uf = u_ref[:, cs].astype(_F32)
                dcb_ref[:, cs] += _colsum(dc)
                for k, d in enumerate((dc2, dc1, dc)):
                    dcw_ref[k:k + 1, cs] += _colsum(d * uf)
                du = cw_ref[2:3, cs] * dc + cw_ref[1:2, cs] * dc1 + cw_ref[0:1, cs] * dc2
                dub = du.astype(_MXU)
                du_ref[:, cs] = dub
                dh = dh + _dot(dub, wup[cs, :])
        dg_ref[...] += _colsum(dh * xn)
        dx_ref[...] = dyv + _rms_bwd(dh, xn, r, gv)

    any_ = pl.BlockSpec(memory_space=pl.ANY)
    rev = lambda i: (n - 1 - i, 0)
    return pl.pallas_call(
        body, name="ffn_bwd", grid=(n,),
        in_specs=[pl.BlockSpec((T, D), rev), pl.BlockSpec((T, D), rev), pl.BlockSpec((T, F2), rev),
                  pl.BlockSpec((T, F2), rev), _full((1, D)), any_, _full((CONV_W, F2)), any_],
        out_specs=[pl.BlockSpec((T, D), rev), pl.BlockSpec((T, F2), rev), pl.BlockSpec((T, F), rev),
                   pl.BlockSpec((T, D), rev), _full((CONV_W, F2)), _full((1, F2)), _full((1, D))],
        out_shape=[jax.ShapeDtypeStruct((S, D), _F32), jax.ShapeDtypeStruct((S, F2), _MXU),
                   jax.ShapeDtypeStruct((S, F), _MXU), jax.ShapeDtypeStruct((S, D), _MXU),
                   jax.ShapeDtypeStruct((CONV_W, F2), _F32), jax.ShapeDtypeStruct((1, F2), _F32),
                   jax.ShapeDtypeStruct((1, D), _F32)],
        scratch_shapes=[pltpu.VMEM((F2, D), _MXU), pltpu.VMEM((F, D), _MXU),
                        pltpu.VMEM((EDGE, F2), _F32), pltpu.SemaphoreType.DMA((2,))],
        compiler_params=_seq(),
    )(x, dy, u, c, g, wup_t, cw, wdn)


def _tn_matmul(a, b, name, token=None):
    S, M = a.shape
    N = b.shape[1]
    bm = _tile(M, 1408, LANES)
    tk = _tile(S, 512, 16)
    nk = S // tk
    tokens = [] if token is None else [token]

    def body(a_ref, b_ref, *rest):
        o_ref, acc = rest[-2:]
        k = pl.program_id(1)

        @pl.when(k == 0)
        def _():
            acc[...] = jnp.zeros_like(acc)

        acc[...] += _dot(a_ref[...].astype(_MXU), b_ref[...].astype(_MXU), _TN)

        @pl.when(k == nk - 1)
        def _():
            o_ref[...] = acc[...].astype(o_ref.dtype)

    return pl.pallas_call(
        body, name=name, grid=(M // bm, nk),
        in_specs=[pl.BlockSpec((tk, bm), lambda i, k: (k, i)), pl.BlockSpec((tk, N), lambda i, k: (k, 0))]
        + [_full((8, LANES))] * len(tokens),
        out_specs=pl.BlockSpec((bm, N), lambda i, k: (i, 0)),
        out_shape=jax.ShapeDtypeStruct((M, N), _MXU),
        scratch_shapes=[pltpu.VMEM((bm, N), _F32)],
        compiler_params=_seq(2),
    )(a, b, *tokens)


def _kv_fwd(x, g, wkv, bkv):
    S = x.shape[0]
    T = _tile(S, 512, 16)

    def body(x_ref, g_ref, w_ref, b_ref, o_ref):
        hb = (_rms(x_ref[...])[0] * g_ref[...]).astype(_MXU)
        o_ref[...] = (_dot(hb, w_ref[...]) + b_ref[...]).astype(o_ref.dtype)

    return pl.pallas_call(
        body, name="kv_fwd", grid=(S // T,),
        in_specs=[pl.BlockSpec((T, D), lambda i: (i, 0)), _full((1, D)), _full((D, KVD)), _full((1, KVD))],
        out_specs=pl.BlockSpec((T, KVD), lambda i: (i, 0)),
        out_shape=jax.ShapeDtypeStruct((S, KVD), _MXU),
        compiler_params=_seq(),
    )(x, g, wkv, bkv)


def _kv_bwd(x, dx_in, g, wkv, cur_a, prev_a, cur_b, prev_b):
    S = x.shape[0]
    T = _tile(S, 512, BLK)
    n, per = S // T, T // BLK

    def body(x_ref, dxi_ref, g_ref, w_ref, ca, pa, na, cb, pb, nb, dx_ref, dw_ref, db_ref, dg_ref):
        i = pl.program_id(0)

        @pl.when(i == 0)
        def _():
            dw_ref[...] = jnp.zeros_like(dw_ref)
            db_ref[...] = jnp.zeros_like(db_ref)
            dg_ref[...] = jnp.zeros_like(dg_ref)

        gv = g_ref[...]
        xn, r = _rms(x_ref[...])
        nxt = jnp.where(i < n - 1, na[...] + nb[...], 0.0)
        prev = jnp.concatenate([pa[BLK:, :] + pb[BLK:, :], nxt], axis=0) if per > 1 else nxt
        dkv = ca[...] + cb[...] + prev
        db_ref[...] += _colsum(dkv)
        dkb = dkv.astype(_MXU)
        dw_ref[...] += _dot((xn * gv).astype(_MXU), dkb, _TN)
        dh = _dot(dkb, w_ref[...], _NT)
        dg_ref[...] += _colsum(dh * xn)
        dx_ref[...] = dxi_ref[...] + _rms_bwd(dh, xn, r, gv)

    blk = lambda w: pl.BlockSpec((T, w), lambda i: (i, 0))
    nxt = pl.BlockSpec((BLK, KVD), lambda i: (jnp.minimum((i + 1) * per, S // BLK - 1), 0))
    return pl.pallas_call(
        body, name="kv_bwd", grid=(n,),
        in_specs=[blk(D), blk(D), _full((1, D)), _full((D, KVD)), blk(KVD), blk(KVD), nxt, blk(KVD), blk(KVD), nxt],
        out_specs=[blk(D), _full((D, KVD)), _full((1, KVD)), _full((1, D))],
        out_shape=[jax.ShapeDtypeStruct((S, D), _F32), jax.ShapeDtypeStruct((D, KVD), _F32),
                   jax.ShapeDtypeStruct((1, KVD), _F32), jax.ShapeDtypeStruct((1, D), _F32)],
        compiler_params=_seq(),
    )(x, dx_in, g, wkv, cur_a, prev_a, prev_a, cur_b, prev_b, prev_b)


STACK = GROUP * BLK


def _attn_mask(i, rows):
    qi = lax.broadcasted_iota(jnp.int32, (rows, 2 * BLK), 0) & (BLK - 1)
    si = lax.broadcasted_iota(jnp.int32, (rows, 2 * BLK), 1)
    return (si > qi) & (si <= qi + BLK) & jnp.logical_or(i > 0, si >= BLK)


def _low_half():
    return lax.broadcasted_iota(jnp.int32, (BLK, PAIR), 1) < HEAD_DIM


def _stack_heads(ref, kh, dst):
    low = _low_half()
    for pp in range(GROUP // 2):
        pr = kh * (GROUP // 2) + pp
        v2 = ref[:, pr * PAIR:(pr + 1) * PAIR]
        zero = jnp.zeros_like(v2)
        dst[2 * pp * BLK:(2 * pp + 1) * BLK, :] = jnp.where(low, v2, zero)
        dst[(2 * pp + 1) * BLK:(2 * pp + 2) * BLK, :] = jnp.where(low, zero, v2)


def _unstack_heads(st, pp):
    return jnp.where(_low_half(), st[2 * pp * BLK:(2 * pp + 1) * BLK, :], st[(2 * pp + 1) * BLK:(2 * pp + 2) * BLK, :])


def _sink_col(sk_ref, kh):
    return jnp.concatenate([jnp.full((BLK, 1), sk_ref[kh * GROUP + h], _F32) for h in range(GROUP)], axis=0)


def _head_probs(qm, kd, mask, sink):
    s = jnp.where(mask, _dot(qm, kd, _NT) * (HEAD_DIM ** -0.5), NEG)
    m = jnp.maximum(jnp.max(s, axis=-1, keepdims=True), sink)
    p = jnp.exp(s - m)
    es = jnp.exp(sink - m)
    inv = 1.0 / (jnp.sum(p, axis=-1, keepdims=True) + es)
    return p * inv, es * inv


def _attn_fwd(x, g, wq, bq, sinks, kvd, wo, bo):
    S = x.shape[0]
    n = S // BLK

    def body(x_ref, g_ref, wq_ref, bq_ref, sk_ref, kp_ref, kc_ref, wo_ref, bo_ref, xo_ref, q_ref, o_ref, win):
        i = pl.program_id(0)
        xv = x_ref[...]
        hb = (_rms(xv)[0] * g_ref[...]).astype(_MXU)
        q_ref[...] = (_dot(hb, wq_ref[...]) + bq_ref[...]).astype(q_ref.dtype)
        win[0:BLK, :] = kp_ref[...]
        win[BLK:, :] = kc_ref[...]
        mask = _attn_mask(i, BLK)
        low = _low_half()
        for pr in range(N_HEADS // 2):
            kh = (2 * pr) // GROUP
            kd = win[:, kh * PAIR:(kh + 1) * PAIR]
            vd = win[:, (N_KV + kh) * PAIR:(N_KV + kh + 1) * PAIR]
            q2 = q_ref[:, pr * PAIR:(pr + 1) * PAIR]
            outs = []
            for half in range(2):
                qm = jnp.where(low if half == 0 else ~low, q2, jnp.zeros_like(q2))
                pbs, _ = _head_probs(qm, kd, mask, sk_ref[2 * pr + half])
                outs.append(_dot(pbs.astype(_MXU), vd))
            o_ref[:, pr * PAIR:(pr + 1) * PAIR] = jnp.where(low, outs[0], outs[1]).astype(o_ref.dtype)
        xo_ref[...] = xv + _dot(o_ref[...], wo_ref[...]) + bo_ref[...]

    blk = lambda w: pl.BlockSpec((BLK, w), lambda i: (i, 0))
    return pl.pallas_call(
        body, name="attn_fwd", grid=(n,),
        in_specs=[blk(D), _full((1, D)), _full((D, D)), _full((1, D)),
                  pl.BlockSpec(memory_space=pltpu.SMEM),
                  pl.BlockSpec((BLK, KVD), lambda i: (jnp.maximum(i - 1, 0), 0)), blk(KVD),
                  _full((D, D)), _full((1, D))],
        out_specs=[blk(D), blk(D), blk(D)],
        out_shape=[jax.ShapeDtypeStruct((S, D), _F32), jax.ShapeDtypeStruct((S, D), _MXU),
                   jax.ShapeDtypeStruct((S, D), _MXU)],
        scratch_shapes=[pltpu.VMEM((2 * BLK, KVD), _MXU)],
        compiler_params=_seq(),
    )(x, g, wq, bq, sinks, kvd, kvd, wo, bo)


def _attn_bwd(x, dy, q, o, g, wq, sinks, kvd, wo):
    S = x.shape[0]
    n = S // BLK
    all_rows = N_HEADS * BLK

    def body(x_ref, dy_ref, q_ref, o_ref, g_ref, wq_ref, sk_ref, kp_ref, kc_ref, wo_ref,
             dx_ref, dq_ref, h_ref, dc_ref, dp_ref, dbq_ref, dbo_ref, dg_ref, dsk_ref, win, dob, qs, dos, pall, dsall):
        i = pl.program_id(0)

        @pl.when(i == 0)
        def _():
            dbq_ref[...] = jnp.zeros_like(dbq_ref)
            dbo_ref[...] = jnp.zeros_like(dbo_ref)
            dg_ref[...] = jnp.zeros_like(dg_ref)
            dsk_ref[...] = jnp.zeros_like(dsk_ref)

        gv = g_ref[...]
        xv = x_ref[...]
        xn, r = _rms(xv)
        h_ref[...] = (xn * gv).astype(h_ref.dtype)
        dyv = dy_ref[...]
        dbo_ref[...] += _colsum(dyv)
        dob[...] = _dot(dyv.astype(_MXU), wo_ref[...], _NT).astype(dob.dtype)
        win[0:BLK, :] = kp_ref[...]
        win[BLK:, :] = kc_ref[...]
        mask = _attn_mask(i, BLK)
        low = _low_half()
        lane = lax.broadcasted_iota(jnp.int32, (1, LANES), 1)
        for pr in range(N_HEADS // 2):
            kh = (2 * pr) // GROUP
            kd = win[:, kh * PAIR:(kh + 1) * PAIR]
            vd = win[:, (N_KV + kh) * PAIR:(N_KV + kh + 1) * PAIR]
            q2 = q_ref[:, pr * PAIR:(pr + 1) * PAIR]
            do2 = dob[:, pr * PAIR:(pr + 1) * PAIR]
            od = do2.astype(_F32) * o_ref[:, pr * PAIR:(pr + 1) * PAIR].astype(_F32)
            for half in range(2):
                hd = 2 * pr + half
                rows = slice(hd * BLK, (hd + 1) * BLK)
                sel = low if half == 0 else ~low
                qm = jnp.where(sel, q2, jnp.zeros_like(q2))
                dom = jnp.where(sel, do2, jnp.zeros_like(do2))
                qs[rows, :] = qm
                dos[rows, :] = dom
                pbs, ps = _head_probs(qm, kd, mask, sk_ref[hd])
                pall[rows, :] = pbs.astype(_MXU)
                delta = jnp.sum(jnp.where(sel, od, 0.0), axis=-1, keepdims=True)
                dsall[rows, :] = (pbs * (_dot(dom, vd, _NT) - delta) * (HEAD_DIM ** -0.5)).astype(_MXU)
                dsk_ref[...] -= jnp.where(lane == hd, _colsum(ps * delta), 0.0)
        dq_all = []
        for kh in range(N_KV):
            ks = slice(kh * PAIR, (kh + 1) * PAIR)
            vs = slice((N_KV + kh) * PAIR, (N_KV + kh + 1) * PAIR)
            rows = slice(kh * STACK, (kh + 1) * STACK)
            dqst = _dot(dsall[rows, :], win[:, ks])
            dk = _dot(dsall[rows, :], qs[rows, :], _TN)
            dv = _dot(pall[rows, :], dos[rows, :], _TN)
            dp_ref[:, ks], dc_ref[:, ks] = dk[0:BLK, :], dk[BLK:, :]
            dp_ref[:, vs], dc_ref[:, vs] = dv[0:BLK, :], dv[BLK:, :]
            dq_all += [_unstack_heads(dqst, pp) for pp in range(GROUP // 2)]
        dq = jnp.concatenate(dq_all, axis=1)
        dbq_ref[...] += _colsum(dq)
        dqb = dq.astype(_MXU)
        dq_ref[...] = dqb
        dh = _dot(dqb, wq_ref[...], _NT)
        dg_ref[...] += _colsum(dh * xn)
        dx_ref[...] = dyv + _rms_bwd(dh, xn, r, gv)

    blk = lambda w: pl.BlockSpec((BLK, w), lambda i: (i, 0))
    return pl.pallas_call(
        body, name="attn_bwd", grid=(n,),
        in_specs=[blk(D), blk(D), blk(D), blk(D), _full((1, D)), _full((D, D)),
                  pl.BlockSpec(memory_space=pltpu.SMEM),
                  pl.BlockSpec((BLK, KVD), lambda i: (jnp.maximum(i - 1, 0), 0)), blk(KVD), _full((D, D))],
        out_specs=[blk(D), blk(D), blk(D), blk(KVD), blk(KVD),
                   _full((1, D)), _full((1, D)), _full((1, D)), _full((1, LANES))],
        out_shape=[jax.ShapeDtypeStruct((S, D), _F32), jax.ShapeDtypeStruct((S, D), _MXU),
                   jax.ShapeDtypeStruct((S, D), _MXU), jax.ShapeDtypeStruct((S, KVD), _F32),
                   jax.ShapeDtypeStruct((S, KVD), _F32), jax.ShapeDtypeStruct((1, D), _F32),
                   jax.ShapeDtypeStruct((1, D), _F32), jax.ShapeDtypeStruct((1, D), _F32),
                   jax.ShapeDtypeStruct((1, LANES), _F32)],
        scratch_shapes=[pltpu.VMEM((2 * BLK, KVD), _MXU), pltpu.VMEM((BLK, D), _MXU),
                        pltpu.VMEM((all_rows, PAIR), _MXU), pltpu.VMEM((all_rows, PAIR), _MXU),
                        pltpu.VMEM((all_rows, 2 * BLK), _MXU), pltpu.VMEM((all_rows, 2 * BLK), _MXU)],
        compiler_params=_seq(),
    )(x, dy, q, o, g, wq, sinks, kvd, kvd, wo)


def _loss_bwd(x, g, tgt):
    S = x.shape[0]
    T = _tile(S, 512, 8)

    def body(x_ref, g_ref, t_ref, dx_ref, ls_ref, dg_ref):
        @pl.when(pl.program_id(0) == 0)
        def _():
            ls_ref[...] = jnp.zeros_like(ls_ref)
            dg_ref[...] = jnp.zeros_like(dg_ref)

        gv = g_ref[...]
        xn, r = _rms(x_ref[...])
        err = xn * gv - t_ref[...]
        ls_ref[...] += 0.5 * jnp.sum(jnp.mean(err * err, axis=-1, keepdims=True))
        dyv = err * (1.0 / D)
        dg_ref[...] += _colsum(dyv * xn)
        dx_ref[...] = _rms_bwd(dyv, xn, r, gv)

    return pl.pallas_call(
        body, name="loss_bwd", grid=(S // T,),
        in_specs=[pl.BlockSpec((T, D), lambda i: (i, 0)), _full((1, D)), pl.BlockSpec((T, D), lambda i: (i, 0))],
        out_specs=[pl.BlockSpec((T, D), lambda i: (i, 0)), _full((8, LANES)), _full((1, D))],
        out_shape=[jax.ShapeDtypeStruct((S, D), _F32), jax.ShapeDtypeStruct((8, LANES), _F32),
                   jax.ShapeDtypeStruct((1, D), _F32)],
        compiler_params=_seq(),
    )(x, g, tgt)


def _me():
    return 4 * lax.axis_index("x") + 2 * lax.axis_index("y") + lax.axis_index("c")


def _peer(j):
    x, y, c = lax.axis_index("x"), lax.axis_index("y"), lax.axis_index("c")
    px = 1 - x if j & 4 else x
    py = 1 - y if j & 2 else y
    pc = 1 - c if j & 1 else c
    return (px, py, pc), 4 * px + 2 * py + pc


_HBM = pl.BlockSpec(memory_space=pltpu.HBM)
_SEMS = pl.BlockSpec(memory_space=pltpu.SEMAPHORE)
_EFFECT = pltpu.SideEffectType.DATAFLOW_SIDE_EFFECTING


def _in_hbm(a):
    return pltpu.with_memory_space_constraint(a, pltpu.HBM)


def _start_copies(name, groups):
    flat = []
    for srcs, zones, _ in groups:
        flat += [_in_hbm(a) for a in srcs] + [_in_hbm(lax.empty(z.shape, z.dtype)) for z in zones]
    n_in, n_g = len(flat), len(groups)

    def body(*refs):
        sems = refs[n_in:n_in + 2 * n_g]
        me, k = _me(), 0
        for gi, (srcs, zones, plan) in enumerate(groups):
            src_refs, zone_refs = refs[k:k + len(srcs)], refs[k + len(srcs):k + len(srcs) + len(zones)]
            k += len(srcs) + len(zones)
            for t, (si, zi, src_of, dst_of) in enumerate(plan):
                for j in range(1, N_DEV):
                    dev, pk = _peer(j)
                    pltpu.make_async_remote_copy(
                        src_ref=src_of(src_refs[si], pk), dst_ref=dst_of(zone_refs[zi], me),
                        send_sem=sems[2 * gi].at[t * (N_DEV - 1) + j - 1], recv_sem=sems[2 * gi + 1].at[t * (N_DEV - 1) + j - 1],
                        device_id=dev, device_id_type=pl.DeviceIdType.MESH).start()
                pltpu.make_async_copy(src_of(src_refs[si], me), dst_of(zone_refs[zi], me),
                                      sems[2 * gi].at[len(plan) * (N_DEV - 1) + t]).start()
        refs[-1][...] = jnp.zeros_like(refs[-1])

    sem_shapes = []
    for _, _, plan in groups:
        sem_shapes += [pltpu.SemaphoreType.DMA((len(plan) * N_DEV,)), pltpu.SemaphoreType.DMA((len(plan) * (N_DEV - 1),))]
    outs = pl.pallas_call(
        body, name=name,
        out_shape=(*sem_shapes, *[pltpu.HBM(a.shape, a.dtype) for a in flat], jax.ShapeDtypeStruct((8, LANES), _F32)),
        in_specs=[_HBM] * n_in,
        out_specs=(*[_SEMS] * (2 * n_g), *[_HBM] * n_in, pl.BlockSpec(memory_space=pltpu.VMEM)),
        input_output_aliases={k: 2 * n_g + k for k in range(n_in)},
        compiler_params=pltpu.CompilerParams(has_side_effects=_EFFECT),
    )(*flat)
    handles, k = [], 2 * n_g
    for gi, (srcs, zones, plan) in enumerate(groups):
        ns, nz = len(srcs), len(zones)
        handles.append((outs[2 * gi], outs[2 * gi + 1], list(outs[k:k + ns]), list(outs[k + ns:k + ns + nz]), plan))
        k += ns + nz
    return handles, outs[-1]


def _wait_copies(name, handles, after):
    flat = []
    for _, _, srcs, zones, _ in handles:
        flat += srcs + zones
    n_in, n_g = len(flat), len(handles)

    def body(*refs):
        sems = refs[n_in:n_in + 2 * n_g]
        me, k, local, remote = _me(), 0, [], []
        for gi, (_, _, srcs, zones, plan) in enumerate(handles):
            ns, nz = len(srcs), len(zones)
            src_refs, zone_refs = refs[k:k + ns], refs[k + ns:k + ns + nz]
            k += ns + nz
            for t, (si, zi, src_of, dst_of) in enumerate(plan):
                local.append(pltpu.make_async_copy(src_of(src_refs[si], me), dst_of(zone_refs[zi], me),
                                                   sems[2 * gi].at[len(plan) * (N_DEV - 1) + t]))
                for j in range(1, N_DEV):
                    dev, pk = _peer(j)
                    remote.append(pltpu.make_async_remote_copy(
                        src_ref=src_of(src_refs[si], pk), dst_ref=dst_of(zone_refs[zi], pk),
                        send_sem=sems[2 * gi].at[t * (N_DEV - 1) + j - 1], recv_sem=sems[2 * gi + 1].at[t * (N_DEV - 1) + j - 1],
                        device_id=dev, device_id_type=pl.DeviceIdType.MESH))
        for cp in remote:
            cp.wait_send()
            cp.wait_recv()
        for cp in local:
            cp.wait()

    sem_args = []
    for send, recv, _, _, _ in handles:
        sem_args += [send, recv]
    outs = pl.pallas_call(
        body, name=name, out_shape=tuple(pltpu.HBM(a.shape, a.dtype) for a in flat),
        in_specs=[_HBM] * n_in + [_SEMS] * (2 * n_g) + [pl.BlockSpec(memory_space=pl.ANY)],
        out_specs=tuple([_HBM] * n_in), input_output_aliases={k: k for k in range(n_in)},
        compiler_params=pltpu.CompilerParams(has_side_effects=_EFFECT),
    )(*flat, *sem_args, after)
    res, k = [], 0
    for _, _, srcs, zones, _ in handles:
        res.append(list(outs[k + len(srcs):k + len(srcs) + len(zones)]))
        k += len(srcs) + len(zones)
    return res


def _rows(axis, size):
    def of(ref, b):
        start = b * size
        if size % 8 == 0:
            start = pl.multiple_of(start, 8)
        return ref.at[(slice(None),) * axis + (pl.ds(start, size),)]
    return of


def _whole(ref, b):
    return ref


def _slot(ref, b):
    return ref.at[b]


def _gather_group(shards):
    zones, plan = [], []
    for k, (a, axis) in enumerate(shards):
        zones.append(jax.ShapeDtypeStruct(a.shape[:axis] + (N_DEV * a.shape[axis],) + a.shape[axis + 1:], a.dtype))
        plan.append((k, k, _whole, _rows(axis, a.shape[axis])))
    return [a for a, _ in shards], zones, plan


def _scatter_group(grads):
    zones, plan = [], []
    for k, (a, axis) in enumerate(grads):
        size = a.shape[axis] // N_DEV
        zones.append(jax.ShapeDtypeStruct((N_DEV,) + a.shape[:axis] + (size,) + a.shape[axis + 1:], a.dtype))
        plan.append((k, k, _rows(axis, size), _slot))
    return [a for a, _ in grads], zones, plan


def _all_reduce_small(p):
    R = p.shape[0]

    def body(p_ref, o_ref, land, send_sems, recv_sems):
        me = _me()
        land[me] = p_ref[...]
        waits = []
        for j in range(1, N_DEV):
            dev, pk = _peer(j)
            pltpu.make_async_remote_copy(
                src_ref=p_ref, dst_ref=land.at[me], send_sem=send_sems.at[j - 1], recv_sem=recv_sems.at[j - 1],
                device_id=dev, device_id_type=pl.DeviceIdType.MESH).start()
            waits.append(pltpu.make_async_remote_copy(
                src_ref=p_ref, dst_ref=land.at[pk], send_sem=send_sems.at[j - 1], recv_sem=recv_sems.at[j - 1],
                device_id=dev, device_id_type=pl.DeviceIdType.MESH))
        for cp in waits:
            cp.wait()
        tot = land[0]
        for b in range(1, N_DEV):
            tot = tot + land[b]
        o_ref[...] = tot

    vmem = pl.BlockSpec(memory_space=pltpu.VMEM)
    return pl.pallas_call(
        body, name="all_reduce_small", in_specs=[vmem], out_specs=vmem,
        out_shape=jax.ShapeDtypeStruct((R, LANES), _F32),
        scratch_shapes=[pltpu.VMEM((N_DEV, R, LANES), _F32), pltpu.SemaphoreType.DMA((N_DEV - 1,)),
                        pltpu.SemaphoreType.DMA((N_DEV - 1,))],
        compiler_params=_params(),
    )(p)


def _sum_landed(land):
    g = land[0].astype(_F32)
    for b in range(1, N_DEV):
        g = g + land[b].astype(_F32)
    return g


def _landed_specs(n_layers, tr, C, nr):
    def spec(k):
        return pl.BlockSpec((N_DEV, tr, C), lambda l, i: (0, jnp.where(l == k, i, jnp.where(l < k, 0, nr - 1)), 0))
    return [spec(k) for k in range(n_layers)]


def _per_layer(l, zone_refs, fn):
    for k, ref in enumerate(zone_refs):
        @pl.when(l == k)
        def _(ref=ref):
            fn(_sum_landed(ref))


def _sum8(zones):
    L = len(zones)
    _, R, C = zones[0].shape
    tr = _tile(R, 352, 16)
    nr = R // tr

    def body(*refs):
        o_ref = refs[L]

        def put(g):
            o_ref[...] = g

        _per_layer(pl.program_id(0), refs[:L], put)

    return pl.pallas_call(
        body, name="sum8", grid=(L, nr), in_specs=_landed_specs(L, tr, C, nr),
        out_specs=pl.BlockSpec((tr, C), lambda l, i: (l * nr + i, 0)),
        out_shape=jax.ShapeDtypeStruct((L * R, C), _F32), compiler_params=_seq(2),
    )(*zones)


def _adam_update(gv, w_ref, m_ref, v_ref, d_ref, mo_ref, vo_ref):
    mn = ADAM_B1 * m_ref[...] + (1.0 - ADAM_B1) * gv
    vn = ADAM_B2 * v_ref[...] + (1.0 - ADAM_B2) * (gv * gv)
    mo_ref[...] = mn
    vo_ref[...] = vn
    d_ref[...] = -ADAM_LR * ((mn / (1.0 - ADAM_B1 ** ADAM_STEP)) / (jnp.sqrt(vn / (1.0 - ADAM_B2 ** ADAM_STEP)) + ADAM_EPS)
                             + ADAM_WD * w_ref[...])


def _adamw(g, w, m, v, name):
    R, C = w.shape
    tr = _tile(R, 256, 16)

    def body(g_ref, w_ref, m_ref, v_ref, d_ref, mo_ref, vo_ref):
        _adam_update(g_ref[...], w_ref, m_ref, v_ref, d_ref, mo_ref, vo_ref)

    row = pl.BlockSpec((tr, C), lambda i: (i, 0))
    return pl.pallas_call(
        body, name=name, grid=(R // tr,), in_specs=[row] * 4, out_specs=[row] * 3,
        out_shape=[jax.ShapeDtypeStruct((R, C), _F32)] * 3, compiler_params=_seq(),
    )(g, w, m, v)


def _adamw_landed(zones, w, m, v, name):
    L = len(zones)
    _, R, C = zones[0].shape
    tr = _tile(R, 176, 16)
    nr = R // tr

    def body(*refs):
        w_ref, m_ref, v_ref, g_ref, d_ref, mo_ref, vo_ref = refs[L:]

        def update(g):
            g_ref[...] = g
            _adam_update(g, w_ref, m_ref, v_ref, d_ref, mo_ref, vo_ref)

        _per_layer(pl.program_id(0), refs[:L], update)

    row = pl.BlockSpec((tr, C), lambda l, i: (l * nr + i, 0))
    return pl.pallas_call(
        body, name=name, grid=(L, nr), in_specs=_landed_specs(L, tr, C, nr) + [row] * 3, out_specs=[row] * 4,
        out_shape=[jax.ShapeDtypeStruct((L * R, C), _F32)] * 4, compiler_params=_seq(2),
    )(*zones, w, m, v)


def _pack(parts):
    flat = jnp.concatenate([p.reshape(-1).astype(_F32) for p in parts])
    n = flat.shape[0]
    rows = -(-n // (8 * LANES)) * 8
    return jnp.pad(flat, (0, rows * LANES - n)).reshape(rows, LANES)


def _unpack(packed, shapes):
    flat, out, k = packed.reshape(-1), [], 0
    for s in shapes:
        n = 1
        for d in s:
            n *= d
        out.append(flat[k:k + n].reshape(s))
        k += n
    return out


def kernel(x, norm1_g, norm2_g, pool_w, pool_scale, kv_norm_g, w_kv, b_kv, w_q, b_q, sinks, w_o, b_o, ffn_up, ffn_conv_w, ffn_conv_b, ffn_down, final_g, loss_target, m_norm1_g, m_norm2_g, m_pool_w, m_pool_scale, m_kv_norm_g, m_w_kv, m_b_kv, m_w_q, m_b_q, m_sinks, m_w_o, m_b_o, m_ffn_up, m_ffn_conv_w, m_ffn_conv_b, m_ffn_down, m_final_g, v_norm1_g, v_norm2_g, v_pool_w, v_pool_scale, v_kv_norm_g, v_w_kv, v_b_kv, v_w_q, v_b_q, v_sinks, v_w_o, v_b_o, v_ffn_up, v_ffn_conv_w, v_ffn_conv_b, v_ffn_down, v_final_g):
    S = x.shape[1]
    F2s = ffn_up.shape[2]
    F2 = N_DEV * F2s
    me = _me()
    x0 = x.reshape(S, D)
    tgt = loss_target.reshape(S, D)
    row = lambda a: a.reshape(1, -1)

    small = _pack([pool_scale, ffn_conv_w])
    wire = lambda a: a.astype(_MXU)
    ffn_w = lambda l: [(wire(ffn_up[l]).T, 0), (wire(ffn_down[l]), 0)]
    attn_w = lambda j: [(wire(w_q[j]), 0), (wire(w_o[j]), 0)]
    gathers, token = _start_copies("gather_start", [_gather_group(g) for g in (
        [(wire(pool_w[0]), 1), (small[None], 0)], ffn_w(0), [(wire(pool_w[1]), 1)] + ffn_w(1),
        [(wire(w_kv), 0)] + attn_w(0), ffn_w(2), attn_w(1), ffn_w(3))])

    def gathered(k, after):
        return _wait_copies("gather_wait_%d" % k, [gathers[k]], after)[0]

    pw, up_t, down, wq, wo = [None] * N_A, [None] * DEPTH, [None] * DEPTH, [None] * 2, [None] * 2
    pw[0], small_all = gathered(0, token)
    n_ps = pool_scale.size
    small_all = small_all.reshape(N_DEV, -1)
    pscale = jnp.transpose(small_all[:, :n_ps].reshape(N_DEV, N_A, D // N_DEV), (1, 0, 2)).reshape(N_A, D)
    conv_w = jnp.transpose(small_all[:, n_ps:n_ps + ffn_conv_w.size].reshape(N_DEV, DEPTH, CONV_W, F2s),
                           (1, 2, 0, 3)).reshape(DEPTH, CONV_W, F2)

    def dup(a):
        a4 = a.reshape(a.shape[:-1] + (2 * N_KV, 1, HEAD_DIM))
        return jnp.broadcast_to(a4, a.shape[:-1] + (2 * N_KV, 2, HEAD_DIM)).reshape(a.shape[:-1] + (KVD,))

    def fold(a):
        return a.reshape(a.shape[:-1] + (2 * N_KV, 2, HEAD_DIM)).sum(axis=-2).reshape(a.shape[:-1] + (2 * N_KV * HEAD_DIM,))

    xs, us, qs, os_ = [x0], [], [], []
    xc = x0
    kvd = None
    for l in range(DEPTH):
        if l == 1:
            pw[1], up_t[1], down[1] = gathered(2, xc)
        if l == 3:
            wq[1], wo[1] = gathered(5, xc)
        if l < N_A:
            xc = _pool_fwd(xc, row(norm1_g[l]), pw[l], row(pscale[l]))
        else:
            j = l - N_A
            xc, q, o = _attn_fwd(xc, row(norm1_g[l]), wq[j], row(b_q[j]), sinks[j], kvd, wo[j], row(b_o[j]))
            qs.append(q)
            os_.append(o)
        xs.append(xc)
        if l != 1:
            up_t[l], down[l] = gathered((1, None, 4, 6)[l], xc)
        xc, u, c = _ffn_fwd(xc, row(norm2_g[l]), up_t[l], conv_w[l], row(ffn_conv_b[l]), down[l])
        us.append((u, c))
        xs.append(xc)
        if l == N_A - 1:
            wkv, wq[0], wo[0] = gathered(3, xc)
            wkv_d, bkv_d = dup(wkv), dup(row(b_kv))
            kvd = _kv_fwd(xc, row(kv_norm_g), wkv_d, bkv_d)

    dx, loss_p, d_final = _loss_bwd(xc, row(final_g), tgt)
    d_n1, d_n2, d_cw, d_cb = [None] * DEPTH, [None] * DEPTH, [None] * DEPTH, [None] * DEPTH
    d_bq, d_bo, d_sk, d_ps, dkv_parts = [None] * 2, [None] * 2, [None] * 2, [None] * N_A, []
    up_z, down_z, mix_z = [None] * DEPTH, [None] * DEPTH, [None] * DEPTH
    token = None

    def after(gain):
        return gain if token is None else gain + token[0:1, 0:1]

    rep_names = ["norm1_g", "norm2_g", "kv_norm_g", "b_kv", "b_q", "sinks", "b_o", "ffn_conv_b", "final_g"]

    def small_parts(n1_rest, ps_rest):
        zero = jnp.zeros((1, D), _F32)
        return [jnp.concatenate([zero] + n1_rest), jnp.concatenate(d_n2), d_kvg, fold(d_bkv), jnp.concatenate(d_bq),
                jnp.concatenate([s[:, :N_HEADS] for s in d_sk]), jnp.concatenate(d_bo), jnp.concatenate(d_cb), d_final,
                jnp.concatenate([zero] + ps_rest), jnp.stack(d_cw), loss_p[0:1, 0:1]]

    for l in reversed(range(DEPTH)):
        x_in, x_mid, x_out = xs[2 * l], xs[2 * l + 1], xs[2 * l + 2]
        mixer_grads = []
        if l == N_A - 1:
            dx, d_wkv, d_bkv, d_kvg = _kv_bwd(x_out, dx, after(row(kv_norm_g)), wkv_d, *dkv_parts)
            mixer_grads.append((fold(d_wkv).astype(_MXU), 0))
        dy = dx
        dx, du, a, h, d_cw[l], d_cb[l], d_n2[l] = _ffn_bwd(
            x_mid, dy, *us[l], after(row(norm2_g[l])), up_t[l], conv_w[l], down[l])
        g_up = _tn_matmul(du, h, "tn_up")
        if l > 0:
            (up_z[l], down_z[l]), token = _start_copies("scatter_ffn_%d" % l, [
                _scatter_group([(g_up, 0)]), _scatter_group([(_tn_matmul(a, dy, "tn_down"), 0)])])
        else:
            (small_handle, up_z[l]), token = _start_copies("scatter_up_0", [
                _gather_group([(_pack(small_parts(d_n1[1:], d_ps[1:]))[None], 0)]), _scatter_group([(g_up, 0)])])
            (down_z[l],), token = _start_copies("scatter_down_0", [
                _scatter_group([(_tn_matmul(a, dy, "tn_down", token), 0)])])
        dy = dx
        if l < N_A:
            dx, d_pw, d_ps[l], d_n1[l] = _pool_bwd(x_in, dy, after(row(norm1_g[l])), pw[l], row(pscale[l]))
            mixer_grads.append((d_pw.astype(_MXU), 1))
        else:
            j = l - N_A
            dx, dq, h, d_cur, d_prev, d_bq[j], d_bo[j], d_n1[l], d_sk[j] = _attn_bwd(
                x_in, dy, qs[j], os_[j], after(row(norm1_g[l])), wq[j], sinks[j], kvd, wo[j])
            mixer_grads += [(_tn_matmul(h, dq, "tn_q"), 0), (_tn_matmul(os_[j], dy, "tn_o"), 0)]
            dkv_parts += [d_cur, d_prev]
        (mix_z[l],), token = _start_copies("scatter_mixer_%d" % l, [_scatter_group(mixer_grads)])

    late = _all_reduce_small(_pack([d_n1[0], d_ps[0]])).reshape(-1)
    (early,), = _wait_copies("small_wait", [small_handle], late)
    given = dict(norm1_g=norm1_g, norm2_g=norm2_g, kv_norm_g=kv_norm_g, b_kv=b_kv, b_q=b_q, sinks=sinks, b_o=b_o,
                 ffn_conv_b=ffn_conv_b, final_g=final_g, pool_scale=pool_scale, ffn_conv_w=ffn_conv_w)
    tot = _unpack(_sum8([early]), [given[k].shape for k in rep_names] + [(N_A, D), (DEPTH, CONV_W, F2), ()])
    tot[0] = tot[0].at[0].add(late[:D])
    tot[-3] = tot[-3].at[0].add(late[D:2 * D])
    grad = dict(zip(rep_names, tot))
    loss = tot[-1]
    grad["pool_scale"] = lax.dynamic_slice_in_dim(tot[-3], me * (D // N_DEV), D // N_DEV, axis=1)
    grad["ffn_conv_w"] = lax.dynamic_slice_in_dim(tot[-2], me * F2s, F2s, axis=2)

    moms = dict(norm1_g=(m_norm1_g, v_norm1_g), norm2_g=(m_norm2_g, v_norm2_g), pool_w=(m_pool_w, v_pool_w),
                pool_scale=(m_pool_scale, v_pool_scale), kv_norm_g=(m_kv_norm_g, v_kv_norm_g), w_kv=(m_w_kv, v_w_kv),
                b_kv=(m_b_kv, v_b_kv), w_q=(m_w_q, v_w_q), b_q=(m_b_q, v_b_q), sinks=(m_sinks, v_sinks),
                w_o=(m_w_o, v_w_o), b_o=(m_b_o, v_b_o), ffn_up=(m_ffn_up, v_ffn_up),
                ffn_conv_w=(m_ffn_conv_w, v_ffn_conv_w), ffn_conv_b=(m_ffn_conv_b, v_ffn_conv_b),
                ffn_down=(m_ffn_down, v_ffn_down), final_g=(m_final_g, v_final_g))
    given.update(pool_w=pool_w, w_kv=w_kv, w_q=w_q, w_o=w_o, ffn_up=ffn_up, ffn_down=ffn_down)
    delta, new_m, new_v = {}, {}, {}

    small_names = rep_names + ["pool_scale", "ffn_conv_w"]
    shapes = [given[k].shape for k in small_names]
    outs = _adamw(_pack([grad[k] for k in small_names]), _pack([given[k] for k in small_names]),
                  _pack([moms[k][0] for k in small_names]), _pack([moms[k][1] for k in small_names]), "adamw_small")
    for dst, packed in zip((delta, new_m, new_v), outs):
        dst.update(zip(small_names, _unpack(packed, shapes)))

    zones = _wait_copies("scatter_wait", up_z + down_z + mix_z, outs[0])
    up_z, down_z, mix_z = zones[:DEPTH], zones[DEPTH:2 * DEPTH], zones[2 * DEPTH:]
    by_name = dict(ffn_up=[z[0] for z in up_z], ffn_down=[z[0] for z in down_z],
                   w_q=[mix_z[N_A][0], mix_z[N_A + 1][0]], w_o=[mix_z[N_A][1], mix_z[N_A + 1][1]],
                   w_kv=[mix_z[N_A - 1][0]], pool_w=[mix_z[0][0], mix_z[N_A - 1][1]])

    def update(name, g, cols):
        w = given[name]
        two_d = lambda a: a.reshape(-1, cols)
        if g is None:
            landed = [z.reshape(N_DEV, -1, cols) for z in by_name[name]]
            outs = _adamw_landed(landed, two_d(w), two_d(moms[name][0]), two_d(moms[name][1]), "adamw_" + name)
            grad[name] = outs[0].reshape(w.shape)
        else:
            outs = _adamw(g, two_d(w), two_d(moms[name][0]), two_d(moms[name][1]), "adamw_" + name)
        delta[name], new_m[name], new_v[name] = (o.reshape(w.shape) for o in outs[-3:])

    grad["ffn_up"] = jnp.swapaxes(_sum8(by_name["ffn_up"]).reshape(DEPTH, F2s, D), 1, 2)
    update("ffn_up", grad["ffn_up"].reshape(-1, F2s), F2s)
    update("ffn_down", None, D)
    update("w_q", None, D)
    update("w_o", None, D)
    update("w_kv", None, w_kv.shape[1])
    update("pool_w", None, GC)

    names = ["norm1_g", "norm2_g", "pool_w", "pool_scale", "kv_norm_g", "w_kv", "b_kv", "w_q", "b_q", "sinks", "w_o",
             "b_o", "ffn_up", "ffn_conv_w", "ffn_conv_b", "ffn_down", "final_g"]
    return (loss, dx.reshape(x.shape), *[grad[k] for k in names], *[delta[k] for k in names],
            *[new_m[k] for k in names], *[new_v[k] for k in names])
```

```python
import functools

import jax
import jax.numpy as jnp
from jax import lax
from jax.experimental import pallas as pl
from jax.experimental.pallas import tpu as pltpu

_F32 = jnp.float32
_MXU = jnp.bfloat16

N_DEV = 8
D = 1024
DEPTH = 4
N_A = 2
POOL_WINDOWS = (2, 4, 8, 16)
GC = D // len(POOL_WINDOWS)
HALO = 16
HEAD_DIM = 64
N_HEADS = D // HEAD_DIM
GROUP = 8
N_KV = N_HEADS // GROUP
BLK = 128
PAIR = 2 * HEAD_DIM
KVD = 4 * N_KV * HEAD_DIM
CONV_W = 3
EPS = 1e-5
NEG = -1e30

ADAM_LR = 0.001
ADAM_B1 = 0.9
ADAM_B2 = 0.999
ADAM_EPS = 1e-08
ADAM_WD = 0.01
ADAM_STEP = 10

V7X_VMEM_LIMIT = 56 * 1024 * 1024
LANES = 128

_NT = (((1,), (1,)), ((), ()))
_TN = (((0,), (0,)), ((), ()))


def _params(**kw):
    return pltpu.CompilerParams(vmem_limit_bytes=V7X_VMEM_LIMIT, **kw)


def _seq(n=1):
    return _params(dimension_semantics=("arbitrary",) * n)


def _dot(a, b, dims=None):
    if dims is None:
        return jnp.dot(a, b, preferred_element_type=_F32)
    return lax.dot_general(a, b, dims, preferred_element_type=_F32)


def _rms(x):
    r = lax.rsqrt(jnp.mean(x * x, axis=-1, keepdims=True) + EPS)
    return x * r, r


def _rms_bwd(dh, xn, r, g):
    dxn = dh * g
    return r * (dxn - xn * jnp.mean(dxn * xn, axis=-1, keepdims=True))


def _colsum(a):
    return jnp.sum(a, axis=0, keepdims=True)


def _tile(n, want, mult=8):
    for t in range(min(want, n), 0, -1):
        if n % t == 0 and t % mult == 0:
            return t
    return n


def _full(shape):
    zeros = (0,) * len(shape)
    return pl.BlockSpec(shape, lambda *_: zeros)


def _window_sum(ext, win, trailing):
    R = ext.shape[0]
    acc, k = ext, 1
    while k < win:
        acc = acc + pltpu.roll(acc, k if trailing else R - k, axis=0)
        k *= 2
    return acc


def _pool_windows(hbuf, h, row, T):
    out = []
    for gi, win in enumerate(POOL_WINDOWS):
        cs = slice(gi * GC, (gi + 1) * GC)
        acc = _window_sum(hbuf[:, cs], win, True)[HALO:, :]
        cnt = jnp.minimum(row + 1, win).astype(_F32)
        out.append((acc / cnt - h[:, cs], cnt))
    return out


def _pool_fwd(x, g, w, sc):
    S = x.shape[0]
    T = _tile(S, 512, HALO)
    n, hb = S // T, T // HALO

    def body(x_ref, xh_ref, g_ref, w_ref, sc_ref, o_ref, hbuf):
        i = pl.program_id(0)
        gv = g_ref[...]
        xv = x_ref[...]
        h = _rms(xv)[0] * gv
        hbuf[0:HALO, :] = jnp.where(i > 0, _rms(xh_ref[...])[0] * gv, 0.0)
        hbuf[HALO:, :] = h
        row = i * T + lax.broadcasted_iota(jnp.int32, (T, 1), 0)
        for gi, (p, _) in enumerate(_pool_windows(hbuf, h, row, T)):
            cs = slice(gi * GC, (gi + 1) * GC)
            z = _dot(p.astype(_MXU), w_ref[gi])
            o_ref[:, cs] = xv[:, cs] + z * sc_ref[:, cs]

    return pl.pallas_call(
        body, name="pool_fwd", grid=(n,),
        in_specs=[pl.BlockSpec((T, D), lambda i: (i, 0)),
                  pl.BlockSpec((HALO, D), lambda i: (jnp.maximum(i * hb - 1, 0), 0)),
                  _full((1, D)), _full((4, GC, GC)), _full((1, D))],
        out_specs=pl.BlockSpec((T, D), lambda i: (i, 0)),
        out_shape=jax.ShapeDtypeStruct((S, D), _F32),
        scratch_shapes=[pltpu.VMEM((T + HALO, D), _F32)],
        compiler_params=_seq(),
    )(x, x, g, w, sc)


def _pool_bwd(x, dy, g, w, sc):
    S = x.shape[0]
    T = _tile(S, 512, HALO)
    n, hb = S // T, T // HALO

    def body(x_ref, xh_ref, dy_ref, dyh_ref, g_ref, w_ref, sc_ref, dx_ref, dw_ref, dsc_ref, dg_ref,
             hbuf, qbuf, dhbuf):
        i = pl.program_id(0)

        @pl.when(i == 0)
        def _():
            dw_ref[...] = jnp.zeros_like(dw_ref)
            dsc_ref[...] = jnp.zeros_like(dsc_ref)
            dg_ref[...] = jnp.zeros_like(dg_ref)

        gv = g_ref[...]
        xv = x_ref[...]
        xn, r = _rms(xv)
        h = xn * gv
        hbuf[0:HALO, :] = jnp.where(i > 0, _rms(xh_ref[...])[0] * gv, 0.0)
        hbuf[HALO:, :] = h
        dyv = dy_ref[...]
        dz = dyv * sc_ref[...]
        dzh = jnp.where(i < n - 1, dyh_ref[...], 0.0) * sc_ref[...]
        row = i * T + lax.broadcasted_iota(jnp.int32, (T, 1), 0)
        rowh = (i + 1) * T + lax.broadcasted_iota(jnp.int32, (HALO, 1), 0)
        for gi, (p, cnt) in enumerate(_pool_windows(hbuf, h, row, T)):
            win = POOL_WINDOWS[gi]
            cs = slice(gi * GC, (gi + 1) * GC)
            pb = p.astype(_MXU)
            wg = w_ref[gi]
            dsc_ref[:, cs] += _colsum(dyv[:, cs] * _dot(pb, wg))
            dzb = dz[:, cs].astype(_MXU)
            dw_ref[gi] += _dot(pb, dzb, _TN)
            dp = _dot(dzb, wg, _NT)
            dph = _dot(dzh[:, cs].astype(_MXU), wg, _NT)
            qbuf[0:T, cs] = dp / cnt
            qbuf[T:T + HALO, cs] = dph / jnp.minimum(rowh + 1, win).astype(_F32)
            dhbuf[:, cs] = _window_sum(qbuf[:, cs], win, False)[0:T, :] - dp
        dh = dhbuf[...]
        dg_ref[...] += _colsum(dh * xn)
        dx_ref[...] = dyv + _rms_bwd(dh, xn, r, gv)

    return pl.pallas_call(
        body, name="pool_bwd", grid=(n,),
        in_specs=[pl.BlockSpec((T, D), lambda i: (i, 0)),
                  pl.BlockSpec((HALO, D), lambda i: (jnp.maximum(i * hb - 1, 0), 0)),
                  pl.BlockSpec((T, D), lambda i: (i, 0)),
                  pl.BlockSpec((HALO, D), lambda i: (jnp.minimum((i + 1) * hb, S // HALO - 1), 0)),
                  _full((1, D)), _full((4, GC, GC)), _full((1, D))],
        out_specs=[pl.BlockSpec((T, D), lambda i: (i, 0)), _full((4, GC, GC)), _full((1, D)), _full((1, D))],
        out_shape=[jax.ShapeDtypeStruct((S, D), _F32), jax.ShapeDtypeStruct((4, GC, GC), _F32),
                   jax.ShapeDtypeStruct((1, D), _F32), jax.ShapeDtypeStruct((1, D), _F32)],
        scratch_shapes=[pltpu.VMEM((T + HALO, D), _F32), pltpu.VMEM((T + HALO, D), _F32), pltpu.VMEM((T, D), _F32)],
        compiler_params=_seq(),
    )(x, x, dy, dy, g, w, sc)


FFN_FWD_TILE, FFN_FWD_CHUNKS = 256, 2
FFN_BWD_TILE, FFN_BWD_CHUNKS = 128, 2
EDGE = 8


def _shift_down(v, k, prev):
    r = pltpu.roll(v, k, axis=0)
    i8 = lax.broadcasted_iota(jnp.int32, (EDGE, v.shape[1]), 0)
    head = jnp.where(i8 >= k, r[0:EDGE, :], pltpu.roll(prev, k, axis=0))
    return jnp.concatenate([head, r[EDGE:, :]], axis=0)


def _shift_up(v, k, nxt):
    T = v.shape[0]
    r = pltpu.roll(v, T - k, axis=0)
    i8 = lax.broadcasted_iota(jnp.int32, (EDGE, v.shape[1]), 0)
    tail = jnp.where(i8 < EDGE - k, r[T - EDGE:, :], pltpu.roll(nxt, EDGE - k, axis=0))
    return jnp.concatenate([r[:T - EDGE, :], tail], axis=0)


def _load_weights(i, pairs, sems):
    @pl.when(i == 0)
    def _():
        cps = [pltpu.make_async_copy(src, dst, sems.at[k]) for k, (src, dst) in enumerate(pairs)]
        for cp in cps:
            cp.start()
        for cp in cps:
            cp.wait()


def _ffn_fwd(x, g, wup_t, cw, cb, wdn):
    S = x.shape[0]
    F2 = wup_t.shape[0]
    F = F2 // 2
    C = F // FFN_FWD_CHUNKS
    T = _tile(S, FFN_FWD_TILE, 16)
    n = S // T

    def body(x_ref, g_ref, wup_hbm, cw_ref, cb_ref, wdn_hbm, o_ref, u_ref, c_ref, wup, wdnv, carry, sems):
        i = pl.program_id(0)
        _load_weights(i, [(wup_hbm, wup), (wdn_hbm, wdnv)], sems)

        @pl.when(i == 0)
        def _():
            carry[...] = jnp.zeros_like(carry)

        xv = x_ref[...]
        hb = (_rms(xv)[0] * g_ref[...]).astype(_MXU)
        acc = jnp.zeros((T, D), _F32)
        for j in range(FFN_FWD_CHUNKS):
            halves = []
            for cs in (slice(j * C, (j + 1) * C), slice(F + j * C, F + (j + 1) * C)):
                u = _dot(hb, wup[cs, :], _NT)
                u_ref[:, cs] = u.astype(u_ref.dtype)
                prev = carry[:, cs]
                carry[:, cs] = u[T - EDGE:, :]
                c = (cw_ref[0:1, cs] * _shift_down(u, 2, prev) + cw_ref[1:2, cs] * _shift_down(u, 1, prev)
                     + cw_ref[2:3, cs] * u + cb_ref[:, cs])
                c_ref[:, cs] = c.astype(c_ref.dtype)
                halves.append(c)
            cg, cv = halves
            a = (cg * jax.nn.sigmoid(cg)) * cv
            acc = acc + _dot(a.astype(_MXU), wdnv[j * C:(j + 1) * C, :])
        o_ref[...] = xv + acc

    any_ = pl.BlockSpec(memory_space=pl.ANY)
    wide = pl.BlockSpec((T, F2), lambda i: (i, 0))
    return pl.pallas_call(
        body, name="ffn_fwd", grid=(n,),
        in_specs=[pl.BlockSpec((T, D), lambda i: (i, 0)), _full((1, D)), any_, _full((CONV_W, F2)), _full((1, F2)), any_],
        out_specs=[pl.BlockSpec((T, D), lambda i: (i, 0)), wide, wide],
        out_shape=[jax.ShapeDtypeStruct((S, D), _F32), jax.ShapeDtypeStruct((S, F2), _MXU),
                   jax.ShapeDtypeStruct((S, F2), _MXU)],
        scratch_shapes=[pltpu.VMEM((F2, D), _MXU), pltpu.VMEM((F, D), _MXU),
                        pltpu.VMEM((EDGE, F2), _F32), pltpu.SemaphoreType.DMA((2,))],
        compiler_params=_seq(),
    )(x, g, wup_t, cw, cb, wdn)


def _ffn_bwd(x, dy, u, c, g, wup_t, cw, wdn):
    S = x.shape[0]
    F2 = wup_t.shape[0]
    F = F2 // 2
    C = F // FFN_BWD_CHUNKS
    T = _tile(S, FFN_BWD_TILE, 16)
    n = S // T

    def body(x_ref, dy_ref, u_ref, c_ref, g_ref, wup_hbm, cw_ref, wdn_hbm,
             dx_ref, du_ref, a_ref, h_ref, dcw_ref, dcb_ref, dg_ref, wup, wdnv, carry, sems):
        i = pl.program_id(0)
        _load_weights(i, [(wup_hbm, wup), (wdn_hbm, wdnv)], sems)

        @pl.when(i == 0)
        def _():
            carry[...] = jnp.zeros_like(carry)
            dcw_ref[...] = jnp.zeros_like(dcw_ref)
            dcb_ref[...] = jnp.zeros_like(dcb_ref)
            dg_ref[...] = jnp.zeros_like(dg_ref)

        gv = g_ref[...]
        xv = x_ref[...]
        xn, r = _rms(xv)
        hbf = (xn * gv).astype(_MXU)
        h_ref[...] = hbf
        dyv = dy_ref[...]
        dyb = dyv.astype(_MXU)
        dh = jnp.zeros((T, D), _F32)
        for j in range(FFN_BWD_CHUNKS):
            gs, vs = slice(j * C, (j + 1) * C), slice(F + j * C, F + (j + 1) * C)
            cg, cv = c_ref[:, gs].astype(_F32), c_ref[:, vs].astype(_F32)
            sg = jax.nn.sigmoid(cg)
            sl = cg * sg
            a_ref[:, gs] = (sl * cv).astype(a_ref.dtype)
            da = _dot(dyb, wdnv[gs, :], _NT)
            for cs, dc in ((gs, da * cv * (sg * (1.0 + cg * (1.0 - sg)))), (vs, da * sl)):
                nxt = carry[:, cs]
                carry[:, cs] = dc[0:EDGE, :]
                dc1, dc2 = _shift_up(dc, 1, nxt), _shift_up(dc, 2, nxt)
                uf = u_ref[:, cs].astype(_F32)
                dcb_ref[:, cs] += _colsum(dc)
                for k, d in enumerate((dc2, dc1, dc)):
                    dcw_ref[k:k + 1, cs] += _colsum(d * uf)
                du = cw_ref[2:3, cs] * dc + cw_ref[1:2, cs] * dc1 + cw_ref[0:1, cs] * dc2
                dub = du.astype(_MXU)
                du_ref[:, cs] = dub
                dh = dh + _dot(dub, wup[cs, :])
        dg_ref[...] += _colsum(dh * xn)
        dx_ref[...] = dyv + _rms_bwd(dh, xn, r, gv)

    any_ = pl.BlockSpec(memory_space=pl.ANY)
    rev = lambda i: (n - 1 - i, 0)
    return pl.pallas_call(
        body, name="ffn_bwd", grid=(n,),
        in_specs=[pl.BlockSpec((T, D), rev), pl.BlockSpec((T, D), rev), pl.BlockSpec((T, F2), rev),
                  pl.BlockSpec((T, F2), rev), _full((1, D)), any_, _full((CONV_W, F2)), any_],
        out_specs=[pl.BlockSpec((T, D), rev), pl.BlockSpec((T, F2), rev), pl.BlockSpec((T, F), rev),
                   pl.BlockSpec((T, D), rev), _full((CONV_W, F2)), _full((1, F2)), _full((1, D))],
        out_shape=[jax.ShapeDtypeStruct((S, D), _F32), jax.ShapeDtypeStruct((S, F2), _MXU),
                   jax.ShapeDtypeStruct((S, F), _MXU), jax.ShapeDtypeStruct((S, D), _MXU),
                   jax.ShapeDtypeStruct((CONV_W, F2), _F32), jax.ShapeDtypeStruct((1, F2), _F32),
                   jax.ShapeDtypeStruct((1, D), _F32)],
        scratch_shapes=[pltpu.VMEM((F2, D), _MXU), pltpu.VMEM((F, D), _MXU),
                        pltpu.VMEM((EDGE, F2), _F32), pltpu.SemaphoreType.DMA((2,))],
        compiler_params=_seq(),
    )(x, dy, u, c, g, wup_t, cw, wdn)


def _tn_matmul(a, b, name, token=None):
    S, M = a.shape
    N = b.shape[1]
    bm = _tile(M, 1408, LANES)
    tk = _tile(S, 2048, 16)
    nk = S // tk
    tokens = [] if token is None else [token]

    def body(a_ref, b_ref, *rest):
        o_ref, acc = rest[-2:]
        k = pl.program_id(1)

        @pl.when(k == 0)
        def _():
            acc[...] = jnp.zeros_like(acc)

        acc[...] += _dot(a_ref[...].astype(_MXU), b_ref[...].astype(_MXU), _TN)

        @pl.when(k == nk - 1)
        def _():
            o_ref[...] = acc[...].astype(o_ref.dtype)

    return pl.pallas_call(
        body, name=name, grid=(M // bm, nk),
        in_specs=[pl.BlockSpec((tk, bm), lambda i, k: (k, i)), pl.BlockSpec((tk, N), lambda i, k: (k, 0))]
        + [_full((8, LANES))] * len(tokens),
        out_specs=pl.BlockSpec((bm, N), lambda i, k: (i, 0)),
        out_shape=jax.ShapeDtypeStruct((M, N), _MXU),
        scratch_shapes=[pltpu.VMEM((bm, N), _F32)],
        compiler_params=_seq(2),
    )(a, b, *tokens)


def _kv_fwd(x, g, wkv, bkv):
    S = x.shape[0]
    T = _tile(S, 512, 16)

    def body(x_ref, g_ref, w_ref, b_ref, o_ref):
        hb = (_rms(x_ref[...])[0] * g_ref[...]).astype(_MXU)
        o_ref[...] = (_dot(hb, w_ref[...]) + b_ref[...]).astype(o_ref.dtype)

    return pl.pallas_call(
        body, name="kv_fwd", grid=(S // T,),
        in_specs=[pl.BlockSpec((T, D), lambda i: (i, 0)), _full((1, D)), _full((D, KVD)), _full((1, KVD))],
        out_specs=pl.BlockSpec((T, KVD), lambda i: (i, 0)),
        out_shape=jax.ShapeDtypeStruct((S, KVD), _MXU),
        compiler_params=_seq(),
    )(x, g, wkv, bkv)


def _kv_bwd(x, dx_in, g, wkv, cur_a, prev_a, cur_b, prev_b):
    S = x.shape[0]
    T = _tile(S, 512, BLK)
    n, per = S // T, T // BLK

    def body(x_ref, dxi_ref, g_ref, w_ref, ca, pa, na, cb, pb, nb, dx_ref, dw_ref, db_ref, dg_ref):
        i = pl.program_id(0)

        @pl.when(i == 0)
        def _():
            dw_ref[...] = jnp.zeros_like(dw_ref)
            db_ref[...] = jnp.zeros_like(db_ref)
            dg_ref[...] = jnp.zeros_like(dg_ref)

        gv = g_ref[...]
        xn, r = _rms(x_ref[...])
        nxt = jnp.where(i < n - 1, na[...] + nb[...], 0.0)
        prev = jnp.concatenate([pa[BLK:, :] + pb[BLK:, :], nxt], axis=0) if per > 1 else nxt
        dkv = ca[...] + cb[...] + prev
        db_ref[...] += _colsum(dkv)
        dkb = dkv.astype(_MXU)
        dw_ref[...] += _dot((xn * gv).astype(_MXU), dkb, _TN)
        dh = _dot(dkb, w_ref[...], _NT)
        dg_ref[...] += _colsum(dh * xn)
        dx_ref[...] = dxi_ref[...] + _rms_bwd(dh, xn, r, gv)

    blk = lambda w: pl.BlockSpec((T, w), lambda i: (i, 0))
    nxt = pl.BlockSpec((BLK, KVD), lambda i: (jnp.minimum((i + 1) * per, S // BLK - 1), 0))
    return pl.pallas_call(
        body, name="kv_bwd", grid=(n,),
        in_specs=[blk(D), blk(D), _full((1, D)), _full((D, KVD)), blk(KVD), blk(KVD), nxt, blk(KVD), blk(KVD), nxt],
        out_specs=[blk(D), _full((D, KVD)), _full((1, KVD)), _full((1, D))],
        out_shape=[jax.ShapeDtypeStruct((S, D), _F32), jax.ShapeDtypeStruct((D, KVD), _F32),
                   jax.ShapeDtypeStruct((1, KVD), _F32), jax.ShapeDtypeStruct((1, D), _F32)],
        compiler_params=_seq(),
    )(x, dx_in, g, wkv, cur_a, prev_a, prev_a, cur_b, prev_b, prev_b)


STACK = GROUP * BLK


def _attn_mask(i, rows):
    qi = lax.broadcasted_iota(jnp.int32, (rows, 2 * BLK), 0) & (BLK - 1)
    si = lax.broadcasted_iota(jnp.int32, (rows, 2 * BLK), 1)
    return (si > qi) & (si <= qi + BLK) & jnp.logical_or(i > 0, si >= BLK)


def _low_half():
    return lax.broadcasted_iota(jnp.int32, (BLK, PAIR), 1) < HEAD_DIM


def _stack_heads(ref, kh, dst):
    low = _low_half()
    for pp in range(GROUP // 2):
        pr = kh * (GROUP // 2) + pp
        v2 = ref[:, pr * PAIR:(pr + 1) * PAIR]
        zero = jnp.zeros_like(v2)
        dst[2 * pp * BLK:(2 * pp + 1) * BLK, :] = jnp.where(low, v2, zero)
        dst[(2 * pp + 1) * BLK:(2 * pp + 2) * BLK, :] = jnp.where(low, zero, v2)


def _unstack_heads(st, pp):
    return jnp.where(_low_half(), st[2 * pp * BLK:(2 * pp + 1) * BLK, :], st[(2 * pp + 1) * BLK:(2 * pp + 2) * BLK, :])


def _sink_col(sk_ref, kh):
    return jnp.concatenate([jnp.full((BLK, 1), sk_ref[kh * GROUP + h], _F32) for h in range(GROUP)], axis=0)


def _head_probs(qm, kd, mask, sink):
    s = jnp.where(mask, _dot(qm, kd, _NT) * (HEAD_DIM ** -0.5), NEG)
    m = jnp.maximum(jnp.max(s, axis=-1, keepdims=True), sink)
    p = jnp.exp(s - m)
    es = jnp.exp(sink - m)
    inv = 1.0 / (jnp.sum(p, axis=-1, keepdims=True) + es)
    return p * inv, es * inv


def _attn_fwd(x, g, wq, bq, sinks, kvd, wo, bo):
    S = x.shape[0]
    n = S // BLK

    def body(x_ref, g_ref, wq_ref, bq_ref, sk_ref, kp_ref, kc_ref, wo_ref, bo_ref, xo_ref, q_ref, o_ref, win):
        i = pl.program_id(0)
        xv = x_ref[...]
        hb = (_rms(xv)[0] * g_ref[...]).astype(_MXU)
        q_ref[...] = (_dot(hb, wq_ref[...]) + bq_ref[...]).astype(q_ref.dtype)
        win[0:BLK, :] = kp_ref[...]
        win[BLK:, :] = kc_ref[...]
        mask = _attn_mask(i, BLK)
        low = _low_half()
        for pr in range(N_HEADS // 2):
            kh = (2 * pr) // GROUP
            kd = win[:, kh * PAIR:(kh + 1) * PAIR]
            vd = win[:, (N_KV + kh) * PAIR:(N_KV + kh + 1) * PAIR]
            q2 = q_ref[:, pr * PAIR:(pr + 1) * PAIR]
            outs = []
            for half in range(2):
                qm = jnp.where(low if half == 0 else ~low, q2, jnp.zeros_like(q2))
                pbs, _ = _head_probs(qm, kd, mask, sk_ref[2 * pr + half])
                outs.append(_dot(pbs.astype(_MXU), vd))
            o_ref[:, pr * PAIR:(pr + 1) * PAIR] = jnp.where(low, outs[0], outs[1]).astype(o_ref.dtype)
        xo_ref[...] = xv + _dot(o_ref[...], wo_ref[...]) + bo_ref[...]

    blk = lambda w: pl.BlockSpec((BLK, w), lambda i: (i, 0))
    return pl.pallas_call(
        body, name="attn_fwd", grid=(n,),
        in_specs=[blk(D), _full((1, D)), _full((D, D)), _full((1, D)),
                  pl.BlockSpec(memory_space=pltpu.SMEM),
                  pl.BlockSpec((BLK, KVD), lambda i: (jnp.maximum(i - 1, 0), 0)), blk(KVD),
                  _full((D, D)), _full((1, D))],
        out_specs=[blk(D), blk(D), blk(D)],
        out_shape=[jax.ShapeDtypeStruct((S, D), _F32), jax.ShapeDtypeStruct((S, D), _MXU),
                   jax.ShapeDtypeStruct((S, D), _MXU)],
        scratch_shapes=[pltpu.VMEM((2 * BLK, KVD), _MXU)],
        compiler_params=_seq(),
    )(x, g, wq, bq, sinks, kvd, kvd, wo, bo)


def _attn_bwd(x, dy, q, o, g, wq, sinks, kvd, wo):
    S = x.shape[0]
    n = S // BLK
    all_rows = N_HEADS * BLK

    def body(x_ref, dy_ref, q_ref, o_ref, g_ref, wq_ref, sk_ref, kp_ref, kc_ref, wo_ref,
             dx_ref, dq_ref, h_ref, dc_ref, dp_ref, dbq_ref, dbo_ref, dg_ref, dsk_ref, win, dob, qs, dos, pall, dsall):
        i = pl.program_id(0)

        @pl.when(i == 0)
        def _():
            dbq_ref[...] = jnp.zeros_like(dbq_ref)
            dbo_ref[...] = jnp.zeros_like(dbo_ref)
            dg_ref[...] = jnp.zeros_like(dg_ref)
            dsk_ref[...] = jnp.zeros_like(dsk_ref)

        gv = g_ref[...]
        xv = x_ref[...]
        xn, r = _rms(xv)
        h_ref[...] = (xn * gv).astype(h_ref.dtype)
        dyv = dy_ref[...]
        dbo_ref[...] += _colsum(dyv)
        dob[...] = _dot(dyv.astype(_MXU), wo_ref[...], _NT).astype(dob.dtype)
        win[0:BLK, :] = kp_ref[...]
        win[BLK:, :] = kc_ref[...]
        mask = _attn_mask(i, BLK)
        low = _low_half()
        lane = lax.broadcasted_iota(jnp.int32, (1, LANES), 1)
        for pr in range(N_HEADS // 2):
            kh = (2 * pr) // GROUP
            kd = win[:, kh * PAIR:(kh + 1) * PAIR]
            vd = win[:, (N_KV + kh) * PAIR:(N_KV + kh + 1) * PAIR]
            q2 = q_ref[:, pr * PAIR:(pr + 1) * PAIR]
            do2 = dob[:, pr * PAIR:(pr + 1) * PAIR]
            od = do2.astype(_F32) * o_ref[:, pr * PAIR:(pr + 1) * PAIR].astype(_F32)
            for half in range(2):
                hd = 2 * pr + half
                rows = slice(hd * BLK, (hd + 1) * BLK)
                sel = low if half == 0 else ~low
                qm = jnp.where(sel, q2, jnp.zeros_like(q2))
                dom = jnp.where(sel, do2, jnp.zeros_like(do2))
                qs[rows, :] = qm
                dos[rows, :] = dom
                pbs, ps = _head_probs(qm, kd, mask, sk_ref[hd])
                pall[rows, :] = pbs.astype(_MXU)
                delta = jnp.sum(jnp.where(sel, od, 0.0), axis=-1, keepdims=True)
                dsall[rows, :] = (pbs * (_dot(dom, vd, _NT) - delta) * (HEAD_DIM ** -0.5)).astype(_MXU)
                dsk_ref[...] -= jnp.where(lane == hd, _colsum(ps * delta), 0.0)
        dq_all = []
        for kh in range(N_KV):
            ks = slice(kh * PAIR, (kh + 1) * PAIR)
            vs = slice((N_KV + kh) * PAIR, (N_KV + kh + 1) * PAIR)
            rows = slice(kh * STACK, (kh + 1) * STACK)
            dqst = _dot(dsall[rows, :], win[:, ks])
            dk = _dot(dsall[rows, :], qs[rows, :], _TN)
            dv = _dot(pall[rows, :], dos[rows, :], _TN)
            dp_ref[:, ks], dc_ref[:, ks] = dk[0:BLK, :], dk[BLK:, :]
            dp_ref[:, vs], dc_ref[:, vs] = dv[0:BLK, :], dv[BLK:, :]
            dq_all += [_unstack_heads(dqst, pp) for pp in range(GROUP // 2)]
        dq = jnp.concatenate(dq_all, axis=1)
        dbq_ref[...] += _colsum(dq)
        dqb = dq.astype(_MXU)
        dq_ref[...] = dqb
        dh = _dot(dqb, wq_ref[...], _NT)
        dg_ref[...] += _colsum(dh * xn)
        dx_ref[...] = dyv + _rms_bwd(dh, xn, r, gv)

    blk = lambda w: pl.BlockSpec((BLK, w), lambda i: (i, 0))
    return pl.pallas_call(
        body, name="attn_bwd", grid=(n,),
        in_specs=[blk(D), blk(D), blk(D), blk(D), _full((1, D)), _full((D, D)),
                  pl.BlockSpec(memory_space=pltpu.SMEM),
                  pl.BlockSpec((BLK, KVD), lambda i: (jnp.maximum(i - 1, 0), 0)), blk(KVD), _full((D, D))],
        out_specs=[blk(D), blk(D), blk(D), blk(KVD), blk(KVD),
                   _full((1, D)), _full((1, D)), _full((1, D)), _full((1, LANES))],
        out_shape=[jax.ShapeDtypeStruct((S, D), _F32), jax.ShapeDtypeStruct((S, D), _MXU),
                   jax.ShapeDtypeStruct((S, D), _MXU), jax.ShapeDtypeStruct((S, KVD), _F32),
                   jax.ShapeDtypeStruct((S, KVD), _F32), jax.ShapeDtypeStruct((1, D), _F32),
                   jax.ShapeDtypeStruct((1, D), _F32), jax.ShapeDtypeStruct((1, D), _F32),
                   jax.ShapeDtypeStruct((1, LANES), _F32)],
        scratch_shapes=[pltpu.VMEM((2 * BLK, KVD), _MXU), pltpu.VMEM((BLK, D), _MXU),
                        pltpu.VMEM((all_rows, PAIR), _MXU), pltpu.VMEM((all_rows, PAIR), _MXU),
                        pltpu.VMEM((all_rows, 2 * BLK), _MXU), pltpu.VMEM((all_rows, 2 * BLK), _MXU)],
        compiler_params=_seq(),
    )(x, dy, q, o, g, wq, sinks, kvd, kvd, wo)


def _loss_bwd(x, g, tgt):
    S = x.shape[0]
    T = _tile(S, 512, 8)

    def body(x_ref, g_ref, t_ref, dx_ref, ls_ref, dg_ref):
        @pl.when(pl.program_id(0) == 0)
        def _():
            ls_ref[...] = jnp.zeros_like(ls_ref)
            dg_ref[...] = jnp.zeros_like(dg_ref)

        gv = g_ref[...]
        xn, r = _rms(x_ref[...])
        err = xn * gv - t_ref[...]
        ls_ref[...] += 0.5 * jnp.sum(jnp.mean(err * err, axis=-1, keepdims=True))
        dyv = err * (1.0 / D)
        dg_ref[...] += _colsum(dyv * xn)
        dx_ref[...] = _rms_bwd(dyv, xn, r, gv)

    return pl.pallas_call(
        body, name="loss_bwd", grid=(S // T,),
        in_specs=[pl.BlockSpec((T, D), lambda i: (i, 0)), _full((1, D)), pl.BlockSpec((T, D), lambda i: (i, 0))],
        out_specs=[pl.BlockSpec((T, D), lambda i: (i, 0)), _full((8, LANES)), _full((1, D))],
        out_shape=[jax.ShapeDtypeStruct((S, D), _F32), jax.ShapeDtypeStruct((8, LANES), _F32),
                   jax.ShapeDtypeStruct((1, D), _F32)],
        compiler_params=_seq(),
    )(x, g, tgt)


def _me():
    return 4 * lax.axis_index("x") + 2 * lax.axis_index("y") + lax.axis_index("c")


def _peer(j):
    x, y, c = lax.axis_index("x"), lax.axis_index("y"), lax.axis_index("c")
    px = 1 - x if j & 4 else x
    py = 1 - y if j & 2 else y
    pc = 1 - c if j & 1 else c
    return (px, py, pc), 4 * px + 2 * py + pc


_HBM = pl.BlockSpec(memory_space=pltpu.HBM)
_SEMS = pl.BlockSpec(memory_space=pltpu.SEMAPHORE)
_EFFECT = pltpu.SideEffectType.DATAFLOW_SIDE_EFFECTING


def _in_hbm(a):
    return pltpu.with_memory_space_constraint(a, pltpu.HBM)


def _start_copies(name, groups):
    flat = []
    for srcs, zones, _ in groups:
        flat += [_in_hbm(a) for a in srcs] + [_in_hbm(lax.empty(z.shape, z.dtype)) for z in zones]
    n_in, n_g = len(flat), len(groups)

    def body(*refs):
        sems = refs[n_in:n_in + 2 * n_g]
        me, k = _me(), 0
        for gi, (srcs, zones, plan) in enumerate(groups):
            src_refs, zone_refs = refs[k:k + len(srcs)], refs[k + len(srcs):k + len(srcs) + len(zones)]
            k += len(srcs) + len(zones)
            for t, (si, zi, src_of, dst_of) in enumerate(plan):
                for j in range(1, N_DEV):
                    dev, pk = _peer(j)
                    pltpu.make_async_remote_copy(
                        src_ref=src_of(src_refs[si], pk), dst_ref=dst_of(zone_refs[zi], me),
                        send_sem=sems[2 * gi].at[t * (N_DEV - 1) + j - 1], recv_sem=sems[2 * gi + 1].at[t * (N_DEV - 1) + j - 1],
                        device_id=dev, device_id_type=pl.DeviceIdType.MESH).start()
                pltpu.make_async_copy(src_of(src_refs[si], me), dst_of(zone_refs[zi], me),
                                      sems[2 * gi].at[len(plan) * (N_DEV - 1) + t]).start()
        refs[-1][...] = jnp.zeros_like(refs[-1])

    sem_shapes = []
    for _, _, plan in groups:
        sem_shapes += [pltpu.SemaphoreType.DMA((len(plan) * N_DEV,)), pltpu.SemaphoreType.DMA((len(plan) * (N_DEV - 1),))]
    outs = pl.pallas_call(
        body, name=name,
        out_shape=(*sem_shapes, *[pltpu.HBM(a.shape, a.dtype) for a in flat], jax.ShapeDtypeStruct((8, LANES), _F32)),
        in_specs=[_HBM] * n_in,
        out_specs=(*[_SEMS] * (2 * n_g), *[_HBM] * n_in, pl.BlockSpec(memory_space=pltpu.VMEM)),
        input_output_aliases={k: 2 * n_g + k for k in range(n_in)},
        compiler_params=pltpu.CompilerParams(has_side_effects=_EFFECT),
    )(*flat)
    handles, k = [], 2 * n_g
    for gi, (srcs, zones, plan) in enumerate(groups):
        ns, nz = len(srcs), len(zones)
        handles.append((outs[2 * gi], outs[2 * gi + 1], list(outs[k:k + ns]), list(outs[k + ns:k + ns + nz]), plan))
        k += ns + nz
    return handles, outs[-1]


def _wait_copies(name, handles, after):
    flat = []
    for _, _, srcs, zones, _ in handles:
        flat += srcs + zones
    n_in, n_g = len(flat), len(handles)

    def body(*refs):
        sems = refs[n_in:n_in + 2 * n_g]
        me, k, local, remote = _me(), 0, [], []
        for gi, (_, _, srcs, zones, plan) in enumerate(handles):
            ns, nz = len(srcs), len(zones)
            src_refs, zone_refs = refs[k:k + ns], refs[k + ns:k + ns + nz]
            k += ns + nz
            for t, (si, zi, src_of, dst_of) in enumerate(plan):
                local.append(pltpu.make_async_copy(src_of(src_refs[si], me), dst_of(zone_refs[zi], me),
                                                   sems[2 * gi].at[len(plan) * (N_DEV - 1) + t]))
                for j in range(1, N_DEV):
                    dev, pk = _peer(j)
                    remote.append(pltpu.make_async_remote_copy(
                        src_ref=src_of(src_refs[si], pk), dst_ref=dst_of(zone_refs[zi], pk),
                        send_sem=sems[2 * gi].at[t * (N_DEV - 1) + j - 1], recv_sem=sems[2 * gi + 1].at[t * (N_DEV - 1) + j - 1],
                        device_id=dev, device_id_type=pl.DeviceIdType.MESH))
        for cp in remote:
            cp.wait_send()
            cp.wait_recv()
        for cp in local:
            cp.wait()

    sem_args = []
    for send, recv, _, _, _ in handles:
        sem_args += [send, recv]
    outs = pl.pallas_call(
        body, name=name, out_shape=tuple(pltpu.HBM(a.shape, a.dtype) for a in flat),
        in_specs=[_HBM] * n_in + [_SEMS] * (2 * n_g) + [pl.BlockSpec(memory_space=pl.ANY)],
        out_specs=tuple([_HBM] * n_in), input_output_aliases={k: k for k in range(n_in)},
        compiler_params=pltpu.CompilerParams(has_side_effects=_EFFECT),
    )(*flat, *sem_args, after)
    res, k = [], 0
    for _, _, srcs, zones, _ in handles:
        res.append(list(outs[k + len(srcs):k + len(srcs) + len(zones)]))
        k += len(srcs) + len(zones)
    return res


def _rows(axis, size):
    def of(ref, b):
        start = b * size
        if size % 8 == 0:
            start = pl.multiple_of(start, 8)
        return ref.at[(slice(None),) * axis + (pl.ds(start, size),)]
    return of


def _whole(ref, b):
    return ref


def _slot(ref, b):
    return ref.at[b]


def _gather_group(shards):
    zones, plan = [], []
    for k, (a, axis) in enumerate(shards):
        zones.append(jax.ShapeDtypeStruct(a.shape[:axis] + (N_DEV * a.shape[axis],) + a.shape[axis + 1:], a.dtype))
        plan.append((k, k, _whole, _rows(axis, a.shape[axis])))
    return [a for a, _ in shards], zones, plan


def _scatter_group(grads):
    zones, plan = [], []
    for k, (a, axis) in enumerate(grads):
        size = a.shape[axis] // N_DEV
        zones.append(jax.ShapeDtypeStruct((N_DEV,) + a.shape[:axis] + (size,) + a.shape[axis + 1:], a.dtype))
        plan.append((k, k, _rows(axis, size), _slot))
    return [a for a, _ in grads], zones, plan


def _all_reduce_small(p):
    R = p.shape[0]

    def body(p_ref, o_ref, land, send_sems, recv_sems):
        me = _me()
        land[me] = p_ref[...]
        waits = []
        for j in range(1, N_DEV):
            dev, pk = _peer(j)
            pltpu.make_async_remote_copy(
                src_ref=p_ref, dst_ref=land.at[me], send_sem=send_sems.at[j - 1], recv_sem=recv_sems.at[j - 1],
                device_id=dev, device_id_type=pl.DeviceIdType.MESH).start()
            waits.append(pltpu.make_async_remote_copy(
                src_ref=p_ref, dst_ref=land.at[pk], send_sem=send_sems.at[j - 1], recv_sem=recv_sems.at[j - 1],
                device_id=dev, device_id_type=pl.DeviceIdType.MESH))
        for cp in waits:
            cp.wait()
        tot = land[0]
        for b in range(1, N_DEV):
            tot = tot + land[b]
        o_ref[...] = tot

    vmem = pl.BlockSpec(memory_space=pltpu.VMEM)
    return pl.pallas_call(
        body, name="all_reduce_small", in_specs=[vmem], out_specs=vmem,
        out_shape=jax.ShapeDtypeStruct((R, LANES), _F32),
        scratch_shapes=[pltpu.VMEM((N_DEV, R, LANES), _F32), pltpu.SemaphoreType.DMA((N_DEV - 1,)),
                        pltpu.SemaphoreType.DMA((N_DEV - 1,))],
        compiler_params=_params(),
    )(p)


def _sum_landed(land):
    g = land[0].astype(_F32)
    for b in range(1, N_DEV):
        g = g + land[b].astype(_F32)
    return g


def _landed_specs(n_layers, tr, C, nr):
    def spec(k):
        return pl.BlockSpec((N_DEV, tr, C), lambda l, i: (0, jnp.where(l == k, i, jnp.where(l < k, 0, nr - 1)), 0))
    return [spec(k) for k in range(n_layers)]


def _per_layer(l, zone_refs, fn):
    for k, ref in enumerate(zone_refs):
        @pl.when(l == k)
        def _(ref=ref):
            fn(_sum_landed(ref))


def _sum8(zones):
    L = len(zones)
    _, R, C = zones[0].shape
    tr = _tile(R, 352, 16)
    nr = R // tr

    def body(*refs):
        o_ref = refs[L]

        def put(g):
            o_ref[...] = g

        _per_layer(pl.program_id(0), refs[:L], put)

    return pl.pallas_call(
        body, name="sum8", grid=(L, nr), in_specs=_landed_specs(L, tr, C, nr),
        out_specs=pl.BlockSpec((tr, C), lambda l, i: (l * nr + i, 0)),
        out_shape=jax.ShapeDtypeStruct((L * R, C), _F32), compiler_params=_seq(2),
    )(*zones)


def _adam_update(gv, w_ref, m_ref, v_ref, d_ref, mo_ref, vo_ref):
    mn = ADAM_B1 * m_ref[...] + (1.0 - ADAM_B1) * gv
    vn = ADAM_B2 * v_ref[...] + (1.0 - ADAM_B2) * (gv * gv)
    mo_ref[...] = mn
    vo_ref[...] = vn
    d_ref[...] = -ADAM_LR * ((mn / (1.0 - ADAM_B1 ** ADAM_STEP)) / (jnp.sqrt(vn / (1.0 - ADAM_B2 ** ADAM_STEP)) + ADAM_EPS)
                             + ADAM_WD * w_ref[...])


def _adamw(g, w, m, v, name):
    R, C = w.shape
    tr = _tile(R, 256, 16)

    def body(g_ref, w_ref, m_ref, v_ref, d_ref, mo_ref, vo_ref):
        _adam_update(g_ref[...], w_ref, m_ref, v_ref, d_ref, mo_ref, vo_ref)

    row = pl.BlockSpec((tr, C), lambda i: (i, 0))
    return pl.pallas_call(
        body, name=name, grid=(R // tr,), in_specs=[row] * 4, out_specs=[row] * 3,
        out_shape=[jax.ShapeDtypeStruct((R, C), _F32)] * 3, compiler_params=_seq(),
    )(g, w, m, v)


def _adamw_landed(zones, w, m, v, name):
    L = len(zones)
    _, R, C = zones[0].shape
    tr = _tile(R, 176, 16)
    nr = R // tr

    def body(*refs):
        w_ref, m_ref, v_ref, g_ref, d_ref, mo_ref, vo_ref = refs[L:]

        def update(g):
            g_ref[...] = g
            _adam_update(g, w_ref, m_ref, v_ref, d_ref, mo_ref, vo_ref)

        _per_layer(pl.program_id(0), refs[:L], update)

    row = pl.BlockSpec((tr, C), lambda l, i: (l * nr + i, 0))
    return pl.pallas_call(
        body, name=name, grid=(L, nr), in_specs=_landed_specs(L, tr, C, nr) + [row] * 3, out_specs=[row] * 4,
        out_shape=[jax.ShapeDtypeStruct((L * R, C), _F32)] * 4, compiler_params=_seq(2),
    )(*zones, w, m, v)


def _pack(parts):
    flat = jnp.concatenate([p.reshape(-1).astype(_F32) for p in parts])
    n = flat.shape[0]
    rows = -(-n // (8 * LANES)) * 8
    return jnp.pad(flat, (0, rows * LANES - n)).reshape(rows, LANES)


def _unpack(packed, shapes):
    flat, out, k = packed.reshape(-1), [], 0
    for s in shapes:
        n = 1
        for d in s:
            n *= d
        out.append(flat[k:k + n].reshape(s))
        k += n
    return out


def kernel(x, norm1_g, norm2_g, pool_w, pool_scale, kv_norm_g, w_kv, b_kv, w_q, b_q, sinks, w_o, b_o, ffn_up, ffn_conv_w, ffn_conv_b, ffn_down, final_g, loss_target, m_norm1_g, m_norm2_g, m_pool_w, m_pool_scale, m_kv_norm_g, m_w_kv, m_b_kv, m_w_q, m_b_q, m_sinks, m_w_o, m_b_o, m_ffn_up, m_ffn_conv_w, m_ffn_conv_b, m_ffn_down, m_final_g, v_norm1_g, v_norm2_g, v_pool_w, v_pool_scale, v_kv_norm_g, v_w_kv, v_b_kv, v_w_q, v_b_q, v_sinks, v_w_o, v_b_o, v_ffn_up, v_ffn_conv_w, v_ffn_conv_b, v_ffn_down, v_final_g):
    S = x.shape[1]
    F2s = ffn_up.shape[2]
    F2 = N_DEV * F2s
    me = _me()
    x0 = x.reshape(S, D)
    tgt = loss_target.reshape(S, D)
    row = lambda a: a.reshape(1, -1)

    small = _pack([pool_scale, ffn_conv_w])
    wire = lambda a: a.astype(_MXU)
    ffn_w = lambda l: [(wire(ffn_up[l]).T, 0), (wire(ffn_down[l]), 0)]
    attn_w = lambda j: [(wire(w_q[j]), 0), (wire(w_o[j]), 0)]
    gathers, token = _start_copies("gather_start", [_gather_group(g) for g in (
        [(wire(pool_w[0]), 1), (small[None], 0)], ffn_w(0), [(wire(pool_w[1]), 1)] + ffn_w(1),
        [(wire(w_kv), 0)] + attn_w(0), ffn_w(2), attn_w(1), ffn_w(3))])

    def gathered(k, after):
        return _wait_copies("gather_wait_%d" % k, [gathers[k]], after)[0]

    pw, up_t, down, wq, wo = [None] * N_A, [None] * DEPTH, [None] * DEPTH, [None] * 2, [None] * 2
    pw[0], small_all = gathered(0, token)
    n_ps = pool_scale.size
    small_all = small_all.reshape(N_DEV, -1)
    pscale = jnp.transpose(small_all[:, :n_ps].reshape(N_DEV, N_A, D // N_DEV), (1, 0, 2)).reshape(N_A, D)
    conv_w = jnp.transpose(small_all[:, n_ps:n_ps + ffn_conv_w.size].reshape(N_DEV, DEPTH, CONV_W, F2s),
                           (1, 2, 0, 3)).reshape(DEPTH, CONV_W, F2)

    def dup(a):
        a4 = a.reshape(a.shape[:-1] + (2 * N_KV, 1, HEAD_DIM))
        return jnp.broadcast_to(a4, a.shape[:-1] + (2 * N_KV, 2, HEAD_DIM)).reshape(a.shape[:-1] + (KVD,))

    def fold(a):
        return a.reshape(a.shape[:-1] + (2 * N_KV, 2, HEAD_DIM)).sum(axis=-2).reshape(a.shape[:-1] + (2 * N_KV * HEAD_DIM,))

    xs, us, qs, os_ = [x0], [], [], []
    xc = x0
    kvd = None
    for l in range(DEPTH):
        if l == 1:
            pw[1], up_t[1], down[1] = gathered(2, xc)
        if l == 3:
            wq[1], wo[1] = gathered(5, xc)
        if l < N_A:
            xc = _pool_fwd(xc, row(norm1_g[l]), pw[l], row(pscale[l]))
        else:
            j = l - N_A
            xc, q, o = _attn_fwd(xc, row(norm1_g[l]), wq[j], row(b_q[j]), sinks[j], kvd, wo[j], row(b_o[j]))
            qs.append(q)
            os_.append(o)
        xs.append(xc)
        if l != 1:
            up_t[l], down[l] = gathered((1, None, 4, 6)[l], xc)
        xc, u, c = _ffn_fwd(xc, row(norm2_g[l]), up_t[l], conv_w[l], row(ffn_conv_b[l]), down[l])
        us.append((u, c))
        xs.append(xc)
        if l == N_A - 1:
            wkv, wq[0], wo[0] = gathered(3, xc)
            wkv_d, bkv_d = dup(wkv), dup(row(b_kv))
            kvd = _kv_fwd(xc, row(kv_norm_g), wkv_d, bkv_d)

    dx, loss_p, d_final = _loss_bwd(xc, row(final_g), tgt)
    d_n1, d_n2, d_cw, d_cb = [None] * DEPTH, [None] * DEPTH, [None] * DEPTH, [None] * DEPTH
    d_bq, d_bo, d_sk, d_ps, dkv_parts = [None] * 2, [None] * 2, [None] * 2, [None] * N_A, []
    up_z, down_z, mix_z = [None] * DEPTH, [None] * DEPTH, [None] * DEPTH
    token = None

    def after(gain):
        return gain if token is None else gain + token[0:1, 0:1]

    rep_names = ["norm1_g", "norm2_g", "kv_norm_g", "b_kv", "b_q", "sinks", "b_o", "ffn_conv_b", "final_g"]

    def small_parts(n1_rest, ps_rest):
        zero = jnp.zeros((1, D), _F32)
        return [jnp.concatenate([zero] + n1_rest), jnp.concatenate(d_n2), d_kvg, fold(d_bkv), jnp.concatenate(d_bq),
                jnp.concatenate([s[:, :N_HEADS] for s in d_sk]), jnp.concatenate(d_bo), jnp.concatenate(d_cb), d_final,
                jnp.concatenate([zero] + ps_rest), jnp.stack(d_cw), loss_p[0:1, 0:1]]

    for l in reversed(range(DEPTH)):
        x_in, x_mid, x_out = xs[2 * l], xs[2 * l + 1], xs[2 * l + 2]
        mixer_grads = []
        if l == N_A - 1:
            dx, d_wkv, d_bkv, d_kvg = _kv_bwd(x_out, dx, after(row(kv_norm_g)), wkv_d, *dkv_parts)
            mixer_grads.append((fold(d_wkv).astype(_MXU), 0))
        dy = dx
        dx, du, a, h, d_cw[l], d_cb[l], d_n2[l] = _ffn_bwd(
            x_mid, dy, *us[l], after(row(norm2_g[l])), up_t[l], conv_w[l], down[l])
        g_up = _tn_matmul(du, h, "tn_up")
        if l > 0:
            (up_z[l], down_z[l]), token = _start_copies("scatter_ffn_%d" % l, [
                _scatter_group([(g_up, 0)]), _scatter_group([(_tn_matmul(a, dy, "tn_down"), 0)])])
        else:
            (small_handle, up_z[l]), token = _start_copies("scatter_up_0", [
                _gather_group([(_pack(small_parts(d_n1[1:], d_ps[1:]))[None], 0)]), _scatter_group([(g_up, 0)])])
            (down_z[l],), token = _start_copies("scatter_down_0", [
                _scatter_group([(_tn_matmul(a, dy, "tn_down", token), 0)])])
        dy = dx
        if l < N_A:
            dx, d_pw, d_ps[l], d_n1[l] = _pool_bwd(x_in, dy, after(row(norm1_g[l])), pw[l], row(pscale[l]))
            mixer_grads.append((d_pw.astype(_MXU), 1))
        else:
            j = l - N_A
            dx, dq, h, d_cur, d_prev, d_bq[j], d_bo[j], d_n1[l], d_sk[j] = _attn_bwd(
                x_in, dy, qs[j], os_[j], after(row(norm1_g[l])), wq[j], sinks[j], kvd, wo[j])
            mixer_grads += [(_tn_matmul(h, dq, "tn_q"), 0), (_tn_matmul(os_[j], dy, "tn_o"), 0)]
            dkv_parts += [d_cur, d_prev]
        (mix_z[l],), token = _start_copies("scatter_mixer_%d" % l, [_scatter_group(mixer_grads)])

    late = _all_reduce_small(_pack([d_n1[0], d_ps[0]])).reshape(-1)
    (early,), = _wait_copies("small_wait", [small_handle], late)
    given = dict(norm1_g=norm1_g, norm2_g=norm2_g, kv_norm_g=kv_norm_g, b_kv=b_kv, b_q=b_q, sinks=sinks, b_o=b_o,
                 ffn_conv_b=ffn_conv_b, final_g=final_g, pool_scale=pool_scale, ffn_conv_w=ffn_conv_w)
    tot = _unpack(_sum8([early]), [given[k].shape for k in rep_names] + [(N_A, D), (DEPTH, CONV_W, F2), ()])
    tot[0] = tot[0].at[0].add(late[:D])
    tot[-3] = tot[-3].at[0].add(late[D:2 * D])
    grad = dict(zip(rep_names, tot))
    loss = tot[-1]
    grad["pool_scale"] = lax.dynamic_slice_in_dim(tot[-3], me * (D // N_DEV), D // N_DEV, axis=1)
    grad["ffn_conv_w"] = lax.dynamic_slice_in_dim(tot[-2], me * F2s, F2s, axis=2)

    moms = dict(norm1_g=(m_norm1_g, v_norm1_g), norm2_g=(m_norm2_g, v_norm2_g), pool_w=(m_pool_w, v_pool_w),
                pool_scale=(m_pool_scale, v_pool_scale), kv_norm_g=(m_kv_norm_g, v_kv_norm_g), w_kv=(m_w_kv, v_w_kv),
                b_kv=(m_b_kv, v_b_kv), w_q=(m_w_q, v_w_q), b_q=(m_b_q, v_b_q), sinks=(m_sinks, v_sinks),
                w_o=(m_w_o, v_w_o), b_o=(m_b_o, v_b_o), ffn_up=(m_ffn_up, v_ffn_up),
                ffn_conv_w=(m_ffn_conv_w, v_ffn_conv_w), ffn_conv_b=(m_ffn_conv_b, v_ffn_conv_b),
                ffn_down=(m_ffn_down, v_ffn_down), final_g=(m_final_g, v_final_g))
    given.update(pool_w=pool_w, w_kv=w_kv, w_q=w_q, w_o=w_o, ffn_up=ffn_up, ffn_down=ffn_down)
    delta, new_m, new_v = {}, {}, {}

    small_names = rep_names + ["pool_scale", "ffn_conv_w"]
    shapes = [given[k].shape for k in small_names]
    outs = _adamw(_pack([grad[k] for k in small_names]), _pack([given[k] for k in small_names]),
                  _pack([moms[k][0] for k in small_names]), _pack([moms[k][1] for k in small_names]), "adamw_small")
    for dst, packed in zip((delta, new_m, new_v), outs):
        dst.update(zip(small_names, _unpack(packed, shapes)))

    zones = _wait_copies("scatter_wait", up_z + down_z + mix_z, outs[0])
    up_z, down_z, mix_z = zones[:DEPTH], zones[DEPTH:2 * DEPTH], zones[2 * DEPTH:]
    by_name = dict(ffn_up=[z[0] for z in up_z], ffn_down=[z[0] for z in down_z],
                   w_q=[mix_z[N_A][0], mix_z[N_A + 1][0]], w_o=[mix_z[N_A][1], mix_z[N_A + 1][1]],
                   w_kv=[mix_z[N_A - 1][0]], pool_w=[mix_z[0][0], mix_z[N_A - 1][1]])

    def update(name, g, cols):
        w = given[name]
        two_d = lambda a: a.reshape(-1, cols)
        if g is None:
            landed = [z.reshape(N_DEV, -1, cols) for z in by_name[name]]
            outs = _adamw_landed(landed, two_d(w), two_d(moms[name][0]), two_d(moms[name][1]), "adamw_" + name)
            grad[name] = outs[0].reshape(w.shape)
        else:
            outs = _adamw(g, two_d(w), two_d(moms[name][0]), two_d(moms[name][1]), "adamw_" + name)
        delta[name], new_m[name], new_v[name] = (o.reshape(w.shape) for o in outs[-3:])

    grad["ffn_up"] = jnp.swapaxes(_sum8(by_name["ffn_up"]).reshape(DEPTH, F2s, D), 1, 2)
    update("ffn_up", grad["ffn_up"].reshape(-1, F2s), F2s)
    update("ffn_down", None, D)
    update("w_q", None, D)
    update("w_o", None, D)
    update("w_kv", None, w_kv.shape[1])
    update("pool_w", None, GC)

    names = ["norm1_g", "norm2_g", "pool_w", "pool_scale", "kv_norm_g", "w_kv", "b_kv", "w_q", "b_q", "sinks", "w_o",
             "b_o", "ffn_up", "ffn_conv_w", "ffn_conv_b", "ffn_down", "final_g"]
    return (loss, dx.reshape(x.shape), *[grad[k] for k in names], *[delta[k] for k in names],
            *[new_m[k] for k in names], *[new_v[k] for k in names])
```

```python
import functools

import jax
import jax.numpy as jnp
from jax import lax
from jax.experimental import pallas as pl
from jax.experimental.pallas import tpu as pltpu

_F32 = jnp.float32
_MXU = jnp.bfloat16

N_DEV = 8
D = 1024
DEPTH = 4
N_A = 2
POOL_WINDOWS = (2, 4, 8, 16)
GC = D // len(POOL_WINDOWS)
HALO = 16
HEAD_DIM = 64
N_HEADS = D // HEAD_DIM
GROUP = 8
N_KV = N_HEADS // GROUP
BLK = 128
PAIR = 2 * HEAD_DIM
KVD = 4 * N_KV * HEAD_DIM
CONV_W = 3
EPS = 1e-5
NEG = -1e30

ADAM_LR = 0.001
ADAM_B1 = 0.9
ADAM_B2 = 0.999
ADAM_EPS = 1e-08
ADAM_WD = 0.01
ADAM_STEP = 10

V7X_VMEM_LIMIT = 56 * 1024 * 1024
LANES = 128

_NT = (((1,), (1,)), ((), ()))
_TN = (((0,), (0,)), ((), ()))


def _params(**kw):
    return pltpu.CompilerParams(vmem_limit_bytes=V7X_VMEM_LIMIT, **kw)


def _seq(n=1):
    return _params(dimension_semantics=("arbitrary",) * n)


def _dot(a, b, dims=None):
    if dims is None:
        return jnp.dot(a, b, preferred_element_type=_F32)
    return lax.dot_general(a, b, dims, preferred_element_type=_F32)


def _rms(x):
    r = lax.rsqrt(jnp.mean(x * x, axis=-1, keepdims=True) + EPS)
    return x * r, r


def _rms_bwd(dh, xn, r, g):
    dxn = dh * g
    return r * (dxn - xn * jnp.mean(dxn * xn, axis=-1, keepdims=True))


def _colsum(a):
    return jnp.sum(a, axis=0, keepdims=True)


def _tile(n, want, mult=8):
    for t in range(min(want, n), 0, -1):
        if n % t == 0 and t % mult == 0:
            return t
    return n


def _full(shape):
    zeros = (0,) * len(shape)
    return pl.BlockSpec(shape, lambda *_: zeros)


def _window_sum(ext, win, trailing):
    R = ext.shape[0]
    acc, k = ext, 1
    while k < win:
        acc = acc + pltpu.roll(acc, k if trailing else R - k, axis=0)
        k *= 2
    return acc


def _pool_windows(hbuf, h, row, T):
    out = []
    for gi, win in enumerate(POOL_WINDOWS):
        cs = slice(gi * GC, (gi + 1) * GC)
        acc = _window_sum(hbuf[:, cs], win, True)[HALO:, :]
        cnt = jnp.minimum(row + 1, win).astype(_F32)
        out.append((acc / cnt - h[:, cs], cnt))
    return out


def _pool_fwd(x, g, w, sc):
    S = x.shape[0]
    T = _tile(S, 512, HALO)
    n, hb = S // T, T // HALO

    def body(x_ref, xh_ref, g_ref, w_ref, sc_ref, o_ref, hbuf):
        i = pl.program_id(0)
        gv = g_ref[...]
        xv = x_ref[...]
        h = _rms(xv)[0] * gv
        hbuf[0:HALO, :] = jnp.where(i > 0, _rms(xh_ref[...])[0] * gv, 0.0)
        hbuf[HALO:, :] = h
        row = i * T + lax.broadcasted_iota(jnp.int32, (T, 1), 0)
        for gi, (p, _) in enumerate(_pool_windows(hbuf, h, row, T)):
            cs = slice(gi * GC, (gi + 1) * GC)
            z = _dot(p.astype(_MXU), w_ref[gi])
            o_ref[:, cs] = xv[:, cs] + z * sc_ref[:, cs]

    return pl.pallas_call(
        body, name="pool_fwd", grid=(n,),
        in_specs=[pl.BlockSpec((T, D), lambda i: (i, 0)),
                  pl.BlockSpec((HALO, D), lambda i: (jnp.maximum(i * hb - 1, 0), 0)),
                  _full((1, D)), _full((4, GC, GC)), _full((1, D))],
        out_specs=pl.BlockSpec((T, D), lambda i: (i, 0)),
        out_shape=jax.ShapeDtypeStruct((S, D), _F32),
        scratch_shapes=[pltpu.VMEM((T + HALO, D), _F32)],
        compiler_params=_seq(),
    )(x, x, g, w, sc)


def _pool_bwd(x, dy, g, w, sc):
    S = x.shape[0]
    T = _tile(S, 512, HALO)
    n, hb = S // T, T // HALO

    def body(x_ref, xh_ref, dy_ref, dyh_ref, g_ref, w_ref, sc_ref, dx_ref, dw_ref, dsc_ref, dg_ref,
             hbuf, qbuf, dhbuf):
        i = pl.program_id(0)

        @pl.when(i == 0)
        def _():
            dw_ref[...] = jnp.zeros_like(dw_ref)
            dsc_ref[...] = jnp.zeros_like(dsc_ref)
            dg_ref[...] = jnp.zeros_like(dg_ref)

        gv = g_ref[...]
        xv = x_ref[...]
        xn, r = _rms(xv)
        h = xn * gv
        hbuf[0:HALO, :] = jnp.where(i > 0, _rms(xh_ref[...])[0] * gv, 0.0)
        hbuf[HALO:, :] = h
        dyv = dy_ref[...]
        dz = dyv * sc_ref[...]
        dzh = jnp.where(i < n - 1, dyh_ref[...], 0.0) * sc_ref[...]
        row = i * T + lax.broadcasted_iota(jnp.int32, (T, 1), 0)
        rowh = (i + 1) * T + lax.broadcasted_iota(jnp.int32, (HALO, 1), 0)
        for gi, (p, cnt) in enumerate(_pool_windows(hbuf, h, row, T)):
            win = POOL_WINDOWS[gi]
            cs = slice(gi * GC, (gi + 1) * GC)
            pb = p.astype(_MXU)
            wg = w_ref[gi]
            dsc_ref[:, cs] += _colsum(dyv[:, cs] * _dot(pb, wg))
            dzb = dz[:, cs].astype(_MXU)
            dw_ref[gi] += _dot(pb, dzb, _TN)
            dp = _dot(dzb, wg, _NT)
            dph = _dot(dzh[:, cs].astype(_MXU), wg, _NT)
            qbuf[0:T, cs] = dp / cnt
            qbuf[T:T + HALO, cs] = dph / jnp.minimum(rowh + 1, win).astype(_F32)
            dhbuf[:, cs] = _window_sum(qbuf[:, cs], win, False)[0:T, :] - dp
        dh = dhbuf[...]
        dg_ref[...] += _colsum(dh * xn)
        dx_ref[...] = dyv + _rms_bwd(dh, xn, r, gv)

    return pl.pallas_call(
        body, name="pool_bwd", grid=(n,),
        in_specs=[pl.BlockSpec((T, D), lambda i: (i, 0)),
                  pl.BlockSpec((HALO, D), lambda i: (jnp.maximum(i * hb - 1, 0), 0)),
                  pl.BlockSpec((T, D), lambda i: (i, 0)),
                  pl.BlockSpec((HALO, D), lambda i: (jnp.minimum((i + 1) * hb, S // HALO - 1), 0)),
                  _full((1, D)), _full((4, GC, GC)), _full((1, D))],
        out_specs=[pl.BlockSpec((T, D), lambda i: (i, 0)), _full((4, GC, GC)), _full((1, D)), _full((1, D))],
        out_shape=[jax.ShapeDtypeStruct((S, D), _F32), jax.ShapeDtypeStruct((4, GC, GC), _F32),
                   jax.ShapeDtypeStruct((1, D), _F32), jax.ShapeDtypeStruct((1, D), _F32)],
        scratch_shapes=[pltpu.VMEM((T + HALO, D), _F32), pltpu.VMEM((T + HALO, D), _F32), pltpu.VMEM((T, D), _F32)],
        compiler_params=_seq(),
    )(x, x, dy, dy, g, w, sc)


FFN_FWD_TILE, FFN_FWD_CHUNKS = 256, 2
FFN_BWD_TILE, FFN_BWD_CHUNKS = 128, 2
EDGE = 8


def _shift_down(v, k, prev):
    r = pltpu.roll(v, k, axis=0)
    i8 = lax.broadcasted_iota(jnp.int32, (EDGE, v.shape[1]), 0)
    head = jnp.where(i8 >= k, r[0:EDGE, :], pltpu.roll(prev, k, axis=0))
    return jnp.concatenate([head, r[EDGE:, :]], axis=0)


def _shift_up(v, k, nxt):
    T = v.shape[0]
    r = pltpu.roll(v, T - k, axis=0)
    i8 = lax.broadcasted_iota(jnp.int32, (EDGE, v.shape[1]), 0)
    tail = jnp.where(i8 < EDGE - k, r[T - EDGE:, :], pltpu.roll(nxt, EDGE - k, axis=0))
    return jnp.concatenate([r[:T - EDGE, :], tail], axis=0)


def _load_weights(i, pairs, sems):
    @pl.when(i == 0)
    def _():
        cps = [pltpu.make_async_copy(src, dst, sems.at[k]) for k, (src, dst) in enumerate(pairs)]
        for cp in cps:
            cp.start()
        for cp in cps:
            cp.wait()


def _ffn_fwd(x, g, wup_t, cw, cb, wdn):
    S = x.shape[0]
    F2 = wup_t.shape[0]
    F = F2 // 2
    C = F // FFN_FWD_CHUNKS
    T = _tile(S, FFN_FWD_TILE, 16)
    n = S // T

    def body(x_ref, g_ref, wup_hbm, cw_ref, cb_ref, wdn_hbm, o_ref, u_ref, c_ref, wup, wdnv, carry, sems):
        i = pl.program_id(0)
        _load_weights(i, [(wup_hbm, wup), (wdn_hbm, wdnv)], sems)

        @pl.when(i == 0)
        def _():
            carry[...] = jnp.zeros_like(carry)

        xv = x_ref[...]
        hb = (_rms(xv)[0] * g_ref[...]).astype(_MXU)
        acc = jnp.zeros((T, D), _F32)
        for j in range(FFN_FWD_CHUNKS):
            halves = []
            for cs in (slice(j * C, (j + 1) * C), slice(F + j * C, F + (j + 1) * C)):
                u = _dot(hb, wup[cs, :], _NT)
                u_ref[:, cs] = u.astype(u_ref.dtype)
                prev = carry[:, cs]
                carry[:, cs] = u[T - EDGE:, :]
                c = (cw_ref[0:1, cs] * _shift_down(u, 2, prev) + cw_ref[1:2, cs] * _shift_down(u, 1, prev)
                     + cw_ref[2:3, cs] * u + cb_ref[:, cs])
                c_ref[:, cs] = c.astype(c_ref.dtype)
                halves.append(c)
            cg, cv = halves
            a = (cg * jax.nn.sigmoid(cg)) * cv
            acc = acc + _dot(a.astype(_MXU), wdnv[j * C:(j + 1) * C, :])
        o_ref[...] = xv + acc

    any_ = pl.BlockSpec(memory_space=pl.ANY)
    wide = pl.BlockSpec((T, F2), lambda i: (i, 0))
    return pl.pallas_call(
        body, name="ffn_fwd", grid=(n,),
        in_specs=[pl.BlockSpec((T, D), lambda i: (i, 0)), _full((1, D)), any_, _full((CONV_W, F2)), _full((1, F2)), any_],
        out_specs=[pl.BlockSpec((T, D), lambda i: (i, 0)), wide, wide],
        out_shape=[jax.ShapeDtypeStruct((S, D), _F32), jax.ShapeDtypeStruct((S, F2), _MXU),
                   jax.ShapeDtypeStruct((S, F2), _MXU)],
        scratch_shapes=[pltpu.VMEM((F2, D), _MXU), pltpu.VMEM((F, D), _MXU),
                        pltpu.VMEM((EDGE, F2), _F32), pltpu.SemaphoreType.DMA((2,))],
        compiler_params=_seq(),
    )(x, g, wup_t, cw, cb, wdn)


def _ffn_bwd(x, dy, u, c, g, wup_t, cw, wdn):
    S = x.shape[0]
    F2 = wup_t.shape[0]
    F = F2 // 2
    C = F // FFN_BWD_CHUNKS
    T = _tile(S, FFN_BWD_TILE, 16)
    n = S // T

    def body(x_ref, dy_ref, u_ref, c_ref, g_ref, wup_hbm, cw_ref, wdn_hbm,
             dx_ref, du_ref, a_ref, h_ref, dcw_ref, dcb_ref, dg_ref, wup, wdnv, carry, sems):
        i = pl.program_id(0)
        _load_weights(i, [(wup_hbm, wup), (wdn_hbm, wdnv)], sems)

        @pl.when(i == 0)
        def _():
            carry[...] = jnp.zeros_like(carry)
            dcw_ref[...] = jnp.zeros_like(dcw_ref)
            dcb_ref[...] = jnp.zeros_like(dcb_ref)
            dg_ref[...] = jnp.zeros_like(dg_ref)

        gv = g_ref[...]
        xv = x_ref[...]
        xn, r = _rms(xv)
        hbf = (xn * gv).astype(_MXU)
        h_ref[...] = hbf
        dyv = dy_ref[...]
        dyb = dyv.astype(_MXU)
        dh = jnp.zeros((T, D), _F32)
        for j in range(FFN_BWD_CHUNKS):
            gs, vs = slice(j * C, (j + 1) * C), slice(F + j * C, F + (j + 1) * C)
            cg, cv = c_ref[:, gs].astype(_F32), c_ref[:, vs].astype(_F32)
            sg = jax.nn.sigmoid(cg)
            sl = cg * sg
            a_ref[:, gs] = (sl * cv).astype(a_ref.dtype)
            da = _dot(dyb, wdnv[gs, :], _NT)
            for cs, dc in ((gs, da * cv * (sg * (1.0 + cg * (1.0 - sg)))), (vs, da * sl)):
                nxt = carry[:, cs]
                carry[:, cs] = dc[0:EDGE, :]
                dc1, dc2 = _shift_up(dc, 1, nxt), _shift_up(dc, 2, nxt)
                uf = u_ref[:, cs].astype(_F32)
                dcb_ref[:, cs] += _colsum(dc)
                for k, d in enumerate((dc2, dc1, dc)):
                    dcw_ref[k:k + 1, cs] += _colsum(d * uf)
                du = cw_ref[2:3, cs] * dc + cw_ref[1:2, cs] * dc1 + cw_ref[0:1, cs] * dc2
                dub = du.astype(_MXU)
                du_ref[:, cs] = dub
                dh = dh + _dot(dub, wup[cs, :])
        dg_ref[...] += _colsum(dh * xn)
        dx_ref[...] = dyv + _rms_bwd(dh, xn, r, gv)

    any_ = pl.BlockSpec(memory_space=pl.ANY)
    rev = lambda i: (n - 1 - i, 0)
    return pl.pallas_call(
        body, name="ffn_bwd", grid=(n,),
        in_specs=[pl.BlockSpec((T, D), rev), pl.BlockSpec((T, D), rev), pl.BlockSpec((T, F2), rev),
                  pl.BlockSpec((T, F2), rev), _full((1, D)), any_, _full((CONV_W, F2)), any_],
        out_specs=[pl.BlockSpec((T, D), rev), pl.BlockSpec((T, F2), rev), pl.BlockSpec((T, F), rev),
                   pl.BlockSpec((T, D), rev), _full((CONV_W, F2)), _full((1, F2)), _full((1, D))],
        out_shape=[jax.ShapeDtypeStruct((S, D), _F32), jax.ShapeDtypeStruct((S, F2), _MXU),
                   jax.ShapeDtypeStruct((S, F), _MXU), jax.ShapeDtypeStruct((S, D), _MXU),
                   jax.ShapeDtypeStruct((CONV_W, F2), _F32), jax.ShapeDtypeStruct((1, F2), _F32),
                   jax.ShapeDtypeStruct((1, D), _F32)],
        scratch_shapes=[pltpu.VMEM((F2, D), _MXU), pltpu.VMEM((F, D), _MXU),
                        pltpu.VMEM((EDGE, F2), _F32), pltpu.SemaphoreType.DMA((2,))],
        compiler_params=_seq(),
    )(x, dy, u, c, g, wup_t, cw, wdn)


def _tn_matmul(a, b, name, token=None):
    S, M = a.shape
    N = b.shape[1]
    bm = _tile(M, 1408, LANES)
    tk = _tile(S, 2048, 16)
    nk = S // tk
    tokens = [] if token is None else [token]

    def body(a_ref, b_ref, *rest):
        o_ref, acc = rest[-2:]
        k = pl.program_id(1)

        @pl.when(k == 0)
        def _():
            acc[...] = jnp.zeros_like(acc)

        acc[...] += _dot(a_ref[...].astype(_MXU), b_ref[...].astype(_MXU), _TN)

        @pl.when(k == nk - 1)
        def _():
            o_ref[...] = acc[...].astype(o_ref.dtype)

    return pl.pallas_call(
        body, name=name, grid=(M // bm, nk),
        in_specs=[pl.BlockSpec((tk, bm), lambda i, k: (k, i)), pl.BlockSpec((tk, N), lambda i, k: (k, 0))]
        + [_full((8, LANES))] * len(tokens),
        out_specs=pl.BlockSpec((bm, N), lambda i, k: (i, 0)),
        out_shape=jax.ShapeDtypeStruct((M, N), _MXU),
        scratch_shapes=[pltpu.VMEM((bm, N), _F32)],
        compiler_params=_seq(2),
    )(a, b, *tokens)


def _kv_fwd(x, g, wkv, bkv):
    S = x.shape[0]
    T = _tile(S, 512, 16)

    def body(x_ref, g_ref, w_ref, b_ref, o_ref):
        hb = (_rms(x_ref[...])[0] * g_ref[...]).astype(_MXU)
        o_ref[...] = (_dot(hb, w_ref[...]) + b_ref[...]).astype(o_ref.dtype)

    return pl.pallas_call(
        body, name="kv_fwd", grid=(S // T,),
        in_specs=[pl.BlockSpec((T, D), lambda i: (i, 0)), _full((1, D)), _full((D, KVD)), _full((1, KVD))],
        out_specs=pl.BlockSpec((T, KVD), lambda i: (i, 0)),
        out_shape=jax.ShapeDtypeStruct((S, KVD), _MXU),
        compiler_params=_seq(),
    )(x, g, wkv, bkv)


def _kv_bwd(x, dx_in, g, wkv, cur_a, prev_a, cur_b, prev_b):
    S = x.shape[0]
    T = _tile(S, 512, BLK)
    n, per = S // T, T // BLK

    def body(x_ref, dxi_ref, g_ref, w_ref, ca, pa, na, cb, pb, nb, dx_ref, dw_ref, db_ref, dg_ref):
        i = pl.program_id(0)

        @pl.when(i == 0)
        def _():
            dw_ref[...] = jnp.zeros_like(dw_ref)
            db_ref[...] = jnp.zeros_like(db_ref)
            dg_ref[...] = jnp.zeros_like(dg_ref)

        gv = g_ref[...]
        xn, r = _rms(x_ref[...])
        nxt = jnp.where(i < n - 1, na[...] + nb[...], 0.0)
        prev = jnp.concatenate([pa[BLK:, :] + pb[BLK:, :], nxt], axis=0) if per > 1 else nxt
        dkv = ca[...] + cb[...] + prev
        db_ref[...] += _colsum(dkv)
        dkb = dkv.astype(_MXU)
        dw_ref[...] += _dot((xn * gv).astype(_MXU), dkb, _TN)
        dh = _dot(dkb, w_ref[...], _NT)
        dg_ref[...] += _colsum(dh * xn)
        dx_ref[...] = dxi_ref[...] + _rms_bwd(dh, xn, r, gv)

    blk = lambda w: pl.BlockSpec((T, w), lambda i: (i, 0))
    nxt = pl.BlockSpec((BLK, KVD), lambda i: (jnp.minimum((i + 1) * per, S // BLK - 1), 0))
    return pl.pallas_call(
        body, name="kv_bwd", grid=(n,),
        in_specs=[blk(D), blk(D), _full((1, D)), _full((D, KVD)), blk(KVD), blk(KVD), nxt, blk(KVD), blk(KVD), nxt],
        out_specs=[blk(D), _full((D, KVD)), _full((1, KVD)), _full((1, D))],
        out_shape=[jax.ShapeDtypeStruct((S, D), _F32), jax.ShapeDtypeStruct((D, KVD), _F32),
                   jax.ShapeDtypeStruct((1, KVD), _F32), jax.ShapeDtypeStruct((1, D), _F32)],
        compiler_params=_seq(),
    )(x, dx_in, g, wkv, cur_a, prev_a, prev_a, cur_b, prev_b, prev_b)


STACK = GROUP * BLK


def _attn_mask(i, rows):
    qi = lax.broadcasted_iota(jnp.int32, (rows, 2 * BLK), 0) & (BLK - 1)
    si = lax.broadcasted_iota(jnp.int32, (rows, 2 * BLK), 1)
    return (si > qi) & (si <= qi + BLK) & jnp.logical_or(i > 0, si >= BLK)


def _low_half():
    return lax.broadcasted_iota(jnp.int32, (BLK, PAIR), 1) < HEAD_DIM


def _stack_heads(ref, kh, dst):
    low = _low_half()
    for pp in range(GROUP // 2):
        pr = kh * (GROUP // 2) + pp
        v2 = ref[:, pr * PAIR:(pr + 1) * PAIR]
        zero = jnp.zeros_like(v2)
        dst[2 * pp * BLK:(2 * pp + 1) * BLK, :] = jnp.where(low, v2, zero)
        dst[(2 * pp + 1) * BLK:(2 * pp + 2) * BLK, :] = jnp.where(low, zero, v2)


def _unstack_heads(st, pp):
    return jnp.where(_low_half(), st[2 * pp * BLK:(2 * pp + 1) * BLK, :], st[(2 * pp + 1) * BLK:(2 * pp + 2) * BLK, :])


def _sink_col(sk_ref, kh):
    return jnp.concatenate([jnp.full((BLK, 1), sk_ref[kh * GROUP + h], _F32) for h in range(GROUP)], axis=0)


def _head_probs(qm, kd, mask, sink):
    s = jnp.where(mask, _dot(qm, kd, _NT) * (HEAD_DIM ** -0.5), NEG)
    m = jnp.maximum(jnp.max(s, axis=-1, keepdims=True), sink)
    p = jnp.exp(s - m)
    es = jnp.exp(sink - m)
    inv = 1.0 / (jnp.sum(p, axis=-1, keepdims=True) + es)
    return p * inv, es * inv


def _attn_fwd(x, g, wq, bq, sinks, kvd, wo, bo):
    S = x.shape[0]
    n = S // BLK

    def body(x_ref, g_ref, wq_ref, bq_ref, sk_ref, kp_ref, kc_ref, wo_ref, bo_ref, xo_ref, q_ref, o_ref, win):
        i = pl.program_id(0)
        xv = x_ref[...]
        hb = (_rms(xv)[0] * g_ref[...]).astype(_MXU)
        q_ref[...] = (_dot(hb, wq_ref[...]) + bq_ref[...]).astype(q_ref.dtype)
        win[0:BLK, :] = kp_ref[...]
        win[BLK:, :] = kc_ref[...]
        mask = _attn_mask(i, BLK)
        low = _low_half()
        for pr in range(N_HEADS // 2):
            kh = (2 * pr) // GROUP
            kd = win[:, kh * PAIR:(kh + 1) * PAIR]
            vd = win[:, (N_KV + kh) * PAIR:(N_KV + kh + 1) * PAIR]
            q2 = q_ref[:, pr * PAIR:(pr + 1) * PAIR]
            outs = []
            for half in range(2):
                qm = jnp.where(low if half == 0 else ~low, q2, jnp.zeros_like(q2))
                pbs, _ = _head_probs(qm, kd, mask, sk_ref[2 * pr + half])
                outs.append(_dot(pbs.astype(_MXU), vd))
            o_ref[:, pr * PAIR:(pr + 1) * PAIR] = jnp.where(low, outs[0], outs[1]).astype(o_ref.dtype)
        xo_ref[...] = xv + _dot(o_ref[...], wo_ref[...]) + bo_ref[...]

    blk = lambda w: pl.BlockSpec((BLK, w), lambda i: (i, 0))
    return pl.pallas_call(
        body, name="attn_fwd", grid=(n,),
        in_specs=[blk(D), _full((1, D)), _full((D, D)), _full((1, D)),
                  pl.BlockSpec(memory_space=pltpu.SMEM),
                  pl.BlockSpec((BLK, KVD), lambda i: (jnp.maximum(i - 1, 0), 0)), blk(KVD),
                  _full((D, D)), _full((1, D))],
        out_specs=[blk(D), blk(D), blk(D)],
        out_shape=[jax.ShapeDtypeStruct((S, D), _F32), jax.ShapeDtypeStruct((S, D), _MXU),
                   jax.ShapeDtypeStruct((S, D), _MXU)],
        scratch_shapes=[pltpu.VMEM((2 * BLK, KVD), _MXU)],
        compiler_params=_seq(),
    )(x, g, wq, bq, sinks, kvd, kvd, wo, bo)


def _attn_bwd(x, dy, q, o, g, wq, sinks, kvd, wo):
    S = x.shape[0]
    n = S // BLK
    all_rows = N_HEADS * BLK

    def body(x_ref, dy_ref, q_ref, o_ref, g_ref, wq_ref, sk_ref, kp_ref, kc_ref, wo_ref,
             dx_ref, dq_ref, h_ref, dc_ref, dp_ref, dbq_ref, dbo_ref, dg_ref, dsk_ref, win, dob, qs, dos, pall, dsall):
        i = pl.program_id(0)

        @pl.when(i == 0)
        def _():
            dbq_ref[...] = jnp.zeros_like(dbq_ref)
            dbo_ref[...] = jnp.zeros_like(dbo_ref)
            dg_ref[...] = jnp.zeros_like(dg_ref)
            dsk_ref[...] = jnp.zeros_like(dsk_ref)

        gv = g_ref[...]
        xv = x_ref[...]
        xn, r = _rms(xv)
        h_ref[...] = (xn * gv).astype(h_ref.dtype)
        dyv = dy_ref[...]
        dbo_ref[...] += _colsum(dyv)
        dob[...] = _dot(dyv.astype(_MXU), wo_ref[...], _NT).astype(dob.dtype)
        win[0:BLK, :] = kp_ref[...]
        win[BLK:, :] = kc_ref[...]
        mask = _attn_mask(i, BLK)
        low = _low_half()
        lane = lax.broadcasted_iota(jnp.int32, (1, LANES), 1)
        for pr in range(N_HEADS // 2):
            kh = (2 * pr) // GROUP
            kd = win[:, kh * PAIR:(kh + 1) * PAIR]
            vd = win[:, (N_KV + kh) * PAIR:(N_KV + kh + 1) * PAIR]
            q2 = q_ref[:, pr * PAIR:(pr + 1) * PAIR]
            do2 = dob[:, pr * PAIR:(pr + 1) * PAIR]
            od = do2.astype(_F32) * o_ref[:, pr * PAIR:(pr + 1) * PAIR].astype(_F32)
            for half in range(2):
                hd = 2 * pr + half
                rows = slice(hd * BLK, (hd + 1) * BLK)
                sel = low if half == 0 else ~low
                qm = jnp.where(sel, q2, jnp.zeros_like(q2))
                dom = jnp.where(sel, do2, jnp.zeros_like(do2))
                qs[rows, :] = qm
                dos[rows, :] = dom
                pbs, ps = _head_probs(qm, kd, mask, sk_ref[hd])
                pall[rows, :] = pbs.astype(_MXU)
                delta = jnp.sum(jnp.where(sel, od, 0.0), axis=-1, keepdims=True)
                dsall[rows, :] = (pbs * (_dot(dom, vd, _NT) - delta) * (HEAD_DIM ** -0.5)).astype(_MXU)
                dsk_ref[...] -= jnp.where(lane == hd, _colsum(ps * delta), 0.0)
        dq_all = []
        for kh in range(N_KV):
            ks = slice(kh * PAIR, (kh + 1) * PAIR)
            vs = slice((N_KV + kh) * PAIR, (N_KV + kh + 1) * PAIR)
            rows = slice(kh * STACK, (kh + 1) * STACK)
            dqst = _dot(dsall[rows, :], win[:, ks])
            dk = _dot(dsall[rows, :], qs[rows, :], _TN)
            dv = _dot(pall[rows, :], dos[rows, :], _TN)
            dp_ref[:, ks], dc_ref[:, ks] = dk[0:BLK, :], dk[BLK:, :]
            dp_ref[:, vs], dc_ref[:, vs] = dv[0:BLK, :], dv[BLK:, :]
            dq_all += [_unstack_heads(dqst, pp) for pp in range(GROUP // 2)]
        dq = jnp.concatenate(dq_all, axis=1)
        dbq_ref[...] += _colsum(dq)
        dqb = dq.astype(_MXU)
        dq_ref[...] = dqb
        dh = _dot(dqb, wq_ref[...], _NT)
        dg_ref[...] += _colsum(dh * xn)
        dx_ref[...] = dyv + _rms_bwd(dh, xn, r, gv)

    blk = lambda w: pl.BlockSpec((BLK, w), lambda i: (i, 0))
    return pl.pallas_call(
        body, name="attn_bwd", grid=(n,),
        in_specs=[blk(D), blk(D), blk(D), blk(D), _full((1, D)), _full((D, D)),
                  pl.BlockSpec(memory_space=pltpu.SMEM),
                  pl.BlockSpec((BLK, KVD), lambda i: (jnp.maximum(i - 1, 0), 0)), blk(KVD), _full((D, D))],
        out_specs=[blk(D), blk(D), blk(D), blk(KVD), blk(KVD),
                   _full((1, D)), _full((1, D)), _full((1, D)), _full((1, LANES))],
        out_shape=[jax.ShapeDtypeStruct((S, D), _F32), jax.ShapeDtypeStruct((S, D), _MXU),
                   jax.ShapeDtypeStruct((S, D), _MXU), jax.ShapeDtypeStruct((S, KVD), _F32),
                   jax.ShapeDtypeStruct((S, KVD), _F32), jax.ShapeDtypeStruct((1, D), _F32),
                   jax.ShapeDtypeStruct((1, D), _F32), jax.ShapeDtypeStruct((1, D), _F32),
                   jax.ShapeDtypeStruct((1, LANES), _F32)],
        scratch_shapes=[pltpu.VMEM((2 * BLK, KVD), _MXU), pltpu.VMEM((BLK, D), _MXU),
                        pltpu.VMEM((all_rows, PAIR), _MXU), pltpu.VMEM((all_rows, PAIR), _MXU),
                        pltpu.VMEM((all_rows, 2 * BLK), _MXU), pltpu.VMEM((all_rows, 2 * BLK), _MXU)],
        compiler_params=_seq(),
    )(x, dy, q, o, g, wq, sinks, kvd, kvd, wo)


def _loss_bwd(x, g, tgt):
    S = x.shape[0]
    T = _tile(S, 512, 8)

    def body(x_ref, g_ref, t_ref, dx_ref, ls_ref, dg_ref):
        @pl.when(pl.program_id(0) == 0)
        def _():
            ls_ref[...] = jnp.zeros_like(ls_ref)
            dg_ref[...] = jnp.zeros_like(dg_ref)

        gv = g_ref[...]
        xn, r = _rms(x_ref[...])
        err = xn * gv - t_ref[...]
        ls_ref[...] += 0.5 * jnp.sum(jnp.mean(err * err, axis=-1, keepdims=True))
        dyv = err * (1.0 / D)
        dg_ref[...] += _colsum(dyv * xn)
        dx_ref[...] = _rms_bwd(dyv, xn, r, gv)

    return pl.pallas_call(
        body, name="loss_bwd", grid=(S // T,),
        in_specs=[pl.BlockSpec((T, D), lambda i: (i, 0)), _full((1, D)), pl.BlockSpec((T, D), lambda i: (i, 0))],
        out_specs=[pl.BlockSpec((T, D), lambda i: (i, 0)), _full((8, LANES)), _full((1, D))],
        out_shape=[jax.ShapeDtypeStruct((S, D), _F32), jax.ShapeDtypeStruct((8, LANES), _F32),
                   jax.ShapeDtypeStruct((1, D), _F32)],
        compiler_params=_seq(),
    )(x, g, tgt)


def _me():
    return 4 * lax.axis_index("x") + 2 * lax.axis_index("y") + lax.axis_index("c")


def _peer(j):
    x, y, c = lax.axis_index("x"), lax.axis_index("y"), lax.axis_index("c")
    px = 1 - x if j & 4 else x
    py = 1 - y if j & 2 else y
    pc = 1 - c if j & 1 else c
    return (px, py, pc), 4 * px + 2 * py + pc


_HBM = pl.BlockSpec(memory_space=pltpu.HBM)
_SEMS = pl.BlockSpec(memory_space=pltpu.SEMAPHORE)
_EFFECT = pltpu.SideEffectType.DATAFLOW_SIDE_EFFECTING


def _in_hbm(a):
    return pltpu.with_memory_space_constraint(a, pltpu.HBM)


def _start_copies(name, groups):
    flat = []
    for srcs, zones, _ in groups:
        flat += [_in_hbm(a) for a in srcs] + [_in_hbm(lax.empty(z.shape, z.dtype)) for z in zones]
    n_in, n_g = len(flat), len(groups)

    def body(*refs):
        sems = refs[n_in:n_in + 2 * n_g]
        me, k = _me(), 0
        for gi, (srcs, zones, plan) in enumerate(groups):
            src_refs, zone_refs = refs[k:k + len(srcs)], refs[k + len(srcs):k + len(srcs) + len(zones)]
            k += len(srcs) + len(zones)
            for t, (si, zi, src_of, dst_of) in enumerate(plan):
                for j in range(1, N_DEV):
                    dev, pk = _peer(j)
                    pltpu.make_async_remote_copy(
                        src_ref=src_of(src_refs[si], pk), dst_ref=dst_of(zone_refs[zi], me),
                        send_sem=sems[2 * gi].at[t * (N_DEV - 1) + j - 1], recv_sem=sems[2 * gi + 1].at[t * (N_DEV - 1) + j - 1],
                        device_id=dev, device_id_type=pl.DeviceIdType.MESH).start()
                pltpu.make_async_copy(src_of(src_refs[si], me), dst_of(zone_refs[zi], me),
                                      sems[2 * gi].at[len(plan) * (N_DEV - 1) + t]).start()
        refs[-1][...] = jnp.zeros_like(refs[-1])

    sem_shapes = []
    for _, _, plan in groups:
        sem_shapes += [pltpu.SemaphoreType.DMA((len(plan) * N_DEV,)), pltpu.SemaphoreType.DMA((len(plan) * (N_DEV - 1),))]
    outs = pl.pallas_call(
        body, name=name,
        out_shape=(*sem_shapes, *[pltpu.HBM(a.shape, a.dtype) for a in flat], jax.ShapeDtypeStruct((8, LANES), _F32)),
        in_specs=[_HBM] * n_in,
        out_specs=(*[_SEMS] * (2 * n_g), *[_HBM] * n_in, pl.BlockSpec(memory_space=pltpu.VMEM)),
        input_output_aliases={k: 2 * n_g + k for k in range(n_in)},
        compiler_params=pltpu.CompilerParams(has_side_effects=_EFFECT),
    )(*flat)
    handles, k = [], 2 * n_g
    for gi, (srcs, zones, plan) in enumerate(groups):
        ns, nz = len(srcs), len(zones)
        handles.append((outs[2 * gi], outs[2 * gi + 1], list(outs[k:k + ns]), list(outs[k + ns:k + ns + nz]), plan))
        k += ns + nz
    return handles, outs[-1]


def _wait_copies(name, handles, after):
    flat = []
    for _, _, srcs, zones, _ in handles:
        flat += srcs + zones
    n_in, n_g = len(flat), len(handles)

    def body(*refs):
        sems = refs[n_in:n_in + 2 * n_g]
        me, k, local, remote = _me(), 0, [], []
        for gi, (_, _, srcs, zones, plan) in enumerate(handles):
            ns, nz = len(srcs), len(zones)
            src_refs, zone_refs = refs[k:k + ns], refs[k + ns:k + ns + nz]
            k += ns + nz
            for t, (si, zi, src_of, dst_of) in enumerate(plan):
                local.append(pltpu.make_async_copy(src_of(src_refs[si], me), dst_of(zone_refs[zi], me),
                                                   sems[2 * gi].at[len(plan) * (N_DEV - 1) + t]))
                for j in range(1, N_DEV):
                    dev, pk = _peer(j)
                    remote.append(pltpu.make_async_remote_copy(
                        src_ref=src_of(src_refs[si], pk), dst_ref=dst_of(zone_refs[zi], pk),
                        send_sem=sems[2 * gi].at[t * (N_DEV - 1) + j - 1], recv_sem=sems[2 * gi + 1].at[t * (N_DEV - 1) + j - 1],
                        device_id=dev, device_id_type=pl.DeviceIdType.MESH))
        for cp in remote:
            cp.wait_send()
            cp.wait_recv()
        for cp in local:
            cp.wait()

    sem_args = []
    for send, recv, _, _, _ in handles:
        sem_args += [send, recv]
    outs = pl.pallas_call(
        body, name=name, out_shape=tuple(pltpu.HBM(a.shape, a.dtype) for a in flat),
        in_specs=[_HBM] * n_in + [_SEMS] * (2 * n_g) + [pl.BlockSpec(memory_space=pl.ANY)],
        out_specs=tuple([_HBM] * n_in), input_output_aliases={k: k for k in range(n_in)},
        compiler_params=pltpu.CompilerParams(has_side_effects=_EFFECT),
    )(*flat, *sem_args, after)
    res, k = [], 0
    for _, _, srcs, zones, _ in handles:
        res.append(list(outs[k + len(srcs):k + len(srcs) + len(zones)]))
        k += len(srcs) + len(zones)
    return res


def _rows(axis, size):
    def of(ref, b):
        start = b * size
        if size % 8 == 0:
            start = pl.multiple_of(start, 8)
        return ref.at[(slice(None),) * axis + (pl.ds(start, size),)]
    return of


def _whole(ref, b):
    return ref


def _slot(ref, b):
    return ref.at[b]


def _gather_group(shards):
    zones, plan = [], []
    for k, (a, axis) in enumerate(shards):
        zones.append(jax.ShapeDtypeStruct(a.shape[:axis] + (N_DEV * a.shape[axis],) + a.shape[axis + 1:], a.dtype))
        plan.append((k, k, _whole, _rows(axis, a.shape[axis])))
    return [a for a, _ in shards], zones, plan


def _scatter_group(grads):
    zones, plan = [], []
    for k, (a, axis) in enumerate(grads):
        size = a.shape[axis] // N_DEV
        zones.append(jax.ShapeDtypeStruct((N_DEV,) + a.shape[:axis] + (size,) + a.shape[axis + 1:], a.dtype))
        plan.append((k, k, _rows(axis, size), _slot))
    return [a for a, _ in grads], zones, plan


def _all_reduce_small(p):
    R = p.shape[0]

    def body(p_ref, o_ref, land, send_sems, recv_sems):
        me = _me()
        land[me] = p_ref[...]
        waits = []
        for j in range(1, N_DEV):
            dev, pk = _peer(j)
            pltpu.make_async_remote_copy(
                src_ref=p_ref, dst_ref=land.at[me], send_sem=send_sems.at[j - 1], recv_sem=recv_sems.at[j - 1],
                device_id=dev, device_id_type=pl.DeviceIdType.MESH).start()
            waits.append(pltpu.make_async_remote_copy(
                src_ref=p_ref, dst_ref=land.at[pk], send_sem=send_sems.at[j - 1], recv_sem=recv_sems.at[j - 1],
                device_id=dev, device_id_type=pl.DeviceIdType.MESH))
        for cp in waits:
            cp.wait()
        tot = land[0]
        for b in range(1, N_DEV):
            tot = tot + land[b]
        o_ref[...] = tot

    vmem = pl.BlockSpec(memory_space=pltpu.VMEM)
    return pl.pallas_call(
        body, name="all_reduce_small", in_specs=[vmem], out_specs=vmem,
        out_shape=jax.ShapeDtypeStruct((R, LANES), _F32),
        scratch_shapes=[pltpu.VMEM((N_DEV, R, LANES), _F32), pltpu.SemaphoreType.DMA((N_DEV - 1,)),
                        pltpu.SemaphoreType.DMA((N_DEV - 1,))],
        compiler_params=_params(),
    )(p)


def _sum_landed(land):
    g = land[0].astype(_F32)
    for b in range(1, N_DEV):
        g = g + land[b].astype(_F32)
    return g


def _landed_specs(n_layers, tr, C, nr):
    def spec(k):
        return pl.BlockSpec((N_DEV, tr, C), lambda l, i: (0, jnp.where(l == k, i, jnp.where(l < k, 0, nr - 1)), 0))
    return [spec(k) for k in range(n_layers)]


def _per_layer(l, zone_refs, fn):
    for k, ref in enumerate(zone_refs):
        @pl.when(l == k)
        def _(ref=ref):
            fn(_sum_landed(ref))


def _sum8(zones):
    L = len(zones)
    _, R, C = zones[0].shape
    tr = _tile(R, 352, 16)
    nr = R // tr

    def body(*refs):
        o_ref = refs[L]

        def put(g):
            o_ref[...] = g

        _per_layer(pl.program_id(0), refs[:L], put)

    return pl.pallas_call(
        body, name="sum8", grid=(L, nr), in_specs=_landed_specs(L, tr, C, nr),
        out_specs=pl.BlockSpec((tr, C), lambda l, i: (l * nr + i, 0)),
        out_shape=jax.ShapeDtypeStruct((L * R, C), _F32), compiler_params=_seq(2),
    )(*zones)


def _adam_update(gv, w_ref, m_ref, v_ref, d_ref, mo_ref, vo_ref):
    mn = ADAM_B1 * m_ref[...] + (1.0 - ADAM_B1) * gv
    vn = ADAM_B2 * v_ref[...] + (1.0 - ADAM_B2) * (gv * gv)
    mo_ref[...] = mn
    vo_ref[...] = vn
    d_ref[...] = -ADAM_LR * ((mn / (1.0 - ADAM_B1 ** ADAM_STEP)) / (jnp.sqrt(vn / (1.0 - ADAM_B2 ** ADAM_STEP)) + ADAM_EPS)
                             + ADAM_WD * w_ref[...])


def _earlier(outs):
    outs = list(outs or [])
    return outs, [pl.BlockSpec(memory_space=pl.ANY)] * len(outs)


def _adamw(g, w, m, v, name, first_row=0, earlier=None):
    Rg, C = g.shape
    R = w.shape[0]
    tr = _tile(Rg if first_row == 0 else min(Rg, first_row), 256, 16)
    off = first_row // tr
    more, more_specs = _earlier(earlier)

    def body(g_ref, w_ref, m_ref, v_ref, *rest):
        go_ref, d_ref, mo_ref, vo_ref = rest[-4:]
        gv = g_ref[...]
        go_ref[...] = gv
        _adam_update(gv, w_ref, m_ref, v_ref, d_ref, mo_ref, vo_ref)

    row = pl.BlockSpec((tr, C), lambda i: (i + off, 0))
    return pl.pallas_call(
        body, name=name, grid=(Rg // tr,), in_specs=[pl.BlockSpec((tr, C), lambda i: (i, 0))] + [row] * 3 + more_specs,
        out_specs=[row] * 4, out_shape=[jax.ShapeDtypeStruct((R, C), _F32)] * 4,
        input_output_aliases={4 + k: k for k in range(len(more))}, compiler_params=_seq(),
    )(g, w, m, v, *more)


def _adamw_landed(zones, w, m, v, name, first_layer=0, earlier=None):
    L = len(zones)
    _, R, C = zones[0].shape
    tr = _tile(R, 176, 16)
    nr = R // tr
    more, more_specs = _earlier(earlier)

    def body(*refs):
        w_ref, m_ref, v_ref = refs[L:L + 3]
        g_ref, d_ref, mo_ref, vo_ref = refs[-4:]

        def update(g):
            g_ref[...] = g
            _adam_update(g, w_ref, m_ref, v_ref, d_ref, mo_ref, vo_ref)

        _per_layer(pl.program_id(0), refs[:L], update)

    row = pl.BlockSpec((tr, C), lambda l, i: ((l + first_layer) * nr + i, 0))
    return pl.pallas_call(
        body, name=name, grid=(L, nr), in_specs=_landed_specs(L, tr, C, nr) + [row] * 3 + more_specs,
        out_specs=[row] * 4, out_shape=[jax.ShapeDtypeStruct(w.shape, _F32)] * 4,
        input_output_aliases={L + 3 + k: k for k in range(len(more))}, compiler_params=_seq(2),
    )(*zones, w, m, v, *more)


def _pack(parts):
    flat = jnp.concatenate([p.reshape(-1).astype(_F32) for p in parts])
    n = flat.shape[0]
    rows = -(-n // (8 * LANES)) * 8
    return jnp.pad(flat, (0, rows * LANES - n)).reshape(rows, LANES)


def _unpack(packed, shapes):
    flat, out, k = packed.reshape(-1), [], 0
    for s in shapes:
        n = 1
        for d in s:
            n *= d
        out.append(flat[k:k + n].reshape(s))
        k += n
    return out


def kernel(x, norm1_g, norm2_g, pool_w, pool_scale, kv_norm_g, w_kv, b_kv, w_q, b_q, sinks, w_o, b_o, ffn_up, ffn_conv_w, ffn_conv_b, ffn_down, final_g, loss_target, m_norm1_g, m_norm2_g, m_pool_w, m_pool_scale, m_kv_norm_g, m_w_kv, m_b_kv, m_w_q, m_b_q, m_sinks, m_w_o, m_b_o, m_ffn_up, m_ffn_conv_w, m_ffn_conv_b, m_ffn_down, m_final_g, v_norm1_g, v_norm2_g, v_pool_w, v_pool_scale, v_kv_norm_g, v_w_kv, v_b_kv, v_w_q, v_b_q, v_sinks, v_w_o, v_b_o, v_ffn_up, v_ffn_conv_w, v_ffn_conv_b, v_ffn_down, v_final_g):
    S = x.shape[1]
    F2s = ffn_up.shape[2]
    F2 = N_DEV * F2s
    me = _me()
    x0 = x.reshape(S, D)
    tgt = loss_target.reshape(S, D)
    row = lambda a: a.reshape(1, -1)

    small = _pack([pool_scale, ffn_conv_w])
    wire = lambda a: a.astype(_MXU)
    ffn_w = lambda l: [(wire(ffn_up[l]).T, 0), (wire(ffn_down[l]), 0)]
    attn_w = lambda j: [(wire(w_q[j]), 0), (wire(w_o[j]), 0)]
    gathers, token = _start_copies("gather_start", [_gather_group(g) for g in (
        [(wire(pool_w[0]), 1), (small[None], 0)], ffn_w(0), [(wire(pool_w[1]), 1)] + ffn_w(1),
        [(wire(w_kv), 0)] + attn_w(0), ffn_w(2), attn_w(1), ffn_w(3))])

    def gathered(k, after):
        return _wait_copies("gather_wait_%d" % k, [gathers[k]], after)[0]

    pw, up_t, down, wq, wo = [None] * N_A, [None] * DEPTH, [None] * DEPTH, [None] * 2, [None] * 2
    pw[0], small_all = gathered(0, token)
    n_ps = pool_scale.size
    small_all = small_all.reshape(N_DEV, -1)
    pscale = jnp.transpose(small_all[:, :n_ps].reshape(N_DEV, N_A, D // N_DEV), (1, 0, 2)).reshape(N_A, D)
    conv_w = jnp.transpose(small_all[:, n_ps:n_ps + ffn_conv_w.size].reshape(N_DEV, DEPTH, CONV_W, F2s),
                           (1, 2, 0, 3)).reshape(DEPTH, CONV_W, F2)

    def dup(a):
        a4 = a.reshape(a.shape[:-1] + (2 * N_KV, 1, HEAD_DIM))
        return jnp.broadcast_to(a4, a.shape[:-1] + (2 * N_KV, 2, HEAD_DIM)).reshape(a.shape[:-1] + (KVD,))

    def fold(a):
        return a.reshape(a.shape[:-1] + (2 * N_KV, 2, HEAD_DIM)).sum(axis=-2).reshape(a.shape[:-1] + (2 * N_KV * HEAD_DIM,))

    xs, us, qs, os_ = [x0], [], [], []
    xc = x0
    kvd = None
    for l in range(DEPTH):
        if l == 1:
            pw[1], up_t[1], down[1] = gathered(2, xc)
        if l == 3:
            wq[1], wo[1] = gathered(5, xc)
        if l < N_A:
            xc = _pool_fwd(xc, row(norm1_g[l]), pw[l], row(pscale[l]))
        else:
            j = l - N_A
            xc, q, o = _attn_fwd(xc, row(norm1_g[l]), wq[j], row(b_q[j]), sinks[j], kvd, wo[j], row(b_o[j]))
            qs.append(q)
            os_.append(o)
        xs.append(xc)
        if l != 1:
            up_t[l], down[l] = gathered((1, None, 4, 6)[l], xc)
        xc, u, c = _ffn_fwd(xc, row(norm2_g[l]), up_t[l], conv_w[l], row(ffn_conv_b[l]), down[l])
        us.append((u, c))
        xs.append(xc)
        if l == N_A - 1:
            wkv, wq[0], wo[0] = gathered(3, xc)
            wkv_d, bkv_d = dup(wkv), dup(row(b_kv))
            kvd = _kv_fwd(xc, row(kv_norm_g), wkv_d, bkv_d)

    dx, loss_p, d_final = _loss_bwd(xc, row(final_g), tgt)
    d_n1, d_n2, d_cw, d_cb = [None] * DEPTH, [None] * DEPTH, [None] * DEPTH, [None] * DEPTH
    d_bq, d_bo, d_sk, d_ps, dkv_parts = [None] * 2, [None] * 2, [None] * 2, [None] * N_A, []
    up_z, down_z, mix_z = [None] * DEPTH, [None] * DEPTH, [None] * DEPTH
    token = None

    def after(gain):
        return gain if token is None else gain + token[0:1, 0:1]

    rep_names = ["norm1_g", "norm2_g", "kv_norm_g", "b_kv", "b_q", "sinks", "b_o", "ffn_conv_b", "final_g"]

    def small_parts(n1_rest, ps_rest):
        zero = jnp.zeros((1, D), _F32)
        return [jnp.concatenate([zero] + n1_rest), jnp.concatenate(d_n2), d_kvg, fold(d_bkv), jnp.concatenate(d_bq),
                jnp.concatenate([s[:, :N_HEADS] for s in d_sk]), jnp.concatenate(d_bo), jnp.concatenate(d_cb), d_final,
                jnp.concatenate([zero] + ps_rest), jnp.stack(d_cw), loss_p[0:1, 0:1]]

    for l in reversed(range(DEPTH)):
        x_in, x_mid, x_out = xs[2 * l], xs[2 * l + 1], xs[2 * l + 2]
        mixer_grads = []
        if l == N_A - 1:
            dx, d_wkv, d_bkv, d_kvg = _kv_bwd(x_out, dx, after(row(kv_norm_g)), wkv_d, *dkv_parts)
            mixer_grads.append((fold(d_wkv).astype(_MXU), 0))
        dy = dx
        dx, du, a, h, d_cw[l], d_cb[l], d_n2[l] = _ffn_bwd(
            x_mid, dy, *us[l], after(row(norm2_g[l])), up_t[l], conv_w[l], down[l])
        g_up = _tn_matmul(du, h, "tn_up")
        if l > 0:
            (up_z[l], down_z[l]), token = _start_copies("scatter_ffn_%d" % l, [
                _scatter_group([(g_up, 0)]), _scatter_group([(_tn_matmul(a, dy, "tn_down"), 0)])])
        else:
            (small_handle, up_z[l]), token = _start_copies("scatter_up_0", [
                _gather_group([(_pack(small_parts(d_n1[1:], d_ps[1:]))[None], 0)]), _scatter_group([(g_up, 0)])])
            (down_z[l],), token = _start_copies("scatter_down_0", [
                _scatter_group([(_tn_matmul(a, dy, "tn_down", token), 0)])])
        dy = dx
        if l < N_A:
            dx, d_pw, d_ps[l], d_n1[l] = _pool_bwd(x_in, dy, after(row(norm1_g[l])), pw[l], row(pscale[l]))
            mixer_grads.append((d_pw.astype(_MXU), 1))
        else:
            j = l - N_A
            dx, dq, h, d_cur, d_prev, d_bq[j], d_bo[j], d_n1[l], d_sk[j] = _attn_bwd(
                x_in, dy, qs[j], os_[j], after(row(norm1_g[l])), wq[j], sinks[j], kvd, wo[j])
            mixer_grads += [(_tn_matmul(h, dq, "tn_q"), 0), (_tn_matmul(os_[j], dy, "tn_o"), 0)]
            dkv_parts += [d_cur, d_prev]
        (mix_z[l],), token = _start_copies("scatter_mixer_%d" % l, [_scatter_group(mixer_grads)])

    moms = dict(norm1_g=(m_norm1_g, v_norm1_g), norm2_g=(m_norm2_g, v_norm2_g), pool_w=(m_pool_w, v_pool_w),
                pool_scale=(m_pool_scale, v_pool_scale), kv_norm_g=(m_kv_norm_g, v_kv_norm_g), w_kv=(m_w_kv, v_w_kv),
                b_kv=(m_b_kv, v_b_kv), w_q=(m_w_q, v_w_q), b_q=(m_b_q, v_b_q), sinks=(m_sinks, v_sinks),
                w_o=(m_w_o, v_w_o), b_o=(m_b_o, v_b_o), ffn_up=(m_ffn_up, v_ffn_up),
                ffn_conv_w=(m_ffn_conv_w, v_ffn_conv_w), ffn_conv_b=(m_ffn_conv_b, v_ffn_conv_b),
                ffn_down=(m_ffn_down, v_ffn_down), final_g=(m_final_g, v_final_g))
    given = dict(norm1_g=norm1_g, norm2_g=norm2_g, kv_norm_g=kv_norm_g, b_kv=b_kv, b_q=b_q, sinks=sinks, b_o=b_o,
                 ffn_conv_b=ffn_conv_b, final_g=final_g, pool_scale=pool_scale, ffn_conv_w=ffn_conv_w,
                 pool_w=pool_w, w_kv=w_kv, w_q=w_q, w_o=w_o, ffn_up=ffn_up, ffn_down=ffn_down)
    grad, delta, new_m, new_v = {}, {}, {}, {}

    def update(name, zones, cols, first_layer=0, earlier=None, last=True):
        w = given[name]
        two_d = lambda a: a.reshape(-1, cols)
        wmv = (two_d(w), two_d(moms[name][0]), two_d(moms[name][1]))
        if name == "ffn_up":
            g = jnp.swapaxes(_sum8(zones).reshape(len(zones), F2s, D), 1, 2).reshape(-1, F2s)
            outs = _adamw(g, *wmv, "adamw_" + name, first_layer * D, earlier)
        else:
            landed = [z.reshape(N_DEV, -1, cols) for z in zones]
            outs = _adamw_landed(landed, *wmv, "adamw_" + name, first_layer, earlier)
        if last:
            grad[name], delta[name], new_m[name], new_v[name] = (o.reshape(w.shape) for o in outs)
        return outs

    rest = _wait_copies("scatter_wait_rest", up_z[1:] + down_z[1:] + mix_z[1:], token)
    up_r, down_r, mix_r = rest[:DEPTH - 1], rest[DEPTH - 1:2 * (DEPTH - 1)], rest[2 * (DEPTH - 1):]
    up_1 = update("ffn_up", [z[0] for z in up_r], F2s, 1, last=False)
    down_1 = update("ffn_down", [z[0] for z in down_r], D, 1, last=False)
    update("w_q", [mix_r[N_A - 1][0], mix_r[N_A][0]], D)
    update("w_o", [mix_r[N_A - 1][1], mix_r[N_A][1]], D)
    update("w_kv", [mix_r[N_A - 2][0]], w_kv.shape[1])
    pw_1 = update("pool_w", [mix_r[N_A - 2][1]], GC, 1, last=False)

    late_in, _ = lax.optimization_barrier((_pack([d_n1[0], d_ps[0]]), down_1[1]))
    late = _all_reduce_small(late_in).reshape(-1)
    (early,), = _wait_copies("small_wait", [small_handle], late)
    tot = _unpack(_sum8([early]), [given[k].shape for k in rep_names] + [(N_A, D), (DEPTH, CONV_W, F2), ()])
    tot[0] = tot[0].at[0].add(late[:D])
    tot[-3] = tot[-3].at[0].add(late[D:2 * D])
    grad.update(zip(rep_names, tot))
    loss = tot[-1]
    grad["pool_scale"] = lax.dynamic_slice_in_dim(tot[-3], me * (D // N_DEV), D // N_DEV, axis=1)
    grad["ffn_conv_w"] = lax.dynamic_slice_in_dim(tot[-2], me * F2s, F2s, axis=2)
    small_names = rep_names + ["pool_scale", "ffn_conv_w"]
    shapes = [given[k].shape for k in small_names]
    outs = _adamw(_pack([grad[k] for k in small_names]), _pack([given[k] for k in small_names]),
                  _pack([moms[k][0] for k in small_names]), _pack([moms[k][1] for k in small_names]), "adamw_small")
    for dst, packed in zip((delta, new_m, new_v), outs[1:]):
        dst.update(zip(small_names, _unpack(packed, shapes)))

    (up_0,), (down_0,), (pw_0,) = _wait_copies("scatter_wait_0", [up_z[0], down_z[0], mix_z[0]], outs[1])
    update("ffn_up", [up_0], F2s, 0, up_1)
    update("ffn_down", [down_0], D, 0, down_1)
    update("pool_w", [pw_0], GC, 0, pw_1)

    names = ["norm1_g", "norm2_g", "pool_w", "pool_scale", "kv_norm_g", "w_kv", "b_kv", "w_q", "b_q", "sinks", "w_o",
             "b_o", "ffn_up", "ffn_conv_w", "ffn_conv_b", "ffn_down", "final_g"]
    return (loss, dx.reshape(x.shape), *[grad[k] for k in names], *[delta[k] for k in names],
            *[new_m[k] for k in names], *[new_v[k] for k in names])
```

```python
import functools

import jax
import jax.numpy as jnp
from jax import lax
from jax.experimental import pallas as pl
from jax.experimental.pallas import tpu as pltpu

_F32 = jnp.float32
_MXU = jnp.bfloat16

N_DEV = 8
D = 1024
DEPTH = 4
N_A = 2
POOL_WINDOWS = (2, 4, 8, 16)
GC = D // len(POOL_WINDOWS)
HALO = 16
HEAD_DIM = 64
N_HEADS = D // HEAD_DIM
GROUP = 8
N_KV = N_HEADS // GROUP
BLK = 128
PAIR = 2 * HEAD_DIM
KVD = 4 * N_KV * HEAD_DIM
CONV_W = 3
EPS = 1e-5
NEG = -1e30

ADAM_LR = 0.001
ADAM_B1 = 0.9
ADAM_B2 = 0.999
ADAM_EPS = 1e-08
ADAM_WD = 0.01
ADAM_STEP = 10

V7X_VMEM_LIMIT = 56 * 1024 * 1024
LANES = 128

_NT = (((1,), (1,)), ((), ()))
_TN = (((0,), (0,)), ((), ()))


def _params(**kw):
    return pltpu.CompilerParams(vmem_limit_bytes=V7X_VMEM_LIMIT, **kw)


def _seq(n=1):
    return _params(dimension_semantics=("arbitrary",) * n)


def _dot(a, b, dims=None):
    if dims is None:
        return jnp.dot(a, b, preferred_element_type=_F32)
    return lax.dot_general(a, b, dims, preferred_element_type=_F32)


def _rms(x):
    r = lax.rsqrt(jnp.mean(x * x, axis=-1, keepdims=True) + EPS)
    return x * r, r


def _rms_bwd(dh, xn, r, g):
    dxn = dh * g
    return r * (dxn - xn * jnp.mean(dxn * xn, axis=-1, keepdims=True))


def _colsum(a):
    return jnp.sum(a, axis=0, keepdims=True)


def _tile(n, want, mult=8):
    for t in range(min(want, n), 0, -1):
        if n % t == 0 and t % mult == 0:
            return t
    return n


def _full(shape):
    zeros = (0,) * len(shape)
    return pl.BlockSpec(shape, lambda *_: zeros)


def _window_sum(ext, win, trailing):
    R = ext.shape[0]
    acc, k = ext, 1
    while k < win:
        acc = acc + pltpu.roll(acc, k if trailing else R - k, axis=0)
        k *= 2
    return acc


def _pool_windows(hbuf, h, row, T):
    out = []
    for gi, win in enumerate(POOL_WINDOWS):
        cs = slice(gi * GC, (gi + 1) * GC)
        acc = _window_sum(hbuf[:, cs], win, True)[HALO:, :]
        cnt = jnp.minimum(row + 1, win).astype(_F32)
        out.append((acc / cnt - h[:, cs], cnt))
    return out


def _pool_fwd(x, g, w, sc):
    S = x.shape[0]
    T = _tile(S, 512, HALO)
    n, hb = S // T, T // HALO

    def body(x_ref, xh_ref, g_ref, w_ref, sc_ref, o_ref, hbuf):
        i = pl.program_id(0)
        gv = g_ref[...]
        xv = x_ref[...]
        h = _rms(xv)[0] * gv
        hbuf[0:HALO, :] = jnp.where(i > 0, _rms(xh_ref[...])[0] * gv, 0.0)
        hbuf[HALO:, :] = h
        row = i * T + lax.broadcasted_iota(jnp.int32, (T, 1), 0)
        for gi, (p, _) in enumerate(_pool_windows(hbuf, h, row, T)):
            cs = slice(gi * GC, (gi + 1) * GC)
            z = _dot(p.astype(_MXU), w_ref[gi])
            o_ref[:, cs] = xv[:, cs] + z * sc_ref[:, cs]

    return pl.pallas_call(
        body, name="pool_fwd", grid=(n,),
        in_specs=[pl.BlockSpec((T, D), lambda i: (i, 0)),
                  pl.BlockSpec((HALO, D), lambda i: (jnp.maximum(i * hb - 1, 0), 0)),
                  _full((1, D)), _full((4, GC, GC)), _full((1, D))],
        out_specs=pl.BlockSpec((T, D), lambda i: (i, 0)),
        out_shape=jax.ShapeDtypeStruct((S, D), _F32),
        scratch_shapes=[pltpu.VMEM((T + HALO, D), _F32)],
        compiler_params=_seq(),
    )(x, x, g, w, sc)


def _pool_bwd(x, dy, g, w, sc):
    S = x.shape[0]
    T = _tile(S, 512, HALO)
    n, hb = S // T, T // HALO

    def body(x_ref, xh_ref, dy_ref, dyh_ref, g_ref, w_ref, sc_ref, dx_ref, dw_ref, dsc_ref, dg_ref,
             hbuf, qbuf, dhbuf):
        i = pl.program_id(0)

        @pl.when(i == 0)
        def _():
            dw_ref[...] = jnp.zeros_like(dw_ref)
            dsc_ref[...] = jnp.zeros_like(dsc_ref)
            dg_ref[...] = jnp.zeros_like(dg_ref)

        gv = g_ref[...]
        xv = x_ref[...]
        xn, r = _rms(xv)
        h = xn * gv
        hbuf[0:HALO, :] = jnp.where(i > 0, _rms(xh_ref[...])[0] * gv, 0.0)
        hbuf[HALO:, :] = h
        dyv = dy_ref[...]
        dz = dyv * sc_ref[...]
        dzh = jnp.where(i < n - 1, dyh_ref[...], 0.0) * sc_ref[...]
        row = i * T + lax.broadcasted_iota(jnp.int32, (T, 1), 0)
        rowh = (i + 1) * T + lax.broadcasted_iota(jnp.int32, (HALO, 1), 0)
        for gi, (p, cnt) in enumerate(_pool_windows(hbuf, h, row, T)):
            win = POOL_WINDOWS[gi]
            cs = slice(gi * GC, (gi + 1) * GC)
            pb = p.astype(_MXU)
            wg = w_ref[gi]
            dsc_ref[:, cs] += _colsum(dyv[:, cs] * _dot(pb, wg))
            dzb = dz[:, cs].astype(_MXU)
            dw_ref[gi] += _dot(pb, dzb, _TN)
            dp = _dot(dzb, wg, _NT)
            dph = _dot(dzh[:, cs].astype(_MXU), wg, _NT)
            qbuf[0:T, cs] = dp / cnt
            qbuf[T:T + HALO, cs] = dph / jnp.minimum(rowh + 1, win).astype(_F32)
            dhbuf[:, cs] = _window_sum(qbuf[:, cs], win, False)[0:T, :] - dp
        dh = dhbuf[...]
        dg_ref[...] += _colsum(dh * xn)
        dx_ref[...] = dyv + _rms_bwd(dh, xn, r, gv)

    return pl.pallas_call(
        body, name="pool_bwd", grid=(n,),
        in_specs=[pl.BlockSpec((T, D), lambda i: (i, 0)),
                  pl.BlockSpec((HALO, D), lambda i: (jnp.maximum(i * hb - 1, 0), 0)),
                  pl.BlockSpec((T, D), lambda i: (i, 0)),
                  pl.BlockSpec((HALO, D), lambda i: (jnp.minimum((i + 1) * hb, S // HALO - 1), 0)),
                  _full((1, D)), _full((4, GC, GC)), _full((1, D))],
        out_specs=[pl.BlockSpec((T, D), lambda i: (i, 0)), _full((4, GC, GC)), _full((1, D)), _full((1, D))],
        out_shape=[jax.ShapeDtypeStruct((S, D), _F32), jax.ShapeDtypeStruct((4, GC, GC), _F32),
                   jax.ShapeDtypeStruct((1, D), _F32), jax.ShapeDtypeStruct((1, D), _F32)],
        scratch_shapes=[pltpu.VMEM((T + HALO, D), _F32), pltpu.VMEM((T + HALO, D), _F32), pltpu.VMEM((T, D), _F32)],
        compiler_params=_seq(),
    )(x, x, dy, dy, g, w, sc)


FFN_FWD_TILE, FFN_FWD_CHUNKS = 256, 1
FFN_BWD_TILE, FFN_BWD_CHUNKS = 128, 1
EDGE = 8


def _shift_down(v, k, prev):
    r = pltpu.roll(v, k, axis=0)
    i8 = lax.broadcasted_iota(jnp.int32, (EDGE, v.shape[1]), 0)
    head = jnp.where(i8 >= k, r[0:EDGE, :], pltpu.roll(prev, k, axis=0))
    return jnp.concatenate([head, r[EDGE:, :]], axis=0)


def _shift_up(v, k, nxt):
    T = v.shape[0]
    r = pltpu.roll(v, T - k, axis=0)
    i8 = lax.broadcasted_iota(jnp.int32, (EDGE, v.shape[1]), 0)
    tail = jnp.where(i8 < EDGE - k, r[T - EDGE:, :], pltpu.roll(nxt, EDGE - k, axis=0))
    return jnp.concatenate([r[:T - EDGE, :], tail], axis=0)


def _load_weights(i, pairs, sems):
    @pl.when(i == 0)
    def _():
        cps = [pltpu.make_async_copy(src, dst, sems.at[k]) for k, (src, dst) in enumerate(pairs)]
        for cp in cps:
            cp.start()
        for cp in cps:
            cp.wait()


def _ffn_fwd(x, g, wup_t, cw, cb, wdn):
    S = x.shape[0]
    F2 = wup_t.shape[0]
    F = F2 // 2
    C = F // FFN_FWD_CHUNKS
    T = _tile(S, FFN_FWD_TILE, 16)
    n = S // T

    def body(x_ref, g_ref, wup_hbm, cw_ref, cb_ref, wdn_hbm, o_ref, u_ref, c_ref, wup, wdnv, carry, sems):
        i = pl.program_id(0)
        _load_weights(i, [(wup_hbm, wup), (wdn_hbm, wdnv)], sems)

        @pl.when(i == 0)
        def _():
            carry[...] = jnp.zeros_like(carry)

        xv = x_ref[...]
        hb = (_rms(xv)[0] * g_ref[...]).astype(_MXU)
        acc = jnp.zeros((T, D), _F32)
        for j in range(FFN_FWD_CHUNKS):
            halves = []
            for cs in (slice(j * C, (j + 1) * C), slice(F + j * C, F + (j + 1) * C)):
                u = _dot(hb, wup[cs, :], _NT)
                u_ref[:, cs] = u.astype(u_ref.dtype)
                prev = carry[:, cs]
                carry[:, cs] = u[T - EDGE:, :]
                c = (cw_ref[0:1, cs] * _shift_down(u, 2, prev) + cw_ref[1:2, cs] * _shift_down(u, 1, prev)
                     + cw_ref[2:3, cs] * u + cb_ref[:, cs])
                c_ref[:, cs] = c.astype(c_ref.dtype)
                halves.append(c)
            cg, cv = halves
            a = (cg * jax.nn.sigmoid(cg)) * cv
            acc = acc + _dot(a.astype(_MXU), wdnv[j * C:(j + 1) * C, :])
        o_ref[...] = xv + acc

    any_ = pl.BlockSpec(memory_space=pl.ANY)
    wide = pl.BlockSpec((T, F2), lambda i: (i, 0))
    return pl.pallas_call(
        body, name="ffn_fwd", grid=(n,),
        in_specs=[pl.BlockSpec((T, D), lambda i: (i, 0)), _full((1, D)), any_, _full((CONV_W, F2)), _full((1, F2)), any_],
        out_specs=[pl.BlockSpec((T, D), lambda i: (i, 0)), wide, wide],
        out_shape=[jax.ShapeDtypeStruct((S, D), _F32), jax.ShapeDtypeStruct((S, F2), _MXU),
                   jax.ShapeDtypeStruct((S, F2), _MXU)],
        scratch_shapes=[pltpu.VMEM((F2, D), _MXU), pltpu.VMEM((F, D), _MXU),
                        pltpu.VMEM((EDGE, F2), _F32), pltpu.SemaphoreType.DMA((2,))],
        compiler_params=_seq(),
    )(x, g, wup_t, cw, cb, wdn)


def _ffn_bwd(x, dy, u, c, g, wup_t, cw, wdn):
    S = x.shape[0]
    F2 = wup_t.shape[0]
    F = F2 // 2
    C = F // FFN_BWD_CHUNKS
    T = _tile(S, FFN_BWD_TILE, 16)
    n = S // T

    def body(x_ref, dy_ref, u_ref, c_ref, g_ref, wup_hbm, cw_ref, wdn_hbm,
             dx_ref, du_ref, a_ref, h_ref, dcw_ref, dcb_ref, dg_ref, wup, wdnv, carry, sems):
        i = pl.program_id(0)
        _load_weights(i, [(wup_hbm, wup), (wdn_hbm, wdnv)], sems)

        @pl.when(i == 0)
        def _():
            carry[...] = jnp.zeros_like(carry)
            dcw_ref[...] = jnp.zeros_like(dcw_ref)
            dcb_ref[...] = jnp.zeros_like(dcb_ref)
            dg_ref[...] = jnp.zeros_like(dg_ref)

        gv = g_ref[...]
        xv = x_ref[...]
        xn, r = _rms(xv)
        hbf = (xn * gv).astype(_MXU)
        h_ref[...] = hbf
        dyv = dy_ref[...]
        dyb = dyv.astype(_MXU)
        dh = jnp.zeros((T, D), _F32)
        for j in range(FFN_BWD_CHUNKS):
            gs, vs = slice(j * C, (j + 1) * C), slice(F + j * C, F + (j + 1) * C)
            cg, cv = c_ref[:, gs].astype(_F32), c_ref[:, vs].astype(_F32)
            sg = jax.nn.sigmoid(cg)
            sl = cg * sg
            a_ref[:, gs] = (sl * cv).astype(a_ref.dtype)
            da = _dot(dyb, wdnv[gs, :], _NT)
            for cs, dc in ((gs, da * cv * (sg * (1.0 + cg * (1.0 - sg)))), (vs, da * sl)):
                nxt = carry[:, cs]
                carry[:, cs] = dc[0:EDGE, :]
                dc1, dc2 = _shift_up(dc, 1, nxt), _shift_up(dc, 2, nxt)
                uf = u_ref[:, cs].astype(_F32)
                dcb_ref[:, cs] += _colsum(dc)
                for k, d in enumerate((dc2, dc1, dc)):
                    dcw_ref[k:k + 1, cs] += _colsum(d * uf)
                du = cw_ref[2:3, cs] * dc + cw_ref[1:2, cs] * dc1 + cw_ref[0:1, cs] * dc2
                dub = du.astype(_MXU)
                du_ref[:, cs] = dub
                dh = dh + _dot(dub, wup[cs, :])
        dg_ref[...] += _colsum(dh * xn)
        dx_ref[...] = dyv + _rms_bwd(dh, xn, r, gv)

    any_ = pl.BlockSpec(memory_space=pl.ANY)
    rev = lambda i: (n - 1 - i, 0)
    return pl.pallas_call(
        body, name="ffn_bwd", grid=(n,),
        in_specs=[pl.BlockSpec((T, D), rev), pl.BlockSpec((T, D), rev), pl.BlockSpec((T, F2), rev),
                  pl.BlockSpec((T, F2), rev), _full((1, D)), any_, _full((CONV_W, F2)), any_],
        out_specs=[pl.BlockSpec((T, D), rev), pl.BlockSpec((T, F2), rev), pl.BlockSpec((T, F), rev),
                   pl.BlockSpec((T, D), rev), _full((CONV_W, F2)), _full((1, F2)), _full((1, D))],
        out_shape=[jax.ShapeDtypeStruct((S, D), _F32), jax.ShapeDtypeStruct((S, F2), _MXU),
                   jax.ShapeDtypeStruct((S, F), _MXU), jax.ShapeDtypeStruct((S, D), _MXU),
                   jax.ShapeDtypeStruct((CONV_W, F2), _F32), jax.ShapeDtypeStruct((1, F2), _F32),
                   jax.ShapeDtypeStruct((1, D), _F32)],
        scratch_shapes=[pltpu.VMEM((F2, D), _MXU), pltpu.VMEM((F, D), _MXU),
                        pltpu.VMEM((EDGE, F2), _F32), pltpu.SemaphoreType.DMA((2,))],
        compiler_params=_seq(),
    )(x, dy, u, c, g, wup_t, cw, wdn)


def _tn_matmul(a, b, name, token=None):
    S, M = a.shape
    N = b.shape[1]
    bm = _tile(M, 1408, LANES)
    tk = _tile(S, 2048, 16)
    nk = S // tk
    tokens = [] if token is None else [token]

    def body(a_ref, b_ref, *rest):
        o_ref, acc = rest[-2:]
        k = pl.program_id(1)

        @pl.when(k == 0)
        def _():
            acc[...] = jnp.zeros_like(acc)

        acc[...] += _dot(a_ref[...].astype(_MXU), b_ref[...].astype(_MXU), _TN)

        @pl.when(k == nk - 1)
        def _():
            o_ref[...] = acc[...].astype(o_ref.dtype)

    return pl.pallas_call(
        body, name=name, grid=(M // bm, nk),
        in_specs=[pl.BlockSpec((tk, bm), lambda i, k: (k, i)), pl.BlockSpec((tk, N), lambda i, k: (k, 0))]
        + [_full((8, LANES))] * len(tokens),
        out_specs=pl.BlockSpec((bm, N), lambda i, k: (i, 0)),
        out_shape=jax.ShapeDtypeStruct((M, N), _MXU),
        scratch_shapes=[pltpu.VMEM((bm, N), _F32)],
        compiler_params=_seq(2),
    )(a, b, *tokens)


def _kv_fwd(x, g, wkv, bkv):
    S = x.shape[0]
    T = _tile(S, 512, 16)

    def body(x_ref, g_ref, w_ref, b_ref, o_ref):
        hb = (_rms(x_ref[...])[0] * g_ref[...]).astype(_MXU)
        o_ref[...] = (_dot(hb, w_ref[...]) + b_ref[...]).astype(o_ref.dtype)

    return pl.pallas_call(
        body, name="kv_fwd", grid=(S // T,),
        in_specs=[pl.BlockSpec((T, D), lambda i: (i, 0)), _full((1, D)), _full((D, KVD)), _full((1, KVD))],
        out_specs=pl.BlockSpec((T, KVD), lambda i: (i, 0)),
        out_shape=jax.ShapeDtypeStruct((S, KVD), _MXU),
        compiler_params=_seq(),
    )(x, g, wkv, bkv)


def _kv_bwd(x, dx_in, g, wkv, cur_a, prev_a, cur_b, prev_b):
    S = x.shape[0]
    T = _tile(S, 512, BLK)
    n, per = S // T, T // BLK

    def body(x_ref, dxi_ref, g_ref, w_ref, ca, pa, na, cb, pb, nb, dx_ref, dw_ref, db_ref, dg_ref):
        i = pl.program_id(0)

        @pl.when(i == 0)
        def _():
            dw_ref[...] = jnp.zeros_like(dw_ref)
            db_ref[...] = jnp.zeros_like(db_ref)
            dg_ref[...] = jnp.zeros_like(dg_ref)

        gv = g_ref[...]
        xn, r = _rms(x_ref[...])
        nxt = jnp.where(i < n - 1, na[...] + nb[...], 0.0)
        prev = jnp.concatenate([pa[BLK:, :] + pb[BLK:, :], nxt], axis=0) if per > 1 else nxt
        dkv = ca[...] + cb[...] + prev
        db_ref[...] += _colsum(dkv)
        dkb = dkv.astype(_MXU)
        dw_ref[...] += _dot((xn * gv).astype(_MXU), dkb, _TN)
        dh = _dot(dkb, w_ref[...], _NT)
        dg_ref[...] += _colsum(dh * xn)
        dx_ref[...] = dxi_ref[...] + _rms_bwd(dh, xn, r, gv)

    blk = lambda w: pl.BlockSpec((T, w), lambda i: (i, 0))
    nxt = pl.BlockSpec((BLK, KVD), lambda i: (jnp.minimum((i + 1) * per, S // BLK - 1), 0))
    return pl.pallas_call(
        body, name="kv_bwd", grid=(n,),
        in_specs=[blk(D), blk(D), _full((1, D)), _full((D, KVD)), blk(KVD), blk(KVD), nxt, blk(KVD), blk(KVD), nxt],
        out_specs=[blk(D), _full((D, KVD)), _full((1, KVD)), _full((1, D))],
        out_shape=[jax.ShapeDtypeStruct((S, D), _F32), jax.ShapeDtypeStruct((D, KVD), _F32),
                   jax.ShapeDtypeStruct((1, KVD), _F32), jax.ShapeDtypeStruct((1, D), _F32)],
        compiler_params=_seq(),
    )(x, dx_in, g, wkv, cur_a, prev_a, prev_a, cur_b, prev_b, prev_b)


STACK = GROUP * BLK


def _attn_mask(i, rows):
    qi = lax.broadcasted_iota(jnp.int32, (rows, 2 * BLK), 0) & (BLK - 1)
    si = lax.broadcasted_iota(jnp.int32, (rows, 2 * BLK), 1)
    return (si > qi) & (si <= qi + BLK) & jnp.logical_or(i > 0, si >= BLK)


def _low_half():
    return lax.broadcasted_iota(jnp.int32, (BLK, PAIR), 1) < HEAD_DIM


def _stack_heads(ref, kh, dst):
    low = _low_half()
    for pp in range(GROUP // 2):
        pr = kh * (GROUP // 2) + pp
        v2 = ref[:, pr * PAIR:(pr + 1) * PAIR]
        zero = jnp.zeros_like(v2)
        dst[2 * pp * BLK:(2 * pp + 1) * BLK, :] = jnp.where(low, v2, zero)
        dst[(2 * pp + 1) * BLK:(2 * pp + 2) * BLK, :] = jnp.where(low, zero, v2)


def _unstack_heads(st, pp):
    return jnp.where(_low_half(), st[2 * pp * BLK:(2 * pp + 1) * BLK, :], st[(2 * pp + 1) * BLK:(2 * pp + 2) * BLK, :])


def _sink_col(sk_ref, kh):
    return jnp.concatenate([jnp.full((BLK, 1), sk_ref[kh * GROUP + h], _F32) for h in range(GROUP)], axis=0)


def _head_probs(qm, kd, mask, sink):
    s = jnp.where(mask, _dot(qm, kd, _NT) * (HEAD_DIM ** -0.5), NEG)
    m = jnp.maximum(jnp.max(s, axis=-1, keepdims=True), sink)
    p = jnp.exp(s - m)
    es = jnp.exp(sink - m)
    inv = 1.0 / (jnp.sum(p, axis=-1, keepdims=True) + es)
    return p * inv, es * inv


def _attn_fwd(x, g, wq, bq, sinks, kvd, wo, bo):
    S = x.shape[0]
    n = S // BLK

    def body(x_ref, g_ref, wq_ref, bq_ref, sk_ref, kp_ref, kc_ref, wo_ref, bo_ref, xo_ref, q_ref, o_ref, win):
        i = pl.program_id(0)
        xv = x_ref[...]
        hb = (_rms(xv)[0] * g_ref[...]).astype(_MXU)
        q_ref[...] = (_dot(hb, wq_ref[...]) + bq_ref[...]).astype(q_ref.dtype)
        win[0:BLK, :] = kp_ref[...]
        win[BLK:, :] = kc_ref[...]
        mask = _attn_mask(i, BLK)
        low = _low_half()
        for pr in range(N_HEADS // 2):
            kh = (2 * pr) // GROUP
            kd = win[:, kh * PAIR:(kh + 1) * PAIR]
            vd = win[:, (N_KV + kh) * PAIR:(N_KV + kh + 1) * PAIR]
            q2 = q_ref[:, pr * PAIR:(pr + 1) * PAIR]
            outs = []
            for half in range(2):
                qm = jnp.where(low if half == 0 else ~low, q2, jnp.zeros_like(q2))
                pbs, _ = _head_probs(qm, kd, mask, sk_ref[2 * pr + half])
                outs.append(_dot(pbs.astype(_MXU), vd))
            o_ref[:, pr * PAIR:(pr + 1) * PAIR] = jnp.where(low, outs[0], outs[1]).astype(o_ref.dtype)
        xo_ref[...] = xv + _dot(o_ref[...], wo_ref[...]) + bo_ref[...]

    blk = lambda w: pl.BlockSpec((BLK, w), lambda i: (i, 0))
    return pl.pallas_call(
        body, name="attn_fwd", grid=(n,),
        in_specs=[blk(D), _full((1, D)), _full((D, D)), _full((1, D)),
                  pl.BlockSpec(memory_space=pltpu.SMEM),
                  pl.BlockSpec((BLK, KVD), lambda i: (jnp.maximum(i - 1, 0), 0)), blk(KVD),
                  _full((D, D)), _full((1, D))],
        out_specs=[blk(D), blk(D), blk(D)],
        out_shape=[jax.ShapeDtypeStruct((S, D), _F32), jax.ShapeDtypeStruct((S, D), _MXU),
                   jax.ShapeDtypeStruct((S, D), _MXU)],
        scratch_shapes=[pltpu.VMEM((2 * BLK, KVD), _MXU)],
        compiler_params=_seq(),
    )(x, g, wq, bq, sinks, kvd, kvd, wo, bo)


def _attn_bwd(x, dy, q, o, g, wq, sinks, kvd, wo):
    S = x.shape[0]
    n = S // BLK
    all_rows = N_HEADS * BLK

    def body(x_ref, dy_ref, q_ref, o_ref, g_ref, wq_ref, sk_ref, kp_ref, kc_ref, wo_ref,
             dx_ref, dq_ref, h_ref, dc_ref, dp_ref, dbq_ref, dbo_ref, dg_ref, dsk_ref, win, dob, qs, dos, pall, dsall):
        i = pl.program_id(0)

        @pl.when(i == 0)
        def _():
            dbq_ref[...] = jnp.zeros_like(dbq_ref)
            dbo_ref[...] = jnp.zeros_like(dbo_ref)
            dg_ref[...] = jnp.zeros_like(dg_ref)
            dsk_ref[...] = jnp.zeros_like(dsk_ref)

        gv = g_ref[...]
        xv = x_ref[...]
        xn, r = _rms(xv)
        h_ref[...] = (xn * gv).astype(h_ref.dtype)
        dyv = dy_ref[...]
        dbo_ref[...] += _colsum(dyv)
        dob[...] = _dot(dyv.astype(_MXU), wo_ref[...], _NT).astype(dob.dtype)
        win[0:BLK, :] = kp_ref[...]
        win[BLK:, :] = kc_ref[...]
        mask = _attn_mask(i, BLK)
        low = _low_half()
        lane = lax.broadcasted_iota(jnp.int32, (1, LANES), 1)
        for pr in range(N_HEADS // 2):
            kh = (2 * pr) // GROUP
            kd = win[:, kh * PAIR:(kh + 1) * PAIR]
            vd = win[:, (N_KV + kh) * PAIR:(N_KV + kh + 1) * PAIR]
            q2 = q_ref[:, pr * PAIR:(pr + 1) * PAIR]
            do2 = dob[:, pr * PAIR:(pr + 1) * PAIR]
            od = do2.astype(_F32) * o_ref[:, pr * PAIR:(pr + 1) * PAIR].astype(_F32)
            for half in range(2):
                hd = 2 * pr + half
                rows = slice(hd * BLK, (hd + 1) * BLK)
                sel = low if half == 0 else ~low
                qm = jnp.where(sel, q2, jnp.zeros_like(q2))
                dom = jnp.where(sel, do2, jnp.zeros_like(do2))
                qs[rows, :] = qm
                dos[rows, :] = dom
                pbs, ps = _head_probs(qm, kd, mask, sk_ref[hd])
                pall[rows, :] = pbs.astype(_MXU)
                delta = jnp.sum(jnp.where(sel, od, 0.0), axis=-1, keepdims=True)
                dsall[rows, :] = (pbs * (_dot(dom, vd, _NT) - delta) * (HEAD_DIM ** -0.5)).astype(_MXU)
                dsk_ref[...] -= jnp.where(lane == hd, _colsum(ps * delta), 0.0)
        dq_all = []
        for kh in range(N_KV):
            ks = slice(kh * PAIR, (kh + 1) * PAIR)
            vs = slice((N_KV + kh) * PAIR, (N_KV + kh + 1) * PAIR)
            rows = slice(kh * STACK, (kh + 1) * STACK)
            dqst = _dot(dsall[rows, :], win[:, ks])
            dk = _dot(dsall[rows, :], qs[rows, :], _TN)
            dv = _dot(pall[rows, :], dos[rows, :], _TN)
            dp_ref[:, ks], dc_ref[:, ks] = dk[0:BLK, :], dk[BLK:, :]
            dp_ref[:, vs], dc_ref[:, vs] = dv[0:BLK, :], dv[BLK:, :]
            dq_all += [_unstack_heads(dqst, pp) for pp in range(GROUP // 2)]
        dq = jnp.concatenate(dq_all, axis=1)
        dbq_ref[...] += _colsum(dq)
        dqb = dq.astype(_MXU)
        dq_ref[...] = dqb
        dh = _dot(dqb, wq_ref[...], _NT)
        dg_ref[...] += _colsum(dh * xn)
        dx_ref[...] = dyv + _rms_bwd(dh, xn, r, gv)

    blk = lambda w: pl.BlockSpec((BLK, w), lambda i: (i, 0))
    return pl.pallas_call(
        body, name="attn_bwd", grid=(n,),
        in_specs=[blk(D), blk(D), blk(D), blk(D), _full((1, D)), _full((D, D)),
                  pl.BlockSpec(memory_space=pltpu.SMEM),
                  pl.BlockSpec((BLK, KVD), lambda i: (jnp.maximum(i - 1, 0), 0)), blk(KVD), _full((D, D))],
        out_specs=[blk(D), blk(D), blk(D), blk(KVD), blk(KVD),
                   _full((1, D)), _full((1, D)), _full((1, D)), _full((1, LANES))],
        out_shape=[jax.ShapeDtypeStruct((S, D), _F32), jax.ShapeDtypeStruct((S, D), _MXU),
                   jax.ShapeDtypeStruct((S, D), _MXU), jax.ShapeDtypeStruct((S, KVD), _F32),
                   jax.ShapeDtypeStruct((S, KVD), _F32), jax.ShapeDtypeStruct((1, D), _F32),
                   jax.ShapeDtypeStruct((1, D), _F32), jax.ShapeDtypeStruct((1, D), _F32),
                   jax.ShapeDtypeStruct((1, LANES), _F32)],
        scratch_shapes=[pltpu.VMEM((2 * BLK, KVD), _MXU), pltpu.VMEM((BLK, D), _MXU),
                        pltpu.VMEM((all_rows, PAIR), _MXU), pltpu.VMEM((all_rows, PAIR), _MXU),
                        pltpu.VMEM((all_rows, 2 * BLK), _MXU), pltpu.VMEM((all_rows, 2 * BLK), _MXU)],
        compiler_params=_seq(),
    )(x, dy, q, o, g, wq, sinks, kvd, kvd, wo)


def _loss_bwd(x, g, tgt):
    S = x.shape[0]
    T = _tile(S, 512, 8)

    def body(x_ref, g_ref, t_ref, dx_ref, ls_ref, dg_ref):
        @pl.when(pl.program_id(0) == 0)
        def _():
            ls_ref[...] = jnp.zeros_like(ls_ref)
            dg_ref[...] = jnp.zeros_like(dg_ref)

        gv = g_ref[...]
        xn, r = _rms(x_ref[...])
        err = xn * gv - t_ref[...]
        ls_ref[...] += 0.5 * jnp.sum(jnp.mean(err * err, axis=-1, keepdims=True))
        dyv = err * (1.0 / D)
        dg_ref[...] += _colsum(dyv * xn)
        dx_ref[...] = _rms_bwd(dyv, xn, r, gv)

    return pl.pallas_call(
        body, name="loss_bwd", grid=(S // T,),
        in_specs=[pl.BlockSpec((T, D), lambda i: (i, 0)), _full((1, D)), pl.BlockSpec((T, D), lambda i: (i, 0))],
        out_specs=[pl.BlockSpec((T, D), lambda i: (i, 0)), _full((8, LANES)), _full((1, D))],
        out_shape=[jax.ShapeDtypeStruct((S, D), _F32), jax.ShapeDtypeStruct((8, LANES), _F32),
                   jax.ShapeDtypeStruct((1, D), _F32)],
        compiler_params=_seq(),
    )(x, g, tgt)


def _me():
    return 4 * lax.axis_index("x") + 2 * lax.axis_index("y") + lax.axis_index("c")


def _peer(j):
    x, y, c = lax.axis_index("x"), lax.axis_index("y"), lax.axis_index("c")
    px = 1 - x if j & 4 else x
    py = 1 - y if j & 2 else y
    pc = 1 - c if j & 1 else c
    return (px, py, pc), 4 * px + 2 * py + pc


_HBM = pl.BlockSpec(memory_space=pltpu.HBM)
_SEMS = pl.BlockSpec(memory_space=pltpu.SEMAPHORE)
_EFFECT = pltpu.SideEffectType.DATAFLOW_SIDE_EFFECTING


def _in_hbm(a):
    return pltpu.with_memory_space_constraint(a, pltpu.HBM)


def _start_copies(name, groups):
    flat = []
    for srcs, zones, _ in groups:
        flat += [_in_hbm(a) for a in srcs] + [_in_hbm(lax.empty(z.shape, z.dtype)) for z in zones]
    n_in, n_g = len(flat), len(groups)

    def body(*refs):
        sems = refs[n_in:n_in + 2 * n_g]
        me, k = _me(), 0
        for gi, (srcs, zones, plan) in enumerate(groups):
            src_refs, zone_refs = refs[k:k + len(srcs)], refs[k + len(srcs):k + len(srcs) + len(zones)]
            k += len(srcs) + len(zones)
            for t, (si, zi, src_of, dst_of) in enumerate(plan):
                for j in range(1, N_DEV):
                    dev, pk = _peer(j)
                    pltpu.make_async_remote_copy(
                        src_ref=src_of(src_refs[si], pk), dst_ref=dst_of(zone_refs[zi], me),
                        send_sem=sems[2 * gi].at[t * (N_DEV - 1) + j - 1], recv_sem=sems[2 * gi + 1].at[t * (N_DEV - 1) + j - 1],
                        device_id=dev, device_id_type=pl.DeviceIdType.MESH).start()
                pltpu.make_async_copy(src_of(src_refs[si], me), dst_of(zone_refs[zi], me),
                                      sems[2 * gi].at[len(plan) * (N_DEV - 1) + t]).start()
        refs[-1][...] = jnp.zeros_like(refs[-1])

    sem_shapes = []
    for _, _, plan in groups:
        sem_shapes += [pltpu.SemaphoreType.DMA((len(plan) * N_DEV,)), pltpu.SemaphoreType.DMA((len(plan) * (N_DEV - 1),))]
    outs = pl.pallas_call(
        body, name=name,
        out_shape=(*sem_shapes, *[pltpu.HBM(a.shape, a.dtype) for a in flat], jax.ShapeDtypeStruct((8, LANES), _F32)),
        in_specs=[_HBM] * n_in,
        out_specs=(*[_SEMS] * (2 * n_g), *[_HBM] * n_in, pl.BlockSpec(memory_space=pltpu.VMEM)),
        input_output_aliases={k: 2 * n_g + k for k in range(n_in)},
        compiler_params=pltpu.CompilerParams(has_side_effects=_EFFECT),
    )(*flat)
    handles, k = [], 2 * n_g
    for gi, (srcs, zones, plan) in enumerate(groups):
        ns, nz = len(srcs), len(zones)
        handles.append((outs[2 * gi], outs[2 * gi + 1], list(outs[k:k + ns]), list(outs[k + ns:k + ns + nz]), plan))
        k += ns + nz
    return handles, outs[-1]


def _wait_copies(name, handles, after):
    flat = []
    for _, _, srcs, zones, _ in handles:
        flat += srcs + zones
    n_in, n_g = len(flat), len(handles)

    def body(*refs):
        sems = refs[n_in:n_in + 2 * n_g]
        me, k, local, remote = _me(), 0, [], []
        for gi, (_, _, srcs, zones, plan) in enumerate(handles):
            ns, nz = len(srcs), len(zones)
            src_refs, zone_refs = refs[k:k + ns], refs[k + ns:k + ns + nz]
            k += ns + nz
            for t, (si, zi, src_of, dst_of) in enumerate(plan):
                local.append(pltpu.make_async_copy(src_of(src_refs[si], me), dst_of(zone_refs[zi], me),
                                                   sems[2 * gi].at[len(plan) * (N_DEV - 1) + t]))
                for j in range(1, N_DEV):
                    dev, pk = _peer(j)
                    remote.append(pltpu.make_async_remote_copy(
                        src_ref=src_of(src_refs[si], pk), dst_ref=dst_of(zone_refs[zi], pk),
                        send_sem=sems[2 * gi].at[t * (N_DEV - 1) + j - 1], recv_sem=sems[2 * gi + 1].at[t * (N_DEV - 1) + j - 1],
                        device_id=dev, device_id_type=pl.DeviceIdType.MESH))
        for cp in remote:
            cp.wait_send()
            cp.wait_recv()
        for cp in local:
            cp.wait()

    sem_args = []
    for send, recv, _, _, _ in handles:
        sem_args += [send, recv]
    outs = pl.pallas_call(
        body, name=name, out_shape=tuple(pltpu.HBM(a.shape, a.dtype) for a in flat),
        in_specs=[_HBM] * n_in + [_SEMS] * (2 * n_g) + [pl.BlockSpec(memory_space=pl.ANY)],
        out_specs=tuple([_HBM] * n_in), input_output_aliases={k: k for k in range(n_in)},
        compiler_params=pltpu.CompilerParams(has_side_effects=_EFFECT),
    )(*flat, *sem_args, after)
    res, k = [], 0
    for _, _, srcs, zones, _ in handles:
        res.append(list(outs[k + len(srcs):k + len(srcs) + len(zones)]))
        k += len(srcs) + len(zones)
    return res


def _rows(axis, size):
    def of(ref, b):
        start = b * size
        if size % 8 == 0:
            start = pl.multiple_of(start, 8)
        return ref.at[(slice(None),) * axis + (pl.ds(start, size),)]
    return of


def _whole(ref, b):
    return ref


def _slot(ref, b):
    return ref.at[b]


def _gather_group(shards):
    zones, plan = [], []
    for k, (a, axis) in enumerate(shards):
        zones.append(jax.ShapeDtypeStruct(a.shape[:axis] + (N_DEV * a.shape[axis],) + a.shape[axis + 1:], a.dtype))
        plan.append((k, k, _whole, _rows(axis, a.shape[axis])))
    return [a for a, _ in shards], zones, plan


def _scatter_group(grads):
    zones, plan = [], []
    for k, (a, axis) in enumerate(grads):
        size = a.shape[axis] // N_DEV
        zones.append(jax.ShapeDtypeStruct((N_DEV,) + a.shape[:axis] + (size,) + a.shape[axis + 1:], a.dtype))
        plan.append((k, k, _rows(axis, size), _slot))
    return [a for a, _ in grads], zones, plan


def _all_reduce_small(p):
    R = p.shape[0]

    def body(p_ref, o_ref, land, send_sems, recv_sems):
        me = _me()
        land[me] = p_ref[...]
        waits = []
        for j in range(1, N_DEV):
            dev, pk = _peer(j)
            pltpu.make_async_remote_copy(
                src_ref=p_ref, dst_ref=land.at[me], send_sem=send_sems.at[j - 1], recv_sem=recv_sems.at[j - 1],
                device_id=dev, device_id_type=pl.DeviceIdType.MESH).start()
            waits.append(pltpu.make_async_remote_copy(
                src_ref=p_ref, dst_ref=land.at[pk], send_sem=send_sems.at[j - 1], recv_sem=recv_sems.at[j - 1],
                device_id=dev, device_id_type=pl.DeviceIdType.MESH))
        for cp in waits:
            cp.wait()
        tot = land[0]
        for b in range(1, N_DEV):
            tot = tot + land[b]
        o_ref[...] = tot

    vmem = pl.BlockSpec(memory_space=pltpu.VMEM)
    return pl.pallas_call(
        body, name="all_reduce_small", in_specs=[vmem], out_specs=vmem,
        out_shape=jax.ShapeDtypeStruct((R, LANES), _F32),
        scratch_shapes=[pltpu.VMEM((N_DEV, R, LANES), _F32), pltpu.SemaphoreType.DMA((N_DEV - 1,)),
                        pltpu.SemaphoreType.DMA((N_DEV - 1,))],
        compiler_params=_params(),
    )(p)


def _sum_landed(land):
    g = land[0].astype(_F32)
    for b in range(1, N_DEV):
        g = g + land[b].astype(_F32)
    return g


def _landed_specs(n_layers, tr, C, nr):
    def spec(k):
        return pl.BlockSpec((N_DEV, tr, C), lambda l, i: (0, jnp.where(l == k, i, jnp.where(l < k, 0, nr - 1)), 0))
    return [spec(k) for k in range(n_layers)]


def _per_layer(l, zone_refs, fn):
    for k, ref in enumerate(zone_refs):
        @pl.when(l == k)
        def _(ref=ref):
            fn(_sum_landed(ref))


def _sum8(zones):
    L = len(zones)
    _, R, C = zones[0].shape
    tr = _tile(R, 352, 16)
    nr = R // tr

    def body(*refs):
        o_ref = refs[L]

        def put(g):
            o_ref[...] = g

        _per_layer(pl.program_id(0), refs[:L], put)

    return pl.pallas_call(
        body, name="sum8", grid=(L, nr), in_specs=_landed_specs(L, tr, C, nr),
        out_specs=pl.BlockSpec((tr, C), lambda l, i: (l * nr + i, 0)),
        out_shape=jax.ShapeDtypeStruct((L * R, C), _F32), compiler_params=_seq(2),
    )(*zones)


def _adam_update(gv, w_ref, m_ref, v_ref, d_ref, mo_ref, vo_ref):
    mn = ADAM_B1 * m_ref[...] + (1.0 - ADAM_B1) * gv
    vn = ADAM_B2 * v_ref[...] + (1.0 - ADAM_B2) * (gv * gv)
    mo_ref[...] = mn
    vo_ref[...] = vn
    d_ref[...] = -ADAM_LR * ((mn / (1.0 - ADAM_B1 ** ADAM_STEP)) / (jnp.sqrt(vn / (1.0 - ADAM_B2 ** ADAM_STEP)) + ADAM_EPS)
                             + ADAM_WD * w_ref[...])


def _earlier(outs):
    outs = list(outs or [])
    return outs, [pl.BlockSpec(memory_space=pl.ANY)] * len(outs)


def _adamw(g, w, m, v, name, first_row=0, earlier=None):
    Rg, C = g.shape
    R = w.shape[0]
    tr = _tile(Rg if first_row == 0 else min(Rg, first_row), 256, 16)
    off = first_row // tr
    more, more_specs = _earlier(earlier)

    def body(g_ref, w_ref, m_ref, v_ref, *rest):
        go_ref, d_ref, mo_ref, vo_ref = rest[-4:]
        gv = g_ref[...]
        go_ref[...] = gv
        _adam_update(gv, w_ref, m_ref, v_ref, d_ref, mo_ref, vo_ref)

    row = pl.BlockSpec((tr, C), lambda i: (i + off, 0))
    return pl.pallas_call(
        body, name=name, grid=(Rg // tr,), in_specs=[pl.BlockSpec((tr, C), lambda i: (i, 0))] + [row] * 3 + more_specs,
        out_specs=[row] * 4, out_shape=[jax.ShapeDtypeStruct((R, C), _F32)] * 4,
        input_output_aliases={4 + k: k for k in range(len(more))}, compiler_params=_seq(),
    )(g, w, m, v, *more)


def _adamw_landed(zones, w, m, v, name, first_layer=0, earlier=None):
    L = len(zones)
    _, R, C = zones[0].shape
    tr = _tile(R, 176, 16)
    nr = R // tr
    more, more_specs = _earlier(earlier)

    def body(*refs):
        w_ref, m_ref, v_ref = refs[L:L + 3]
        g_ref, d_ref, mo_ref, vo_ref = refs[-4:]

        def update(g):
            g_ref[...] = g
            _adam_update(g, w_ref, m_ref, v_ref, d_ref, mo_ref, vo_ref)

        _per_layer(pl.program_id(0), refs[:L], update)

    row = pl.BlockSpec((tr, C), lambda l, i: ((l + first_layer) * nr + i, 0))
    return pl.pallas_call(
        body, name=name, grid=(L, nr), in_specs=_landed_specs(L, tr, C, nr) + [row] * 3 + more_specs,
        out_specs=[row] * 4, out_shape=[jax.ShapeDtypeStruct(w.shape, _F32)] * 4,
        input_output_aliases={L + 3 + k: k for k in range(len(more))}, compiler_params=_seq(2),
    )(*zones, w, m, v, *more)


def _pack(parts):
    flat = jnp.concatenate([p.reshape(-1).astype(_F32) for p in parts])
    n = flat.shape[0]
    rows = -(-n // (8 * LANES)) * 8
    return jnp.pad(flat, (0, rows * LANES - n)).reshape(rows, LANES)


def _unpack(packed, shapes):
    flat, out, k = packed.reshape(-1), [], 0
    for s in shapes:
        n = 1
        for d in s:
            n *= d
        out.append(flat[k:k + n].reshape(s))
        k += n
    return out


def kernel(x, norm1_g, norm2_g, pool_w, pool_scale, kv_norm_g, w_kv, b_kv, w_q, b_q, sinks, w_o, b_o, ffn_up, ffn_conv_w, ffn_conv_b, ffn_down, final_g, loss_target, m_norm1_g, m_norm2_g, m_pool_w, m_pool_scale, m_kv_norm_g, m_w_kv, m_b_kv, m_w_q, m_b_q, m_sinks, m_w_o, m_b_o, m_ffn_up, m_ffn_conv_w, m_ffn_conv_b, m_ffn_down, m_final_g, v_norm1_g, v_norm2_g, v_pool_w, v_pool_scale, v_kv_norm_g, v_w_kv, v_b_kv, v_w_q, v_b_q, v_sinks, v_w_o, v_b_o, v_ffn_up, v_ffn_conv_w, v_ffn_conv_b, v_ffn_down, v_final_g):
    S = x.shape[1]
    F2s = ffn_up.shape[2]
    F2 = N_DEV * F2s
    me = _me()
    x0 = x.reshape(S, D)
    tgt = loss_target.reshape(S, D)
    row = lambda a: a.reshape(1, -1)

    small = _pack([pool_scale, ffn_conv_w])
    wire = lambda a: a.astype(_MXU)
    ffn_w = lambda l: [(wire(ffn_up[l]).T, 0), (wire(ffn_down[l]), 0)]
    attn_w = lambda j: [(wire(w_q[j]), 0), (wire(w_o[j]), 0)]
    gathers, token = _start_copies("gather_start", [_gather_group(g) for g in (
        [(wire(pool_w[0]), 1), (small[None], 0)], ffn_w(0), [(wire(pool_w[1]), 1)] + ffn_w(1),
        [(wire(w_kv), 0)] + attn_w(0), ffn_w(2), attn_w(1), ffn_w(3))])

    def gathered(k, after):
        return _wait_copies("gather_wait_%d" % k, [gathers[k]], after)[0]

    pw, up_t, down, wq, wo = [None] * N_A, [None] * DEPTH, [None] * DEPTH, [None] * 2, [None] * 2
    pw[0], small_all = gathered(0, token)
    n_ps = pool_scale.size
    small_all = small_all.reshape(N_DEV, -1)
    pscale = jnp.transpose(small_all[:, :n_ps].reshape(N_DEV, N_A, D // N_DEV), (1, 0, 2)).reshape(N_A, D)
    conv_w = jnp.transpose(small_all[:, n_ps:n_ps + ffn_conv_w.size].reshape(N_DEV, DEPTH, CONV_W, F2s),
                           (1, 2, 0, 3)).reshape(DEPTH, CONV_W, F2)

    def dup(a):
        a4 = a.reshape(a.shape[:-1] + (2 * N_KV, 1, HEAD_DIM))
        return jnp.broadcast_to(a4, a.shape[:-1] + (2 * N_KV, 2, HEAD_DIM)).reshape(a.shape[:-1] + (KVD,))

    def fold(a):
        return a.reshape(a.shape[:-1] + (2 * N_KV, 2, HEAD_DIM)).sum(axis=-2).reshape(a.shape[:-1] + (2 * N_KV * HEAD_DIM,))

    xs, us, qs, os_ = [x0], [], [], []
    xc = x0
    kvd = None
    for l in range(DEPTH):
        if l == 1:
            pw[1], up_t[1], down[1] = gathered(2, xc)
        if l == 3:
            wq[1], wo[1] = gathered(5, xc)
        if l < N_A:
            xc = _pool_fwd(xc, row(norm1_g[l]), pw[l], row(pscale[l]))
        else:
            j = l - N_A
            xc, q, o = _attn_fwd(xc, row(norm1_g[l]), wq[j], row(b_q[j]), sinks[j], kvd, wo[j], row(b_o[j]))
            qs.append(q)
            os_.append(o)
        xs.append(xc)
        if l != 1:
            up_t[l], down[l] = gathered((1, None, 4, 6)[l], xc)
        xc, u, c = _ffn_fwd(xc, row(norm2_g[l]), up_t[l], conv_w[l], row(ffn_conv_b[l]), down[l])
        us.append((u, c))
        xs.append(xc)
        if l == N_A - 1:
            wkv, wq[0], wo[0] = gathered(3, xc)
            wkv_d, bkv_d = dup(wkv), dup(row(b_kv))
            kvd = _kv_fwd(xc, row(kv_norm_g), wkv_d, bkv_d)

    dx, loss_p, d_final = _loss_bwd(xc, row(final_g), tgt)
    d_n1, d_n2, d_cw, d_cb = [None] * DEPTH, [None] * DEPTH, [None] * DEPTH, [None] * DEPTH
    d_bq, d_bo, d_sk, d_ps, dkv_parts = [None] * 2, [None] * 2, [None] * 2, [None] * N_A, []
    up_z, down_z, mix_z = [None] * DEPTH, [None] * DEPTH, [None] * DEPTH
    token = None

    def after(gain):
        return gain if token is None else gain + token[0:1, 0:1]

    rep_names = ["norm1_g", "norm2_g", "kv_norm_g", "b_kv", "b_q", "sinks", "b_o", "ffn_conv_b", "final_g"]

    def small_parts(n1_rest, ps_rest):
        zero = jnp.zeros((1, D), _F32)
        return [jnp.concatenate([zero] + n1_rest), jnp.concatenate(d_n2), d_kvg, fold(d_bkv), jnp.concatenate(d_bq),
                jnp.concatenate([s[:, :N_HEADS] for s in d_sk]), jnp.concatenate(d_bo), jnp.concatenate(d_cb), d_final,
                jnp.concatenate([zero] + ps_rest), jnp.stack(d_cw), loss_p[0:1, 0:1]]

    for l in reversed(range(DEPTH)):
        x_in, x_mid, x_out = xs[2 * l], xs[2 * l + 1], xs[2 * l + 2]
        mixer_grads = []
        if l == N_A - 1:
            dx, d_wkv, d_bkv, d_kvg = _kv_bwd(x_out, dx, after(row(kv_norm_g)), wkv_d, *dkv_parts)
            mixer_grads.append((fold(d_wkv).astype(_MXU), 0))
        dy = dx
        dx, du, a, h, d_cw[l], d_cb[l], d_n2[l] = _ffn_bwd(
            x_mid, dy, *us[l], after(row(norm2_g[l])), up_t[l], conv_w[l], down[l])
        g_up = _tn_matmul(du, h, "tn_up")
        if l > 0:
            (up_z[l], down_z[l]), token = _start_copies("scatter_ffn_%d" % l, [
                _scatter_group([(g_up, 0)]), _scatter_group([(_tn_matmul(a, dy, "tn_down"), 0)])])
        else:
            (small_handle, up_z[l]), token = _start_copies("scatter_up_0", [
                _gather_group([(_pack(small_parts(d_n1[1:], d_ps[1:]))[None], 0)]), _scatter_group([(g_up, 0)])])
            (down_z[l],), token = _start_copies("scatter_down_0", [
                _scatter_group([(_tn_matmul(a, dy, "tn_down", token), 0)])])
        dy = dx
        if l < N_A:
            dx, d_pw, d_ps[l], d_n1[l] = _pool_bwd(x_in, dy, after(row(norm1_g[l])), pw[l], row(pscale[l]))
            mixer_grads.append((d_pw.astype(_MXU), 1))
        else:
            j = l - N_A
            dx, dq, h, d_cur, d_prev, d_bq[j], d_bo[j], d_n1[l], d_sk[j] = _attn_bwd(
                x_in, dy, qs[j], os_[j], after(row(norm1_g[l])), wq[j], sinks[j], kvd, wo[j])
            mixer_grads += [(_tn_matmul(h, dq, "tn_q"), 0), (_tn_matmul(os_[j], dy, "tn_o"), 0)]
            dkv_parts += [d_cur, d_prev]
        (mix_z[l],), token = _start_copies("scatter_mixer_%d" % l, [_scatter_group(mixer_grads)])

    moms = dict(norm1_g=(m_norm1_g, v_norm1_g), norm2_g=(m_norm2_g, v_norm2_g), pool_w=(m_pool_w, v_pool_w),
                pool_scale=(m_pool_scale, v_pool_scale), kv_norm_g=(m_kv_norm_g, v_kv_norm_g), w_kv=(m_w_kv, v_w_kv),
                b_kv=(m_b_kv, v_b_kv), w_q=(m_w_q, v_w_q), b_q=(m_b_q, v_b_q), sinks=(m_sinks, v_sinks),
                w_o=(m_w_o, v_w_o), b_o=(m_b_o, v_b_o), ffn_up=(m_ffn_up, v_ffn_up),
                ffn_conv_w=(m_ffn_conv_w, v_ffn_conv_w), ffn_conv_b=(m_ffn_conv_b, v_ffn_conv_b),
                ffn_down=(m_ffn_down, v_ffn_down), final_g=(m_final_g, v_final_g))
    given = dict(norm1_g=norm1_g, norm2_g=norm2_g, kv_norm_g=kv_norm_g, b_kv=b_kv, b_q=b_q, sinks=sinks, b_o=b_o,
                 ffn_conv_b=ffn_conv_b, final_g=final_g, pool_scale=pool_scale, ffn_conv_w=ffn_conv_w,
                 pool_w=pool_w, w_kv=w_kv, w_q=w_q, w_o=w_o, ffn_up=ffn_up, ffn_down=ffn_down)
    grad, delta, new_m, new_v = {}, {}, {}, {}

    def update(name, zones, cols, first_layer=0, earlier=None, last=True):
        w = given[name]
        two_d = lambda a: a.reshape(-1, cols)
        wmv = (two_d(w), two_d(moms[name][0]), two_d(moms[name][1]))
        if name == "ffn_up":
            g = jnp.swapaxes(_sum8(zones).reshape(len(zones), F2s, D), 1, 2).reshape(-1, F2s)
            outs = _adamw(g, *wmv, "adamw_" + name, first_layer * D, earlier)
        else:
            landed = [z.reshape(N_DEV, -1, cols) for z in zones]
            outs = _adamw_landed(landed, *wmv, "adamw_" + name, first_layer, earlier)
        if last:
            grad[name], delta[name], new_m[name], new_v[name] = (o.reshape(w.shape) for o in outs)
        return outs

    rest = _wait_copies("scatter_wait_rest", up_z[1:] + down_z[1:] + mix_z[1:], token)
    up_r, down_r, mix_r = rest[:DEPTH - 1], rest[DEPTH - 1:2 * (DEPTH - 1)], rest[2 * (DEPTH - 1):]
    up_1 = update("ffn_up", [z[0] for z in up_r], F2s, 1, last=False)
    down_1 = update("ffn_down", [z[0] for z in down_r], D, 1, last=False)
    update("w_q", [mix_r[N_A - 1][0], mix_r[N_A][0]], D)
    update("w_o", [mix_r[N_A - 1][1], mix_r[N_A][1]], D)
    update("w_kv", [mix_r[N_A - 2][0]], w_kv.shape[1])
    pw_1 = update("pool_w", [mix_r[N_A - 2][1]], GC, 1, last=False)

    late_in, _ = lax.optimization_barrier((_pack([d_n1[0], d_ps[0]]), down_1[1]))
    late = _all_reduce_small(late_in).reshape(-1)
    (early,), = _wait_copies("small_wait", [small_handle], late)
    tot = _unpack(_sum8([early]), [given[k].shape for k in rep_names] + [(N_A, D), (DEPTH, CONV_W, F2), ()])
    tot[0] = tot[0].at[0].add(late[:D])
    tot[-3] = tot[-3].at[0].add(late[D:2 * D])
    grad.update(zip(rep_names, tot))
    loss = tot[-1]
    grad["pool_scale"] = lax.dynamic_slice_in_dim(tot[-3], me * (D // N_DEV), D // N_DEV, axis=1)
    grad["ffn_conv_w"] = lax.dynamic_slice_in_dim(tot[-2], me * F2s, F2s, axis=2)
    small_names = rep_names + ["pool_scale", "ffn_conv_w"]
    shapes = [given[k].shape for k in small_names]
    outs = _adamw(_pack([grad[k] for k in small_names]), _pack([given[k] for k in small_names]),
                  _pack([moms[k][0] for k in small_names]), _pack([moms[k][1] for k in small_names]), "adamw_small")
    for dst, packed in zip((delta, new_m, new_v), outs[1:]):
        dst.update(zip(small_names, _unpack(packed, shapes)))

    (up_0,), (down_0,), (pw_0,) = _wait_copies("scatter_wait_0", [up_z[0], down_z[0], mix_z[0]], outs[1])
    update("ffn_up", [up_0], F2s, 0, up_1)
    update("ffn_down", [down_0], D, 0, down_1)
    update("pool_w", [pw_0], GC, 0, pw_1)

    names = ["norm1_g", "norm2_g", "pool_w", "pool_scale", "kv_norm_g", "w_kv", "b_kv", "w_q", "b_q", "sinks", "w_o",
             "b_o", "ffn_up", "ffn_conv_w", "ffn_conv_b", "ffn_down", "final_g"]
    return (loss, dx.reshape(x.shape), *[grad[k] for k in names], *[delta[k] for k in names],
            *[new_m[k] for k in names], *[new_v[k] for k in names])
```

```python
import functools

import jax
import jax.numpy as jnp
from jax import lax
from jax.experimental import pallas as pl
from jax.experimental.pallas import tpu as pltpu

_F32 = jnp.float32
_MXU = jnp.bfloat16

N_DEV = 8
D = 1024
DEPTH = 4
N_A = 2
POOL_WINDOWS = (2, 4, 8, 16)
GC = D // len(POOL_WINDOWS)
HALO = 16
HEAD_DIM = 64
N_HEADS = D // HEAD_DIM
GROUP = 8
N_KV = N_HEADS // GROUP
BLK = 128
PAIR = 2 * HEAD_DIM
KVD = 4 * N_KV * HEAD_DIM
CONV_W = 3
EPS = 1e-5
NEG = -1e30

ADAM_LR = 0.001
ADAM_B1 = 0.9
ADAM_B2 = 0.999
ADAM_EPS = 1e-08
ADAM_WD = 0.01
ADAM_STEP = 10

V7X_VMEM_LIMIT = 56 * 1024 * 1024
LANES = 128

_NT = (((1,), (1,)), ((), ()))
_TN = (((0,), (0,)), ((), ()))


def _params(**kw):
    return pltpu.CompilerParams(vmem_limit_bytes=V7X_VMEM_LIMIT, **kw)


def _seq(n=1):
    return _params(dimension_semantics=("arbitrary",) * n)


def _dot(a, b, dims=None):
    if dims is None:
        return jnp.dot(a, b, preferred_element_type=_F32)
    return lax.dot_general(a, b, dims, preferred_element_type=_F32)


def _rms(x):
    r = lax.rsqrt(jnp.mean(x * x, axis=-1, keepdims=True) + EPS)
    return x * r, r


def _rms_bwd(dh, xn, r, g):
    dxn = dh * g
    return r * (dxn - xn * jnp.mean(dxn * xn, axis=-1, keepdims=True))


def _colsum(a):
    return jnp.sum(a, axis=0, keepdims=True)


def _tile(n, want, mult=8):
    for t in range(min(want, n), 0, -1):
        if n % t == 0 and t % mult == 0:
            return t
    return n


def _full(shape):
    zeros = (0,) * len(shape)
    return pl.BlockSpec(shape, lambda *_: zeros)


def _window_sum(ext, win, trailing):
    R = ext.shape[0]
    acc, k = ext, 1
    while k < win:
        acc = acc + pltpu.roll(acc, k if trailing else R - k, axis=0)
        k *= 2
    return acc


def _pool_windows(hbuf, h, row, T):
    out = []
    for gi, win in enumerate(POOL_WINDOWS):
        cs = slice(gi * GC, (gi + 1) * GC)
        acc = _window_sum(hbuf[:, cs], win, True)[HALO:, :]
        cnt = jnp.minimum(row + 1, win).astype(_F32)
        out.append((acc / cnt - h[:, cs], cnt))
    return out


def _pool_fwd(x, g, w, sc):
    S = x.shape[0]
    T = _tile(S, 512, HALO)
    n, hb = S // T, T // HALO

    def body(x_ref, xh_ref, g_ref, w_ref, sc_ref, o_ref, hbuf):
        i = pl.program_id(0)
        gv = g_ref[...]
        xv = x_ref[...]
        h = _rms(xv)[0] * gv
        hbuf[0:HALO, :] = jnp.where(i > 0, _rms(xh_ref[...])[0] * gv, 0.0)
        hbuf[HALO:, :] = h
        row = i * T + lax.broadcasted_iota(jnp.int32, (T, 1), 0)
        for gi, (p, _) in enumerate(_pool_windows(hbuf, h, row, T)):
            cs = slice(gi * GC, (gi + 1) * GC)
            z = _dot(p.astype(_MXU), w_ref[gi])
            o_ref[:, cs] = xv[:, cs] + z * sc_ref[:, cs]

    return pl.pallas_call(
        body, name="pool_fwd", grid=(n,),
        in_specs=[pl.BlockSpec((T, D), lambda i: (i, 0)),
                  pl.BlockSpec((HALO, D), lambda i: (jnp.maximum(i * hb - 1, 0), 0)),
                  _full((1, D)), _full((4, GC, GC)), _full((1, D))],
        out_specs=pl.BlockSpec((T, D), lambda i: (i, 0)),
        out_shape=jax.ShapeDtypeStruct((S, D), _F32),
        scratch_shapes=[pltpu.VMEM((T + HALO, D), _F32)],
        compiler_params=_seq(),
    )(x, x, g, w, sc)


def _pool_bwd(x, dy, g, w, sc):
    S = x.shape[0]
    T = _tile(S, 512, HALO)
    n, hb = S // T, T // HALO

    def body(x_ref, xh_ref, dy_ref, dyh_ref, g_ref, w_ref, sc_ref, dx_ref, dw_ref, dsc_ref, dg_ref,
             hbuf, qbuf, dhbuf):
        i = pl.program_id(0)

        @pl.when(i == 0)
        def _():
            dw_ref[...] = jnp.zeros_like(dw_ref)
            dsc_ref[...] = jnp.zeros_like(dsc_ref)
            dg_ref[...] = jnp.zeros_like(dg_ref)

        gv = g_ref[...]
        xv = x_ref[...]
        xn, r = _rms(xv)
        h = xn * gv
        hbuf[0:HALO, :] = jnp.where(i > 0, _rms(xh_ref[...])[0] * gv, 0.0)
        hbuf[HALO:, :] = h
        dyv = dy_ref[...]
        dz = dyv * sc_ref[...]
        dzh = jnp.where(i < n - 1, dyh_ref[...], 0.0) * sc_ref[...]
        row = i * T + lax.broadcasted_iota(jnp.int32, (T, 1), 0)
        rowh = (i + 1) * T + lax.broadcasted_iota(jnp.int32, (HALO, 1), 0)
        for gi, (p, cnt) in enumerate(_pool_windows(hbuf, h, row, T)):
            win = POOL_WINDOWS[gi]
            cs = slice(gi * GC, (gi + 1) * GC)
            pb = p.astype(_MXU)
            wg = w_ref[gi]
            dsc_ref[:, cs] += _colsum(dyv[:, cs] * _dot(pb, wg))
            dzb = dz[:, cs].astype(_MXU)
            dw_ref[gi] += _dot(pb, dzb, _TN)
            dp = _dot(dzb, wg, _NT)
            dph = _dot(dzh[:, cs].astype(_MXU), wg, _NT)
            qbuf[0:T, cs] = dp / cnt
            qbuf[T:T + HALO, cs] = dph / jnp.minimum(rowh + 1, win).astype(_F32)
            dhbuf[:, cs] = _window_sum(qbuf[:, cs], win, False)[0:T, :] - dp
        dh = dhbuf[...]
        dg_ref[...] += _colsum(dh * xn)
        dx_ref[...] = dyv + _rms_bwd(dh, xn, r, gv)

    return pl.pallas_call(
        body, name="pool_bwd", grid=(n,),
        in_specs=[pl.BlockSpec((T, D), lambda i: (i, 0)),
                  pl.BlockSpec((HALO, D), lambda i: (jnp.maximum(i * hb - 1, 0), 0)),
                  pl.BlockSpec((T, D), lambda i: (i, 0)),
                  pl.BlockSpec((HALO, D), lambda i: (jnp.minimum((i + 1) * hb, S // HALO - 1), 0)),
                  _full((1, D)), _full((4, GC, GC)), _full((1, D))],
        out_specs=[pl.BlockSpec((T, D), lambda i: (i, 0)), _full((4, GC, GC)), _full((1, D)), _full((1, D))],
        out_shape=[jax.ShapeDtypeStruct((S, D), _F32), jax.ShapeDtypeStruct((4, GC, GC), _F32),
                   jax.ShapeDtypeStruct((1, D), _F32), jax.ShapeDtypeStruct((1, D), _F32)],
        scratch_shapes=[pltpu.VMEM((T + HALO, D), _F32), pltpu.VMEM((T + HALO, D), _F32), pltpu.VMEM((T, D), _F32)],
        compiler_params=_seq(),
    )(x, x, dy, dy, g, w, sc)


FFN_FWD_TILE, FFN_FWD_CHUNKS = 256, 1
FFN_BWD_TILE, FFN_BWD_CHUNKS = 256, 1
EDGE = 8


def _shift_down(v, k, prev):
    r = pltpu.roll(v, k, axis=0)
    i8 = lax.broadcasted_iota(jnp.int32, (EDGE, v.shape[1]), 0)
    head = jnp.where(i8 >= k, r[0:EDGE, :], pltpu.roll(prev, k, axis=0))
    return jnp.concatenate([head, r[EDGE:, :]], axis=0)


def _shift_up(v, k, nxt):
    T = v.shape[0]
    r = pltpu.roll(v, T - k, axis=0)
    i8 = lax.broadcasted_iota(jnp.int32, (EDGE, v.shape[1]), 0)
    tail = jnp.where(i8 < EDGE - k, r[T - EDGE:, :], pltpu.roll(nxt, EDGE - k, axis=0))
    return jnp.concatenate([r[:T - EDGE, :], tail], axis=0)


def _load_weights(i, pairs, sems):
    @pl.when(i == 0)
    def _():
        cps = [pltpu.make_async_copy(src, dst, sems.at[k]) for k, (src, dst) in enumerate(pairs)]
        for cp in cps:
            cp.start()
        for cp in cps:
            cp.wait()


def _ffn_fwd(x, g, wup_t, cw, cb, wdn):
    S = x.shape[0]
    F2 = wup_t.shape[0]
    F = F2 // 2
    C = F // FFN_FWD_CHUNKS
    T = _tile(S, FFN_FWD_TILE, 16)
    n = S // T

    def body(x_ref, g_ref, wup_hbm, cw_ref, cb_ref, wdn_hbm, o_ref, u_ref, c_ref, wup, wdnv, carry, sems):
        i = pl.program_id(0)
        _load_weights(i, [(wup_hbm, wup), (wdn_hbm, wdnv)], sems)

        @pl.when(i == 0)
        def _():
            carry[...] = jnp.zeros_like(carry)

        xv = x_ref[...]
        hb = (_rms(xv)[0] * g_ref[...]).astype(_MXU)
        acc = jnp.zeros((T, D), _F32)
        for j in range(FFN_FWD_CHUNKS):
            halves = []
            for cs in (slice(j * C, (j + 1) * C), slice(F + j * C, F + (j + 1) * C)):
                u = _dot(hb, wup[cs, :], _NT)
                u_ref[:, cs] = u.astype(u_ref.dtype)
                prev = carry[:, cs]
                carry[:, cs] = u[T - EDGE:, :]
                c = (cw_ref[0:1, cs] * _shift_down(u, 2, prev) + cw_ref[1:2, cs] * _shift_down(u, 1, prev)
                     + cw_ref[2:3, cs] * u + cb_ref[:, cs])
                c_ref[:, cs] = c.astype(c_ref.dtype)
                halves.append(c)
            cg, cv = halves
            a = (cg * jax.nn.sigmoid(cg)) * cv
            acc = acc + _dot(a.astype(_MXU), wdnv[j * C:(j + 1) * C, :])
        o_ref[...] = xv + acc

    any_ = pl.BlockSpec(memory_space=pl.ANY)
    wide = pl.BlockSpec((T, F2), lambda i: (i, 0))
    return pl.pallas_call(
        body, name="ffn_fwd", grid=(n,),
        in_specs=[pl.BlockSpec((T, D), lambda i: (i, 0)), _full((1, D)), any_, _full((CONV_W, F2)), _full((1, F2)), any_],
        out_specs=[pl.BlockSpec((T, D), lambda i: (i, 0)), wide, wide],
        out_shape=[jax.ShapeDtypeStruct((S, D), _F32), jax.ShapeDtypeStruct((S, F2), _MXU),
                   jax.ShapeDtypeStruct((S, F2), _MXU)],
        scratch_shapes=[pltpu.VMEM((F2, D), _MXU), pltpu.VMEM((F, D), _MXU),
                        pltpu.VMEM((EDGE, F2), _F32), pltpu.SemaphoreType.DMA((2,))],
        compiler_params=_seq(),
    )(x, g, wup_t, cw, cb, wdn)


def _ffn_bwd(x, dy, u, c, g, wup_t, cw, wdn):
    S = x.shape[0]
    F2 = wup_t.shape[0]
    F = F2 // 2
    C = F // FFN_BWD_CHUNKS
    T = _tile(S, FFN_BWD_TILE, 16)
    n = S // T

    def body(x_ref, dy_ref, u_ref, c_ref, g_ref, wup_hbm, cw_ref, wdn_hbm,
             dx_ref, du_ref, a_ref, h_ref, dcw_ref, dcb_ref, dg_ref, wup, wdnv, carry, sems):
        i = pl.program_id(0)
        _load_weights(i, [(wup_hbm, wup), (wdn_hbm, wdnv)], sems)

        @pl.when(i == 0)
        def _():
            carry[...] = jnp.zeros_like(carry)
            dcw_ref[...] = jnp.zeros_like(dcw_ref)
            dcb_ref[...] = jnp.zeros_like(dcb_ref)
            dg_ref[...] = jnp.zeros_like(dg_ref)

        gv = g_ref[...]
        xv = x_ref[...]
        xn, r = _rms(xv)
        hbf = (xn * gv).astype(_MXU)
        h_ref[...] = hbf
        dyv = dy_ref[...]
        dyb = dyv.astype(_MXU)
        dh = jnp.zeros((T, D), _F32)
        for j in range(FFN_BWD_CHUNKS):
            gs, vs = slice(j * C, (j + 1) * C), slice(F + j * C, F + (j + 1) * C)
            cg, cv = c_ref[:, gs].astype(_F32), c_ref[:, vs].astype(_F32)
            sg = jax.nn.sigmoid(cg)
            sl = cg * sg
            a_ref[:, gs] = (sl * cv).astype(a_ref.dtype)
            da = _dot(dyb, wdnv[gs, :], _NT)
            for cs, dc in ((gs, da * cv * (sg * (1.0 + cg * (1.0 - sg)))), (vs, da * sl)):
                nxt = carry[:, cs]
                carry[:, cs] = dc[0:EDGE, :]
                dc1, dc2 = _shift_up(dc, 1, nxt), _shift_up(dc, 2, nxt)
                uf = u_ref[:, cs].astype(_F32)
                dcb_ref[:, cs] += _colsum(dc)
                for k, d in enumerate((dc2, dc1, dc)):
                    dcw_ref[k:k + 1, cs] += _colsum(d * uf)
                du = cw_ref[2:3, cs] * dc + cw_ref[1:2, cs] * dc1 + cw_ref[0:1, cs] * dc2
                dub = du.astype(_MXU)
                du_ref[:, cs] = dub
                dh = dh + _dot(dub, wup[cs, :])
        dg_ref[...] += _colsum(dh * xn)
        dx_ref[...] = dyv + _rms_bwd(dh, xn, r, gv)

    any_ = pl.BlockSpec(memory_space=pl.ANY)
    rev = lambda i: (n - 1 - i, 0)
    return pl.pallas_call(
        body, name="ffn_bwd", grid=(n,),
        in_specs=[pl.BlockSpec((T, D), rev), pl.BlockSpec((T, D), rev), pl.BlockSpec((T, F2), rev),
                  pl.BlockSpec((T, F2), rev), _full((1, D)), any_, _full((CONV_W, F2)), any_],
        out_specs=[pl.BlockSpec((T, D), rev), pl.BlockSpec((T, F2), rev), pl.BlockSpec((T, F), rev),
                   pl.BlockSpec((T, D), rev), _full((CONV_W, F2)), _full((1, F2)), _full((1, D))],
        out_shape=[jax.ShapeDtypeStruct((S, D), _F32), jax.ShapeDtypeStruct((S, F2), _MXU),
                   jax.ShapeDtypeStruct((S, F), _MXU), jax.ShapeDtypeStruct((S, D), _MXU),
                   jax.ShapeDtypeStruct((CONV_W, F2), _F32), jax.ShapeDtypeStruct((1, F2), _F32),
                   jax.ShapeDtypeStruct((1, D), _F32)],
        scratch_shapes=[pltpu.VMEM((F2, D), _MXU), pltpu.VMEM((F, D), _MXU),
                        pltpu.VMEM((EDGE, F2), _F32), pltpu.SemaphoreType.DMA((2,))],
        compiler_params=_seq(),
    )(x, dy, u, c, g, wup_t, cw, wdn)


def _tn_matmul(a, b, name, token=None):
    S, M = a.shape
    N = b.shape[1]
    bm = _tile(M, 1408, LANES)
    tk = _tile(S, 2048, 16)
    nk = S // tk
    tokens = [] if token is None else [token]

    def body(a_ref, b_ref, *rest):
        o_ref, acc = rest[-2:]
        k = pl.program_id(1)

        @pl.when(k == 0)
        def _():
            acc[...] = jnp.zeros_like(acc)

        acc[...] += _dot(a_ref[...].astype(_MXU), b_ref[...].astype(_MXU), _TN)

        @pl.when(k == nk - 1)
        def _():
            o_ref[...] = acc[...].astype(o_ref.dtype)

    return pl.pallas_call(
        body, name=name, grid=(M // bm, nk),
        in_specs=[pl.BlockSpec((tk, bm), lambda i, k: (k, i)), pl.BlockSpec((tk, N), lambda i, k: (k, 0))]
        + [_full((8, LANES))] * len(tokens),
        out_specs=pl.BlockSpec((bm, N), lambda i, k: (i, 0)),
        out_shape=jax.ShapeDtypeStruct((M, N), _MXU),
        scratch_shapes=[pltpu.VMEM((bm, N), _F32)],
        compiler_params=_seq(2),
    )(a, b, *tokens)


def _kv_fwd(x, g, wkv, bkv):
    S = x.shape[0]
    T = _tile(S, 512, 16)

    def body(x_ref, g_ref, w_ref, b_ref, o_ref):
        hb = (_rms(x_ref[...])[0] * g_ref[...]).astype(_MXU)
        o_ref[...] = (_dot(hb, w_ref[...]) + b_ref[...]).astype(o_ref.dtype)

    return pl.pallas_call(
        body, name="kv_fwd", grid=(S // T,),
        in_specs=[pl.BlockSpec((T, D), lambda i: (i, 0)), _full((1, D)), _full((D, KVD)), _full((1, KVD))],
        out_specs=pl.BlockSpec((T, KVD), lambda i: (i, 0)),
        out_shape=jax.ShapeDtypeStruct((S, KVD), _MXU),
        compiler_params=_seq(),
    )(x, g, wkv, bkv)


def _kv_bwd(x, dx_in, g, wkv, cur_a, prev_a, cur_b, prev_b):
    S = x.shape[0]
    T = _tile(S, 512, BLK)
    n, per = S // T, T // BLK

    def body(x_ref, dxi_ref, g_ref, w_ref, ca, pa, na, cb, pb, nb, dx_ref, dw_ref, db_ref, dg_ref):
        i = pl.program_id(0)

        @pl.when(i == 0)
        def _():
            dw_ref[...] = jnp.zeros_like(dw_ref)
            db_ref[...] = jnp.zeros_like(db_ref)
            dg_ref[...] = jnp.zeros_like(dg_ref)

        gv = g_ref[...]
        xn, r = _rms(x_ref[...])
        nxt = jnp.where(i < n - 1, na[...] + nb[...], 0.0)
        prev = jnp.concatenate([pa[BLK:, :] + pb[BLK:, :], nxt], axis=0) if per > 1 else nxt
        dkv = ca[...] + cb[...] + prev
        db_ref[...] += _colsum(dkv)
        dkb = dkv.astype(_MXU)
        dw_ref[...] += _dot((xn * gv).astype(_MXU), dkb, _TN)
        dh = _dot(dkb, w_ref[...], _NT)
        dg_ref[...] += _colsum(dh * xn)
        dx_ref[...] = dxi_ref[...] + _rms_bwd(dh, xn, r, gv)

    blk = lambda w: pl.BlockSpec((T, w), lambda i: (i, 0))
    nxt = pl.BlockSpec((BLK, KVD), lambda i: (jnp.minimum((i + 1) * per, S // BLK - 1), 0))
    return pl.pallas_call(
        body, name="kv_bwd", grid=(n,),
        in_specs=[blk(D), blk(D), _full((1, D)), _full((D, KVD)), blk(KVD), blk(KVD), nxt, blk(KVD), blk(KVD), nxt],
        out_specs=[blk(D), _full((D, KVD)), _full((1, KVD)), _full((1, D))],
        out_shape=[jax.ShapeDtypeStruct((S, D), _F32), jax.ShapeDtypeStruct((D, KVD), _F32),
                   jax.ShapeDtypeStruct((1, KVD), _F32), jax.ShapeDtypeStruct((1, D), _F32)],
        compiler_params=_seq(),
    )(x, dx_in, g, wkv, cur_a, prev_a, prev_a, cur_b, prev_b, prev_b)


STACK = GROUP * BLK


def _attn_mask(i, rows):
    qi = lax.broadcasted_iota(jnp.int32, (rows, 2 * BLK), 0) & (BLK - 1)
    si = lax.broadcasted_iota(jnp.int32, (rows, 2 * BLK), 1)
    return (si > qi) & (si <= qi + BLK) & jnp.logical_or(i > 0, si >= BLK)


def _low_half():
    return lax.broadcasted_iota(jnp.int32, (BLK, PAIR), 1) < HEAD_DIM


def _stack_heads(ref, kh, dst):
    low = _low_half()
    for pp in range(GROUP // 2):
        pr = kh * (GROUP // 2) + pp
        v2 = ref[:, pr * PAIR:(pr + 1) * PAIR]
        zero = jnp.zeros_like(v2)
        dst[2 * pp * BLK:(2 * pp + 1) * BLK, :] = jnp.where(low, v2, zero)
        dst[(2 * pp + 1) * BLK:(2 * pp + 2) * BLK, :] = jnp.where(low, zero, v2)


def _unstack_heads(st, pp):
    return jnp.where(_low_half(), st[2 * pp * BLK:(2 * pp + 1) * BLK, :], st[(2 * pp + 1) * BLK:(2 * pp + 2) * BLK, :])


def _sink_col(sk_ref, kh):
    return jnp.concatenate([jnp.full((BLK, 1), sk_ref[kh * GROUP + h], _F32) for h in range(GROUP)], axis=0)


def _head_probs(qm, kd, mask, sink):
    s = jnp.where(mask, _dot(qm, kd, _NT) * (HEAD_DIM ** -0.5), NEG)
    m = jnp.maximum(jnp.max(s, axis=-1, keepdims=True), sink)
    p = jnp.exp(s - m)
    es = jnp.exp(sink - m)
    inv = 1.0 / (jnp.sum(p, axis=-1, keepdims=True) + es)
    return p * inv, es * inv


def _attn_fwd(x, g, wq, bq, sinks, kvd, wo, bo):
    S = x.shape[0]
    n = S // BLK

    def body(x_ref, g_ref, wq_ref, bq_ref, sk_ref, kp_ref, kc_ref, wo_ref, bo_ref, xo_ref, q_ref, o_ref, win):
        i = pl.program_id(0)
        xv = x_ref[...]
        hb = (_rms(xv)[0] * g_ref[...]).astype(_MXU)
        q_ref[...] = (_dot(hb, wq_ref[...]) + bq_ref[...]).astype(q_ref.dtype)
        win[0:BLK, :] = kp_ref[...]
        win[BLK:, :] = kc_ref[...]
        mask = _attn_mask(i, BLK)
        low = _low_half()
        for pr in range(N_HEADS // 2):
            kh = (2 * pr) // GROUP
            kd = win[:, kh * PAIR:(kh + 1) * PAIR]
            vd = win[:, (N_KV + kh) * PAIR:(N_KV + kh + 1) * PAIR]
            q2 = q_ref[:, pr * PAIR:(pr + 1) * PAIR]
            outs = []
            for half in range(2):
                qm = jnp.where(low if half == 0 else ~low, q2, jnp.zeros_like(q2))
                pbs, _ = _head_probs(qm, kd, mask, sk_ref[2 * pr + half])
                outs.append(_dot(pbs.astype(_MXU), vd))
            o_ref[:, pr * PAIR:(pr + 1) * PAIR] = jnp.where(low, outs[0], outs[1]).astype(o_ref.dtype)
        xo_ref[...] = xv + _dot(o_ref[...], wo_ref[...]) + bo_ref[...]

    blk = lambda w: pl.BlockSpec((BLK, w), lambda i: (i, 0))
    return pl.pallas_call(
        body, name="attn_fwd", grid=(n,),
        in_specs=[blk(D), _full((1, D)), _full((D, D)), _full((1, D)),
                  pl.BlockSpec(memory_space=pltpu.SMEM),
                  pl.BlockSpec((BLK, KVD), lambda i: (jnp.maximum(i - 1, 0), 0)), blk(KVD),
                  _full((D, D)), _full((1, D))],
        out_specs=[blk(D), blk(D), blk(D)],
        out_shape=[jax.ShapeDtypeStruct((S, D), _F32), jax.ShapeDtypeStruct((S, D), _MXU),
                   jax.ShapeDtypeStruct((S, D), _MXU)],
        scratch_shapes=[pltpu.VMEM((2 * BLK, KVD), _MXU)],
        compiler_params=_seq(),
    )(x, g, wq, bq, sinks, kvd, kvd, wo, bo)


def _attn_bwd(x, dy, q, o, g, wq, sinks, kvd, wo):
    S = x.shape[0]
    n = S // BLK
    all_rows = N_HEADS * BLK

    def body(x_ref, dy_ref, q_ref, o_ref, g_ref, wq_ref, sk_ref, kp_ref, kc_ref, wo_ref,
             dx_ref, dq_ref, h_ref, dc_ref, dp_ref, dbq_ref, dbo_ref, dg_ref, dsk_ref, win, dob, qs, dos, pall, dsall):
        i = pl.program_id(0)

        @pl.when(i == 0)
        def _():
            dbq_ref[...] = jnp.zeros_like(dbq_ref)
            dbo_ref[...] = jnp.zeros_like(dbo_ref)
            dg_ref[...] = jnp.zeros_like(dg_ref)
            dsk_ref[...] = jnp.zeros_like(dsk_ref)

        gv = g_ref[...]
        xv = x_ref[...]
        xn, r = _rms(xv)
        h_ref[...] = (xn * gv).astype(h_ref.dtype)
        dyv = dy_ref[...]
        dbo_ref[...] += _colsum(dyv)
        dob[...] = _dot(dyv.astype(_MXU), wo_ref[...], _NT).astype(dob.dtype)
        win[0:BLK, :] = kp_ref[...]
        win[BLK:, :] = kc_ref[...]
        mask = _attn_mask(i, BLK)
        low = _low_half()
        lane = lax.broadcasted_iota(jnp.int32, (1, LANES), 1)
        for pr in range(N_HEADS // 2):
            kh = (2 * pr) // GROUP
            kd = win[:, kh * PAIR:(kh + 1) * PAIR]
            vd = win[:, (N_KV + kh) * PAIR:(N_KV + kh + 1) * PAIR]
            q2 = q_ref[:, pr * PAIR:(pr + 1) * PAIR]
            do2 = dob[:, pr * PAIR:(pr + 1) * PAIR]
            od = do2.astype(_F32) * o_ref[:, pr * PAIR:(pr + 1) * PAIR].astype(_F32)
            for half in range(2):
                hd = 2 * pr + half
                rows = slice(hd * BLK, (hd + 1) * BLK)
                sel = low if half == 0 else ~low
                qm = jnp.where(sel, q2, jnp.zeros_like(q2))
                dom = jnp.where(sel, do2, jnp.zeros_like(do2))
                qs[rows, :] = qm
                dos[rows, :] = dom
                pbs, ps = _head_probs(qm, kd, mask, sk_ref[hd])
                pall[rows, :] = pbs.astype(_MXU)
                delta = jnp.sum(jnp.where(sel, od, 0.0), axis=-1, keepdims=True)
                dsall[rows, :] = (pbs * (_dot(dom, vd, _NT) - delta) * (HEAD_DIM ** -0.5)).astype(_MXU)
                dsk_ref[...] -= jnp.where(lane == hd, _colsum(ps * delta), 0.0)
        dq_all = []
        for kh in range(N_KV):
            ks = slice(kh * PAIR, (kh + 1) * PAIR)
            vs = slice((N_KV + kh) * PAIR, (N_KV + kh + 1) * PAIR)
            rows = slice(kh * STACK, (kh + 1) * STACK)
            dqst = _dot(dsall[rows, :], win[:, ks])
            dk = _dot(dsall[rows, :], qs[rows, :], _TN)
            dv = _dot(pall[rows, :], dos[rows, :], _TN)
            dp_ref[:, ks], dc_ref[:, ks] = dk[0:BLK, :], dk[BLK:, :]
            dp_ref[:, vs], dc_ref[:, vs] = dv[0:BLK, :], dv[BLK:, :]
            dq_all += [_unstack_heads(dqst, pp) for pp in range(GROUP // 2)]
        dq = jnp.concatenate(dq_all, axis=1)
        dbq_ref[...] += _colsum(dq)
        dqb = dq.astype(_MXU)
        dq_ref[...] = dqb
        dh = _dot(dqb, wq_ref[...], _NT)
        dg_ref[...] += _colsum(dh * xn)
        dx_ref[...] = dyv + _rms_bwd(dh, xn, r, gv)

    blk = lambda w: pl.BlockSpec((BLK, w), lambda i: (i, 0))
    return pl.pallas_call(
        body, name="attn_bwd", grid=(n,),
        in_specs=[blk(D), blk(D), blk(D), blk(D), _full((1, D)), _full((D, D)),
                  pl.BlockSpec(memory_space=pltpu.SMEM),
                  pl.BlockSpec((BLK, KVD), lambda i: (jnp.maximum(i - 1, 0), 0)), blk(KVD), _full((D, D))],
        out_specs=[blk(D), blk(D), blk(D), blk(KVD), blk(KVD),
                   _full((1, D)), _full((1, D)), _full((1, D)), _full((1, LANES))],
        out_shape=[jax.ShapeDtypeStruct((S, D), _F32), jax.ShapeDtypeStruct((S, D), _MXU),
                   jax.ShapeDtypeStruct((S, D), _MXU), jax.ShapeDtypeStruct((S, KVD), _F32),
                   jax.ShapeDtypeStruct((S, KVD), _F32), jax.ShapeDtypeStruct((1, D), _F32),
                   jax.ShapeDtypeStruct((1, D), _F32), jax.ShapeDtypeStruct((1, D), _F32),
                   jax.ShapeDtypeStruct((1, LANES), _F32)],
        scratch_shapes=[pltpu.VMEM((2 * BLK, KVD), _MXU), pltpu.VMEM((BLK, D), _MXU),
                        pltpu.VMEM((all_rows, PAIR), _MXU), pltpu.VMEM((all_rows, PAIR), _MXU),
                        pltpu.VMEM((all_rows, 2 * BLK), _MXU), pltpu.VMEM((all_rows, 2 * BLK), _MXU)],
        compiler_params=_seq(),
    )(x, dy, q, o, g, wq, sinks, kvd, kvd, wo)


def _loss_bwd(x, g, tgt):
    S = x.shape[0]
    T = _tile(S, 512, 8)

    def body(x_ref, g_ref, t_ref, dx_ref, ls_ref, dg_ref):
        @pl.when(pl.program_id(0) == 0)
        def _():
            ls_ref[...] = jnp.zeros_like(ls_ref)
            dg_ref[...] = jnp.zeros_like(dg_ref)

        gv = g_ref[...]
        xn, r = _rms(x_ref[...])
        err = xn * gv - t_ref[...]
        ls_ref[...] += 0.5 * jnp.sum(jnp.mean(err * err, axis=-1, keepdims=True))
        dyv = err * (1.0 / D)
        dg_ref[...] += _colsum(dyv * xn)
        dx_ref[...] = _rms_bwd(dyv, xn, r, gv)

    return pl.pallas_call(
        body, name="loss_bwd", grid=(S // T,),
        in_specs=[pl.BlockSpec((T, D), lambda i: (i, 0)), _full((1, D)), pl.BlockSpec((T, D), lambda i: (i, 0))],
        out_specs=[pl.BlockSpec((T, D), lambda i: (i, 0)), _full((8, LANES)), _full((1, D))],
        out_shape=[jax.ShapeDtypeStruct((S, D), _F32), jax.ShapeDtypeStruct((8, LANES), _F32),
                   jax.ShapeDtypeStruct((1, D), _F32)],
        compiler_params=_seq(),
    )(x, g, tgt)


def _me():
    return 4 * lax.axis_index("x") + 2 * lax.axis_index("y") + lax.axis_index("c")


def _peer(j):
    x, y, c = lax.axis_index("x"), lax.axis_index("y"), lax.axis_index("c")
    px = 1 - x if j & 4 else x
    py = 1 - y if j & 2 else y
    pc = 1 - c if j & 1 else c
    return (px, py, pc), 4 * px + 2 * py + pc


_HBM = pl.BlockSpec(memory_space=pltpu.HBM)
_SEMS = pl.BlockSpec(memory_space=pltpu.SEMAPHORE)
_EFFECT = pltpu.SideEffectType.DATAFLOW_SIDE_EFFECTING


def _in_hbm(a):
    return pltpu.with_memory_space_constraint(a, pltpu.HBM)


def _start_copies(name, groups):
    flat = []
    for srcs, zones, _ in groups:
        flat += [_in_hbm(a) for a in srcs] + [_in_hbm(lax.empty(z.shape, z.dtype)) for z in zones]
    n_in, n_g = len(flat), len(groups)

    def body(*refs):
        sems = refs[n_in:n_in + 2 * n_g]
        me, k = _me(), 0
        for gi, (srcs, zones, plan) in enumerate(groups):
            src_refs, zone_refs = refs[k:k + len(srcs)], refs[k + len(srcs):k + len(srcs) + len(zones)]
            k += len(srcs) + len(zones)
            for t, (si, zi, src_of, dst_of) in enumerate(plan):
                for j in range(1, N_DEV):
                    dev, pk = _peer(j)
                    pltpu.make_async_remote_copy(
                        src_ref=src_of(src_refs[si], pk), dst_ref=dst_of(zone_refs[zi], me),
                        send_sem=sems[2 * gi].at[t * (N_DEV - 1) + j - 1], recv_sem=sems[2 * gi + 1].at[t * (N_DEV - 1) + j - 1],
                        device_id=dev, device_id_type=pl.DeviceIdType.MESH).start()
                pltpu.make_async_copy(src_of(src_refs[si], me), dst_of(zone_refs[zi], me),
                                      sems[2 * gi].at[len(plan) * (N_DEV - 1) + t]).start()
        refs[-1][...] = jnp.zeros_like(refs[-1])

    sem_shapes = []
    for _, _, plan in groups:
        sem_shapes += [pltpu.SemaphoreType.DMA((len(plan) * N_DEV,)), pltpu.SemaphoreType.DMA((len(plan) * (N_DEV - 1),))]
    outs = pl.pallas_call(
        body, name=name,
        out_shape=(*sem_shapes, *[pltpu.HBM(a.shape, a.dtype) for a in flat], jax.ShapeDtypeStruct((8, LANES), _F32)),
        in_specs=[_HBM] * n_in,
        out_specs=(*[_SEMS] * (2 * n_g), *[_HBM] * n_in, pl.BlockSpec(memory_space=pltpu.VMEM)),
        input_output_aliases={k: 2 * n_g + k for k in range(n_in)},
        compiler_params=pltpu.CompilerParams(has_side_effects=_EFFECT),
    )(*flat)
    handles, k = [], 2 * n_g
    for gi, (srcs, zones, plan) in enumerate(groups):
        ns, nz = len(srcs), len(zones)
        handles.append((outs[2 * gi], outs[2 * gi + 1], list(outs[k:k + ns]), list(outs[k + ns:k + ns + nz]), plan))
        k += ns + nz
    return handles, outs[-1]


def _wait_copies(name, handles, after):
    flat = []
    for _, _, srcs, zones, _ in handles:
        flat += srcs + zones
    n_in, n_g = len(flat), len(handles)

    def body(*refs):
        sems = refs[n_in:n_in + 2 * n_g]
        me, k, local, remote = _me(), 0, [], []
        for gi, (_, _, srcs, zones, plan) in enumerate(handles):
            ns, nz = len(srcs), len(zones)
            src_refs, zone_refs = refs[k:k + ns], refs[k + ns:k + ns + nz]
            k += ns + nz
            for t, (si, zi, src_of, dst_of) in enumerate(plan):
                local.append(pltpu.make_async_copy(src_of(src_refs[si], me), dst_of(zone_refs[zi], me),
                                                   sems[2 * gi].at[len(plan) * (N_DEV - 1) + t]))
                for j in range(1, N_DEV):
                    dev, pk = _peer(j)
                    remote.append(pltpu.make_async_remote_copy(
                        src_ref=src_of(src_refs[si], pk), dst_ref=dst_of(zone_refs[zi], pk),
                        send_sem=sems[2 * gi].at[t * (N_DEV - 1) + j - 1], recv_sem=sems[2 * gi + 1].at[t * (N_DEV - 1) + j - 1],
                        device_id=dev, device_id_type=pl.DeviceIdType.MESH))
        for cp in remote:
            cp.wait_send()
            cp.wait_recv()
        for cp in local:
            cp.wait()

    sem_args = []
    for send, recv, _, _, _ in handles:
        sem_args += [send, recv]
    outs = pl.pallas_call(
        body, name=name, out_shape=tuple(pltpu.HBM(a.shape, a.dtype) for a in flat),
        in_specs=[_HBM] * n_in + [_SEMS] * (2 * n_g) + [pl.BlockSpec(memory_space=pl.ANY)],
        out_specs=tuple([_HBM] * n_in), input_output_aliases={k: k for k in range(n_in)},
        compiler_params=pltpu.CompilerParams(has_side_effects=_EFFECT),
    )(*flat, *sem_args, after)
    res, k = [], 0
    for _, _, srcs, zones, _ in handles:
        res.append(list(outs[k + len(srcs):k + len(srcs) + len(zones)]))
        k += len(srcs) + len(zones)
    return res


def _rows(axis, size):
    def of(ref, b):
        start = b * size
        if size % 8 == 0:
            start = pl.multiple_of(start, 8)
        return ref.at[(slice(None),) * axis + (pl.ds(start, size),)]
    return of


def _whole(ref, b):
    return ref


def _slot(ref, b):
    return ref.at[b]


def _gather_group(shards):
    zones, plan = [], []
    for k, (a, axis) in enumerate(shards):
        zones.append(jax.ShapeDtypeStruct(a.shape[:axis] + (N_DEV * a.shape[axis],) + a.shape[axis + 1:], a.dtype))
        plan.append((k, k, _whole, _rows(axis, a.shape[axis])))
    return [a for a, _ in shards], zones, plan


def _scatter_group(grads):
    zones, plan = [], []
    for k, (a, axis) in enumerate(grads):
        size = a.shape[axis] // N_DEV
        zones.append(jax.ShapeDtypeStruct((N_DEV,) + a.shape[:axis] + (size,) + a.shape[axis + 1:], a.dtype))
        plan.append((k, k, _rows(axis, size), _slot))
    return [a for a, _ in grads], zones, plan


def _all_reduce_small(p):
    R = p.shape[0]

    def body(p_ref, o_ref, land, send_sems, recv_sems):
        me = _me()
        land[me] = p_ref[...]
        waits = []
        for j in range(1, N_DEV):
            dev, pk = _peer(j)
            pltpu.make_async_remote_copy(
                src_ref=p_ref, dst_ref=land.at[me], send_sem=send_sems.at[j - 1], recv_sem=recv_sems.at[j - 1],
                device_id=dev, device_id_type=pl.DeviceIdType.MESH).start()
            waits.append(pltpu.make_async_remote_copy(
                src_ref=p_ref, dst_ref=land.at[pk], send_sem=send_sems.at[j - 1], recv_sem=recv_sems.at[j - 1],
                device_id=dev, device_id_type=pl.DeviceIdType.MESH))
        for cp in waits:
            cp.wait()
        tot = land[0]
        for b in range(1, N_DEV):
            tot = tot + land[b]
        o_ref[...] = tot

    vmem = pl.BlockSpec(memory_space=pltpu.VMEM)
    return pl.pallas_call(
        body, name="all_reduce_small", in_specs=[vmem], out_specs=vmem,
        out_shape=jax.ShapeDtypeStruct((R, LANES), _F32),
        scratch_shapes=[pltpu.VMEM((N_DEV, R, LANES), _F32), pltpu.SemaphoreType.DMA((N_DEV - 1,)),
                        pltpu.SemaphoreType.DMA((N_DEV - 1,))],
        compiler_params=_params(),
    )(p)


def _sum_landed(land):
    g = land[0].astype(_F32)
    for b in range(1, N_DEV):
        g = g + land[b].astype(_F32)
    return g


def _landed_specs(n_layers, tr, C, nr):
    def spec(k):
        return pl.BlockSpec((N_DEV, tr, C), lambda l, i: (0, jnp.where(l == k, i, jnp.where(l < k, 0, nr - 1)), 0))
    return [spec(k) for k in range(n_layers)]


def _per_layer(l, zone_refs, fn):
    for k, ref in enumerate(zone_refs):
        @pl.when(l == k)
        def _(ref=ref):
            fn(_sum_landed(ref))


def _sum8(zones):
    L = len(zones)
    _, R, C = zones[0].shape
    tr = _tile(R, 352, 16)
    nr = R // tr

    def body(*refs):
        o_ref = refs[L]

        def put(g):
            o_ref[...] = g

        _per_layer(pl.program_id(0), refs[:L], put)

    return pl.pallas_call(
        body, name="sum8", grid=(L, nr), in_specs=_landed_specs(L, tr, C, nr),
        out_specs=pl.BlockSpec((tr, C), lambda l, i: (l * nr + i, 0)),
        out_shape=jax.ShapeDtypeStruct((L * R, C), _F32), compiler_params=_seq(2),
    )(*zones)


def _adam_update(gv, w_ref, m_ref, v_ref, d_ref, mo_ref, vo_ref):
    mn = ADAM_B1 * m_ref[...] + (1.0 - ADAM_B1) * gv
    vn = ADAM_B2 * v_ref[...] + (1.0 - ADAM_B2) * (gv * gv)
    mo_ref[...] = mn
    vo_ref[...] = vn
    d_ref[...] = -ADAM_LR * ((mn / (1.0 - ADAM_B1 ** ADAM_STEP)) / (jnp.sqrt(vn / (1.0 - ADAM_B2 ** ADAM_STEP)) + ADAM_EPS)
                             + ADAM_WD * w_ref[...])


def _earlier(outs):
    outs = list(outs or [])
    return outs, [pl.BlockSpec(memory_space=pl.ANY)] * len(outs)


def _adamw(g, w, m, v, name, first_row=0, earlier=None):
    Rg, C = g.shape
    R = w.shape[0]
    tr = _tile(Rg if first_row == 0 else min(Rg, first_row), 256, 16)
    off = first_row // tr
    more, more_specs = _earlier(earlier)

    def body(g_ref, w_ref, m_ref, v_ref, *rest):
        go_ref, d_ref, mo_ref, vo_ref = rest[-4:]
        gv = g_ref[...]
        go_ref[...] = gv
        _adam_update(gv, w_ref, m_ref, v_ref, d_ref, mo_ref, vo_ref)

    row = pl.BlockSpec((tr, C), lambda i: (i + off, 0))
    return pl.pallas_call(
        body, name=name, grid=(Rg // tr,), in_specs=[pl.BlockSpec((tr, C), lambda i: (i, 0))] + [row] * 3 + more_specs,
        out_specs=[row] * 4, out_shape=[jax.ShapeDtypeStruct((R, C), _F32)] * 4,
        input_output_aliases={4 + k: k for k in range(len(more))}, compiler_params=_seq(),
    )(g, w, m, v, *more)


def _adamw_landed(zones, w, m, v, name, first_layer=0, earlier=None):
    L = len(zones)
    _, R, C = zones[0].shape
    tr = _tile(R, 176, 16)
    nr = R // tr
    more, more_specs = _earlier(earlier)

    def body(*refs):
        w_ref, m_ref, v_ref = refs[L:L + 3]
        g_ref, d_ref, mo_ref, vo_ref = refs[-4:]

        def update(g):
            g_ref[...] = g
            _adam_update(g, w_ref, m_ref, v_ref, d_ref, mo_ref, vo_ref)

        _per_layer(pl.program_id(0), refs[:L], update)

    row = pl.BlockSpec((tr, C), lambda l, i: ((l + first_layer) * nr + i, 0))
    return pl.pallas_call(
        body, name=name, grid=(L, nr), in_specs=_landed_specs(L, tr, C, nr) + [row] * 3 + more_specs,
        out_specs=[row] * 4, out_shape=[jax.ShapeDtypeStruct(w.shape, _F32)] * 4,
        input_output_aliases={L + 3 + k: k for k in range(len(more))}, compiler_params=_seq(2),
    )(*zones, w, m, v, *more)


def _pack(parts):
    flat = jnp.concatenate([p.reshape(-1).astype(_F32) for p in parts])
    n = flat.shape[0]
    rows = -(-n // (8 * LANES)) * 8
    return jnp.pad(flat, (0, rows * LANES - n)).reshape(rows, LANES)


def _unpack(packed, shapes):
    flat, out, k = packed.reshape(-1), [], 0
    for s in shapes:
        n = 1
        for d in s:
            n *= d
        out.append(flat[k:k + n].reshape(s))
        k += n
    return out


def kernel(x, norm1_g, norm2_g, pool_w, pool_scale, kv_norm_g, w_kv, b_kv, w_q, b_q, sinks, w_o, b_o, ffn_up, ffn_conv_w, ffn_conv_b, ffn_down, final_g, loss_target, m_norm1_g, m_norm2_g, m_pool_w, m_pool_scale, m_kv_norm_g, m_w_kv, m_b_kv, m_w_q, m_b_q, m_sinks, m_w_o, m_b_o, m_ffn_up, m_ffn_conv_w, m_ffn_conv_b, m_ffn_down, m_final_g, v_norm1_g, v_norm2_g, v_pool_w, v_pool_scale, v_kv_norm_g, v_w_kv, v_b_kv, v_w_q, v_b_q, v_sinks, v_w_o, v_b_o, v_ffn_up, v_ffn_conv_w, v_ffn_conv_b, v_ffn_down, v_final_g):
    S = x.shape[1]
    F2s = ffn_up.shape[2]
    F2 = N_DEV * F2s
    me = _me()
    x0 = x.reshape(S, D)
    tgt = loss_target.reshape(S, D)
    row = lambda a: a.reshape(1, -1)

    small = _pack([pool_scale, ffn_conv_w])
    wire = lambda a: a.astype(_MXU)
    ffn_w = lambda l: [(wire(ffn_up[l]).T, 0), (wire(ffn_down[l]), 0)]
    attn_w = lambda j: [(wire(w_q[j]), 0), (wire(w_o[j]), 0)]
    gathers, token = _start_copies("gather_start", [_gather_group(g) for g in (
        [(wire(pool_w[0]), 1), (small[None], 0)], ffn_w(0), [(wire(pool_w[1]), 1)] + ffn_w(1),
        [(wire(w_kv), 0)] + attn_w(0), ffn_w(2), attn_w(1), ffn_w(3))])

    def gathered(k, after):
        return _wait_copies("gather_wait_%d" % k, [gathers[k]], after)[0]

    pw, up_t, down, wq, wo = [None] * N_A, [None] * DEPTH, [None] * DEPTH, [None] * 2, [None] * 2
    pw[0], small_all = gathered(0, token)
    n_ps = pool_scale.size
    small_all = small_all.reshape(N_DEV, -1)
    pscale = jnp.transpose(small_all[:, :n_ps].reshape(N_DEV, N_A, D // N_DEV), (1, 0, 2)).reshape(N_A, D)
    conv_w = jnp.transpose(small_all[:, n_ps:n_ps + ffn_conv_w.size].reshape(N_DEV, DEPTH, CONV_W, F2s),
                           (1, 2, 0, 3)).reshape(DEPTH, CONV_W, F2)

    def dup(a):
        a4 = a.reshape(a.shape[:-1] + (2 * N_KV, 1, HEAD_DIM))
        return jnp.broadcast_to(a4, a.shape[:-1] + (2 * N_KV, 2, HEAD_DIM)).reshape(a.shape[:-1] + (KVD,))

    def fold(a):
        return a.reshape(a.shape[:-1] + (2 * N_KV, 2, HEAD_DIM)).sum(axis=-2).reshape(a.shape[:-1] + (2 * N_KV * HEAD_DIM,))

    xs, us, qs, os_ = [x0], [], [], []
    xc = x0
    kvd = None
    for l in range(DEPTH):
        if l == 1:
            pw[1], up_t[1], down[1] = gathered(2, xc)
        if l == 3:
            wq[1], wo[1] = gathered(5, xc)
        if l < N_A:
            xc = _pool_fwd(xc, row(norm1_g[l]), pw[l], row(pscale[l]))
        else:
            j = l - N_A
            xc, q, o = _attn_fwd(xc, row(norm1_g[l]), wq[j], row(b_q[j]), sinks[j], kvd, wo[j], row(b_o[j]))
            qs.append(q)
            os_.append(o)
        xs.append(xc)
        if l != 1:
            up_t[l], down[l] = gathered((1, None, 4, 6)[l], xc)
        xc, u, c = _ffn_fwd(xc, row(norm2_g[l]), up_t[l], conv_w[l], row(ffn_conv_b[l]), down[l])
        us.append((u, c))
        xs.append(xc)
        if l == N_A - 1:
            wkv, wq[0], wo[0] = gathered(3, xc)
            wkv_d, bkv_d = dup(wkv), dup(row(b_kv))
            kvd = _kv_fwd(xc, row(kv_norm_g), wkv_d, bkv_d)

    dx, loss_p, d_final = _loss_bwd(xc, row(final_g), tgt)
    d_n1, d_n2, d_cw, d_cb = [None] * DEPTH, [None] * DEPTH, [None] * DEPTH, [None] * DEPTH
    d_bq, d_bo, d_sk, d_ps, dkv_parts = [None] * 2, [None] * 2, [None] * 2, [None] * N_A, []
    up_z, down_z, mix_z = [None] * DEPTH, [None] * DEPTH, [None] * DEPTH
    token = None

    def after(gain):
        return gain if token is None else gain + token[0:1, 0:1]

    rep_names = ["norm1_g", "norm2_g", "kv_norm_g", "b_kv", "b_q", "sinks", "b_o", "ffn_conv_b", "final_g"]

    def small_parts(n1_rest, ps_rest):
        zero = jnp.zeros((1, D), _F32)
        return [jnp.concatenate([zero] + n1_rest), jnp.concatenate(d_n2), d_kvg, fold(d_bkv), jnp.concatenate(d_bq),
                jnp.concatenate([s[:, :N_HEADS] for s in d_sk]), jnp.concatenate(d_bo), jnp.concatenate(d_cb), d_final,
                jnp.concatenate([zero] + ps_rest), jnp.stack(d_cw), loss_p[0:1, 0:1]]

    for l in reversed(range(DEPTH)):
        x_in, x_mid, x_out = xs[2 * l], xs[2 * l + 1], xs[2 * l + 2]
        mixer_grads = []
        if l == N_A - 1:
            dx, d_wkv, d_bkv, d_kvg = _kv_bwd(x_out, dx, after(row(kv_norm_g)), wkv_d, *dkv_parts)
            mixer_grads.append((fold(d_wkv).astype(_MXU), 0))
        dy = dx
        dx, du, a, h, d_cw[l], d_cb[l], d_n2[l] = _ffn_bwd(
            x_mid, dy, *us[l], after(row(norm2_g[l])), up_t[l], conv_w[l], down[l])
        g_up = _tn_matmul(du, h, "tn_up")
        if l > 0:
            (up_z[l], down_z[l]), token = _start_copies("scatter_ffn_%d" % l, [
                _scatter_group([(g_up, 0)]), _scatter_group([(_tn_matmul(a, dy, "tn_down"), 0)])])
        else:
            (small_handle, up_z[l]), token = _start_copies("scatter_up_0", [
                _gather_group([(_pack(small_parts(d_n1[1:], d_ps[1:]))[None], 0)]), _scatter_group([(g_up, 0)])])
            (down_z[l],), token = _start_copies("scatter_down_0", [
                _scatter_group([(_tn_matmul(a, dy, "tn_down", token), 0)])])
        dy = dx
        if l < N_A:
            dx, d_pw, d_ps[l], d_n1[l] = _pool_bwd(x_in, dy, after(row(norm1_g[l])), pw[l], row(pscale[l]))
            mixer_grads.append((d_pw.astype(_MXU), 1))
        else:
            j = l - N_A
            dx, dq, h, d_cur, d_prev, d_bq[j], d_bo[j], d_n1[l], d_sk[j] = _attn_bwd(
                x_in, dy, qs[j], os_[j], after(row(norm1_g[l])), wq[j], sinks[j], kvd, wo[j])
            mixer_grads += [(_tn_matmul(h, dq, "tn_q"), 0), (_tn_matmul(os_[j], dy, "tn_o"), 0)]
            dkv_parts += [d_cur, d_prev]
        (mix_z[l],), token = _start_copies("scatter_mixer_%d" % l, [_scatter_group(mixer_grads)])

    moms = dict(norm1_g=(m_norm1_g, v_norm1_g), norm2_g=(m_norm2_g, v_norm2_g), pool_w=(m_pool_w, v_pool_w),
                pool_scale=(m_pool_scale, v_pool_scale), kv_norm_g=(m_kv_norm_g, v_kv_norm_g), w_kv=(m_w_kv, v_w_kv),
                b_kv=(m_b_kv, v_b_kv), w_q=(m_w_q, v_w_q), b_q=(m_b_q, v_b_q), sinks=(m_sinks, v_sinks),
                w_o=(m_w_o, v_w_o), b_o=(m_b_o, v_b_o), ffn_up=(m_ffn_up, v_ffn_up),
                ffn_conv_w=(m_ffn_conv_w, v_ffn_conv_w), ffn_conv_b=(m_ffn_conv_b, v_ffn_conv_b),
                ffn_down=(m_ffn_down, v_ffn_down), final_g=(m_final_g, v_final_g))
    given = dict(norm1_g=norm1_g, norm2_g=norm2_g, kv_norm_g=kv_norm_g, b_kv=b_kv, b_q=b_q, sinks=sinks, b_o=b_o,
                 ffn_conv_b=ffn_conv_b, final_g=final_g, pool_scale=pool_scale, ffn_conv_w=ffn_conv_w,
                 pool_w=pool_w, w_kv=w_kv, w_q=w_q, w_o=w_o, ffn_up=ffn_up, ffn_down=ffn_down)
    grad, delta, new_m, new_v = {}, {}, {}, {}

    def update(name, zones, cols, first_layer=0, earlier=None, last=True):
        w = given[name]
        two_d = lambda a: a.reshape(-1, cols)
        wmv = (two_d(w), two_d(moms[name][0]), two_d(moms[name][1]))
        if name == "ffn_up":
            g = jnp.swapaxes(_sum8(zones).reshape(len(zones), F2s, D), 1, 2).reshape(-1, F2s)
            outs = _adamw(g, *wmv, "adamw_" + name, first_layer * D, earlier)
        else:
            landed = [z.reshape(N_DEV, -1, cols) for z in zones]
            outs = _adamw_landed(landed, *wmv, "adamw_" + name, first_layer, earlier)
        if last:
            grad[name], delta[name], new_m[name], new_v[name] = (o.reshape(w.shape) for o in outs)
        return outs

    rest = _wait_copies("scatter_wait_rest", up_z[1:] + down_z[1:] + mix_z[1:], token)
    up_r, down_r, mix_r = rest[:DEPTH - 1], rest[DEPTH - 1:2 * (DEPTH - 1)], rest[2 * (DEPTH - 1):]
    up_1 = update("ffn_up", [z[0] for z in up_r], F2s, 1, last=False)
    down_1 = update("ffn_down", [z[0] for z in down_r], D, 1, last=False)
    update("w_q", [mix_r[N_A - 1][0], mix_r[N_A][0]], D)
    update("w_o", [mix_r[N_A - 1][1], mix_r[N_A][1]], D)
    update("w_kv", [mix_r[N_A - 2][0]], w_kv.shape[1])
    pw_1 = update("pool_w", [mix_r[N_A - 2][1]], GC, 1, last=False)

    late_in, _ = lax.optimization_barrier((_pack([d_n1[0], d_ps[0]]), down_1[1]))
    late = _all_reduce_small(late_in).reshape(-1)
    (early,), = _wait_copies("small_wait", [small_handle], late)
    tot = _unpack(_sum8([early]), [given[k].shape for k in rep_names] + [(N_A, D), (DEPTH, CONV_W, F2), ()])
    tot[0] = tot[0].at[0].add(late[:D])
    tot[-3] = tot[-3].at[0].add(late[D:2 * D])
    grad.update(zip(rep_names, tot))
    loss = tot[-1]
    grad["pool_scale"] = lax.dynamic_slice_in_dim(tot[-3], me * (D // N_DEV), D // N_DEV, axis=1)
    grad["ffn_conv_w"] = lax.dynamic_slice_in_dim(tot[-2], me * F2s, F2s, axis=2)
    small_names = rep_names + ["pool_scale", "ffn_conv_w"]
    shapes = [given[k].shape for k in small_names]
    outs = _adamw(_pack([grad[k] for k in small_names]), _pack([given[k] for k in small_names]),
                  _pack([moms[k][0] for k in small_names]), _pack([moms[k][1] for k in small_names]), "adamw_small")
    for dst, packed in zip((delta, new_m, new_v), outs[1:]):
        dst.update(zip(small_names, _unpack(packed, shapes)))

    (up_0,), (down_0,), (pw_0,) = _wait_copies("scatter_wait_0", [up_z[0], down_z[0], mix_z[0]], outs[1])
    update("ffn_up", [up_0], F2s, 0, up_1)
    update("ffn_down", [down_0], D, 0, down_1)
    update("pool_w", [pw_0], GC, 0, pw_1)

    names = ["norm1_g", "norm2_g", "pool_w", "pool_scale", "kv_norm_g", "w_kv", "b_kv", "w_q", "b_q", "sinks", "w_o",
             "b_o", "ffn_up", "ffn_conv_w", "ffn_conv_b", "ffn_down", "final_g"]
    return (loss, dx.reshape(x.shape), *[grad[k] for k in names], *[delta[k] for k in names],
            *[new_m[k] for k in names], *[new_v[k] for k in names])
```

```python
import jax
import jax.numpy as jnp
from jax import lax
from jax.experimental import pallas as pl
from jax.experimental.pallas import tpu as pltpu

_F32 = jnp.float32
_MXU = jnp.bfloat16

N_DEV = 8
D = 1024
DEPTH = 4
N_A = 2
POOL_WINDOWS = (2, 4, 8, 16)
GC = D // len(POOL_WINDOWS)
HALO = 16
HEAD_DIM = 64
N_HEADS = D // HEAD_DIM
GROUP = 8
N_KV = N_HEADS // GROUP
BLK = 128
PAIR = 2 * HEAD_DIM
KVD = 4 * N_KV * HEAD_DIM
CONV_W = 3
EPS = 1e-5
NEG = -1e30

ADAM_LR = 0.001
ADAM_B1 = 0.9
ADAM_B2 = 0.999
ADAM_EPS = 1e-08
ADAM_WD = 0.01
ADAM_STEP = 10

V7X_VMEM_LIMIT = 56 * 1024 * 1024
LANES = 128

_NT = (((1,), (1,)), ((), ()))
_TN = (((0,), (0,)), ((), ()))


def _params(**kw):
    return pltpu.CompilerParams(vmem_limit_bytes=V7X_VMEM_LIMIT, **kw)


def _seq(n=1):
    return _params(dimension_semantics=("arbitrary",) * n)


def _dot(a, b, dims=None):
    if dims is None:
        return jnp.dot(a, b, preferred_element_type=_F32)
    return lax.dot_general(a, b, dims, preferred_element_type=_F32)


def _rms(x):
    r = lax.rsqrt(jnp.mean(x * x, axis=-1, keepdims=True) + EPS)
    return x * r, r


def _rms_bwd(dh, xn, r, g):
    dxn = dh * g
    return r * (dxn - xn * jnp.mean(dxn * xn, axis=-1, keepdims=True))


def _colsum(a):
    return jnp.sum(a, axis=0, keepdims=True)


def _tile(n, want, mult=8):
    for t in range(min(want, n), 0, -1):
        if n % t == 0 and t % mult == 0:
            return t
    return n


def _full(shape):
    zeros = (0,) * len(shape)
    return pl.BlockSpec(shape, lambda *_: zeros)


def _window_sum(ext, win, trailing):
    R = ext.shape[0]
    acc, k = ext, 1
    while k < win:
        acc = acc + pltpu.roll(acc, k if trailing else R - k, axis=0)
        k *= 2
    return acc


def _pool_windows(hbuf, h, row, T):
    out = []
    for gi, win in enumerate(POOL_WINDOWS):
        cs = slice(gi * GC, (gi + 1) * GC)
        acc = _window_sum(hbuf[:, cs], win, True)[HALO:, :]
        cnt = jnp.minimum(row + 1, win).astype(_F32)
        out.append((acc / cnt - h[:, cs], cnt))
    return out


def _pool_fwd(x, g, w, sc):
    S = x.shape[0]
    T = _tile(S, 512, HALO)
    n, hb = S // T, T // HALO

    def body(x_ref, xh_ref, g_ref, w_ref, sc_ref, o_ref, hbuf):
        i = pl.program_id(0)
        gv = g_ref[...]
        xv = x_ref[...]
        h = _rms(xv)[0] * gv
        hbuf[0:HALO, :] = jnp.where(i > 0, _rms(xh_ref[...])[0] * gv, 0.0)
        hbuf[HALO:, :] = h
        row = i * T + lax.broadcasted_iota(jnp.int32, (T, 1), 0)
        for gi, (p, _) in enumerate(_pool_windows(hbuf, h, row, T)):
            cs = slice(gi * GC, (gi + 1) * GC)
            z = _dot(p.astype(_MXU), w_ref[gi])
            o_ref[:, cs] = xv[:, cs] + z * sc_ref[:, cs]

    return pl.pallas_call(
        body, name="pool_fwd", grid=(n,),
        in_specs=[pl.BlockSpec((T, D), lambda i: (i, 0)),
                  pl.BlockSpec((HALO, D), lambda i: (jnp.maximum(i * hb - 1, 0), 0)),
                  _full((1, D)), _full((4, GC, GC)), _full((1, D))],
        out_specs=pl.BlockSpec((T, D), lambda i: (i, 0)),
        out_shape=jax.ShapeDtypeStruct((S, D), _F32),
        scratch_shapes=[pltpu.VMEM((T + HALO, D), _F32)],
        compiler_params=_seq(),
    )(x, x, g, w, sc)


def _pool_bwd(x, dy, g, w, sc):
    S = x.shape[0]
    T = _tile(S, 512, HALO)
    n, hb = S // T, T // HALO

    def body(x_ref, xh_ref, dy_ref, dyh_ref, g_ref, w_ref, sc_ref, dx_ref, dw_ref, dsc_ref, dg_ref,
             hbuf, qbuf, dhbuf):
        i = pl.program_id(0)

        @pl.when(i == 0)
        def _():
            dw_ref[...] = jnp.zeros_like(dw_ref)
            dsc_ref[...] = jnp.zeros_like(dsc_ref)
            dg_ref[...] = jnp.zeros_like(dg_ref)

        gv = g_ref[...]
        xv = x_ref[...]
        xn, r = _rms(xv)
        h = xn * gv
        hbuf[0:HALO, :] = jnp.where(i > 0, _rms(xh_ref[...])[0] * gv, 0.0)
        hbuf[HALO:, :] = h
        dyv = dy_ref[...]
        dz = dyv * sc_ref[...]
        dzh = jnp.where(i < n - 1, dyh_ref[...], 0.0) * sc_ref[...]
        row = i * T + lax.broadcasted_iota(jnp.int32, (T, 1), 0)
        rowh = (i + 1) * T + lax.broadcasted_iota(jnp.int32, (HALO, 1), 0)
        for gi, (p, cnt) in enumerate(_pool_windows(hbuf, h, row, T)):
            win = POOL_WINDOWS[gi]
            cs = slice(gi * GC, (gi + 1) * GC)
            pb = p.astype(_MXU)
            wg = w_ref[gi]
            dsc_ref[:, cs] += _colsum(dyv[:, cs] * _dot(pb, wg))
            dzb = dz[:, cs].astype(_MXU)
            dw_ref[gi] += _dot(pb, dzb, _TN)
            dp = _dot(dzb, wg, _NT)
            dph = _dot(dzh[:, cs].astype(_MXU), wg, _NT)
            qbuf[0:T, cs] = dp / cnt
            qbuf[T:T + HALO, cs] = dph / jnp.minimum(rowh + 1, win).astype(_F32)
            dhbuf[:, cs] = _window_sum(qbuf[:, cs], win, False)[0:T, :] - dp
        dh = dhbuf[...]
        dg_ref[...] += _colsum(dh * xn)
        dx_ref[...] = dyv + _rms_bwd(dh, xn, r, gv)

    return pl.pallas_call(
        body, name="pool_bwd", grid=(n,),
        in_specs=[pl.BlockSpec((T, D), lambda i: (i, 0)),
                  pl.BlockSpec((HALO, D), lambda i: (jnp.maximum(i * hb - 1, 0), 0)),
                  pl.BlockSpec((T, D), lambda i: (i, 0)),
                  pl.BlockSpec((HALO, D), lambda i: (jnp.minimum((i + 1) * hb, S // HALO - 1), 0)),
                  _full((1, D)), _full((4, GC, GC)), _full((1, D))],
        out_specs=[pl.BlockSpec((T, D), lambda i: (i, 0)), _full((4, GC, GC)), _full((1, D)), _full((1, D))],
        out_shape=[jax.ShapeDtypeStruct((S, D), _F32), jax.ShapeDtypeStruct((4, GC, GC), _F32),
                   jax.ShapeDtypeStruct((1, D), _F32), jax.ShapeDtypeStruct((1, D), _F32)],
        scratch_shapes=[pltpu.VMEM((T + HALO, D), _F32), pltpu.VMEM((T + HALO, D), _F32), pltpu.VMEM((T, D), _F32)],
        compiler_params=_seq(),
    )(x, x, dy, dy, g, w, sc)


FFN_FWD_TILE, FFN_FWD_CHUNKS = 256, 1
FFN_BWD_TILE, FFN_BWD_CHUNKS = 256, 1
EDGE = 8


def _shift_down(v, k, prev):
    r = pltpu.roll(v, k, axis=0)
    i8 = lax.broadcasted_iota(jnp.int32, (EDGE, v.shape[1]), 0)
    head = jnp.where(i8 >= k, r[0:EDGE, :], pltpu.roll(prev, k, axis=0))
    return jnp.concatenate([head, r[EDGE:, :]], axis=0)


def _shift_up(v, k, nxt):
    T = v.shape[0]
    r = pltpu.roll(v, T - k, axis=0)
    i8 = lax.broadcasted_iota(jnp.int32, (EDGE, v.shape[1]), 0)
    tail = jnp.where(i8 < EDGE - k, r[T - EDGE:, :], pltpu.roll(nxt, EDGE - k, axis=0))
    return jnp.concatenate([r[:T - EDGE, :], tail], axis=0)


def _load_weights(i, pairs, sems):
    @pl.when(i == 0)
    def _():
        cps = [pltpu.make_async_copy(src, dst, sems.at[k]) for k, (src, dst) in enumerate(pairs)]
        for cp in cps:
            cp.start()
        for cp in cps:
            cp.wait()


def _ffn_fwd(x, g, wup_t, cw, cb, wdn):
    S = x.shape[0]
    F2 = wup_t.shape[0]
    F = F2 // 2
    C = F // FFN_FWD_CHUNKS
    T = _tile(S, FFN_FWD_TILE, 16)
    n = S // T

    def body(x_ref, g_ref, wup_hbm, cw_ref, cb_ref, wdn_hbm, o_ref, u_ref, c_ref, wup, wdnv, carry, sems):
        i = pl.program_id(0)
        _load_weights(i, [(wup_hbm, wup), (wdn_hbm, wdnv)], sems)

        @pl.when(i == 0)
        def _():
            carry[...] = jnp.zeros_like(carry)

        xv = x_ref[...]
        hb = (_rms(xv)[0] * g_ref[...]).astype(_MXU)
        acc = jnp.zeros((T, D), _F32)
        for j in range(FFN_FWD_CHUNKS):
            halves = []
            for cs in (slice(j * C, (j + 1) * C), slice(F + j * C, F + (j + 1) * C)):
                u = _dot(hb, wup[cs, :], _NT)
                u_ref[:, cs] = u.astype(u_ref.dtype)
                prev = carry[:, cs]
                carry[:, cs] = u[T - EDGE:, :]
                c = (cw_ref[0:1, cs] * _shift_down(u, 2, prev) + cw_ref[1:2, cs] * _shift_down(u, 1, prev)
                     + cw_ref[2:3, cs] * u + cb_ref[:, cs])
                c_ref[:, cs] = c.astype(c_ref.dtype)
                halves.append(c)
            cg, cv = halves
            a = (cg * jax.nn.sigmoid(cg)) * cv
            acc = acc + _dot(a.astype(_MXU), wdnv[j * C:(j + 1) * C, :])
        o_ref[...] = xv + acc

    any_ = pl.BlockSpec(memory_space=pl.ANY)
    wide = pl.BlockSpec((T, F2), lambda i: (i, 0))
    return pl.pallas_call(
        body, name="ffn_fwd", grid=(n,),
        in_specs=[pl.BlockSpec((T, D), lambda i: (i, 0)), _full((1, D)), any_, _full((CONV_W, F2)), _full((1, F2)), any_],
        out_specs=[pl.BlockSpec((T, D), lambda i: (i, 0)), wide, wide],
        out_shape=[jax.ShapeDtypeStruct((S, D), _F32), jax.ShapeDtypeStruct((S, F2), _MXU),
                   jax.ShapeDtypeStruct((S, F2), _MXU)],
        scratch_shapes=[pltpu.VMEM((F2, D), _MXU), pltpu.VMEM((F, D), _MXU),
                        pltpu.VMEM((EDGE, F2), _F32), pltpu.SemaphoreType.DMA((2,))],
        compiler_params=_seq(),
    )(x, g, wup_t, cw, cb, wdn)


def _ffn_bwd(x, dy, u, c, g, wup_t, cw, wdn):
    S = x.shape[0]
    F2 = wup_t.shape[0]
    F = F2 // 2
    C = F // FFN_BWD_CHUNKS
    T = _tile(S, FFN_BWD_TILE, 16)
    n = S // T

    def body(x_ref, dy_ref, u_ref, c_ref, g_ref, wup_hbm, cw_ref, wdn_hbm,
             dx_ref, du_ref, a_ref, h_ref, dcw_ref, dcb_ref, dg_ref, wup, wdnv, carry, sems):
        i = pl.program_id(0)
        _load_weights(i, [(wup_hbm, wup), (wdn_hbm, wdnv)], sems)

        @pl.when(i == 0)
        def _():
            carry[...] = jnp.zeros_like(carry)
            dcw_ref[...] = jnp.zeros_like(dcw_ref)
            dcb_ref[...] = jnp.zeros_like(dcb_ref)
            dg_ref[...] = jnp.zeros_like(dg_ref)

        gv = g_ref[...]
        xv = x_ref[...]
        xn, r = _rms(xv)
        hbf = (xn * gv).astype(_MXU)
        h_ref[...] = hbf
        dyv = dy_ref[...]
        dyb = dyv.astype(_MXU)
        dh = jnp.zeros((T, D), _F32)
        for j in range(FFN_BWD_CHUNKS):
            gs, vs = slice(j * C, (j + 1) * C), slice(F + j * C, F + (j + 1) * C)
            cg, cv = c_ref[:, gs].astype(_F32), c_ref[:, vs].astype(_F32)
            sg = jax.nn.sigmoid(cg)
            sl = cg * sg
            a_ref[:, gs] = (sl * cv).astype(a_ref.dtype)
            da = _dot(dyb, wdnv[gs, :], _NT)
            for cs, dc in ((gs, da * cv * (sg * (1.0 + cg * (1.0 - sg)))), (vs, da * sl)):
                nxt = carry[:, cs]
                carry[:, cs] = dc[0:EDGE, :]
                dc1, dc2 = _shift_up(dc, 1, nxt), _shift_up(dc, 2, nxt)
                uf = u_ref[:, cs].astype(_F32)
                dcb_ref[:, cs] += _colsum(dc)
                for k, d in enumerate((dc2, dc1, dc)):
                    dcw_ref[k:k + 1, cs] += _colsum(d * uf)
                du = cw_ref[2:3, cs] * dc + cw_ref[1:2, cs] * dc1 + cw_ref[0:1, cs] * dc2
                dub = du.astype(_MXU)
                du_ref[:, cs] = dub
                dh = dh + _dot(dub, wup[cs, :])
        dg_ref[...] += _colsum(dh * xn)
        dx_ref[...] = dyv + _rms_bwd(dh, xn, r, gv)

    any_ = pl.BlockSpec(memory_space=pl.ANY)
    rev = lambda i: (n - 1 - i, 0)
    return pl.pallas_call(
        body, name="ffn_bwd", grid=(n,),
        in_specs=[pl.BlockSpec((T, D), rev), pl.BlockSpec((T, D), rev), pl.BlockSpec((T, F2), rev),
                  pl.BlockSpec((T, F2), rev), _full((1, D)), any_, _full((CONV_W, F2)), any_],
        out_specs=[pl.BlockSpec((T, D), rev), pl.BlockSpec((T, F2), rev), pl.BlockSpec((T, F), rev),
                   pl.BlockSpec((T, D), rev), _full((CONV_W, F2)), _full((1, F2)), _full((1, D))],
        out_shape=[jax.ShapeDtypeStruct((S, D), _F32), jax.ShapeDtypeStruct((S, F2), _MXU),
                   jax.ShapeDtypeStruct((S, F), _MXU), jax.ShapeDtypeStruct((S, D), _MXU),
                   jax.ShapeDtypeStruct((CONV_W, F2), _F32), jax.ShapeDtypeStruct((1, F2), _F32),
                   jax.ShapeDtypeStruct((1, D), _F32)],
        scratch_shapes=[pltpu.VMEM((F2, D), _MXU), pltpu.VMEM((F, D), _MXU),
                        pltpu.VMEM((EDGE, F2), _F32), pltpu.SemaphoreType.DMA((2,))],
        compiler_params=_seq(),
    )(x, dy, u, c, g, wup_t, cw, wdn)


def _tn_matmul(a, b, name, token=None):
    S, M = a.shape
    N = b.shape[1]
    bm = _tile(M, 1408, LANES)
    tk = _tile(S, 2048, 16)
    nk = S // tk
    tokens = [] if token is None else [token]

    def body(a_ref, b_ref, *rest):
        o_ref, acc = rest[-2:]
        k = pl.program_id(1)

        @pl.when(k == 0)
        def _():
            acc[...] = jnp.zeros_like(acc)

        acc[...] += _dot(a_ref[...].astype(_MXU), b_ref[...].astype(_MXU), _TN)

        @pl.when(k == nk - 1)
        def _():
            o_ref[...] = acc[...].astype(o_ref.dtype)

    return pl.pallas_call(
        body, name=name, grid=(M // bm, nk),
        in_specs=[pl.BlockSpec((tk, bm), lambda i, k: (k, i)), pl.BlockSpec((tk, N), lambda i, k: (k, 0))]
        + [_full((8, LANES))] * len(tokens),
        out_specs=pl.BlockSpec((bm, N), lambda i, k: (i, 0)),
        out_shape=jax.ShapeDtypeStruct((M, N), _MXU),
        scratch_shapes=[pltpu.VMEM((bm, N), _F32)],
        compiler_params=_seq(2),
    )(a, b, *tokens)


def _kv_fwd(x, g, wkv, bkv):
    S = x.shape[0]
    T = _tile(S, 512, 16)

    def body(x_ref, g_ref, w_ref, b_ref, o_ref):
        hb = (_rms(x_ref[...])[0] * g_ref[...]).astype(_MXU)
        o_ref[...] = (_dot(hb, w_ref[...]) + b_ref[...]).astype(o_ref.dtype)

    return pl.pallas_call(
        body, name="kv_fwd", grid=(S // T,),
        in_specs=[pl.BlockSpec((T, D), lambda i: (i, 0)), _full((1, D)), _full((D, KVD)), _full((1, KVD))],
        out_specs=pl.BlockSpec((T, KVD), lambda i: (i, 0)),
        out_shape=jax.ShapeDtypeStruct((S, KVD), _MXU),
        compiler_params=_seq(),
    )(x, g, wkv, bkv)


def _kv_bwd(x, dx_in, g, wkv, cur_a, prev_a, cur_b, prev_b):
    S = x.shape[0]
    T = _tile(S, 512, BLK)
    n, per = S // T, T // BLK

    def body(x_ref, dxi_ref, g_ref, w_ref, ca, pa, na, cb, pb, nb, dx_ref, dw_ref, db_ref, dg_ref):
        i = pl.program_id(0)

        @pl.when(i == 0)
        def _():
            dw_ref[...] = jnp.zeros_like(dw_ref)
            db_ref[...] = jnp.zeros_like(db_ref)
            dg_ref[...] = jnp.zeros_like(dg_ref)

        gv = g_ref[...]
        xn, r = _rms(x_ref[...])
        nxt = jnp.where(i < n - 1, na[...] + nb[...], 0.0)
        prev = jnp.concatenate([pa[BLK:, :] + pb[BLK:, :], nxt], axis=0) if per > 1 else nxt
        dkv = ca[...] + cb[...] + prev
        db_ref[...] += _colsum(dkv)
        dkb = dkv.astype(_MXU)
        dw_ref[...] += _dot((xn * gv).astype(_MXU), dkb, _TN)
        dh = _dot(dkb, w_ref[...], _NT)
        dg_ref[...] += _colsum(dh * xn)
        dx_ref[...] = dxi_ref[...] + _rms_bwd(dh, xn, r, gv)

    blk = lambda w: pl.BlockSpec((T, w), lambda i: (i, 0))
    nxt = pl.BlockSpec((BLK, KVD), lambda i: (jnp.minimum((i + 1) * per, S // BLK - 1), 0))
    return pl.pallas_call(
        body, name="kv_bwd", grid=(n,),
        in_specs=[blk(D), blk(D), _full((1, D)), _full((D, KVD)), blk(KVD), blk(KVD), nxt, blk(KVD), blk(KVD), nxt],
        out_specs=[blk(D), _full((D, KVD)), _full((1, KVD)), _full((1, D))],
        out_shape=[jax.ShapeDtypeStruct((S, D), _F32), jax.ShapeDtypeStruct((D, KVD), _F32),
                   jax.ShapeDtypeStruct((1, KVD), _F32), jax.ShapeDtypeStruct((1, D), _F32)],
        compiler_params=_seq(),
    )(x, dx_in, g, wkv, cur_a, prev_a, prev_a, cur_b, prev_b, prev_b)


STACK = GROUP * BLK


def _attn_mask(i, rows):
    qi = lax.broadcasted_iota(jnp.int32, (rows, 2 * BLK), 0) & (BLK - 1)
    si = lax.broadcasted_iota(jnp.int32, (rows, 2 * BLK), 1)
    return (si > qi) & (si <= qi + BLK) & jnp.logical_or(i > 0, si >= BLK)


def _low_half():
    return lax.broadcasted_iota(jnp.int32, (BLK, PAIR), 1) < HEAD_DIM


def _stack_heads(ref, kh, dst):
    low = _low_half()
    for pp in range(GROUP // 2):
        pr = kh * (GROUP // 2) + pp
        v2 = ref[:, pr * PAIR:(pr + 1) * PAIR]
        zero = jnp.zeros_like(v2)
        dst[2 * pp * BLK:(2 * pp + 1) * BLK, :] = jnp.where(low, v2, zero)
        dst[(2 * pp + 1) * BLK:(2 * pp + 2) * BLK, :] = jnp.where(low, zero, v2)


def _unstack_heads(st, pp):
    return jnp.where(_low_half(), st[2 * pp * BLK:(2 * pp + 1) * BLK, :], st[(2 * pp + 1) * BLK:(2 * pp + 2) * BLK, :])


def _sink_col(sk_ref, kh):
    return jnp.concatenate([jnp.full((BLK, 1), sk_ref[kh * GROUP + h], _F32) for h in range(GROUP)], axis=0)


def _head_probs(qm, kd, mask, sink):
    s = jnp.where(mask, _dot(qm, kd, _NT) * (HEAD_DIM ** -0.5), NEG)
    m = jnp.maximum(jnp.max(s, axis=-1, keepdims=True), sink)
    p = jnp.exp(s - m)
    es = jnp.exp(sink - m)
    inv = 1.0 / (jnp.sum(p, axis=-1, keepdims=True) + es)
    return p * inv, es * inv


def _attn_fwd(x, g, wq, bq, sinks, kvd, wo, bo):
    S = x.shape[0]
    n = S // BLK

    def body(x_ref, g_ref, wq_ref, bq_ref, sk_ref, kp_ref, kc_ref, wo_ref, bo_ref, xo_ref, q_ref, o_ref, win):
        i = pl.program_id(0)
        xv = x_ref[...]
        hb = (_rms(xv)[0] * g_ref[...]).astype(_MXU)
        q_ref[...] = (_dot(hb, wq_ref[...]) + bq_ref[...]).astype(q_ref.dtype)
        win[0:BLK, :] = kp_ref[...]
        win[BLK:, :] = kc_ref[...]
        mask = _attn_mask(i, BLK)
        low = _low_half()
        for pr in range(N_HEADS // 2):
            kh = (2 * pr) // GROUP
            kd = win[:, kh * PAIR:(kh + 1) * PAIR]
            vd = win[:, (N_KV + kh) * PAIR:(N_KV + kh + 1) * PAIR]
            q2 = q_ref[:, pr * PAIR:(pr + 1) * PAIR]
            outs = []
            for half in range(2):
                qm = jnp.where(low if half == 0 else ~low, q2, jnp.zeros_like(q2))
                pbs, _ = _head_probs(qm, kd, mask, sk_ref[2 * pr + half])
                outs.append(_dot(pbs.astype(_MXU), vd))
            o_ref[:, pr * PAIR:(pr + 1) * PAIR] = jnp.where(low, outs[0], outs[1]).astype(o_ref.dtype)
        xo_ref[...] = xv + _dot(o_ref[...], wo_ref[...]) + bo_ref[...]

    blk = lambda w: pl.BlockSpec((BLK, w), lambda i: (i, 0))
    return pl.pallas_call(
        body, name="attn_fwd", grid=(n,),
        in_specs=[blk(D), _full((1, D)), _full((D, D)), _full((1, D)),
                  pl.BlockSpec(memory_space=pltpu.SMEM),
                  pl.BlockSpec((BLK, KVD), lambda i: (jnp.maximum(i - 1, 0), 0)), blk(KVD),
                  _full((D, D)), _full((1, D))],
        out_specs=[blk(D), blk(D), blk(D)],
        out_shape=[jax.ShapeDtypeStruct((S, D), _F32), jax.ShapeDtypeStruct((S, D), _MXU),
                   jax.ShapeDtypeStruct((S, D), _MXU)],
        scratch_shapes=[pltpu.VMEM((2 * BLK, KVD), _MXU)],
        compiler_params=_seq(),
    )(x, g, wq, bq, sinks, kvd, kvd, wo, bo)


def _attn_bwd(x, dy, q, o, g, wq, sinks, kvd, wo):
    S = x.shape[0]
    n = S // BLK
    all_rows = N_HEADS * BLK

    def body(x_ref, dy_ref, q_ref, o_ref, g_ref, wq_ref, sk_ref, kp_ref, kc_ref, wo_ref,
             dx_ref, dq_ref, h_ref, dc_ref, dp_ref, dbq_ref, dbo_ref, dg_ref, dsk_ref, win, dob, qs, dos, pall, dsall):
        i = pl.program_id(0)

        @pl.when(i == 0)
        def _():
            dbq_ref[...] = jnp.zeros_like(dbq_ref)
            dbo_ref[...] = jnp.zeros_like(dbo_ref)
            dg_ref[...] = jnp.zeros_like(dg_ref)
            dsk_ref[...] = jnp.zeros_like(dsk_ref)

        gv = g_ref[...]
        xv = x_ref[...]
        xn, r = _rms(xv)
        h_ref[...] = (xn * gv).astype(h_ref.dtype)
        dyv = dy_ref[...]
        dbo_ref[...] += _colsum(dyv)
        dob[...] = _dot(dyv.astype(_MXU), wo_ref[...], _NT).astype(dob.dtype)
        win[0:BLK, :] = kp_ref[...]
        win[BLK:, :] = kc_ref[...]
        mask = _attn_mask(i, BLK)
        low = _low_half()
        lane = lax.broadcasted_iota(jnp.int32, (1, LANES), 1)
        for pr in range(N_HEADS // 2):
            kh = (2 * pr) // GROUP
            kd = win[:, kh * PAIR:(kh + 1) * PAIR]
            vd = win[:, (N_KV + kh) * PAIR:(N_KV + kh + 1) * PAIR]
            q2 = q_ref[:, pr * PAIR:(pr + 1) * PAIR]
            do2 = dob[:, pr * PAIR:(pr + 1) * PAIR]
            od = do2.astype(_F32) * o_ref[:, pr * PAIR:(pr + 1) * PAIR].astype(_F32)
            for half in range(2):
                hd = 2 * pr + half
                rows = slice(hd * BLK, (hd + 1) * BLK)
                sel = low if half == 0 else ~low
                qm = jnp.where(sel, q2, jnp.zeros_like(q2))
                dom = jnp.where(sel, do2, jnp.zeros_like(do2))
                qs[rows, :] = qm
                dos[rows, :] = dom
                pbs, ps = _head_probs(qm, kd, mask, sk_ref[hd])
                pall[rows, :] = pbs.astype(_MXU)
                delta = jnp.sum(jnp.where(sel, od, 0.0), axis=-1, keepdims=True)
                dsall[rows, :] = (pbs * (_dot(dom, vd, _NT) - delta) * (HEAD_DIM ** -0.5)).astype(_MXU)
                dsk_ref[...] -= jnp.where(lane == hd, _colsum(ps * delta), 0.0)
        dq_all = []
        for kh in range(N_KV):
            ks = slice(kh * PAIR, (kh + 1) * PAIR)
            vs = slice((N_KV + kh) * PAIR, (N_KV + kh + 1) * PAIR)
            rows = slice(kh * STACK, (kh + 1) * STACK)
            dqst = _dot(dsall[rows, :], win[:, ks])
            dk = _dot(dsall[rows, :], qs[rows, :], _TN)
            dv = _dot(pall[rows, :], dos[rows, :], _TN)
            dp_ref[:, ks], dc_ref[:, ks] = dk[0:BLK, :], dk[BLK:, :]
            dp_ref[:, vs], dc_ref[:, vs] = dv[0:BLK, :], dv[BLK:, :]
            dq_all += [_unstack_heads(dqst, pp) for pp in range(GROUP // 2)]
        dq = jnp.concatenate(dq_all, axis=1)
        dbq_ref[...] += _colsum(dq)
        dqb = dq.astype(_MXU)
        dq_ref[...] = dqb
        dh = _dot(dqb, wq_ref[...], _NT)
        dg_ref[...] += _colsum(dh * xn)
        dx_ref[...] = dyv + _rms_bwd(dh, xn, r, gv)

    blk = lambda w: pl.BlockSpec((BLK, w), lambda i: (i, 0))
    return pl.pallas_call(
        body, name="attn_bwd", grid=(n,),
        in_specs=[blk(D), blk(D), blk(D), blk(D), _full((1, D)), _full((D, D)),
                  pl.BlockSpec(memory_space=pltpu.SMEM),
                  pl.BlockSpec((BLK, KVD), lambda i: (jnp.maximum(i - 1, 0), 0)), blk(KVD), _full((D, D))],
        out_specs=[blk(D), blk(D), blk(D), blk(KVD), blk(KVD),
                   _full((1, D)), _full((1, D)), _full((1, D)), _full((1, LANES))],
        out_shape=[jax.ShapeDtypeStruct((S, D), _F32), jax.ShapeDtypeStruct((S, D), _MXU),
                   jax.ShapeDtypeStruct((S, D), _MXU), jax.ShapeDtypeStruct((S, KVD), _F32),
                   jax.ShapeDtypeStruct((S, KVD), _F32), jax.ShapeDtypeStruct((1, D), _F32),
                   jax.ShapeDtypeStruct((1, D), _F32), jax.ShapeDtypeStruct((1, D), _F32),
                   jax.ShapeDtypeStruct((1, LANES), _F32)],
        scratch_shapes=[pltpu.VMEM((2 * BLK, KVD), _MXU), pltpu.VMEM((BLK, D), _MXU),
                        pltpu.VMEM((all_rows, PAIR), _MXU), pltpu.VMEM((all_rows, PAIR), _MXU),
                        pltpu.VMEM((all_rows, 2 * BLK), _MXU), pltpu.VMEM((all_rows, 2 * BLK), _MXU)],
        compiler_params=_seq(),
    )(x, dy, q, o, g, wq, sinks, kvd, kvd, wo)


def _loss_bwd(x, g, tgt):
    S = x.shape[0]
    T = _tile(S, 512, 8)

    def body(x_ref, g_ref, t_ref, dx_ref, ls_ref, dg_ref):
        @pl.when(pl.program_id(0) == 0)
        def _():
            ls_ref[...] = jnp.zeros_like(ls_ref)
            dg_ref[...] = jnp.zeros_like(dg_ref)

        gv = g_ref[...]
        xn, r = _rms(x_ref[...])
        err = xn * gv - t_ref[...]
        ls_ref[...] += 0.5 * jnp.sum(jnp.mean(err * err, axis=-1, keepdims=True))
        dyv = err * (1.0 / D)
        dg_ref[...] += _colsum(dyv * xn)
        dx_ref[...] = _rms_bwd(dyv, xn, r, gv)

    return pl.pallas_call(
        body, name="loss_bwd", grid=(S // T,),
        in_specs=[pl.BlockSpec((T, D), lambda i: (i, 0)), _full((1, D)), pl.BlockSpec((T, D), lambda i: (i, 0))],
        out_specs=[pl.BlockSpec((T, D), lambda i: (i, 0)), _full((8, LANES)), _full((1, D))],
        out_shape=[jax.ShapeDtypeStruct((S, D), _F32), jax.ShapeDtypeStruct((8, LANES), _F32),
                   jax.ShapeDtypeStruct((1, D), _F32)],
        compiler_params=_seq(),
    )(x, g, tgt)


def _me():
    return 4 * lax.axis_index("x") + 2 * lax.axis_index("y") + lax.axis_index("c")


def _peer(j):
    x, y, c = lax.axis_index("x"), lax.axis_index("y"), lax.axis_index("c")
    px = 1 - x if j & 4 else x
    py = 1 - y if j & 2 else y
    pc = 1 - c if j & 1 else c
    return (px, py, pc), 4 * px + 2 * py + pc


_HBM = pl.BlockSpec(memory_space=pltpu.HBM)
_SEMS = pl.BlockSpec(memory_space=pltpu.SEMAPHORE)
_EFFECT = pltpu.SideEffectType.DATAFLOW_SIDE_EFFECTING


def _in_hbm(a):
    return pltpu.with_memory_space_constraint(a, pltpu.HBM)


def _start_copies(name, groups):
    flat = []
    for srcs, zones, _ in groups:
        flat += [_in_hbm(a) for a in srcs] + [_in_hbm(lax.empty(z.shape, z.dtype)) for z in zones]
    n_in, n_g = len(flat), len(groups)

    def body(*refs):
        sems = refs[n_in:n_in + 2 * n_g]
        me, k = _me(), 0
        for gi, (srcs, zones, plan) in enumerate(groups):
            src_refs, zone_refs = refs[k:k + len(srcs)], refs[k + len(srcs):k + len(srcs) + len(zones)]
            k += len(srcs) + len(zones)
            for t, (si, zi, src_of, dst_of) in enumerate(plan):
                for j in range(1, N_DEV):
                    dev, pk = _peer(j)
                    pltpu.make_async_remote_copy(
                        src_ref=src_of(src_refs[si], pk), dst_ref=dst_of(zone_refs[zi], me),
                        send_sem=sems[2 * gi].at[t * (N_DEV - 1) + j - 1], recv_sem=sems[2 * gi + 1].at[t * (N_DEV - 1) + j - 1],
                        device_id=dev, device_id_type=pl.DeviceIdType.MESH).start()
                pltpu.make_async_copy(src_of(src_refs[si], me), dst_of(zone_refs[zi], me),
                                      sems[2 * gi].at[len(plan) * (N_DEV - 1) + t]).start()
        refs[-1][...] = jnp.zeros_like(refs[-1])

    sem_shapes = []
    for _, _, plan in groups:
        sem_shapes += [pltpu.SemaphoreType.DMA((len(plan) * N_DEV,)), pltpu.SemaphoreType.DMA((len(plan) * (N_DEV - 1),))]
    outs = pl.pallas_call(
        body, name=name,
        out_shape=(*sem_shapes, *[pltpu.HBM(a.shape, a.dtype) for a in flat], jax.ShapeDtypeStruct((8, LANES), _F32)),
        in_specs=[_HBM] * n_in,
        out_specs=(*[_SEMS] * (2 * n_g), *[_HBM] * n_in, pl.BlockSpec(memory_space=pltpu.VMEM)),
        input_output_aliases={k: 2 * n_g + k for k in range(n_in)},
        compiler_params=pltpu.CompilerParams(has_side_effects=_EFFECT),
    )(*flat)
    handles, k = [], 2 * n_g
    for gi, (srcs, zones, plan) in enumerate(groups):
        ns, nz = len(srcs), len(zones)
        handles.append((outs[2 * gi], outs[2 * gi + 1], list(outs[k:k + ns]), list(outs[k + ns:k + ns + nz]), plan))
        k += ns + nz
    return handles, outs[-1]


def _wait_copies(name, handles, after):
    flat = []
    for _, _, srcs, zones, _ in handles:
        flat += srcs + zones
    n_in, n_g = len(flat), len(handles)

    def body(*refs):
        sems = refs[n_in:n_in + 2 * n_g]
        me, k, local, remote = _me(), 0, [], []
        for gi, (_, _, srcs, zones, plan) in enumerate(handles):
            ns, nz = len(srcs), len(zones)
            src_refs, zone_refs = refs[k:k + ns], refs[k + ns:k + ns + nz]
            k += ns + nz
            for t, (si, zi, src_of, dst_of) in enumerate(plan):
                local.append(pltpu.make_async_copy(src_of(src_refs[si], me), dst_of(zone_refs[zi], me),
                                                   sems[2 * gi].at[len(plan) * (N_DEV - 1) + t]))
                for j in range(1, N_DEV):
                    dev, pk = _peer(j)
                    remote.append(pltpu.make_async_remote_copy(
                        src_ref=src_of(src_refs[si], pk), dst_ref=dst_of(zone_refs[zi], pk),
                        send_sem=sems[2 * gi].at[t * (N_DEV - 1) + j - 1], recv_sem=sems[2 * gi + 1].at[t * (N_DEV - 1) + j - 1],
                        device_id=dev, device_id_type=pl.DeviceIdType.MESH))
        for cp in remote:
            cp.wait_send()
            cp.wait_recv()
        for cp in local:
            cp.wait()

    sem_args = []
    for send, recv, _, _, _ in handles:
        sem_args += [send, recv]
    outs = pl.pallas_call(
        body, name=name, out_shape=tuple(pltpu.HBM(a.shape, a.dtype) for a in flat),
        in_specs=[_HBM] * n_in + [_SEMS] * (2 * n_g) + [pl.BlockSpec(memory_space=pl.ANY)],
        out_specs=tuple([_HBM] * n_in), input_output_aliases={k: k for k in range(n_in)},
        compiler_params=pltpu.CompilerParams(has_side_effects=_EFFECT),
    )(*flat, *sem_args, after)
    res, k = [], 0
    for _, _, srcs, zones, _ in handles:
        res.append(list(outs[k + len(srcs):k + len(srcs) + len(zones)]))
        k += len(srcs) + len(zones)
    return res


def _rows(axis, size):
    def of(ref, b):
        start = b * size
        if size % 8 == 0:
            start = pl.multiple_of(start, 8)
        return ref.at[(slice(None),) * axis + (pl.ds(start, size),)]
    return of


def _whole(ref, b):
    return ref


def _slot(ref, b):
    return ref.at[b]


def _gather_group(shards):
    zones, plan = [], []
    for k, (a, axis) in enumerate(shards):
        zones.append(jax.ShapeDtypeStruct(a.shape[:axis] + (N_DEV * a.shape[axis],) + a.shape[axis + 1:], a.dtype))
        plan.append((k, k, _whole, _rows(axis, a.shape[axis])))
    return [a for a, _ in shards], zones, plan


def _scatter_group(grads):
    zones, plan = [], []
    for k, (a, axis) in enumerate(grads):
        size = a.shape[axis] // N_DEV
        zones.append(jax.ShapeDtypeStruct((N_DEV,) + a.shape[:axis] + (size,) + a.shape[axis + 1:], a.dtype))
        plan.append((k, k, _rows(axis, size), _slot))
    return [a for a, _ in grads], zones, plan


def _sum_landed(land):
    g = land[0].astype(_F32)
    for b in range(1, N_DEV):
        g = g + land[b].astype(_F32)
    return g


def _landed_specs(n_layers, tr, C, nr):
    def spec(k):
        return pl.BlockSpec((N_DEV, tr, C), lambda l, i: (0, jnp.where(l == k, i, jnp.where(l < k, 0, nr - 1)), 0))
    return [spec(k) for k in range(n_layers)]


def _per_layer(l, zone_refs, fn):
    for k, ref in enumerate(zone_refs):
        @pl.when(l == k)
        def _(ref=ref):
            fn(_sum_landed(ref))


def _sum8(zones):
    L = len(zones)
    _, R, C = zones[0].shape
    tr = _tile(R, 352, 16)
    nr = R // tr

    def body(*refs):
        o_ref = refs[L]

        def put(g):
            o_ref[...] = g

        _per_layer(pl.program_id(0), refs[:L], put)

    return pl.pallas_call(
        body, name="sum8", grid=(L, nr), in_specs=_landed_specs(L, tr, C, nr),
        out_specs=pl.BlockSpec((tr, C), lambda l, i: (l * nr + i, 0)),
        out_shape=jax.ShapeDtypeStruct((L * R, C), _F32), compiler_params=_seq(2),
    )(*zones)


def _adam_update(gv, w_ref, m_ref, v_ref, d_ref, mo_ref, vo_ref):
    mn = ADAM_B1 * m_ref[...] + (1.0 - ADAM_B1) * gv
    vn = ADAM_B2 * v_ref[...] + (1.0 - ADAM_B2) * (gv * gv)
    mo_ref[...] = mn
    vo_ref[...] = vn
    d_ref[...] = -ADAM_LR * ((mn / (1.0 - ADAM_B1 ** ADAM_STEP)) / (jnp.sqrt(vn / (1.0 - ADAM_B2 ** ADAM_STEP)) + ADAM_EPS)
                             + ADAM_WD * w_ref[...])


def _earlier(outs):
    outs = list(outs or [])
    return outs, [pl.BlockSpec(memory_space=pl.ANY)] * len(outs)


def _adamw(g, w, m, v, name, first_row=0, earlier=None):
    Rg, C = g.shape
    R = w.shape[0]
    tr = _tile(Rg if first_row == 0 else min(Rg, first_row), 256, 16)
    off = first_row // tr
    more, more_specs = _earlier(earlier)

    def body(g_ref, w_ref, m_ref, v_ref, *rest):
        go_ref, d_ref, mo_ref, vo_ref = rest[-4:]
        gv = g_ref[...]
        go_ref[...] = gv
        _adam_update(gv, w_ref, m_ref, v_ref, d_ref, mo_ref, vo_ref)

    row = pl.BlockSpec((tr, C), lambda i: (i + off, 0))
    return pl.pallas_call(
        body, name=name, grid=(Rg // tr,), in_specs=[pl.BlockSpec((tr, C), lambda i: (i, 0))] + [row] * 3 + more_specs,
        out_specs=[row] * 4, out_shape=[jax.ShapeDtypeStruct((R, C), _F32)] * 4,
        input_output_aliases={4 + k: k for k in range(len(more))}, compiler_params=_seq(),
    )(g, w, m, v, *more)


def _adamw_landed(zones, w, m, v, name, first_layer=0, earlier=None):
    L = len(zones)
    _, R, C = zones[0].shape
    tr = _tile(R, 176, 16)
    nr = R // tr
    more, more_specs = _earlier(earlier)

    def body(*refs):
        w_ref, m_ref, v_ref = refs[L:L + 3]
        g_ref, d_ref, mo_ref, vo_ref = refs[-4:]

        def update(g):
            g_ref[...] = g
            _adam_update(g, w_ref, m_ref, v_ref, d_ref, mo_ref, vo_ref)

        _per_layer(pl.program_id(0), refs[:L], update)

    row = pl.BlockSpec((tr, C), lambda l, i: ((l + first_layer) * nr + i, 0))
    return pl.pallas_call(
        body, name=name, grid=(L, nr), in_specs=_landed_specs(L, tr, C, nr) + [row] * 3 + more_specs,
        out_specs=[row] * 4, out_shape=[jax.ShapeDtypeStruct(w.shape, _F32)] * 4,
        input_output_aliases={L + 3 + k: k for k in range(len(more))}, compiler_params=_seq(2),
    )(*zones, w, m, v, *more)


def _pack(parts):
    flat = jnp.concatenate([p.reshape(-1).astype(_F32) for p in parts])
    n = flat.shape[0]
    rows = -(-n // (8 * LANES)) * 8
    return jnp.pad(flat, (0, rows * LANES - n)).reshape(rows, LANES)


def _unpack(packed, shapes):
    flat, out, k = packed.reshape(-1), [], 0
    for s in shapes:
        n = 1
        for d in s:
            n *= d
        out.append(flat[k:k + n].reshape(s))
        k += n
    return out


def kernel(x, norm1_g, norm2_g, pool_w, pool_scale, kv_norm_g, w_kv, b_kv, w_q, b_q, sinks, w_o, b_o, ffn_up, ffn_conv_w, ffn_conv_b, ffn_down, final_g, loss_target, m_norm1_g, m_norm2_g, m_pool_w, m_pool_scale, m_kv_norm_g, m_w_kv, m_b_kv, m_w_q, m_b_q, m_sinks, m_w_o, m_b_o, m_ffn_up, m_ffn_conv_w, m_ffn_conv_b, m_ffn_down, m_final_g, v_norm1_g, v_norm2_g, v_pool_w, v_pool_scale, v_kv_norm_g, v_w_kv, v_b_kv, v_w_q, v_b_q, v_sinks, v_w_o, v_b_o, v_ffn_up, v_ffn_conv_w, v_ffn_conv_b, v_ffn_down, v_final_g):
    S = x.shape[1]
    F2s = ffn_up.shape[2]
    F2 = N_DEV * F2s
    me = _me()
    x0 = x.reshape(S, D)
    tgt = loss_target.reshape(S, D)
    row = lambda a: a.reshape(1, -1)

    small = _pack([pool_scale, ffn_conv_w])
    wire = lambda a: a.astype(_MXU)
    ffn_w = lambda l: [(wire(ffn_up[l]).T, 0), (wire(ffn_down[l]), 0)]
    attn_w = lambda j: [(wire(w_q[j]), 0), (wire(w_o[j]), 0)]
    gathers, token = _start_copies("gather_start", [_gather_group(g) for g in (
        [(wire(pool_w[0]), 1), (small[None], 0)], ffn_w(0), [(wire(pool_w[1]), 1)] + ffn_w(1),
        [(wire(w_kv), 0)] + attn_w(0), ffn_w(2), attn_w(1), ffn_w(3))])

    def gathered(k, after):
        return _wait_copies("gather_wait_%d" % k, [gathers[k]], after)[0]

    pw, up_t, down, wq, wo = [None] * N_A, [None] * DEPTH, [None] * DEPTH, [None] * 2, [None] * 2
    pw[0], small_all = gathered(0, token)
    n_ps = pool_scale.size
    small_all = small_all.reshape(N_DEV, -1)
    pscale = jnp.transpose(small_all[:, :n_ps].reshape(N_DEV, N_A, D // N_DEV), (1, 0, 2)).reshape(N_A, D)
    conv_w = jnp.transpose(small_all[:, n_ps:n_ps + ffn_conv_w.size].reshape(N_DEV, DEPTH, CONV_W, F2s),
                           (1, 2, 0, 3)).reshape(DEPTH, CONV_W, F2)

    def dup(a):
        a4 = a.reshape(a.shape[:-1] + (2 * N_KV, 1, HEAD_DIM))
        return jnp.broadcast_to(a4, a.shape[:-1] + (2 * N_KV, 2, HEAD_DIM)).reshape(a.shape[:-1] + (KVD,))

    def fold(a):
        return a.reshape(a.shape[:-1] + (2 * N_KV, 2, HEAD_DIM)).sum(axis=-2).reshape(a.shape[:-1] + (2 * N_KV * HEAD_DIM,))

    xs, us, qs, os_ = [x0], [], [], []
    xc = x0
    kvd = None
    for l in range(DEPTH):
        if l == 1:
            pw[1], up_t[1], down[1] = gathered(2, xc)
        if l == 3:
            wq[1], wo[1] = gathered(5, xc)
        if l < N_A:
            xc = _pool_fwd(xc, row(norm1_g[l]), pw[l], row(pscale[l]))
        else:
            j = l - N_A
            xc, q, o = _attn_fwd(xc, row(norm1_g[l]), wq[j], row(b_q[j]), sinks[j], kvd, wo[j], row(b_o[j]))
            qs.append(q)
            os_.append(o)
        xs.append(xc)
        if l != 1:
            up_t[l], down[l] = gathered((1, None, 4, 6)[l], xc)
        xc, u, c = _ffn_fwd(xc, row(norm2_g[l]), up_t[l], conv_w[l], row(ffn_conv_b[l]), down[l])
        us.append((u, c))
        xs.append(xc)
        if l == N_A - 1:
            wkv, wq[0], wo[0] = gathered(3, xc)
            wkv_d, bkv_d = dup(wkv), dup(row(b_kv))
            kvd = _kv_fwd(xc, row(kv_norm_g), wkv_d, bkv_d)

    dx, loss_p, d_final = _loss_bwd(xc, row(final_g), tgt)
    d_n1, d_n2, d_cw, d_cb = [None] * DEPTH, [None] * DEPTH, [None] * DEPTH, [None] * DEPTH
    d_bq, d_bo, d_sk, d_ps, dkv_parts = [None] * 2, [None] * 2, [None] * 2, [None] * N_A, []
    up_z, down_z, mix_z = [None] * DEPTH, [None] * DEPTH, [None] * DEPTH
    token = None

    def after(gain):
        return gain if token is None else gain + token[0:1, 0:1]

    rep_names = ["norm1_g", "norm2_g", "kv_norm_g", "b_kv", "b_q", "sinks", "b_o", "ffn_conv_b", "final_g"]

    def small_parts(n1_rest, ps_rest):
        zero = jnp.zeros((1, D), _F32)
        return [jnp.concatenate([zero] + n1_rest), jnp.concatenate(d_n2), d_kvg, fold(d_bkv), jnp.concatenate(d_bq),
                jnp.concatenate([s[:, :N_HEADS] for s in d_sk]), jnp.concatenate(d_bo), jnp.concatenate(d_cb), d_final,
                jnp.concatenate([zero] + ps_rest), jnp.stack(d_cw), loss_p[0:1, 0:1]]

    for l in reversed(range(DEPTH)):
        x_in, x_mid, x_out = xs[2 * l], xs[2 * l + 1], xs[2 * l + 2]
        mixer_grads = []
        if l == N_A - 1:
            dx, d_wkv, d_bkv, d_kvg = _kv_bwd(x_out, dx, after(row(kv_norm_g)), wkv_d, *dkv_parts)
            mixer_grads.append((fold(d_wkv).astype(_MXU), 0))
        dy = dx
        dx, du, a, h, d_cw[l], d_cb[l], d_n2[l] = _ffn_bwd(
            x_mid, dy, *us[l], after(row(norm2_g[l])), up_t[l], conv_w[l], down[l])
        g_up = _tn_matmul(du, h, "tn_up")
        if l > 0:
            (up_z[l], down_z[l]), token = _start_copies("scatter_ffn_%d" % l, [
                _scatter_group([(g_up, 0)]), _scatter_group([(_tn_matmul(a, dy, "tn_down"), 0)])])
        else:
            (small_handle, up_z[l]), token = _start_copies("scatter_up_0", [
                _gather_group([(_pack(small_parts(d_n1[1:], d_ps[1:]))[None], 0)]), _scatter_group([(g_up, 0)])])
            (down_z[l],), token = _start_copies("scatter_down_0", [
                _scatter_group([(_tn_matmul(a, dy, "tn_down", token), 0)])])
        dy = dx
        if l < N_A:
            dx, d_pw, d_ps[l], d_n1[l] = _pool_bwd(x_in, dy, after(row(norm1_g[l])), pw[l], row(pscale[l]))
            mixer_grads.append((d_pw.astype(_MXU), 1))
        else:
            j = l - N_A
            dx, dq, h, d_cur, d_prev, d_bq[j], d_bo[j], d_n1[l], d_sk[j] = _attn_bwd(
                x_in, dy, qs[j], os_[j], after(row(norm1_g[l])), wq[j], sinks[j], kvd, wo[j])
            mixer_grads += [(_tn_matmul(h, dq, "tn_q"), 0), (_tn_matmul(os_[j], dy, "tn_o"), 0)]
            dkv_parts += [d_cur, d_prev]
        groups = [_scatter_group(mixer_grads)]
        if l == 0:
            groups.append(_gather_group([(_pack([d_n1[0], d_ps[0]])[None], 0)]))
        handles, token = _start_copies("scatter_mixer_%d" % l, groups)
        mix_z[l], late_handle = handles[0], handles[-1]

    moms = dict(norm1_g=(m_norm1_g, v_norm1_g), norm2_g=(m_norm2_g, v_norm2_g), pool_w=(m_pool_w, v_pool_w),
                pool_scale=(m_pool_scale, v_pool_scale), kv_norm_g=(m_kv_norm_g, v_kv_norm_g), w_kv=(m_w_kv, v_w_kv),
                b_kv=(m_b_kv, v_b_kv), w_q=(m_w_q, v_w_q), b_q=(m_b_q, v_b_q), sinks=(m_sinks, v_sinks),
                w_o=(m_w_o, v_w_o), b_o=(m_b_o, v_b_o), ffn_up=(m_ffn_up, v_ffn_up),
                ffn_conv_w=(m_ffn_conv_w, v_ffn_conv_w), ffn_conv_b=(m_ffn_conv_b, v_ffn_conv_b),
                ffn_down=(m_ffn_down, v_ffn_down), final_g=(m_final_g, v_final_g))
    given = dict(norm1_g=norm1_g, norm2_g=norm2_g, kv_norm_g=kv_norm_g, b_kv=b_kv, b_q=b_q, sinks=sinks, b_o=b_o,
                 ffn_conv_b=ffn_conv_b, final_g=final_g, pool_scale=pool_scale, ffn_conv_w=ffn_conv_w,
                 pool_w=pool_w, w_kv=w_kv, w_q=w_q, w_o=w_o, ffn_up=ffn_up, ffn_down=ffn_down)
    grad, delta, new_m, new_v = {}, {}, {}, {}

    def update(name, zones, cols, first_layer=0, earlier=None, last=True):
        w = given[name]
        two_d = lambda a: a.reshape(-1, cols)
        wmv = (two_d(w), two_d(moms[name][0]), two_d(moms[name][1]))
        if name == "ffn_up":
            g = jnp.swapaxes(_sum8(zones).reshape(len(zones), F2s, D), 1, 2).reshape(-1, F2s)
            outs = _adamw(g, *wmv, "adamw_" + name, first_layer * D, earlier)
        else:
            landed = [z.reshape(N_DEV, -1, cols) for z in zones]
            outs = _adamw_landed(landed, *wmv, "adamw_" + name, first_layer, earlier)
        if last:
            grad[name], delta[name], new_m[name], new_v[name] = (o.reshape(w.shape) for o in outs)
        return outs

    rest = _wait_copies("scatter_wait_rest", up_z[1:] + down_z[1:] + mix_z[1:], token)
    up_r, down_r, mix_r = rest[:DEPTH - 1], rest[DEPTH - 1:2 * (DEPTH - 1)], rest[2 * (DEPTH - 1):]
    up_1 = update("ffn_up", [z[0] for z in up_r], F2s, 1, last=False)
    down_1 = update("ffn_down", [z[0] for z in down_r], D, 1, last=False)
    update("w_q", [mix_r[N_A - 1][0], mix_r[N_A][0]], D)
    update("w_o", [mix_r[N_A - 1][1], mix_r[N_A][1]], D)
    update("w_kv", [mix_r[N_A - 2][0]], w_kv.shape[1])
    pw_1 = update("pool_w", [mix_r[N_A - 2][1]], GC, 1, last=False)

    (up_0,), (down_0,), (pw_0,), (early,), (late,) = _wait_copies(
        "scatter_wait_0", [up_z[0], down_z[0], mix_z[0], small_handle, late_handle], down_1[1])
    late = _sum8([late]).reshape(-1)
    tot = _unpack(_sum8([early]), [given[k].shape for k in rep_names] + [(N_A, D), (DEPTH, CONV_W, F2), ()])
    tot[0] = tot[0].at[0].add(late[:D])
    tot[-3] = tot[-3].at[0].add(late[D:2 * D])
    grad.update(zip(rep_names, tot))
    loss = tot[-1]
    grad["pool_scale"] = lax.dynamic_slice_in_dim(tot[-3], me * (D // N_DEV), D // N_DEV, axis=1)
    grad["ffn_conv_w"] = lax.dynamic_slice_in_dim(tot[-2], me * F2s, F2s, axis=2)
    small_names = rep_names + ["pool_scale", "ffn_conv_w"]
    shapes = [given[k].shape for k in small_names]
    outs = _adamw(_pack([grad[k] for k in small_names]), _pack([given[k] for k in small_names]),
                  _pack([moms[k][0] for k in small_names]), _pack([moms[k][1] for k in small_names]), "adamw_small")
    for dst, packed in zip((delta, new_m, new_v), outs[1:]):
        dst.update(zip(small_names, _unpack(packed, shapes)))

    update("ffn_up", [up_0], F2s, 0, up_1)
    update("ffn_down", [down_0], D, 0, down_1)
    update("pool_w", [pw_0], GC, 0, pw_1)

    names = ["norm1_g", "norm2_g", "pool_w", "pool_scale", "kv_norm_g", "w_kv", "b_kv", "w_q", "b_q", "sinks", "w_o",
             "b_o", "ffn_up", "ffn_conv_w", "ffn_conv_b", "ffn_down", "final_g"]
    return (loss, dx.reshape(x.shape), *[grad[k] for k in names], *[delta[k] for k in names],
            *[new_m[k] for k in names], *[new_v[k] for k in names])
```

```python
import jax
import jax.numpy as jnp
from jax import lax
from jax.experimental import pallas as pl
from jax.experimental.pallas import tpu as pltpu

_F32 = jnp.float32
_MXU = jnp.bfloat16

N_DEV = 8
D = 1024
DEPTH = 4
N_A = 2
POOL_WINDOWS = (2, 4, 8, 16)
GC = D // len(POOL_WINDOWS)
HALO = 16
HEAD_DIM = 64
N_HEADS = D // HEAD_DIM
GROUP = 8
N_KV = N_HEADS // GROUP
BLK = 128
PAIR = 2 * HEAD_DIM
KVD = 4 * N_KV * HEAD_DIM
CONV_W = 3
EPS = 1e-5
NEG = -1e30

ADAM_LR = 0.001
ADAM_B1 = 0.9
ADAM_B2 = 0.999
ADAM_EPS = 1e-08
ADAM_WD = 0.01
ADAM_STEP = 10

V7X_VMEM_LIMIT = 56 * 1024 * 1024
LANES = 128

_NT = (((1,), (1,)), ((), ()))
_TN = (((0,), (0,)), ((), ()))


def _params(**kw):
    return pltpu.CompilerParams(vmem_limit_bytes=V7X_VMEM_LIMIT, **kw)


def _seq(n=1):
    return _params(dimension_semantics=("arbitrary",) * n)


def _dot(a, b, dims=None):
    if dims is None:
        return jnp.dot(a, b, preferred_element_type=_F32)
    return lax.dot_general(a, b, dims, preferred_element_type=_F32)


def _rms(x):
    r = lax.rsqrt(jnp.mean(x * x, axis=-1, keepdims=True) + EPS)
    return x * r, r


def _rms_bwd(dh, xn, r, g):
    dxn = dh * g
    return r * (dxn - xn * jnp.mean(dxn * xn, axis=-1, keepdims=True))


def _colsum(a):
    return jnp.sum(a, axis=0, keepdims=True)


def _tile(n, want, mult=8):
    for t in range(min(want, n), 0, -1):
        if n % t == 0 and t % mult == 0:
            return t
    return n


def _full(shape):
    zeros = (0,) * len(shape)
    return pl.BlockSpec(shape, lambda *_: zeros)


def _window_sum(ext, win, trailing):
    R = ext.shape[0]
    acc, k = ext, 1
    while k < win:
        acc = acc + pltpu.roll(acc, k if trailing else R - k, axis=0)
        k *= 2
    return acc


def _pool_windows(hbuf, h, row, T):
    out = []
    for gi, win in enumerate(POOL_WINDOWS):
        cs = slice(gi * GC, (gi + 1) * GC)
        acc = _window_sum(hbuf[:, cs], win, True)[HALO:, :]
        cnt = jnp.minimum(row + 1, win).astype(_F32)
        out.append((acc / cnt - h[:, cs], cnt))
    return out


def _pool_fwd(x, g, w, sc):
    S = x.shape[0]
    T = _tile(S, 512, HALO)
    n, hb = S // T, T // HALO

    def body(x_ref, xh_ref, g_ref, w_ref, sc_ref, o_ref, hbuf):
        i = pl.program_id(0)
        gv = g_ref[...]
        xv = x_ref[...]
        h = _rms(xv)[0] * gv
        hbuf[0:HALO, :] = jnp.where(i > 0, _rms(xh_ref[...])[0] * gv, 0.0)
        hbuf[HALO:, :] = h
        row = i * T + lax.broadcasted_iota(jnp.int32, (T, 1), 0)
        for gi, (p, _) in enumerate(_pool_windows(hbuf, h, row, T)):
            cs = slice(gi * GC, (gi + 1) * GC)
            z = _dot(p.astype(_MXU), w_ref[gi])
            o_ref[:, cs] = xv[:, cs] + z * sc_ref[:, cs]

    return pl.pallas_call(
        body, name="pool_fwd", grid=(n,),
        in_specs=[pl.BlockSpec((T, D), lambda i: (i, 0)),
                  pl.BlockSpec((HALO, D), lambda i: (jnp.maximum(i * hb - 1, 0), 0)),
                  _full((1, D)), _full((4, GC, GC)), _full((1, D))],
        out_specs=pl.BlockSpec((T, D), lambda i: (i, 0)),
        out_shape=jax.ShapeDtypeStruct((S, D), _F32),
        scratch_shapes=[pltpu.VMEM((T + HALO, D), _F32)],
        compiler_params=_seq(),
    )(x, x, g, w, sc)


def _pool_bwd(x, dy, g, w, sc):
    S = x.shape[0]
    T = _tile(S, 512, HALO)
    n, hb = S // T, T // HALO

    def body(x_ref, xh_ref, dy_ref, dyh_ref, g_ref, w_ref, sc_ref, dx_ref, dw_ref, dsc_ref, dg_ref,
             hbuf, qbuf, dhbuf):
        i = pl.program_id(0)

        @pl.when(i == 0)
        def _():
            dw_ref[...] = jnp.zeros_like(dw_ref)
            dsc_ref[...] = jnp.zeros_like(dsc_ref)
            dg_ref[...] = jnp.zeros_like(dg_ref)

        gv = g_ref[...]
        xv = x_ref[...]
        xn, r = _rms(xv)
        h = xn * gv
        hbuf[0:HALO, :] = jnp.where(i > 0, _rms(xh_ref[...])[0] * gv, 0.0)
        hbuf[HALO:, :] = h
        dyv = dy_ref[...]
        dz = dyv * sc_ref[...]
        dzh = jnp.where(i < n - 1, dyh_ref[...], 0.0) * sc_ref[...]
        row = i * T + lax.broadcasted_iota(jnp.int32, (T, 1), 0)
        rowh = (i + 1) * T + lax.broadcasted_iota(jnp.int32, (HALO, 1), 0)
        for gi, (p, cnt) in enumerate(_pool_windows(hbuf, h, row, T)):
            win = POOL_WINDOWS[gi]
            cs = slice(gi * GC, (gi + 1) * GC)
            pb = p.astype(_MXU)
            wg = w_ref[gi]
            dsc_ref[:, cs] += _colsum(dyv[:, cs] * _dot(pb, wg))
            dzb = dz[:, cs].astype(_MXU)
            dw_ref[gi] += _dot(pb, dzb, _TN)
            dp = _dot(dzb, wg, _NT)
            dph = _dot(dzh[:, cs].astype(_MXU), wg, _NT)
            qbuf[0:T, cs] = dp / cnt
            qbuf[T:T + HALO, cs] = dph / jnp.minimum(rowh + 1, win).astype(_F32)
            dhbuf[:, cs] = _window_sum(qbuf[:, cs], win, False)[0:T, :] - dp
        dh = dhbuf[...]
        dg_ref[...] += _colsum(dh * xn)
        dx_ref[...] = dyv + _rms_bwd(dh, xn, r, gv)

    return pl.pallas_call(
        body, name="pool_bwd", grid=(n,),
        in_specs=[pl.BlockSpec((T, D), lambda i: (i, 0)),
                  pl.BlockSpec((HALO, D), lambda i: (jnp.maximum(i * hb - 1, 0), 0)),
                  pl.BlockSpec((T, D), lambda i: (i, 0)),
                  pl.BlockSpec((HALO, D), lambda i: (jnp.minimum((i + 1) * hb, S // HALO - 1), 0)),
                  _full((1, D)), _full((4, GC, GC)), _full((1, D))],
        out_specs=[pl.BlockSpec((T, D), lambda i: (i, 0)), _full((4, GC, GC)), _full((1, D)), _full((1, D))],
        out_shape=[jax.ShapeDtypeStruct((S, D), _F32), jax.ShapeDtypeStruct((4, GC, GC), _F32),
                   jax.ShapeDtypeStruct((1, D), _F32), jax.ShapeDtypeStruct((1, D), _F32)],
        scratch_shapes=[pltpu.VMEM((T + HALO, D), _F32), pltpu.VMEM((T + HALO, D), _F32), pltpu.VMEM((T, D), _F32)],
        compiler_params=_seq(),
    )(x, x, dy, dy, g, w, sc)


FFN_FWD_TILE, FFN_FWD_CHUNKS = 256, 1
FFN_BWD_TILE, FFN_BWD_CHUNKS = 256, 1
EDGE = 8


def _shift_down(v, k, prev):
    r = pltpu.roll(v, k, axis=0)
    i8 = lax.broadcasted_iota(jnp.int32, (EDGE, v.shape[1]), 0)
    head = jnp.where(i8 >= k, r[0:EDGE, :], pltpu.roll(prev, k, axis=0))
    return jnp.concatenate([head, r[EDGE:, :]], axis=0)


def _shift_up(v, k, nxt):
    T = v.shape[0]
    r = pltpu.roll(v, T - k, axis=0)
    i8 = lax.broadcasted_iota(jnp.int32, (EDGE, v.shape[1]), 0)
    tail = jnp.where(i8 < EDGE - k, r[T - EDGE:, :], pltpu.roll(nxt, EDGE - k, axis=0))
    return jnp.concatenate([r[:T - EDGE, :], tail], axis=0)


def _load_weights(i, pairs, sems):
    @pl.when(i == 0)
    def _():
        cps = [pltpu.make_async_copy(src, dst, sems.at[k]) for k, (src, dst) in enumerate(pairs)]
        for cp in cps:
            cp.start()
        for cp in cps:
            cp.wait()


def _ffn_fwd(x, g, wup_t, cw, cb, wdn):
    S = x.shape[0]
    F2 = wup_t.shape[0]
    F = F2 // 2
    C = F // FFN_FWD_CHUNKS
    T = _tile(S, FFN_FWD_TILE, 16)
    n = S // T

    def body(x_ref, g_ref, wup_hbm, cw_ref, cb_ref, wdn_hbm, o_ref, u_ref, c_ref, wup, wdnv, carry, sems):
        i = pl.program_id(0)
        _load_weights(i, [(wup_hbm, wup), (wdn_hbm, wdnv)], sems)

        @pl.when(i == 0)
        def _():
            carry[...] = jnp.zeros_like(carry)

        xv = x_ref[...]
        hb = (_rms(xv)[0] * g_ref[...]).astype(_MXU)
        acc = jnp.zeros((T, D), _F32)
        for j in range(FFN_FWD_CHUNKS):
            halves = []
            for cs in (slice(j * C, (j + 1) * C), slice(F + j * C, F + (j + 1) * C)):
                u = _dot(hb, wup[cs, :], _NT)
                u_ref[:, cs] = u.astype(u_ref.dtype)
                prev = carry[:, cs]
                carry[:, cs] = u[T - EDGE:, :]
                c = (cw_ref[0:1, cs] * _shift_down(u, 2, prev) + cw_ref[1:2, cs] * _shift_down(u, 1, prev)
                     + cw_ref[2:3, cs] * u + cb_ref[:, cs])
                c_ref[:, cs] = c.astype(c_ref.dtype)
                halves.append(c)
            cg, cv = halves
            a = (cg * jax.nn.sigmoid(cg)) * cv
            acc = acc + _dot(a.astype(_MXU), wdnv[j * C:(j + 1) * C, :])
        o_ref[...] = xv + acc

    any_ = pl.BlockSpec(memory_space=pl.ANY)
    wide = pl.BlockSpec((T, F2), lambda i: (i, 0))
    return pl.pallas_call(
        body, name="ffn_fwd", grid=(n,),
        in_specs=[pl.BlockSpec((T, D), lambda i: (i, 0)), _full((1, D)), any_, _full((CONV_W, F2)), _full((1, F2)), any_],
        out_specs=[pl.BlockSpec((T, D), lambda i: (i, 0)), wide, wide],
        out_shape=[jax.ShapeDtypeStruct((S, D), _F32), jax.ShapeDtypeStruct((S, F2), _MXU),
                   jax.ShapeDtypeStruct((S, F2), _MXU)],
        scratch_shapes=[pltpu.VMEM((F2, D), _MXU), pltpu.VMEM((F, D), _MXU),
                        pltpu.VMEM((EDGE, F2), _F32), pltpu.SemaphoreType.DMA((2,))],
        compiler_params=_seq(),
    )(x, g, wup_t, cw, cb, wdn)


def _ffn_bwd(x, dy, u, c, g, wup_t, cw, wdn):
    S = x.shape[0]
    F2 = wup_t.shape[0]
    F = F2 // 2
    C = F // FFN_BWD_CHUNKS
    T = _tile(S, FFN_BWD_TILE, 16)
    n = S // T

    def body(x_ref, dy_ref, u_ref, c_ref, g_ref, wup_hbm, cw_ref, wdn_hbm,
             dx_ref, du_ref, a_ref, h_ref, dcw_ref, dcb_ref, dg_ref, wup, wdnv, carry, sems):
        i = pl.program_id(0)
        _load_weights(i, [(wup_hbm, wup), (wdn_hbm, wdnv)], sems)

        @pl.when(i == 0)
        def _():
            carry[...] = jnp.zeros_like(carry)
            dcw_ref[...] = jnp.zeros_like(dcw_ref)
            dcb_ref[...] = jnp.zeros_like(dcb_ref)
            dg_ref[...] = jnp.zeros_like(dg_ref)

        gv = g_ref[...]
        xv = x_ref[...]
        xn, r = _rms(xv)
        hbf = (xn * gv).astype(_MXU)
        h_ref[...] = hbf
        dyv = dy_ref[...]
        dyb = dyv.astype(_MXU)
        dh = jnp.zeros((T, D), _F32)
        for j in range(FFN_BWD_CHUNKS):
            gs, vs = slice(j * C, (j + 1) * C), slice(F + j * C, F + (j + 1) * C)
            cg, cv = c_ref[:, gs].astype(_F32), c_ref[:, vs].astype(_F32)
            sg = jax.nn.sigmoid(cg)
            sl = cg * sg
            a_ref[:, gs] = (sl * cv).astype(a_ref.dtype)
            da = _dot(dyb, wdnv[gs, :], _NT)
            for cs, dc in ((gs, da * cv * (sg * (1.0 + cg * (1.0 - sg)))), (vs, da * sl)):
                nxt = carry[:, cs]
                carry[:, cs] = dc[0:EDGE, :]
                dc1, dc2 = _shift_up(dc, 1, nxt), _shift_up(dc, 2, nxt)
                uf = u_ref[:, cs].astype(_F32)
                dcb_ref[:, cs] += _colsum(dc)
                for k, d in enumerate((dc2, dc1, dc)):
                    dcw_ref[k:k + 1, cs] += _colsum(d * uf)
                du = cw_ref[2:3, cs] * dc + cw_ref[1:2, cs] * dc1 + cw_ref[0:1, cs] * dc2
                dub = du.astype(_MXU)
                du_ref[:, cs] = dub
                dh = dh + _dot(dub, wup[cs, :])
        dg_ref[...] += _colsum(dh * xn)
        dx_ref[...] = dyv + _rms_bwd(dh, xn, r, gv)

    any_ = pl.BlockSpec(memory_space=pl.ANY)
    rev = lambda i: (n - 1 - i, 0)
    return pl.pallas_call(
        body, name="ffn_bwd", grid=(n,),
        in_specs=[pl.BlockSpec((T, D), rev), pl.BlockSpec((T, D), rev), pl.BlockSpec((T, F2), rev),
                  pl.BlockSpec((T, F2), rev), _full((1, D)), any_, _full((CONV_W, F2)), any_],
        out_specs=[pl.BlockSpec((T, D), rev), pl.BlockSpec((T, F2), rev), pl.BlockSpec((T, F), rev),
                   pl.BlockSpec((T, D), rev), _full((CONV_W, F2)), _full((1, F2)), _full((1, D))],
        out_shape=[jax.ShapeDtypeStruct((S, D), _F32), jax.ShapeDtypeStruct((S, F2), _MXU),
                   jax.ShapeDtypeStruct((S, F), _MXU), jax.ShapeDtypeStruct((S, D), _MXU),
                   jax.ShapeDtypeStruct((CONV_W, F2), _F32), jax.ShapeDtypeStruct((1, F2), _F32),
                   jax.ShapeDtypeStruct((1, D), _F32)],
        scratch_shapes=[pltpu.VMEM((F2, D), _MXU), pltpu.VMEM((F, D), _MXU),
                        pltpu.VMEM((EDGE, F2), _F32), pltpu.SemaphoreType.DMA((2,))],
        compiler_params=_seq(),
    )(x, dy, u, c, g, wup_t, cw, wdn)


def _tn_matmul(a, b, name, token=None):
    S, M = a.shape
    N = b.shape[1]
    bm = _tile(M, 1408, LANES)
    tk = _tile(S, 2048, 16)
    nk = S // tk
    tokens = [] if token is None else [token]

    def body(a_ref, b_ref, *rest):
        o_ref, acc = rest[-2:]
        k = pl.program_id(1)

        @pl.when(k == 0)
        def _():
            acc[...] = jnp.zeros_like(acc)

        acc[...] += _dot(a_ref[...].astype(_MXU), b_ref[...].astype(_MXU), _TN)

        @pl.when(k == nk - 1)
        def _():
            o_ref[...] = acc[...].astype(o_ref.dtype)

    return pl.pallas_call(
        body, name=name, grid=(M // bm, nk),
        in_specs=[pl.BlockSpec((tk, bm), lambda i, k: (k, i)), pl.BlockSpec((tk, N), lambda i, k: (k, 0))]
        + [_full((8, LANES))] * len(tokens),
        out_specs=pl.BlockSpec((bm, N), lambda i, k: (i, 0)),
        out_shape=jax.ShapeDtypeStruct((M, N), _MXU),
        scratch_shapes=[pltpu.VMEM((bm, N), _F32)],
        compiler_params=_seq(2),
    )(a, b, *tokens)


def _kv_fwd(x, g, wkv, bkv):
    S = x.shape[0]
    T = _tile(S, 512, 16)

    def body(x_ref, g_ref, w_ref, b_ref, o_ref):
        hb = (_rms(x_ref[...])[0] * g_ref[...]).astype(_MXU)
        o_ref[...] = (_dot(hb, w_ref[...]) + b_ref[...]).astype(o_ref.dtype)

    return pl.pallas_call(
        body, name="kv_fwd", grid=(S // T,),
        in_specs=[pl.BlockSpec((T, D), lambda i: (i, 0)), _full((1, D)), _full((D, KVD)), _full((1, KVD))],
        out_specs=pl.BlockSpec((T, KVD), lambda i: (i, 0)),
        out_shape=jax.ShapeDtypeStruct((S, KVD), _MXU),
        compiler_params=_seq(),
    )(x, g, wkv, bkv)


def _kv_bwd(x, dx_in, g, wkv, cur_a, prev_a, cur_b, prev_b):
    S = x.shape[0]
    T = _tile(S, 512, BLK)
    n, per = S // T, T // BLK

    def body(x_ref, dxi_ref, g_ref, w_ref, ca, pa, na, cb, pb, nb, dx_ref, dw_ref, db_ref, dg_ref):
        i = pl.program_id(0)

        @pl.when(i == 0)
        def _():
            dw_ref[...] = jnp.zeros_like(dw_ref)
            db_ref[...] = jnp.zeros_like(db_ref)
            dg_ref[...] = jnp.zeros_like(dg_ref)

        gv = g_ref[...]
        xn, r = _rms(x_ref[...])
        nxt = jnp.where(i < n - 1, na[...] + nb[...], 0.0)
        prev = jnp.concatenate([pa[BLK:, :] + pb[BLK:, :], nxt], axis=0) if per > 1 else nxt
        dkv = ca[...] + cb[...] + prev
        db_ref[...] += _colsum(dkv)
        dkb = dkv.astype(_MXU)
        dw_ref[...] += _dot((xn * gv).astype(_MXU), dkb, _TN)
        dh = _dot(dkb, w_ref[...], _NT)
        dg_ref[...] += _colsum(dh * xn)
        dx_ref[...] = dxi_ref[...] + _rms_bwd(dh, xn, r, gv)

    blk = lambda w: pl.BlockSpec((T, w), lambda i: (i, 0))
    nxt = pl.BlockSpec((BLK, KVD), lambda i: (jnp.minimum((i + 1) * per, S // BLK - 1), 0))
    return pl.pallas_call(
        body, name="kv_bwd", grid=(n,),
        in_specs=[blk(D), blk(D), _full((1, D)), _full((D, KVD)), blk(KVD), blk(KVD), nxt, blk(KVD), blk(KVD), nxt],
        out_specs=[blk(D), _full((D, KVD)), _full((1, KVD)), _full((1, D))],
        out_shape=[jax.ShapeDtypeStruct((S, D), _F32), jax.ShapeDtypeStruct((D, KVD), _F32),
                   jax.ShapeDtypeStruct((1, KVD), _F32), jax.ShapeDtypeStruct((1, D), _F32)],
        compiler_params=_seq(),
    )(x, dx_in, g, wkv, cur_a, prev_a, prev_a, cur_b, prev_b, prev_b)


STACK = GROUP * BLK


def _attn_mask(i, rows):
    qi = lax.broadcasted_iota(jnp.int32, (rows, 2 * BLK), 0) & (BLK - 1)
    si = lax.broadcasted_iota(jnp.int32, (rows, 2 * BLK), 1)
    return (si > qi) & (si <= qi + BLK) & jnp.logical_or(i > 0, si >= BLK)


def _low_half():
    return lax.broadcasted_iota(jnp.int32, (BLK, PAIR), 1) < HEAD_DIM


def _stack_heads(ref, kh, dst):
    low = _low_half()
    for pp in range(GROUP // 2):
        pr = kh * (GROUP // 2) + pp
        v2 = ref[:, pr * PAIR:(pr + 1) * PAIR]
        zero = jnp.zeros_like(v2)
        dst[2 * pp * BLK:(2 * pp + 1) * BLK, :] = jnp.where(low, v2, zero)
        dst[(2 * pp + 1) * BLK:(2 * pp + 2) * BLK, :] = jnp.where(low, zero, v2)


def _unstack_heads(st, pp):
    return jnp.where(_low_half(), st[2 * pp * BLK:(2 * pp + 1) * BLK, :], st[(2 * pp + 1) * BLK:(2 * pp + 2) * BLK, :])


def _sink_col(sk_ref, kh):
    return jnp.concatenate([jnp.full((BLK, 1), sk_ref[kh * GROUP + h], _F32) for h in range(GROUP)], axis=0)


def _head_probs(qm, kd, mask, sink):
    s = jnp.where(mask, _dot(qm, kd, _NT) * (HEAD_DIM ** -0.5), NEG)
    m = jnp.maximum(jnp.max(s, axis=-1, keepdims=True), sink)
    p = jnp.exp(s - m)
    es = jnp.exp(sink - m)
    inv = 1.0 / (jnp.sum(p, axis=-1, keepdims=True) + es)
    return p * inv, es * inv


def _attn_fwd(x, g, wq, bq, sinks, kvd, wo, bo):
    S = x.shape[0]
    n = S // BLK

    def body(x_ref, g_ref, wq_ref, bq_ref, sk_ref, kp_ref, kc_ref, wo_ref, bo_ref, xo_ref, q_ref, o_ref, win):
        i = pl.program_id(0)
        xv = x_ref[...]
        hb = (_rms(xv)[0] * g_ref[...]).astype(_MXU)
        q_ref[...] = (_dot(hb, wq_ref[...]) + bq_ref[...]).astype(q_ref.dtype)
        win[0:BLK, :] = kp_ref[...]
        win[BLK:, :] = kc_ref[...]
        mask = _attn_mask(i, BLK)
        low = _low_half()
        for pr in range(N_HEADS // 2):
            kh = (2 * pr) // GROUP
            kd = win[:, kh * PAIR:(kh + 1) * PAIR]
            vd = win[:, (N_KV + kh) * PAIR:(N_KV + kh + 1) * PAIR]
            q2 = q_ref[:, pr * PAIR:(pr + 1) * PAIR]
            outs = []
            for half in range(2):
                qm = jnp.where(low if half == 0 else ~low, q2, jnp.zeros_like(q2))
                pbs, _ = _head_probs(qm, kd, mask, sk_ref[2 * pr + half])
                outs.append(_dot(pbs.astype(_MXU), vd))
            o_ref[:, pr * PAIR:(pr + 1) * PAIR] = jnp.where(low, outs[0], outs[1]).astype(o_ref.dtype)
        xo_ref[...] = xv + _dot(o_ref[...], wo_ref[...]) + bo_ref[...]

    blk = lambda w: pl.BlockSpec((BLK, w), lambda i: (i, 0))
    return pl.pallas_call(
        body, name="attn_fwd", grid=(n,),
        in_specs=[blk(D), _full((1, D)), _full((D, D)), _full((1, D)),
                  pl.BlockSpec(memory_space=pltpu.SMEM),
                  pl.BlockSpec((BLK, KVD), lambda i: (jnp.maximum(i - 1, 0), 0)), blk(KVD),
                  _full((D, D)), _full((1, D))],
        out_specs=[blk(D), blk(D), blk(D)],
        out_shape=[jax.ShapeDtypeStruct((S, D), _F32), jax.ShapeDtypeStruct((S, D), _MXU),
                   jax.ShapeDtypeStruct((S, D), _MXU)],
        scratch_shapes=[pltpu.VMEM((2 * BLK, KVD), _MXU)],
        compiler_params=_seq(),
    )(x, g, wq, bq, sinks, kvd, kvd, wo, bo)


def _attn_bwd(x, dy, q, o, g, wq, sinks, kvd, wo):
    S = x.shape[0]
    n = S // BLK
    all_rows = N_HEADS * BLK

    def body(x_ref, dy_ref, q_ref, o_ref, g_ref, wq_ref, sk_ref, kp_ref, kc_ref, wo_ref,
             dx_ref, dq_ref, h_ref, dc_ref, dp_ref, dbq_ref, dbo_ref, dg_ref, dsk_ref, win, dob, qs, dos, pall, dsall):
        i = pl.program_id(0)

        @pl.when(i == 0)
        def _():
            dbq_ref[...] = jnp.zeros_like(dbq_ref)
            dbo_ref[...] = jnp.zeros_like(dbo_ref)
            dg_ref[...] = jnp.zeros_like(dg_ref)
            dsk_ref[...] = jnp.zeros_like(dsk_ref)

        gv = g_ref[...]
        xv = x_ref[...]
        xn, r = _rms(xv)
        h_ref[...] = (xn * gv).astype(h_ref.dtype)
        dyv = dy_ref[...]
        dbo_ref[...] += _colsum(dyv)
        dob[...] = _dot(dyv.astype(_MXU), wo_ref[...], _NT).astype(dob.dtype)
        win[0:BLK, :] = kp_ref[...]
        win[BLK:, :] = kc_ref[...]
        mask = _attn_mask(i, BLK)
        low = _low_half()
        lane = lax.broadcasted_iota(jnp.int32, (1, LANES), 1)
        for pr in range(N_HEADS // 2):
            kh = (2 * pr) // GROUP
            kd = win[:, kh * PAIR:(kh + 1) * PAIR]
            vd = win[:, (N_KV + kh) * PAIR:(N_KV + kh + 1) * PAIR]
            q2 = q_ref[:, pr * PAIR:(pr + 1) * PAIR]
            do2 = dob[:, pr * PAIR:(pr + 1) * PAIR]
            od = do2.astype(_F32) * o_ref[:, pr * PAIR:(pr + 1) * PAIR].astype(_F32)
            for half in range(2):
                hd = 2 * pr + half
                rows = slice(hd * BLK, (hd + 1) * BLK)
                sel = low if half == 0 else ~low
                qm = jnp.where(sel, q2, jnp.zeros_like(q2))
                dom = jnp.where(sel, do2, jnp.zeros_like(do2))
                qs[rows, :] = qm
                dos[rows, :] = dom
                pbs, ps = _head_probs(qm, kd, mask, sk_ref[hd])
                pall[rows, :] = pbs.astype(_MXU)
                delta = jnp.sum(jnp.where(sel, od, 0.0), axis=-1, keepdims=True)
                dsall[rows, :] = (pbs * (_dot(dom, vd, _NT) - delta) * (HEAD_DIM ** -0.5)).astype(_MXU)
                dsk_ref[...] -= jnp.where(lane == hd, _colsum(ps * delta), 0.0)
        dq_all = []
        for kh in range(N_KV):
            ks = slice(kh * PAIR, (kh + 1) * PAIR)
            vs = slice((N_KV + kh) * PAIR, (N_KV + kh + 1) * PAIR)
            rows = slice(kh * STACK, (kh + 1) * STACK)
            dqst = _dot(dsall[rows, :], win[:, ks])
            dk = _dot(dsall[rows, :], qs[rows, :], _TN)
            dv = _dot(pall[rows, :], dos[rows, :], _TN)
            dp_ref[:, ks], dc_ref[:, ks] = dk[0:BLK, :], dk[BLK:, :]
            dp_ref[:, vs], dc_ref[:, vs] = dv[0:BLK, :], dv[BLK:, :]
            dq_all += [_unstack_heads(dqst, pp) for pp in range(GROUP // 2)]
        dq = jnp.concatenate(dq_all, axis=1)
        dbq_ref[...] += _colsum(dq)
        dqb = dq.astype(_MXU)
        dq_ref[...] = dqb
        dh = _dot(dqb, wq_ref[...], _NT)
        dg_ref[...] += _colsum(dh * xn)
        dx_ref[...] = dyv + _rms_bwd(dh, xn, r, gv)

    blk = lambda w: pl.BlockSpec((BLK, w), lambda i: (i, 0))
    return pl.pallas_call(
        body, name="attn_bwd", grid=(n,),
        in_specs=[blk(D), blk(D), blk(D), blk(D), _full((1, D)), _full((D, D)),
                  pl.BlockSpec(memory_space=pltpu.SMEM),
                  pl.BlockSpec((BLK, KVD), lambda i: (jnp.maximum(i - 1, 0), 0)), blk(KVD), _full((D, D))],
        out_specs=[blk(D), blk(D), blk(D), blk(KVD), blk(KVD),
                   _full((1, D)), _full((1, D)), _full((1, D)), _full((1, LANES))],
        out_shape=[jax.ShapeDtypeStruct((S, D), _F32), jax.ShapeDtypeStruct((S, D), _MXU),
                   jax.ShapeDtypeStruct((S, D), _MXU), jax.ShapeDtypeStruct((S, KVD), _F32),
                   jax.ShapeDtypeStruct((S, KVD), _F32), jax.ShapeDtypeStruct((1, D), _F32),
                   jax.ShapeDtypeStruct((1, D), _F32), jax.ShapeDtypeStruct((1, D), _F32),
                   jax.ShapeDtypeStruct((1, LANES), _F32)],
        scratch_shapes=[pltpu.VMEM((2 * BLK, KVD), _MXU), pltpu.VMEM((BLK, D), _MXU),
                        pltpu.VMEM((all_rows, PAIR), _MXU), pltpu.VMEM((all_rows, PAIR), _MXU),
                        pltpu.VMEM((all_rows, 2 * BLK), _MXU), pltpu.VMEM((all_rows, 2 * BLK), _MXU)],
        compiler_params=_seq(),
    )(x, dy, q, o, g, wq, sinks, kvd, kvd, wo)


def _loss_bwd(x, g, tgt):
    S = x.shape[0]
    T = _tile(S, 512, 8)

    def body(x_ref, g_ref, t_ref, dx_ref, ls_ref, dg_ref):
        @pl.when(pl.program_id(0) == 0)
        def _():
            ls_ref[...] = jnp.zeros_like(ls_ref)
            dg_ref[...] = jnp.zeros_like(dg_ref)

        gv = g_ref[...]
        xn, r = _rms(x_ref[...])
        err = xn * gv - t_ref[...]
        ls_ref[...] += 0.5 * jnp.sum(jnp.mean(err * err, axis=-1, keepdims=True))
        dyv = err * (1.0 / D)
        dg_ref[...] += _colsum(dyv * xn)
        dx_ref[...] = _rms_bwd(dyv, xn, r, gv)

    return pl.pallas_call(
        body, name="loss_bwd", grid=(S // T,),
        in_specs=[pl.BlockSpec((T, D), lambda i: (i, 0)), _full((1, D)), pl.BlockSpec((T, D), lambda i: (i, 0))],
        out_specs=[pl.BlockSpec((T, D), lambda i: (i, 0)), _full((8, LANES)), _full((1, D))],
        out_shape=[jax.ShapeDtypeStruct((S, D), _F32), jax.ShapeDtypeStruct((8, LANES), _F32),
                   jax.ShapeDtypeStruct((1, D), _F32)],
        compiler_params=_seq(),
    )(x, g, tgt)


def _me():
    return 4 * lax.axis_index("x") + 2 * lax.axis_index("y") + lax.axis_index("c")


def _peer(j):
    x, y, c = lax.axis_index("x"), lax.axis_index("y"), lax.axis_index("c")
    px = 1 - x if j & 4 else x
    py = 1 - y if j & 2 else y
    pc = 1 - c if j & 1 else c
    return (px, py, pc), 4 * px + 2 * py + pc


_HBM = pl.BlockSpec(memory_space=pltpu.HBM)
_SEMS = pl.BlockSpec(memory_space=pltpu.SEMAPHORE)
_EFFECT = pltpu.SideEffectType.DATAFLOW_SIDE_EFFECTING


def _in_hbm(a):
    return pltpu.with_memory_space_constraint(a, pltpu.HBM)


def _start_copies(name, groups):
    flat = []
    for srcs, zones, _ in groups:
        flat += [_in_hbm(a) for a in srcs] + [_in_hbm(lax.empty(z.shape, z.dtype)) for z in zones]
    n_in, n_g = len(flat), len(groups)

    def body(*refs):
        sems = refs[n_in:n_in + 2 * n_g]
        me, k = _me(), 0
        for gi, (srcs, zones, plan) in enumerate(groups):
            src_refs, zone_refs = refs[k:k + len(srcs)], refs[k + len(srcs):k + len(srcs) + len(zones)]
            k += len(srcs) + len(zones)
            for t, (si, zi, src_of, dst_of) in enumerate(plan):
                for j in range(1, N_DEV):
                    dev, pk = _peer(j)
                    pltpu.make_async_remote_copy(
                        src_ref=src_of(src_refs[si], pk), dst_ref=dst_of(zone_refs[zi], me),
                        send_sem=sems[2 * gi].at[t * (N_DEV - 1) + j - 1], recv_sem=sems[2 * gi + 1].at[t * (N_DEV - 1) + j - 1],
                        device_id=dev, device_id_type=pl.DeviceIdType.MESH).start()
                pltpu.make_async_copy(src_of(src_refs[si], me), dst_of(zone_refs[zi], me),
                                      sems[2 * gi].at[len(plan) * (N_DEV - 1) + t]).start()
        refs[-1][...] = jnp.zeros_like(refs[-1])

    sem_shapes = []
    for _, _, plan in groups:
        sem_shapes += [pltpu.SemaphoreType.DMA((len(plan) * N_DEV,)), pltpu.SemaphoreType.DMA((len(plan) * (N_DEV - 1),))]
    outs = pl.pallas_call(
        body, name=name,
        out_shape=(*sem_shapes, *[pltpu.HBM(a.shape, a.dtype) for a in flat], jax.ShapeDtypeStruct((8, LANES), _F32)),
        in_specs=[_HBM] * n_in,
        out_specs=(*[_SEMS] * (2 * n_g), *[_HBM] * n_in, pl.BlockSpec(memory_space=pltpu.VMEM)),
        input_output_aliases={k: 2 * n_g + k for k in range(n_in)},
        compiler_params=pltpu.CompilerParams(has_side_effects=_EFFECT),
    )(*flat)
    handles, k = [], 2 * n_g
    for gi, (srcs, zones, plan) in enumerate(groups):
        ns, nz = len(srcs), len(zones)
        handles.append((outs[2 * gi], outs[2 * gi + 1], list(outs[k:k + ns]), list(outs[k + ns:k + ns + nz]), plan))
        k += ns + nz
    return handles, outs[-1]


def _wait_copies(name, handles, after):
    flat = []
    for _, _, srcs, zones, _ in handles:
        flat += srcs + zones
    n_in, n_g = len(flat), len(handles)

    def body(*refs):
        sems = refs[n_in:n_in + 2 * n_g]
        me, k, local, remote = _me(), 0, [], []
        for gi, (_, _, srcs, zones, plan) in enumerate(handles):
            ns, nz = len(srcs), len(zones)
            src_refs, zone_refs = refs[k:k + ns], refs[k + ns:k + ns + nz]
            k += ns + nz
            for t, (si, zi, src_of, dst_of) in enumerate(plan):
                local.append(pltpu.make_async_copy(src_of(src_refs[si], me), dst_of(zone_refs[zi], me),
                                                   sems[2 * gi].at[len(plan) * (N_DEV - 1) + t]))
                for j in range(1, N_DEV):
                    dev, pk = _peer(j)
                    remote.append(pltpu.make_async_remote_copy(
                        src_ref=src_of(src_refs[si], pk), dst_ref=dst_of(zone_refs[zi], pk),
                        send_sem=sems[2 * gi].at[t * (N_DEV - 1) + j - 1], recv_sem=sems[2 * gi + 1].at[t * (N_DEV - 1) + j - 1],
                        device_id=dev, device_id_type=pl.DeviceIdType.MESH))
        for cp in remote:
            cp.wait_send()
            cp.wait_recv()
        for cp in local:
            cp.wait()

    sem_args = []
    for send, recv, _, _, _ in handles:
        sem_args += [send, recv]
    outs = pl.pallas_call(
        body, name=name, out_shape=tuple(pltpu.HBM(a.shape, a.dtype) for a in flat),
        in_specs=[_HBM] * n_in + [_SEMS] * (2 * n_g) + [pl.BlockSpec(memory_space=pl.ANY)],
        out_specs=tuple([_HBM] * n_in), input_output_aliases={k: k for k in range(n_in)},
        compiler_params=pltpu.CompilerParams(has_side_effects=_EFFECT),
    )(*flat, *sem_args, after)
    res, k = [], 0
    for _, _, srcs, zones, _ in handles:
        res.append(list(outs[k + len(srcs):k + len(srcs) + len(zones)]))
        k += len(srcs) + len(zones)
    return res


def _rows(axis, size):
    def of(ref, b):
        start = b * size
        if size % 8 == 0:
            start = pl.multiple_of(start, 8)
        return ref.at[(slice(None),) * axis + (pl.ds(start, size),)]
    return of


def _whole(ref, b):
    return ref


def _slot(ref, b):
    return ref.at[b]


def _gather_group(shards):
    zones, plan = [], []
    for k, (a, axis) in enumerate(shards):
        zones.append(jax.ShapeDtypeStruct(a.shape[:axis] + (N_DEV * a.shape[axis],) + a.shape[axis + 1:], a.dtype))
        plan.append((k, k, _whole, _rows(axis, a.shape[axis])))
    return [a for a, _ in shards], zones, plan


def _scatter_group(grads):
    zones, plan = [], []
    for k, (a, axis) in enumerate(grads):
        size = a.shape[axis] // N_DEV
        zones.append(jax.ShapeDtypeStruct((N_DEV,) + a.shape[:axis] + (size,) + a.shape[axis + 1:], a.dtype))
        plan.append((k, k, _rows(axis, size), _slot))
    return [a for a, _ in grads], zones, plan


def _sum_landed(land):
    g = land[0].astype(_F32)
    for b in range(1, N_DEV):
        g = g + land[b].astype(_F32)
    return g


def _landed_specs(n_layers, tr, C, nr):
    def spec(k):
        return pl.BlockSpec((N_DEV, tr, C), lambda l, i: (0, jnp.where(l == k, i, jnp.where(l < k, 0, nr - 1)), 0))
    return [spec(k) for k in range(n_layers)]


def _per_layer(l, zone_refs, fn):
    for k, ref in enumerate(zone_refs):
        @pl.when(l == k)
        def _(ref=ref):
            fn(_sum_landed(ref))


def _sum8(zones):
    L = len(zones)
    _, R, C = zones[0].shape
    tr = _tile(R, 352, 16)
    nr = R // tr

    def body(*refs):
        o_ref = refs[L]

        def put(g):
            o_ref[...] = g

        _per_layer(pl.program_id(0), refs[:L], put)

    return pl.pallas_call(
        body, name="sum8", grid=(L, nr), in_specs=_landed_specs(L, tr, C, nr),
        out_specs=pl.BlockSpec((tr, C), lambda l, i: (l * nr + i, 0)),
        out_shape=jax.ShapeDtypeStruct((L * R, C), _F32), compiler_params=_seq(2),
    )(*zones)


def _adam_update(gv, w_ref, m_ref, v_ref, d_ref, mo_ref, vo_ref):
    mn = ADAM_B1 * m_ref[...] + (1.0 - ADAM_B1) * gv
    vn = ADAM_B2 * v_ref[...] + (1.0 - ADAM_B2) * (gv * gv)
    mo_ref[...] = mn
    vo_ref[...] = vn
    d_ref[...] = -ADAM_LR * ((mn / (1.0 - ADAM_B1 ** ADAM_STEP)) / (jnp.sqrt(vn / (1.0 - ADAM_B2 ** ADAM_STEP)) + ADAM_EPS)
                             + ADAM_WD * w_ref[...])


def _earlier(outs):
    outs = list(outs or [])
    return outs, [pl.BlockSpec(memory_space=pl.ANY)] * len(outs)


def _adamw(g, w, m, v, name, first_row=0, earlier=None):
    Rg, C = g.shape
    R = w.shape[0]
    tr = _tile(Rg if first_row == 0 else min(Rg, first_row), 256, 16)
    off = first_row // tr
    more, more_specs = _earlier(earlier)

    def body(g_ref, w_ref, m_ref, v_ref, *rest):
        go_ref, d_ref, mo_ref, vo_ref = rest[-4:]
        gv = g_ref[...]
        go_ref[...] = gv
        _adam_update(gv, w_ref, m_ref, v_ref, d_ref, mo_ref, vo_ref)

    row = pl.BlockSpec((tr, C), lambda i: (i + off, 0))
    return pl.pallas_call(
        body, name=name, grid=(Rg // tr,), in_specs=[pl.BlockSpec((tr, C), lambda i: (i, 0))] + [row] * 3 + more_specs,
        out_specs=[row] * 4, out_shape=[jax.ShapeDtypeStruct((R, C), _F32)] * 4,
        input_output_aliases={4 + k: k for k in range(len(more))}, compiler_params=_seq(),
    )(g, w, m, v, *more)


def _adamw_landed(zones, w, m, v, name, first_layer=0, earlier=None):
    L = len(zones)
    _, R, C = zones[0].shape
    tr = _tile(R, 176, 16)
    nr = R // tr
    more, more_specs = _earlier(earlier)

    def body(*refs):
        w_ref, m_ref, v_ref = refs[L:L + 3]
        g_ref, d_ref, mo_ref, vo_ref = refs[-4:]

        def update(g):
            g_ref[...] = g
            _adam_update(g, w_ref, m_ref, v_ref, d_ref, mo_ref, vo_ref)

        _per_layer(pl.program_id(0), refs[:L], update)

    row = pl.BlockSpec((tr, C), lambda l, i: ((l + first_layer) * nr + i, 0))
    return pl.pallas_call(
        body, name=name, grid=(L, nr), in_specs=_landed_specs(L, tr, C, nr) + [row] * 3 + more_specs,
        out_specs=[row] * 4, out_shape=[jax.ShapeDtypeStruct(w.shape, _F32)] * 4,
        input_output_aliases={L + 3 + k: k for k in range(len(more))}, compiler_params=_seq(2),
    )(*zones, w, m, v, *more)


def _pack(parts):
    flat = jnp.concatenate([p.reshape(-1).astype(_F32) for p in parts])
    n = flat.shape[0]
    rows = -(-n // (8 * LANES)) * 8
    return jnp.pad(flat, (0, rows * LANES - n)).reshape(rows, LANES)


def _unpack(packed, shapes):
    flat, out, k = packed.reshape(-1), [], 0
    for s in shapes:
        n = 1
        for d in s:
            n *= d
        out.append(flat[k:k + n].reshape(s))
        k += n
    return out


def kernel(x, norm1_g, norm2_g, pool_w, pool_scale, kv_norm_g, w_kv, b_kv, w_q, b_q, sinks, w_o, b_o, ffn_up, ffn_conv_w, ffn_conv_b, ffn_down, final_g, loss_target, m_norm1_g, m_norm2_g, m_pool_w, m_pool_scale, m_kv_norm_g, m_w_kv, m_b_kv, m_w_q, m_b_q, m_sinks, m_w_o, m_b_o, m_ffn_up, m_ffn_conv_w, m_ffn_conv_b, m_ffn_down, m_final_g, v_norm1_g, v_norm2_g, v_pool_w, v_pool_scale, v_kv_norm_g, v_w_kv, v_b_kv, v_w_q, v_b_q, v_sinks, v_w_o, v_b_o, v_ffn_up, v_ffn_conv_w, v_ffn_conv_b, v_ffn_down, v_final_g):
    S = x.shape[1]
    F2s = ffn_up.shape[2]
    F2 = N_DEV * F2s
    me = _me()
    x0 = x.reshape(S, D)
    tgt = loss_target.reshape(S, D)
    row = lambda a: a.reshape(1, -1)

    small = _pack([pool_scale, ffn_conv_w])
    wire = lambda a: a.astype(_MXU)
    ffn_w = lambda l: [(wire(ffn_up[l]).T, 0), (wire(ffn_down[l]), 0)]
    attn_w = lambda j: [(wire(w_q[j]), 0), (wire(w_o[j]), 0)]
    gathers, token = _start_copies("gather_start", [_gather_group(g) for g in (
        [(wire(pool_w[0]), 1), (small[None], 0)], ffn_w(0), [(wire(pool_w[1]), 1)] + ffn_w(1),
        [(wire(w_kv), 0)] + attn_w(0), ffn_w(2), attn_w(1), ffn_w(3))])

    def gathered(k, after):
        return _wait_copies("gather_wait_%d" % k, [gathers[k]], after)[0]

    pw, up_t, down, wq, wo = [None] * N_A, [None] * DEPTH, [None] * DEPTH, [None] * 2, [None] * 2
    pw[0], small_all = gathered(0, token)
    n_ps = pool_scale.size
    small_all = small_all.reshape(N_DEV, -1)
    pscale = jnp.transpose(small_all[:, :n_ps].reshape(N_DEV, N_A, D // N_DEV), (1, 0, 2)).reshape(N_A, D)
    conv_w = jnp.transpose(small_all[:, n_ps:n_ps + ffn_conv_w.size].reshape(N_DEV, DEPTH, CONV_W, F2s),
                           (1, 2, 0, 3)).reshape(DEPTH, CONV_W, F2)

    def dup(a):
        a4 = a.reshape(a.shape[:-1] + (2 * N_KV, 1, HEAD_DIM))
        return jnp.broadcast_to(a4, a.shape[:-1] + (2 * N_KV, 2, HEAD_DIM)).reshape(a.shape[:-1] + (KVD,))

    def fold(a):
        return a.reshape(a.shape[:-1] + (2 * N_KV, 2, HEAD_DIM)).sum(axis=-2).reshape(a.shape[:-1] + (2 * N_KV * HEAD_DIM,))

    xs, us, qs, os_ = [x0], [], [], []
    xc = x0
    kvd = None
    for l in range(DEPTH):
        if l == 1:
            pw[1], up_t[1], down[1] = gathered(2, xc)
        if l == 3:
            wq[1], wo[1] = gathered(5, xc)
        if l < N_A:
            xc = _pool_fwd(xc, row(norm1_g[l]), pw[l], row(pscale[l]))
        else:
            j = l - N_A
            xc, q, o = _attn_fwd(xc, row(norm1_g[l]), wq[j], row(b_q[j]), sinks[j], kvd, wo[j], row(b_o[j]))
            qs.append(q)
            os_.append(o)
        xs.append(xc)
        if l != 1:
            up_t[l], down[l] = gathered((1, None, 4, 6)[l], xc)
        xc, u, c = _ffn_fwd(xc, row(norm2_g[l]), up_t[l], conv_w[l], row(ffn_conv_b[l]), down[l])
        us.append((u, c))
        xs.append(xc)
        if l == N_A - 1:
            wkv, wq[0], wo[0] = gathered(3, xc)
            wkv_d, bkv_d = dup(wkv), dup(row(b_kv))
            kvd = _kv_fwd(xc, row(kv_norm_g), wkv_d, bkv_d)

    dx, loss_p, d_final = _loss_bwd(xc, row(final_g), tgt)
    d_n1, d_n2, d_cw, d_cb = [None] * DEPTH, [None] * DEPTH, [None] * DEPTH, [None] * DEPTH
    d_bq, d_bo, d_sk, d_ps, dkv_parts = [None] * 2, [None] * 2, [None] * 2, [None] * N_A, []
    up_z, down_z, mix_z = [None] * DEPTH, [None] * DEPTH, [None] * DEPTH
    token = None

    def after(gain):
        return gain if token is None else gain + token[0:1, 0:1]

    rep_names = ["norm1_g", "norm2_g", "kv_norm_g", "b_kv", "b_q", "sinks", "b_o", "ffn_conv_b", "final_g"]

    def small_parts(n1_rest, ps_rest):
        zero = jnp.zeros((1, D), _F32)
        return [jnp.concatenate([zero] + n1_rest), jnp.concatenate(d_n2), d_kvg, fold(d_bkv), jnp.concatenate(d_bq),
                jnp.concatenate([s[:, :N_HEADS] for s in d_sk]), jnp.concatenate(d_bo), jnp.concatenate(d_cb), d_final,
                jnp.concatenate([zero] + ps_rest), jnp.stack(d_cw), loss_p[0:1, 0:1]]

    for l in reversed(range(DEPTH)):
        x_in, x_mid, x_out = xs[2 * l], xs[2 * l + 1], xs[2 * l + 2]
        mixer_grads = []
        if l == N_A - 1:
            dx, d_wkv, d_bkv, d_kvg = _kv_bwd(x_out, dx, after(row(kv_norm_g)), wkv_d, *dkv_parts)
            mixer_grads.append((fold(d_wkv).astype(_MXU), 0))
        dy = dx
        dx, du, a, h, d_cw[l], d_cb[l], d_n2[l] = _ffn_bwd(
            x_mid, dy, *us[l], after(row(norm2_g[l])), up_t[l], conv_w[l], down[l])
        g_up = _tn_matmul(du, h, "tn_up")
        if l > 0:
            (up_z[l], down_z[l]), token = _start_copies("scatter_ffn_%d" % l, [
                _scatter_group([(g_up, 0)]), _scatter_group([(_tn_matmul(a, dy, "tn_down"), 0)])])
        else:
            (small_handle, up_z[l]), token = _start_copies("scatter_up_0", [
                _gather_group([(_pack(small_parts(d_n1[1:], d_ps[1:]))[None], 0)]), _scatter_group([(g_up, 0)])])
            (down_z[l],), token = _start_copies("scatter_down_0", [
                _scatter_group([(_tn_matmul(a, dy, "tn_down", token), 0)])])
        dy = dx
        if l < N_A:
            dx, d_pw, d_ps[l], d_n1[l] = _pool_bwd(x_in, dy, after(row(norm1_g[l])), pw[l], row(pscale[l]))
            mixer_grads.append((d_pw.astype(_MXU), 1))
        else:
            j = l - N_A
            dx, dq, h, d_cur, d_prev, d_bq[j], d_bo[j], d_n1[l], d_sk[j] = _attn_bwd(
                x_in, dy, qs[j], os_[j], after(row(norm1_g[l])), wq[j], sinks[j], kvd, wo[j])
            mixer_grads += [(_tn_matmul(h, dq, "tn_q"), 0), (_tn_matmul(os_[j], dy, "tn_o"), 0)]
            dkv_parts += [d_cur, d_prev]
        groups = [_scatter_group(mixer_grads)]
        if l == 0:
            groups.append(_gather_group([(_pack([d_n1[0], d_ps[0]])[None], 0)]))
        handles, token = _start_copies("scatter_mixer_%d" % l, groups)
        mix_z[l], late_handle = handles[0], handles[-1]

    moms = dict(norm1_g=(m_norm1_g, v_norm1_g), norm2_g=(m_norm2_g, v_norm2_g), pool_w=(m_pool_w, v_pool_w),
                pool_scale=(m_pool_scale, v_pool_scale), kv_norm_g=(m_kv_norm_g, v_kv_norm_g), w_kv=(m_w_kv, v_w_kv),
                b_kv=(m_b_kv, v_b_kv), w_q=(m_w_q, v_w_q), b_q=(m_b_q, v_b_q), sinks=(m_sinks, v_sinks),
                w_o=(m_w_o, v_w_o), b_o=(m_b_o, v_b_o), ffn_up=(m_ffn_up, v_ffn_up),
                ffn_conv_w=(m_ffn_conv_w, v_ffn_conv_w), ffn_conv_b=(m_ffn_conv_b, v_ffn_conv_b),
                ffn_down=(m_ffn_down, v_ffn_down), final_g=(m_final_g, v_final_g))
    given = dict(norm1_g=norm1_g, norm2_g=norm2_g, kv_norm_g=kv_norm_g, b_kv=b_kv, b_q=b_q, sinks=sinks, b_o=b_o,
                 ffn_conv_b=ffn_conv_b, final_g=final_g, pool_scale=pool_scale, ffn_conv_w=ffn_conv_w,
                 pool_w=pool_w, w_kv=w_kv, w_q=w_q, w_o=w_o, ffn_up=ffn_up, ffn_down=ffn_down)
    grad, delta, new_m, new_v = {}, {}, {}, {}

    def update(name, zones, cols, first_layer=0, earlier=None, last=True):
        w = given[name]
        two_d = lambda a: a.reshape(-1, cols)
        wmv = (two_d(w), two_d(moms[name][0]), two_d(moms[name][1]))
        if name == "ffn_up":
            g = jnp.swapaxes(_sum8(zones).reshape(len(zones), F2s, D), 1, 2).reshape(-1, F2s)
            outs = _adamw(g, *wmv, "adamw_" + name, first_layer * D, earlier)
        else:
            landed = [z.reshape(N_DEV, -1, cols) for z in zones]
            outs = _adamw_landed(landed, *wmv, "adamw_" + name, first_layer, earlier)
        if last:
            grad[name], delta[name], new_m[name], new_v[name] = (o.reshape(w.shape) for o in outs)
        return outs

    rest = _wait_copies("scatter_wait_rest", up_z[1:] + down_z[1:] + mix_z[1:], token)
    up_r, down_r, mix_r = rest[:DEPTH - 1], rest[DEPTH - 1:2 * (DEPTH - 1)], rest[2 * (DEPTH - 1):]
    up_1 = update("ffn_up", [z[0] for z in up_r], F2s, 1, last=False)
    down_1 = update("ffn_down", [z[0] for z in down_r], D, 1, last=False)
    update("w_q", [mix_r[N_A - 1][0], mix_r[N_A][0]], D)
    update("w_o", [mix_r[N_A - 1][1], mix_r[N_A][1]], D)
    update("w_kv", [mix_r[N_A - 2][0]], w_kv.shape[1])
    pw_1 = update("pool_w", [mix_r[N_A - 2][1]], GC, 1, last=False)

    (up_0,), (down_0,), (pw_0,), (early,), (late,) = _wait_copies(
        "scatter_wait_0", [up_z[0], down_z[0], mix_z[0], small_handle, late_handle],
        lax.optimization_barrier((up_1[1], down_1[1], pw_1[1]))[0])
    late = _sum8([late]).reshape(-1)
    tot = _unpack(_sum8([early]), [given[k].shape for k in rep_names] + [(N_A, D), (DEPTH, CONV_W, F2), ()])
    tot[0] = tot[0].at[0].add(late[:D])
    tot[-3] = tot[-3].at[0].add(late[D:2 * D])
    grad.update(zip(rep_names, tot))
    loss = tot[-1]
    grad["pool_scale"] = lax.dynamic_slice_in_dim(tot[-3], me * (D // N_DEV), D // N_DEV, axis=1)
    grad["ffn_conv_w"] = lax.dynamic_slice_in_dim(tot[-2], me * F2s, F2s, axis=2)
    small_names = rep_names + ["pool_scale", "ffn_conv_w"]
    shapes = [given[k].shape for k in small_names]
    outs = _adamw(_pack([grad[k] for k in small_names]), _pack([given[k] for k in small_names]),
                  _pack([moms[k][0] for k in small_names]), _pack([moms[k][1] for k in small_names]), "adamw_small")
    for dst, packed in zip((delta, new_m, new_v), outs[1:]):
        dst.update(zip(small_names, _unpack(packed, shapes)))

    update("ffn_up", [up_0], F2s, 0, up_1)
    update("ffn_down", [down_0], D, 0, down_1)
    update("pool_w", [pw_0], GC, 0, pw_1)

    names = ["norm1_g", "norm2_g", "pool_w", "pool_scale", "kv_norm_g", "w_kv", "b_kv", "w_q", "b_q", "sinks", "w_o",
             "b_o", "ffn_up", "ffn_conv_w", "ffn_conv_b", "ffn_down", "final_g"]
    return (loss, dx.reshape(x.shape), *[grad[k] for k in names], *[delta[k] for k in names],
            *[new_m[k] for k in names], *[new_v[k] for k in names])
```

```python
import jax
import jax.numpy as jnp
from jax import lax
from jax.experimental import pallas as pl
from jax.experimental.pallas import tpu as pltpu

_F32 = jnp.float32
_MXU = jnp.bfloat16

N_DEV = 8
D = 1024
DEPTH = 4
N_A = 2
POOL_WINDOWS = (2, 4, 8, 16)
GC = D // len(POOL_WINDOWS)
HALO = 16
HEAD_DIM = 64
N_HEADS = D // HEAD_DIM
GROUP = 8
N_KV = N_HEADS // GROUP
BLK = 128
PAIR = 2 * HEAD_DIM
KVD = 4 * N_KV * HEAD_DIM
CONV_W = 3
EPS = 1e-5
NEG = -1e30

ADAM_LR = 0.001
ADAM_B1 = 0.9
ADAM_B2 = 0.999
ADAM_EPS = 1e-08
ADAM_WD = 0.01
ADAM_STEP = 10

V7X_VMEM_LIMIT = 56 * 1024 * 1024
LANES = 128

_NT = (((1,), (1,)), ((), ()))
_TN = (((0,), (0,)), ((), ()))


def _params(**kw):
    return pltpu.CompilerParams(vmem_limit_bytes=V7X_VMEM_LIMIT, **kw)


def _seq(n=1):
    return _params(dimension_semantics=("arbitrary",) * n)


def _dot(a, b, dims=None):
    if dims is None:
        return jnp.dot(a, b, preferred_element_type=_F32)
    return lax.dot_general(a, b, dims, preferred_element_type=_F32)


def _rms(x):
    r = lax.rsqrt(jnp.mean(x * x, axis=-1, keepdims=True) + EPS)
    return x * r, r


def _rms_bwd(dh, xn, r, g):
    dxn = dh * g
    return r * (dxn - xn * jnp.mean(dxn * xn, axis=-1, keepdims=True))


def _colsum(a):
    return jnp.sum(a, axis=0, keepdims=True)


def _tile(n, want, mult=8):
    for t in range(min(want, n), 0, -1):
        if n % t == 0 and t % mult == 0:
            return t
    return n


def _full(shape):
    zeros = (0,) * len(shape)
    return pl.BlockSpec(shape, lambda *_: zeros)


def _window_sum(ext, win, trailing):
    R = ext.shape[0]
    acc, k = ext, 1
    while k < win:
        acc = acc + pltpu.roll(acc, k if trailing else R - k, axis=0)
        k *= 2
    return acc


def _pool_windows(hbuf, h, row, T):
    out = []
    for gi, win in enumerate(POOL_WINDOWS):
        cs = slice(gi * GC, (gi + 1) * GC)
        acc = _window_sum(hbuf[:, cs], win, True)[HALO:, :]
        cnt = jnp.minimum(row + 1, win).astype(_F32)
        out.append((acc / cnt - h[:, cs], cnt))
    return out


def _pool_fwd(x, g, w, sc):
    S = x.shape[0]
    T = _tile(S, 512, HALO)
    n, hb = S // T, T // HALO

    def body(x_ref, xh_ref, g_ref, w_ref, sc_ref, o_ref, hbuf):
        i = pl.program_id(0)
        gv = g_ref[...]
        xv = x_ref[...]
        h = _rms(xv)[0] * gv
        hbuf[0:HALO, :] = jnp.where(i > 0, _rms(xh_ref[...])[0] * gv, 0.0)
        hbuf[HALO:, :] = h
        row = i * T + lax.broadcasted_iota(jnp.int32, (T, 1), 0)
        for gi, (p, _) in enumerate(_pool_windows(hbuf, h, row, T)):
            cs = slice(gi * GC, (gi + 1) * GC)
            z = _dot(p.astype(_MXU), w_ref[gi])
            o_ref[:, cs] = xv[:, cs] + z * sc_ref[:, cs]

    return pl.pallas_call(
        body, name="pool_fwd", grid=(n,),
        in_specs=[pl.BlockSpec((T, D), lambda i: (i, 0)),
                  pl.BlockSpec((HALO, D), lambda i: (jnp.maximum(i * hb - 1, 0), 0)),
                  _full((1, D)), _full((4, GC, GC)), _full((1, D))],
        out_specs=pl.BlockSpec((T, D), lambda i: (i, 0)),
        out_shape=jax.ShapeDtypeStruct((S, D), _F32),
        scratch_shapes=[pltpu.VMEM((T + HALO, D), _F32)],
        compiler_params=_seq(),
    )(x, x, g, w, sc)


def _pool_bwd(x, dy, g, w, sc):
    S = x.shape[0]
    T = _tile(S, 512, HALO)
    n, hb = S // T, T // HALO

    def body(x_ref, xh_ref, dy_ref, dyh_ref, g_ref, w_ref, sc_ref, dx_ref, dw_ref, dsc_ref, dg_ref,
             hbuf, qbuf, dhbuf):
        i = pl.program_id(0)

        @pl.when(i == 0)
        def _():
            dw_ref[...] = jnp.zeros_like(dw_ref)
            dsc_ref[...] = jnp.zeros_like(dsc_ref)
            dg_ref[...] = jnp.zeros_like(dg_ref)

        gv = g_ref[...]
        xv = x_ref[...]
        xn, r = _rms(xv)
        h = xn * gv
        hbuf[0:HALO, :] = jnp.where(i > 0, _rms(xh_ref[...])[0] * gv, 0.0)
        hbuf[HALO:, :] = h
        dyv = dy_ref[...]
        dz = dyv * sc_ref[...]
        dzh = jnp.where(i < n - 1, dyh_ref[...], 0.0) * sc_ref[...]
        row = i * T + lax.broadcasted_iota(jnp.int32, (T, 1), 0)
        rowh = (i + 1) * T + lax.broadcasted_iota(jnp.int32, (HALO, 1), 0)
        for gi, (p, cnt) in enumerate(_pool_windows(hbuf, h, row, T)):
            win = POOL_WINDOWS[gi]
            cs = slice(gi * GC, (gi + 1) * GC)
            pb = p.astype(_MXU)
            wg = w_ref[gi]
            dsc_ref[:, cs] += _colsum(dyv[:, cs] * _dot(pb, wg))
            dzb = dz[:, cs].astype(_MXU)
            dw_ref[gi] += _dot(pb, dzb, _TN)
            dp = _dot(dzb, wg, _NT)
            dph = _dot(dzh[:, cs].astype(_MXU), wg, _NT)
            qbuf[0:T, cs] = dp / cnt
            qbuf[T:T + HALO, cs] = dph / jnp.minimum(rowh + 1, win).astype(_F32)
            dhbuf[:, cs] = _window_sum(qbuf[:, cs], win, False)[0:T, :] - dp
        dh = dhbuf[...]
        dg_ref[...] += _colsum(dh * xn)
        dx_ref[...] = dyv + _rms_bwd(dh, xn, r, gv)

    return pl.pallas_call(
        body, name="pool_bwd", grid=(n,),
        in_specs=[pl.BlockSpec((T, D), lambda i: (i, 0)),
                  pl.BlockSpec((HALO, D), lambda i: (jnp.maximum(i * hb - 1, 0), 0)),
                  pl.BlockSpec((T, D), lambda i: (i, 0)),
                  pl.BlockSpec((HALO, D), lambda i: (jnp.minimum((i + 1) * hb, S // HALO - 1), 0)),
                  _full((1, D)), _full((4, GC, GC)), _full((1, D))],
        out_specs=[pl.BlockSpec((T, D), lambda i: (i, 0)), _full((4, GC, GC)), _full((1, D)), _full((1, D))],
        out_shape=[jax.ShapeDtypeStruct((S, D), _F32), jax.ShapeDtypeStruct((4, GC, GC), _F32),
                   jax.ShapeDtypeStruct((1, D), _F32), jax.ShapeDtypeStruct((1, D), _F32)],
        scratch_shapes=[pltpu.VMEM((T + HALO, D), _F32), pltpu.VMEM((T + HALO, D), _F32), pltpu.VMEM((T, D), _F32)],
        compiler_params=_seq(),
    )(x, x, dy, dy, g, w, sc)


FFN_FWD_TILE, FFN_FWD_CHUNKS = 256, 1
FFN_BWD_TILE, FFN_BWD_CHUNKS = 256, 1
EDGE = 8


def _shift_down(v, k, prev):
    r = pltpu.roll(v, k, axis=0)
    i8 = lax.broadcasted_iota(jnp.int32, (EDGE, v.shape[1]), 0)
    head = jnp.where(i8 >= k, r[0:EDGE, :], pltpu.roll(prev, k, axis=0))
    return jnp.concatenate([head, r[EDGE:, :]], axis=0)


def _shift_up(v, k, nxt):
    T = v.shape[0]
    r = pltpu.roll(v, T - k, axis=0)
    i8 = lax.broadcasted_iota(jnp.int32, (EDGE, v.shape[1]), 0)
    tail = jnp.where(i8 < EDGE - k, r[T - EDGE:, :], pltpu.roll(nxt, EDGE - k, axis=0))
    return jnp.concatenate([r[:T - EDGE, :], tail], axis=0)


def _load_weights(i, pairs, sems):
    @pl.when(i == 0)
    def _():
        cps = [pltpu.make_async_copy(src, dst, sems.at[k]) for k, (src, dst) in enumerate(pairs)]
        for cp in cps:
            cp.start()
        for cp in cps:
            cp.wait()


def _ffn_fwd(x, g, wup_t, cw, cb, wdn):
    S = x.shape[0]
    F2 = wup_t.shape[0]
    F = F2 // 2
    C = F // FFN_FWD_CHUNKS
    T = _tile(S, FFN_FWD_TILE, 16)
    n = S // T

    def body(x_ref, g_ref, wup_hbm, cw_ref, cb_ref, wdn_hbm, o_ref, u_ref, c_ref, wup, wdnv, carry, sems):
        i = pl.program_id(0)
        _load_weights(i, [(wup_hbm, wup), (wdn_hbm, wdnv)], sems)

        @pl.when(i == 0)
        def _():
            carry[...] = jnp.zeros_like(carry)

        xv = x_ref[...]
        hb = (_rms(xv)[0] * g_ref[...]).astype(_MXU)
        acc = jnp.zeros((T, D), _F32)
        for j in range(FFN_FWD_CHUNKS):
            halves = []
            for cs in (slice(j * C, (j + 1) * C), slice(F + j * C, F + (j + 1) * C)):
                u = _dot(hb, wup[cs, :], _NT)
                u_ref[:, cs] = u.astype(u_ref.dtype)
                prev = carry[:, cs]
                carry[:, cs] = u[T - EDGE:, :]
                c = (cw_ref[0:1, cs] * _shift_down(u, 2, prev) + cw_ref[1:2, cs] * _shift_down(u, 1, prev)
                     + cw_ref[2:3, cs] * u + cb_ref[:, cs])
                c_ref[:, cs] = c.astype(c_ref.dtype)
                halves.append(c)
            cg, cv = halves
            a = (cg * jax.nn.sigmoid(cg)) * cv
            acc = acc + _dot(a.astype(_MXU), wdnv[j * C:(j + 1) * C, :])
        o_ref[...] = xv + acc

    any_ = pl.BlockSpec(memory_space=pl.ANY)
    wide = pl.BlockSpec((T, F2), lambda i: (i, 0))
    return pl.pallas_call(
        body, name="ffn_fwd", grid=(n,),
        in_specs=[pl.BlockSpec((T, D), lambda i: (i, 0)), _full((1, D)), any_, _full((CONV_W, F2)), _full((1, F2)), any_],
        out_specs=[pl.BlockSpec((T, D), lambda i: (i, 0)), wide, wide],
        out_shape=[jax.ShapeDtypeStruct((S, D), _F32), jax.ShapeDtypeStruct((S, F2), _MXU),
                   jax.ShapeDtypeStruct((S, F2), _MXU)],
        scratch_shapes=[pltpu.VMEM((F2, D), _MXU), pltpu.VMEM((F, D), _MXU),
                        pltpu.VMEM((EDGE, F2), _F32), pltpu.SemaphoreType.DMA((2,))],
        compiler_params=_seq(),
    )(x, g, wup_t, cw, cb, wdn)


def _ffn_bwd(x, dy, u, c, g, wup_t, cw, wdn):
    S = x.shape[0]
    F2 = wup_t.shape[0]
    F = F2 // 2
    C = F // FFN_BWD_CHUNKS
    T = _tile(S, FFN_BWD_TILE, 16)
    n = S // T

    def body(x_ref, dy_ref, u_ref, c_ref, g_ref, wup_hbm, cw_ref, wdn_hbm,
             dx_ref, du_ref, a_ref, h_ref, dcw_ref, dcb_ref, dg_ref, wup, wdnv, carry, sems):
        i = pl.program_id(0)
        _load_weights(i, [(wup_hbm, wup), (wdn_hbm, wdnv)], sems)

        @pl.when(i == 0)
        def _():
            carry[...] = jnp.zeros_like(carry)
            dcw_ref[...] = jnp.zeros_like(dcw_ref)
            dcb_ref[...] = jnp.zeros_like(dcb_ref)
            dg_ref[...] = jnp.zeros_like(dg_ref)

        gv = g_ref[...]
        xv = x_ref[...]
        xn, r = _rms(xv)
        hbf = (xn * gv).astype(_MXU)
        h_ref[...] = hbf
        dyv = dy_ref[...]
        dyb = dyv.astype(_MXU)
        dh = jnp.zeros((T, D), _F32)
        for j in range(FFN_BWD_CHUNKS):
            gs, vs = slice(j * C, (j + 1) * C), slice(F + j * C, F + (j + 1) * C)
            cg, cv = c_ref[:, gs].astype(_F32), c_ref[:, vs].astype(_F32)
            sg = jax.nn.sigmoid(cg)
            sl = cg * sg
            a_ref[:, gs] = (sl * cv).astype(a_ref.dtype)
            da = _dot(dyb, wdnv[gs, :], _NT)
            for cs, dc in ((gs, da * cv * (sg * (1.0 + cg * (1.0 - sg)))), (vs, da * sl)):
                nxt = carry[:, cs]
                carry[:, cs] = dc[0:EDGE, :]
                dc1, dc2 = _shift_up(dc, 1, nxt), _shift_up(dc, 2, nxt)
                uf = u_ref[:, cs].astype(_F32)
                dcb_ref[:, cs] += _colsum(dc)
                for k, d in enumerate((dc2, dc1, dc)):
                    dcw_ref[k:k + 1, cs] += _colsum(d * uf)
                du = cw_ref[2:3, cs] * dc + cw_ref[1:2, cs] * dc1 + cw_ref[0:1, cs] * dc2
                dub = du.astype(_MXU)
                du_ref[:, cs] = dub
                dh = dh + _dot(dub, wup[cs, :])
        dg_ref[...] += _colsum(dh * xn)
        dx_ref[...] = dyv + _rms_bwd(dh, xn, r, gv)

    any_ = pl.BlockSpec(memory_space=pl.ANY)
    rev = lambda i: (n - 1 - i, 0)
    return pl.pallas_call(
        body, name="ffn_bwd", grid=(n,),
        in_specs=[pl.BlockSpec((T, D), rev), pl.BlockSpec((T, D), rev), pl.BlockSpec((T, F2), rev),
                  pl.BlockSpec((T, F2), rev), _full((1, D)), any_, _full((CONV_W, F2)), any_],
        out_specs=[pl.BlockSpec((T, D), rev), pl.BlockSpec((T, F2), rev), pl.BlockSpec((T, F), rev),
                   pl.BlockSpec((T, D), rev), _full((CONV_W, F2)), _full((1, F2)), _full((1, D))],
        out_shape=[jax.ShapeDtypeStruct((S, D), _F32), jax.ShapeDtypeStruct((S, F2), _MXU),
                   jax.ShapeDtypeStruct((S, F), _MXU), jax.ShapeDtypeStruct((S, D), _MXU),
                   jax.ShapeDtypeStruct((CONV_W, F2), _F32), jax.ShapeDtypeStruct((1, F2), _F32),
                   jax.ShapeDtypeStruct((1, D), _F32)],
        scratch_shapes=[pltpu.VMEM((F2, D), _MXU), pltpu.VMEM((F, D), _MXU),
                        pltpu.VMEM((EDGE, F2), _F32), pltpu.SemaphoreType.DMA((2,))],
        compiler_params=_seq(),
    )(x, dy, u, c, g, wup_t, cw, wdn)


def _tn_matmul(a, b, name, token=None):
    S, M = a.shape
    N = b.shape[1]
    bm = _tile(M, 1408, LANES)
    tk = _tile(S, 2048, 16)
    nk = S // tk
    tokens = [] if token is None else [token]

    def body(a_ref, b_ref, *rest):
        o_ref, acc = rest[-2:]
        k = pl.program_id(1)

        @pl.when(k == 0)
        def _():
            acc[...] = jnp.zeros_like(acc)

        acc[...] += _dot(a_ref[...].astype(_MXU), b_ref[...].astype(_MXU), _TN)

        @pl.when(k == nk - 1)
        def _():
            o_ref[...] = acc[...].astype(o_ref.dtype)

    return pl.pallas_call(
        body, name=name, grid=(M // bm, nk),
        in_specs=[pl.BlockSpec((tk, bm), lambda i, k: (k, i)), pl.BlockSpec((tk, N), lambda i, k: (k, 0))]
        + [_full((8, LANES))] * len(tokens),
        out_specs=pl.BlockSpec((bm, N), lambda i, k: (i, 0)),
        out_shape=jax.ShapeDtypeStruct((M, N), _MXU),
        scratch_shapes=[pltpu.VMEM((bm, N), _F32)],
        compiler_params=_seq(2),
    )(a, b, *tokens)


def _kv_fwd(x, g, wkv, bkv):
    S = x.shape[0]
    T = _tile(S, 512, 16)

    def body(x_ref, g_ref, w_ref, b_ref, o_ref):
        hb = (_rms(x_ref[...])[0] * g_ref[...]).astype(_MXU)
        o_ref[...] = (_dot(hb, w_ref[...]) + b_ref[...]).astype(o_ref.dtype)

    return pl.pallas_call(
        body, name="kv_fwd", grid=(S // T,),
        in_specs=[pl.BlockSpec((T, D), lambda i: (i, 0)), _full((1, D)), _full((D, KVD)), _full((1, KVD))],
        out_specs=pl.BlockSpec((T, KVD), lambda i: (i, 0)),
        out_shape=jax.ShapeDtypeStruct((S, KVD), _MXU),
        compiler_params=_seq(),
    )(x, g, wkv, bkv)


def _kv_bwd(x, dx_in, g, wkv, cur_a, prev_a, cur_b, prev_b):
    S = x.shape[0]
    T = _tile(S, 512, BLK)
    n, per = S // T, T // BLK

    def body(x_ref, dxi_ref, g_ref, w_ref, ca, pa, na, cb, pb, nb, dx_ref, dw_ref, db_ref, dg_ref):
        i = pl.program_id(0)

        @pl.when(i == 0)
        def _():
            dw_ref[...] = jnp.zeros_like(dw_ref)
            db_ref[...] = jnp.zeros_like(db_ref)
            dg_ref[...] = jnp.zeros_like(dg_ref)

        gv = g_ref[...]
        xn, r = _rms(x_ref[...])
        nxt = jnp.where(i < n - 1, na[...] + nb[...], 0.0)
        prev = jnp.concatenate([pa[BLK:, :] + pb[BLK:, :], nxt], axis=0) if per > 1 else nxt
        dkv = ca[...] + cb[...] + prev
        db_ref[...] += _colsum(dkv)
        dkb = dkv.astype(_MXU)
        dw_ref[...] += _dot((xn * gv).astype(_MXU), dkb, _TN)
        dh = _dot(dkb, w_ref[...], _NT)
        dg_ref[...] += _colsum(dh * xn)
        dx_ref[...] = dxi_ref[...] + _rms_bwd(dh, xn, r, gv)

    blk = lambda w: pl.BlockSpec((T, w), lambda i: (i, 0))
    nxt = pl.BlockSpec((BLK, KVD), lambda i: (jnp.minimum((i + 1) * per, S // BLK - 1), 0))
    return pl.pallas_call(
        body, name="kv_bwd", grid=(n,),
        in_specs=[blk(D), blk(D), _full((1, D)), _full((D, KVD)), blk(KVD), blk(KVD), nxt, blk(KVD), blk(KVD), nxt],
        out_specs=[blk(D), _full((D, KVD)), _full((1, KVD)), _full((1, D))],
        out_shape=[jax.ShapeDtypeStruct((S, D), _F32), jax.ShapeDtypeStruct((D, KVD), _F32),
                   jax.ShapeDtypeStruct((1, KVD), _F32), jax.ShapeDtypeStruct((1, D), _F32)],
        compiler_params=_seq(),
    )(x, dx_in, g, wkv, cur_a, prev_a, prev_a, cur_b, prev_b, prev_b)


STACK = GROUP * BLK
ATTN_FWD_BLOCKS = 4


def _attn_mask(i, rows):
    qi = lax.broadcasted_iota(jnp.int32, (rows, 2 * BLK), 0) & (BLK - 1)
    si = lax.broadcasted_iota(jnp.int32, (rows, 2 * BLK), 1)
    return (si > qi) & (si <= qi + BLK) & jnp.logical_or(i > 0, si >= BLK)


def _low_half():
    return lax.broadcasted_iota(jnp.int32, (BLK, PAIR), 1) < HEAD_DIM


def _stack_heads(ref, kh, dst):
    low = _low_half()
    for pp in range(GROUP // 2):
        pr = kh * (GROUP // 2) + pp
        v2 = ref[:, pr * PAIR:(pr + 1) * PAIR]
        zero = jnp.zeros_like(v2)
        dst[2 * pp * BLK:(2 * pp + 1) * BLK, :] = jnp.where(low, v2, zero)
        dst[(2 * pp + 1) * BLK:(2 * pp + 2) * BLK, :] = jnp.where(low, zero, v2)


def _unstack_heads(st, pp):
    return jnp.where(_low_half(), st[2 * pp * BLK:(2 * pp + 1) * BLK, :], st[(2 * pp + 1) * BLK:(2 * pp + 2) * BLK, :])


def _sink_col(sk_ref, kh):
    return jnp.concatenate([jnp.full((BLK, 1), sk_ref[kh * GROUP + h], _F32) for h in range(GROUP)], axis=0)


def _head_probs(qm, kd, mask, sink):
    s = jnp.where(mask, _dot(qm, kd, _NT) * (HEAD_DIM ** -0.5), NEG)
    m = jnp.maximum(jnp.max(s, axis=-1, keepdims=True), sink)
    p = jnp.exp(s - m)
    es = jnp.exp(sink - m)
    inv = 1.0 / (jnp.sum(p, axis=-1, keepdims=True) + es)
    return p * inv, es * inv


def _attn_fwd(x, g, wq, bq, sinks, kvd, wo, bo):
    S = x.shape[0]
    nb = min(ATTN_FWD_BLOCKS, S // BLK)
    T = nb * BLK
    n = S // T

    def body(x_ref, g_ref, wq_ref, bq_ref, sk_ref, kp_ref, kc_ref, wo_ref, bo_ref, xo_ref, q_ref, o_ref, win):
        i = pl.program_id(0)
        xv = x_ref[...]
        hb = (_rms(xv)[0] * g_ref[...]).astype(_MXU)
        q_ref[...] = (_dot(hb, wq_ref[...]) + bq_ref[...]).astype(q_ref.dtype)
        win[0:BLK, :] = kp_ref[...]
        win[BLK:, :] = kc_ref[...]
        low = _low_half()
        for b in range(nb):
            rows, keys = slice(b * BLK, (b + 1) * BLK), slice(b * BLK, (b + 2) * BLK)
            mask = _attn_mask(i if b == 0 else 1, BLK)
            for pr in range(N_HEADS // 2):
                kh = (2 * pr) // GROUP
                kd = win[keys, kh * PAIR:(kh + 1) * PAIR]
                vd = win[keys, (N_KV + kh) * PAIR:(N_KV + kh + 1) * PAIR]
                q2 = q_ref[rows, pr * PAIR:(pr + 1) * PAIR]
                outs = []
                for half in range(2):
                    qm = jnp.where(low if half == 0 else ~low, q2, jnp.zeros_like(q2))
                    pbs, _ = _head_probs(qm, kd, mask, sk_ref[2 * pr + half])
                    outs.append(_dot(pbs.astype(_MXU), vd))
                o_ref[rows, pr * PAIR:(pr + 1) * PAIR] = jnp.where(low, outs[0], outs[1]).astype(o_ref.dtype)
        xo_ref[...] = xv + _dot(o_ref[...], wo_ref[...]) + bo_ref[...]

    blk = lambda w: pl.BlockSpec((T, w), lambda i: (i, 0))
    return pl.pallas_call(
        body, name="attn_fwd", grid=(n,),
        in_specs=[blk(D), _full((1, D)), _full((D, D)), _full((1, D)),
                  pl.BlockSpec(memory_space=pltpu.SMEM),
                  pl.BlockSpec((BLK, KVD), lambda i: (jnp.maximum(i * nb - 1, 0), 0)), blk(KVD),
                  _full((D, D)), _full((1, D))],
        out_specs=[blk(D), blk(D), blk(D)],
        out_shape=[jax.ShapeDtypeStruct((S, D), _F32), jax.ShapeDtypeStruct((S, D), _MXU),
                   jax.ShapeDtypeStruct((S, D), _MXU)],
        scratch_shapes=[pltpu.VMEM((T + BLK, KVD), _MXU)],
        compiler_params=_seq(),
    )(x, g, wq, bq, sinks, kvd, kvd, wo, bo)


def _attn_bwd(x, dy, q, o, g, wq, sinks, kvd, wo):
    S = x.shape[0]
    n = S // BLK
    all_rows = N_HEADS * BLK

    def body(x_ref, dy_ref, q_ref, o_ref, g_ref, wq_ref, sk_ref, kp_ref, kc_ref, wo_ref,
             dx_ref, dq_ref, h_ref, dc_ref, dp_ref, dbq_ref, dbo_ref, dg_ref, dsk_ref, win, dob, qs, dos, pall, dsall):
        i = pl.program_id(0)

        @pl.when(i == 0)
        def _():
            dbq_ref[...] = jnp.zeros_like(dbq_ref)
            dbo_ref[...] = jnp.zeros_like(dbo_ref)
            dg_ref[...] = jnp.zeros_like(dg_ref)
            dsk_ref[...] = jnp.zeros_like(dsk_ref)

        gv = g_ref[...]
        xv = x_ref[...]
        xn, r = _rms(xv)
        h_ref[...] = (xn * gv).astype(h_ref.dtype)
        dyv = dy_ref[...]
        dbo_ref[...] += _colsum(dyv)
        dob[...] = _dot(dyv.astype(_MXU), wo_ref[...], _NT).astype(dob.dtype)
        win[0:BLK, :] = kp_ref[...]
        win[BLK:, :] = kc_ref[...]
        mask = _attn_mask(i, BLK)
        low = _low_half()
        lane = lax.broadcasted_iota(jnp.int32, (1, LANES), 1)
        for pr in range(N_HEADS // 2):
            kh = (2 * pr) // GROUP
            kd = win[:, kh * PAIR:(kh + 1) * PAIR]
            vd = win[:, (N_KV + kh) * PAIR:(N_KV + kh + 1) * PAIR]
            q2 = q_ref[:, pr * PAIR:(pr + 1) * PAIR]
            do2 = dob[:, pr * PAIR:(pr + 1) * PAIR]
            od = do2.astype(_F32) * o_ref[:, pr * PAIR:(pr + 1) * PAIR].astype(_F32)
            for half in range(2):
                hd = 2 * pr + half
                rows = slice(hd * BLK, (hd + 1) * BLK)
                sel = low if half == 0 else ~low
                qm = jnp.where(sel, q2, jnp.zeros_like(q2))
                dom = jnp.where(sel, do2, jnp.zeros_like(do2))
                qs[rows, :] = qm
                dos[rows, :] = dom
                pbs, ps = _head_probs(qm, kd, mask, sk_ref[hd])
                pall[rows, :] = pbs.astype(_MXU)
                delta = jnp.sum(jnp.where(sel, od, 0.0), axis=-1, keepdims=True)
                dsall[rows, :] = (pbs * (_dot(dom, vd, _NT) - delta) * (HEAD_DIM ** -0.5)).astype(_MXU)
                dsk_ref[...] -= jnp.where(lane == hd, _colsum(ps * delta), 0.0)
        dq_all = []
        for kh in range(N_KV):
            ks = slice(kh * PAIR, (kh + 1) * PAIR)
            vs = slice((N_KV + kh) * PAIR, (N_KV + kh + 1) * PAIR)
            rows = slice(kh * STACK, (kh + 1) * STACK)
            dqst = _dot(dsall[rows, :], win[:, ks])
            dk = _dot(dsall[rows, :], qs[rows, :], _TN)
            dv = _dot(pall[rows, :], dos[rows, :], _TN)
            dp_ref[:, ks], dc_ref[:, ks] = dk[0:BLK, :], dk[BLK:, :]
            dp_ref[:, vs], dc_ref[:, vs] = dv[0:BLK, :], dv[BLK:, :]
            dq_all += [_unstack_heads(dqst, pp) for pp in range(GROUP // 2)]
        dq = jnp.concatenate(dq_all, axis=1)
        dbq_ref[...] += _colsum(dq)
        dqb = dq.astype(_MXU)
        dq_ref[...] = dqb
        dh = _dot(dqb, wq_ref[...], _NT)
        dg_ref[...] += _colsum(dh * xn)
        dx_ref[...] = dyv + _rms_bwd(dh, xn, r, gv)

    blk = lambda w: pl.BlockSpec((BLK, w), lambda i: (i, 0))
    return pl.pallas_call(
        body, name="attn_bwd", grid=(n,),
        in_specs=[blk(D), blk(D), blk(D), blk(D), _full((1, D)), _full((D, D)),
                  pl.BlockSpec(memory_space=pltpu.SMEM),
                  pl.BlockSpec((BLK, KVD), lambda i: (jnp.maximum(i - 1, 0), 0)), blk(KVD), _full((D, D))],
        out_specs=[blk(D), blk(D), blk(D), blk(KVD), blk(KVD),
                   _full((1, D)), _full((1, D)), _full((1, D)), _full((1, LANES))],
        out_shape=[jax.ShapeDtypeStruct((S, D), _F32), jax.ShapeDtypeStruct((S, D), _MXU),
                   jax.ShapeDtypeStruct((S, D), _MXU), jax.ShapeDtypeStruct((S, KVD), _F32),
                   jax.ShapeDtypeStruct((S, KVD), _F32), jax.ShapeDtypeStruct((1, D), _F32),
                   jax.ShapeDtypeStruct((1, D), _F32), jax.ShapeDtypeStruct((1, D), _F32),
                   jax.ShapeDtypeStruct((1, LANES), _F32)],
        scratch_shapes=[pltpu.VMEM((2 * BLK, KVD), _MXU), pltpu.VMEM((BLK, D), _MXU),
                        pltpu.VMEM((all_rows, PAIR), _MXU), pltpu.VMEM((all_rows, PAIR), _MXU),
                        pltpu.VMEM((all_rows, 2 * BLK), _MXU), pltpu.VMEM((all_rows, 2 * BLK), _MXU)],
        compiler_params=_seq(),
    )(x, dy, q, o, g, wq, sinks, kvd, kvd, wo)


def _loss_bwd(x, g, tgt):
    S = x.shape[0]
    T = _tile(S, 512, 8)

    def body(x_ref, g_ref, t_ref, dx_ref, ls_ref, dg_ref):
        @pl.when(pl.program_id(0) == 0)
        def _():
            ls_ref[...] = jnp.zeros_like(ls_ref)
            dg_ref[...] = jnp.zeros_like(dg_ref)

        gv = g_ref[...]
        xn, r = _rms(x_ref[...])
        err = xn * gv - t_ref[...]
        ls_ref[...] += 0.5 * jnp.sum(jnp.mean(err * err, axis=-1, keepdims=True))
        dyv = err * (1.0 / D)
        dg_ref[...] += _colsum(dyv * xn)
        dx_ref[...] = _rms_bwd(dyv, xn, r, gv)

    return pl.pallas_call(
        body, name="loss_bwd", grid=(S // T,),
        in_specs=[pl.BlockSpec((T, D), lambda i: (i, 0)), _full((1, D)), pl.BlockSpec((T, D), lambda i: (i, 0))],
        out_specs=[pl.BlockSpec((T, D), lambda i: (i, 0)), _full((8, LANES)), _full((1, D))],
        out_shape=[jax.ShapeDtypeStruct((S, D), _F32), jax.ShapeDtypeStruct((8, LANES), _F32),
                   jax.ShapeDtypeStruct((1, D), _F32)],
        compiler_params=_seq(),
    )(x, g, tgt)


def _me():
    return 4 * lax.axis_index("x") + 2 * lax.axis_index("y") + lax.axis_index("c")


def _peer(j):
    x, y, c = lax.axis_index("x"), lax.axis_index("y"), lax.axis_index("c")
    px = 1 - x if j & 4 else x
    py = 1 - y if j & 2 else y
    pc = 1 - c if j & 1 else c
    return (px, py, pc), 4 * px + 2 * py + pc


_HBM = pl.BlockSpec(memory_space=pltpu.HBM)
_SEMS = pl.BlockSpec(memory_space=pltpu.SEMAPHORE)
_EFFECT = pltpu.SideEffectType.DATAFLOW_SIDE_EFFECTING


def _in_hbm(a):
    return pltpu.with_memory_space_constraint(a, pltpu.HBM)


def _start_copies(name, groups):
    flat = []
    for srcs, zones, _ in groups:
        flat += [_in_hbm(a) for a in srcs] + [_in_hbm(lax.empty(z.shape, z.dtype)) for z in zones]
    n_in, n_g = len(flat), len(groups)

    def body(*refs):
        sems = refs[n_in:n_in + 2 * n_g]
        me, k = _me(), 0
        for gi, (srcs, zones, plan) in enumerate(groups):
            src_refs, zone_refs = refs[k:k + len(srcs)], refs[k + len(srcs):k + len(srcs) + len(zones)]
            k += len(srcs) + len(zones)
            for t, (si, zi, src_of, dst_of) in enumerate(plan):
                for j in range(1, N_DEV):
                    dev, pk = _peer(j)
                    pltpu.make_async_remote_copy(
                        src_ref=src_of(src_refs[si], pk), dst_ref=dst_of(zone_refs[zi], me),
                        send_sem=sems[2 * gi].at[t * (N_DEV - 1) + j - 1], recv_sem=sems[2 * gi + 1].at[t * (N_DEV - 1) + j - 1],
                        device_id=dev, device_id_type=pl.DeviceIdType.MESH).start()
                pltpu.make_async_copy(src_of(src_refs[si], me), dst_of(zone_refs[zi], me),
                                      sems[2 * gi].at[len(plan) * (N_DEV - 1) + t]).start()
        refs[-1][...] = jnp.zeros_like(refs[-1])

    sem_shapes = []
    for _, _, plan in groups:
        sem_shapes += [pltpu.SemaphoreType.DMA((len(plan) * N_DEV,)), pltpu.SemaphoreType.DMA((len(plan) * (N_DEV - 1),))]
    outs = pl.pallas_call(
        body, name=name,
        out_shape=(*sem_shapes, *[pltpu.HBM(a.shape, a.dtype) for a in flat], jax.ShapeDtypeStruct((8, LANES), _F32)),
        in_specs=[_HBM] * n_in,
        out_specs=(*[_SEMS] * (2 * n_g), *[_HBM] * n_in, pl.BlockSpec(memory_space=pltpu.VMEM)),
        input_output_aliases={k: 2 * n_g + k for k in range(n_in)},
        compiler_params=pltpu.CompilerParams(has_side_effects=_EFFECT),
    )(*flat)
    handles, k = [], 2 * n_g
    for gi, (srcs, zones, plan) in enumerate(groups):
        ns, nz = len(srcs), len(zones)
        handles.append((outs[2 * gi], outs[2 * gi + 1], list(outs[k:k + ns]), list(outs[k + ns:k + ns + nz]), plan))
        k += ns + nz
    return handles, outs[-1]


def _wait_copies(name, handles, after):
    flat = []
    for _, _, srcs, zones, _ in handles:
        flat += srcs + zones
    n_in, n_g = len(flat), len(handles)

    def body(*refs):
        sems = refs[n_in:n_in + 2 * n_g]
        me, k, local, remote = _me(), 0, [], []
        for gi, (_, _, srcs, zones, plan) in enumerate(handles):
            ns, nz = len(srcs), len(zones)
            src_refs, zone_refs = refs[k:k + ns], refs[k + ns:k + ns + nz]
            k += ns + nz
            for t, (si, zi, src_of, dst_of) in enumerate(plan):
                local.append(pltpu.make_async_copy(src_of(src_refs[si], me), dst_of(zone_refs[zi], me),
                                                   sems[2 * gi].at[len(plan) * (N_DEV - 1) + t]))
                for j in range(1, N_DEV):
                    dev, pk = _peer(j)
                    remote.append(pltpu.make_async_remote_copy(
                        src_ref=src_of(src_refs[si], pk), dst_ref=dst_of(zone_refs[zi], pk),
                        send_sem=sems[2 * gi].at[t * (N_DEV - 1) + j - 1], recv_sem=sems[2 * gi + 1].at[t * (N_DEV - 1) + j - 1],
                        device_id=dev, device_id_type=pl.DeviceIdType.MESH))
        for cp in remote:
            cp.wait_send()
            cp.wait_recv()
        for cp in local:
            cp.wait()

    sem_args = []
    for send, recv, _, _, _ in handles:
        sem_args += [send, recv]
    outs = pl.pallas_call(
        body, name=name, out_shape=tuple(pltpu.HBM(a.shape, a.dtype) for a in flat),
        in_specs=[_HBM] * n_in + [_SEMS] * (2 * n_g) + [pl.BlockSpec(memory_space=pl.ANY)],
        out_specs=tuple([_HBM] * n_in), input_output_aliases={k: k for k in range(n_in)},
        compiler_params=pltpu.CompilerParams(has_side_effects=_EFFECT),
    )(*flat, *sem_args, after)
    res, k = [], 0
    for _, _, srcs, zones, _ in handles:
        res.append(list(outs[k + len(srcs):k + len(srcs) + len(zones)]))
        k += len(srcs) + len(zones)
    return res


def _rows(axis, size):
    def of(ref, b):
        start = b * size
        if size % 8 == 0:
            start = pl.multiple_of(start, 8)
        return ref.at[(slice(None),) * axis + (pl.ds(start, size),)]
    return of


def _whole(ref, b):
    return ref


def _slot(ref, b):
    return ref.at[b]


def _gather_group(shards):
    zones, plan = [], []
    for k, (a, axis) in enumerate(shards):
        zones.append(jax.ShapeDtypeStruct(a.shape[:axis] + (N_DEV * a.shape[axis],) + a.shape[axis + 1:], a.dtype))
        plan.append((k, k, _whole, _rows(axis, a.shape[axis])))
    return [a for a, _ in shards], zones, plan


def _scatter_group(grads):
    zones, plan = [], []
    for k, (a, axis) in enumerate(grads):
        size = a.shape[axis] // N_DEV
        zones.append(jax.ShapeDtypeStruct((N_DEV,) + a.shape[:axis] + (size,) + a.shape[axis + 1:], a.dtype))
        plan.append((k, k, _rows(axis, size), _slot))
    return [a for a, _ in grads], zones, plan


def _sum_landed(land):
    g = land[0].astype(_F32)
    for b in range(1, N_DEV):
        g = g + land[b].astype(_F32)
    return g


def _landed_specs(n_layers, tr, C, nr):
    def spec(k):
        return pl.BlockSpec((N_DEV, tr, C), lambda l, i: (0, jnp.where(l == k, i, jnp.where(l < k, 0, nr - 1)), 0))
    return [spec(k) for k in range(n_layers)]


def _per_layer(l, zone_refs, fn):
    for k, ref in enumerate(zone_refs):
        @pl.when(l == k)
        def _(ref=ref):
            fn(_sum_landed(ref))


def _sum8(zones):
    L = len(zones)
    _, R, C = zones[0].shape
    tr = _tile(R, 352, 16)
    nr = R // tr

    def body(*refs):
        o_ref = refs[L]

        def put(g):
            o_ref[...] = g

        _per_layer(pl.program_id(0), refs[:L], put)

    return pl.pallas_call(
        body, name="sum8", grid=(L, nr), in_specs=_landed_specs(L, tr, C, nr),
        out_specs=pl.BlockSpec((tr, C), lambda l, i: (l * nr + i, 0)),
        out_shape=jax.ShapeDtypeStruct((L * R, C), _F32), compiler_params=_seq(2),
    )(*zones)


def _adam_update(gv, w_ref, m_ref, v_ref, d_ref, mo_ref, vo_ref):
    mn = ADAM_B1 * m_ref[...] + (1.0 - ADAM_B1) * gv
    vn = ADAM_B2 * v_ref[...] + (1.0 - ADAM_B2) * (gv * gv)
    mo_ref[...] = mn
    vo_ref[...] = vn
    d_ref[...] = -ADAM_LR * ((mn / (1.0 - ADAM_B1 ** ADAM_STEP)) / (jnp.sqrt(vn / (1.0 - ADAM_B2 ** ADAM_STEP)) + ADAM_EPS)
                             + ADAM_WD * w_ref[...])


def _earlier(outs):
    outs = list(outs or [])
    return outs, [pl.BlockSpec(memory_space=pl.ANY)] * len(outs)


def _adamw(g, w, m, v, name, first_row=0, earlier=None):
    Rg, C = g.shape
    R = w.shape[0]
    tr = _tile(Rg if first_row == 0 else min(Rg, first_row), 256, 16)
    off = first_row // tr
    more, more_specs = _earlier(earlier)

    def body(g_ref, w_ref, m_ref, v_ref, *rest):
        go_ref, d_ref, mo_ref, vo_ref = rest[-4:]
        gv = g_ref[...]
        go_ref[...] = gv
        _adam_update(gv, w_ref, m_ref, v_ref, d_ref, mo_ref, vo_ref)

    row = pl.BlockSpec((tr, C), lambda i: (i + off, 0))
    return pl.pallas_call(
        body, name=name, grid=(Rg // tr,), in_specs=[pl.BlockSpec((tr, C), lambda i: (i, 0))] + [row] * 3 + more_specs,
        out_specs=[row] * 4, out_shape=[jax.ShapeDtypeStruct((R, C), _F32)] * 4,
        input_output_aliases={4 + k: k for k in range(len(more))}, compiler_params=_seq(),
    )(g, w, m, v, *more)


def _adamw_landed(zones, w, m, v, name, first_layer=0, earlier=None):
    L = len(zones)
    _, R, C = zones[0].shape
    tr = _tile(R, 176, 16)
    nr = R // tr
    more, more_specs = _earlier(earlier)

    def body(*refs):
        w_ref, m_ref, v_ref = refs[L:L + 3]
        g_ref, d_ref, mo_ref, vo_ref = refs[-4:]

        def update(g):
            g_ref[...] = g
            _adam_update(g, w_ref, m_ref, v_ref, d_ref, mo_ref, vo_ref)

        _per_layer(pl.program_id(0), refs[:L], update)

    row = pl.BlockSpec((tr, C), lambda l, i: ((l + first_layer) * nr + i, 0))
    return pl.pallas_call(
        body, name=name, grid=(L, nr), in_specs=_landed_specs(L, tr, C, nr) + [row] * 3 + more_specs,
        out_specs=[row] * 4, out_shape=[jax.ShapeDtypeStruct(w.shape, _F32)] * 4,
        input_output_aliases={L + 3 + k: k for k in range(len(more))}, compiler_params=_seq(2),
    )(*zones, w, m, v, *more)


def _pack(parts):
    flat = jnp.concatenate([p.reshape(-1).astype(_F32) for p in parts])
    n = flat.shape[0]
    rows = -(-n // (8 * LANES)) * 8
    return jnp.pad(flat, (0, rows * LANES - n)).reshape(rows, LANES)


def _unpack(packed, shapes):
    flat, out, k = packed.reshape(-1), [], 0
    for s in shapes:
        n = 1
        for d in s:
            n *= d
        out.append(flat[k:k + n].reshape(s))
        k += n
    return out


def kernel(x, norm1_g, norm2_g, pool_w, pool_scale, kv_norm_g, w_kv, b_kv, w_q, b_q, sinks, w_o, b_o, ffn_up, ffn_conv_w, ffn_conv_b, ffn_down, final_g, loss_target, m_norm1_g, m_norm2_g, m_pool_w, m_pool_scale, m_kv_norm_g, m_w_kv, m_b_kv, m_w_q, m_b_q, m_sinks, m_w_o, m_b_o, m_ffn_up, m_ffn_conv_w, m_ffn_conv_b, m_ffn_down, m_final_g, v_norm1_g, v_norm2_g, v_pool_w, v_pool_scale, v_kv_norm_g, v_w_kv, v_b_kv, v_w_q, v_b_q, v_sinks, v_w_o, v_b_o, v_ffn_up, v_ffn_conv_w, v_ffn_conv_b, v_ffn_down, v_final_g):
    S = x.shape[1]
    F2s = ffn_up.shape[2]
    F2 = N_DEV * F2s
    me = _me()
    x0 = x.reshape(S, D)
    tgt = loss_target.reshape(S, D)
    row = lambda a: a.reshape(1, -1)

    small = _pack([pool_scale, ffn_conv_w])
    wire = lambda a: a.astype(_MXU)
    ffn_w = lambda l: [(wire(ffn_up[l]).T, 0), (wire(ffn_down[l]), 0)]
    attn_w = lambda j: [(wire(w_q[j]), 0), (wire(w_o[j]), 0)]
    gathers, token = _start_copies("gather_start", [_gather_group(g) for g in (
        [(wire(pool_w[0]), 1), (small[None], 0)], ffn_w(0), [(wire(pool_w[1]), 1)] + ffn_w(1),
        [(wire(w_kv), 0)] + attn_w(0), ffn_w(2), attn_w(1), ffn_w(3))])

    def gathered(k, after):
        return _wait_copies("gather_wait_%d" % k, [gathers[k]], after)[0]

    pw, up_t, down, wq, wo = [None] * N_A, [None] * DEPTH, [None] * DEPTH, [None] * 2, [None] * 2
    pw[0], small_all = gathered(0, token)
    n_ps = pool_scale.size
    small_all = small_all.reshape(N_DEV, -1)
    pscale = jnp.transpose(small_all[:, :n_ps].reshape(N_DEV, N_A, D // N_DEV), (1, 0, 2)).reshape(N_A, D)
    conv_w = jnp.transpose(small_all[:, n_ps:n_ps + ffn_conv_w.size].reshape(N_DEV, DEPTH, CONV_W, F2s),
                           (1, 2, 0, 3)).reshape(DEPTH, CONV_W, F2)

    def dup(a):
        a4 = a.reshape(a.shape[:-1] + (2 * N_KV, 1, HEAD_DIM))
        return jnp.broadcast_to(a4, a.shape[:-1] + (2 * N_KV, 2, HEAD_DIM)).reshape(a.shape[:-1] + (KVD,))

    def fold(a):
        return a.reshape(a.shape[:-1] + (2 * N_KV, 2, HEAD_DIM)).sum(axis=-2).reshape(a.shape[:-1] + (2 * N_KV * HEAD_DIM,))

    xs, us, qs, os_ = [x0], [], [], []
    xc = x0
    kvd = None
    for l in range(DEPTH):
        if l == 1:
            pw[1], up_t[1], down[1] = gathered(2, xc)
        if l == 3:
            wq[1], wo[1] = gathered(5, xc)
        if l < N_A:
            xc = _pool_fwd(xc, row(norm1_g[l]), pw[l], row(pscale[l]))
        else:
            j = l - N_A
            xc, q, o = _attn_fwd(xc, row(norm1_g[l]), wq[j], row(b_q[j]), sinks[j], kvd, wo[j], row(b_o[j]))
            qs.append(q)
            os_.append(o)
        xs.append(xc)
        if l != 1:
            up_t[l], down[l] = gathered((1, None, 4, 6)[l], xc)
        xc, u, c = _ffn_fwd(xc, row(norm2_g[l]), up_t[l], conv_w[l], row(ffn_conv_b[l]), down[l])
        us.append((u, c))
        xs.append(xc)
        if l == N_A - 1:
            wkv, wq[0], wo[0] = gathered(3, xc)
            wkv_d, bkv_d = dup(wkv), dup(row(b_kv))
            kvd = _kv_fwd(xc, row(kv_norm_g), wkv_d, bkv_d)

    dx, loss_p, d_final = _loss_bwd(xc, row(final_g), tgt)
    d_n1, d_n2, d_cw, d_cb = [None] * DEPTH, [None] * DEPTH, [None] * DEPTH, [None] * DEPTH
    d_bq, d_bo, d_sk, d_ps, dkv_parts = [None] * 2, [None] * 2, [None] * 2, [None] * N_A, []
    up_z, down_z, mix_z = [None] * DEPTH, [None] * DEPTH, [None] * DEPTH
    token = None

    def after(gain):
        return gain if token is None else gain + token[0:1, 0:1]

    rep_names = ["norm1_g", "norm2_g", "kv_norm_g", "b_kv", "b_q", "sinks", "b_o", "ffn_conv_b", "final_g"]

    def small_parts(n1_rest, ps_rest):
        zero = jnp.zeros((1, D), _F32)
        return [jnp.concatenate([zero] + n1_rest), jnp.concatenate(d_n2), d_kvg, fold(d_bkv), jnp.concatenate(d_bq),
                jnp.concatenate([s[:, :N_HEADS] for s in d_sk]), jnp.concatenate(d_bo), jnp.concatenate(d_cb), d_final,
                jnp.concatenate([zero] + ps_rest), jnp.stack(d_cw), loss_p[0:1, 0:1]]

    for l in reversed(range(DEPTH)):
        x_in, x_mid, x_out = xs[2 * l], xs[2 * l + 1], xs[2 * l + 2]
        mixer_grads = []
        if l == N_A - 1:
            dx, d_wkv, d_bkv, d_kvg = _kv_bwd(x_out, dx, after(row(kv_norm_g)), wkv_d, *dkv_parts)
            mixer_grads.append((fold(d_wkv).astype(_MXU), 0))
        dy = dx
        dx, du, a, h, d_cw[l], d_cb[l], d_n2[l] = _ffn_bwd(
            x_mid, dy, *us[l], after(row(norm2_g[l])), up_t[l], conv_w[l], down[l])
        g_up = _tn_matmul(du, h, "tn_up")
        if l > 0:
            (up_z[l], down_z[l]), token = _start_copies("scatter_ffn_%d" % l, [
                _scatter_group([(g_up, 0)]), _scatter_group([(_tn_matmul(a, dy, "tn_down"), 0)])])
        else:
            (small_handle, up_z[l]), token = _start_copies("scatter_up_0", [
                _gather_group([(_pack(small_parts(d_n1[1:], d_ps[1:]))[None], 0)]), _scatter_group([(g_up, 0)])])
            (down_z[l],), token = _start_copies("scatter_down_0", [
                _scatter_group([(_tn_matmul(a, dy, "tn_down", token), 0)])])
        dy = dx
        if l < N_A:
            dx, d_pw, d_ps[l], d_n1[l] = _pool_bwd(x_in, dy, after(row(norm1_g[l])), pw[l], row(pscale[l]))
            mixer_grads.append((d_pw.astype(_MXU), 1))
        else:
            j = l - N_A
            dx, dq, h, d_cur, d_prev, d_bq[j], d_bo[j], d_n1[l], d_sk[j] = _attn_bwd(
                x_in, dy, qs[j], os_[j], after(row(norm1_g[l])), wq[j], sinks[j], kvd, wo[j])
            mixer_grads += [(_tn_matmul(h, dq, "tn_q"), 0), (_tn_matmul(os_[j], dy, "tn_o"), 0)]
            dkv_parts += [d_cur, d_prev]
        groups = [_scatter_group(mixer_grads)]
        if l == 0:
            groups.append(_gather_group([(_pack([d_n1[0], d_ps[0]])[None], 0)]))
        handles, token = _start_copies("scatter_mixer_%d" % l, groups)
        mix_z[l], late_handle = handles[0], handles[-1]

    moms = dict(norm1_g=(m_norm1_g, v_norm1_g), norm2_g=(m_norm2_g, v_norm2_g), pool_w=(m_pool_w, v_pool_w),
                pool_scale=(m_pool_scale, v_pool_scale), kv_norm_g=(m_kv_norm_g, v_kv_norm_g), w_kv=(m_w_kv, v_w_kv),
                b_kv=(m_b_kv, v_b_kv), w_q=(m_w_q, v_w_q), b_q=(m_b_q, v_b_q), sinks=(m_sinks, v_sinks),
                w_o=(m_w_o, v_w_o), b_o=(m_b_o, v_b_o), ffn_up=(m_ffn_up, v_ffn_up),
                ffn_conv_w=(m_ffn_conv_w, v_ffn_conv_w), ffn_conv_b=(m_ffn_conv_b, v_ffn_conv_b),
                ffn_down=(m_ffn_down, v_ffn_down), final_g=(m_final_g, v_final_g))
    given = dict(norm1_g=norm1_g, norm2_g=norm2_g, kv_norm_g=kv_norm_g, b_kv=b_kv, b_q=b_q, sinks=sinks, b_o=b_o,
                 ffn_conv_b=ffn_conv_b, final_g=final_g, pool_scale=pool_scale, ffn_conv_w=ffn_conv_w,
                 pool_w=pool_w, w_kv=w_kv, w_q=w_q, w_o=w_o, ffn_up=ffn_up, ffn_down=ffn_down)
    grad, delta, new_m, new_v = {}, {}, {}, {}

    def update(name, zones, cols, first_layer=0, earlier=None, last=True):
        w = given[name]
        two_d = lambda a: a.reshape(-1, cols)
        wmv = (two_d(w), two_d(moms[name][0]), two_d(moms[name][1]))
        if name == "ffn_up":
            g = jnp.swapaxes(_sum8(zones).reshape(len(zones), F2s, D), 1, 2).reshape(-1, F2s)
            outs = _adamw(g, *wmv, "adamw_" + name, first_layer * D, earlier)
        else:
            landed = [z.reshape(N_DEV, -1, cols) for z in zones]
            outs = _adamw_landed(landed, *wmv, "adamw_" + name, first_layer, earlier)
        if last:
            grad[name], delta[name], new_m[name], new_v[name] = (o.reshape(w.shape) for o in outs)
        return outs

    rest = _wait_copies("scatter_wait_rest", up_z[1:] + down_z[1:] + mix_z[1:], token)
    up_r, down_r, mix_r = rest[:DEPTH - 1], rest[DEPTH - 1:2 * (DEPTH - 1)], rest[2 * (DEPTH - 1):]
    up_1 = update("ffn_up", [z[0] for z in up_r], F2s, 1, last=False)
    down_1 = update("ffn_down", [z[0] for z in down_r], D, 1, last=False)
    update("w_q", [mix_r[N_A - 1][0], mix_r[N_A][0]], D)
    update("w_o", [mix_r[N_A - 1][1], mix_r[N_A][1]], D)
    update("w_kv", [mix_r[N_A - 2][0]], w_kv.shape[1])
    pw_1 = update("pool_w", [mix_r[N_A - 2][1]], GC, 1, last=False)

    (up_0,), (down_0,), (pw_0,), (early,), (late,) = _wait_copies(
        "scatter_wait_0", [up_z[0], down_z[0], mix_z[0], small_handle, late_handle],
        lax.optimization_barrier((up_1[1], down_1[1], pw_1[1]))[0])
    late = _sum8([late]).reshape(-1)
    tot = _unpack(_sum8([early]), [given[k].shape for k in rep_names] + [(N_A, D), (DEPTH, CONV_W, F2), ()])
    tot[0] = tot[0].at[0].add(late[:D])
    tot[-3] = tot[-3].at[0].add(late[D:2 * D])
    grad.update(zip(rep_names, tot))
    loss = tot[-1]
    grad["pool_scale"] = lax.dynamic_slice_in_dim(tot[-3], me * (D // N_DEV), D // N_DEV, axis=1)
    grad["ffn_conv_w"] = lax.dynamic_slice_in_dim(tot[-2], me * F2s, F2s, axis=2)
    small_names = rep_names + ["pool_scale", "ffn_conv_w"]
    shapes = [given[k].shape for k in small_names]
    outs = _adamw(_pack([grad[k] for k in small_names]), _pack([given[k] for k in small_names]),
                  _pack([moms[k][0] for k in small_names]), _pack([moms[k][1] for k in small_names]), "adamw_small")
    for dst, packed in zip((delta, new_m, new_v), outs[1:]):
        dst.update(zip(small_names, _unpack(packed, shapes)))

    update("ffn_up", [up_0], F2s, 0, up_1)
    update("ffn_down", [down_0], D, 0, down_1)
    update("pool_w", [pw_0], GC, 0, pw_1)

    names = ["norm1_g", "norm2_g", "pool_w", "pool_scale", "kv_norm_g", "w_kv", "b_kv", "w_q", "b_q", "sinks", "w_o",
             "b_o", "ffn_up", "ffn_conv_w", "ffn_conv_b", "ffn_down", "final_g"]
    return (loss, dx.reshape(x.shape), *[grad[k] for k in names], *[delta[k] for k in names],
            *[new_m[k] for k in names], *[new_v[k] for k in names])
```

```python
import jax
import jax.numpy as jnp
from jax import lax
from jax.experimental import pallas as pl
from jax.experimental.pallas import tpu as pltpu

_F32 = jnp.float32
_MXU = jnp.bfloat16

N_DEV = 8
D = 1024
DEPTH = 4
N_A = 2
POOL_WINDOWS = (2, 4, 8, 16)
GC = D // len(POOL_WINDOWS)
HALO = 16
HEAD_DIM = 64
N_HEADS = D // HEAD_DIM
GROUP = 8
N_KV = N_HEADS // GROUP
BLK = 128
PAIR = 2 * HEAD_DIM
KVD = 4 * N_KV * HEAD_DIM
CONV_W = 3
EPS = 1e-5
NEG = -1e30

ADAM_LR = 0.001
ADAM_B1 = 0.9
ADAM_B2 = 0.999
ADAM_EPS = 1e-08
ADAM_WD = 0.01
ADAM_STEP = 10

V7X_VMEM_LIMIT = 56 * 1024 * 1024
LANES = 128

_NT = (((1,), (1,)), ((), ()))
_TN = (((0,), (0,)), ((), ()))


def _params(**kw):
    return pltpu.CompilerParams(vmem_limit_bytes=V7X_VMEM_LIMIT, **kw)


def _seq(n=1):
    return _params(dimension_semantics=("arbitrary",) * n)


def _dot(a, b, dims=None):
    if dims is None:
        return jnp.dot(a, b, preferred_element_type=_F32)
    return lax.dot_general(a, b, dims, preferred_element_type=_F32)


def _rms(x):
    r = lax.rsqrt(jnp.mean(x * x, axis=-1, keepdims=True) + EPS)
    return x * r, r


def _rms_bwd(dh, xn, r, g):
    dxn = dh * g
    return r * (dxn - xn * jnp.mean(dxn * xn, axis=-1, keepdims=True))


def _colsum(a):
    return jnp.sum(a, axis=0, keepdims=True)


def _tile(n, want, mult=8):
    for t in range(min(want, n), 0, -1):
        if n % t == 0 and t % mult == 0:
            return t
    return n


def _full(shape):
    zeros = (0,) * len(shape)
    return pl.BlockSpec(shape, lambda *_: zeros)


def _window_sum(ext, win, trailing):
    R = ext.shape[0]
    acc, k = ext, 1
    while k < win:
        acc = acc + pltpu.roll(acc, k if trailing else R - k, axis=0)
        k *= 2
    return acc


def _pool_windows(hbuf, h, row, T):
    out = []
    for gi, win in enumerate(POOL_WINDOWS):
        cs = slice(gi * GC, (gi + 1) * GC)
        acc = _window_sum(hbuf[:, cs], win, True)[HALO:, :]
        cnt = jnp.minimum(row + 1, win).astype(_F32)
        out.append((acc / cnt - h[:, cs], cnt))
    return out


def _pool_fwd(x, g, w, sc):
    S = x.shape[0]
    T = _tile(S, 512, HALO)
    n, hb = S // T, T // HALO

    def body(x_ref, xh_ref, g_ref, w_ref, sc_ref, o_ref, hbuf):
        i = pl.program_id(0)
        gv = g_ref[...]
        xv = x_ref[...]
        h = _rms(xv)[0] * gv
        hbuf[0:HALO, :] = jnp.where(i > 0, _rms(xh_ref[...])[0] * gv, 0.0)
        hbuf[HALO:, :] = h
        row = i * T + lax.broadcasted_iota(jnp.int32, (T, 1), 0)
        for gi, (p, _) in enumerate(_pool_windows(hbuf, h, row, T)):
            cs = slice(gi * GC, (gi + 1) * GC)
            z = _dot(p.astype(_MXU), w_ref[gi])
            o_ref[:, cs] = xv[:, cs] + z * sc_ref[:, cs]

    return pl.pallas_call(
        body, name="pool_fwd", grid=(n,),
        in_specs=[pl.BlockSpec((T, D), lambda i: (i, 0)),
                  pl.BlockSpec((HALO, D), lambda i: (jnp.maximum(i * hb - 1, 0), 0)),
                  _full((1, D)), _full((4, GC, GC)), _full((1, D))],
        out_specs=pl.BlockSpec((T, D), lambda i: (i, 0)),
        out_shape=jax.ShapeDtypeStruct((S, D), _F32),
        scratch_shapes=[pltpu.VMEM((T + HALO, D), _F32)],
        compiler_params=_seq(),
    )(x, x, g, w, sc)


def _pool_bwd(x, dy, g, w, sc):
    S = x.shape[0]
    T = _tile(S, 512, HALO)
    n, hb = S // T, T // HALO

    def body(x_ref, xh_ref, dy_ref, dyh_ref, g_ref, w_ref, sc_ref, dx_ref, dw_ref, dsc_ref, dg_ref,
             hbuf, qbuf, dhbuf):
        i = pl.program_id(0)

        @pl.when(i == 0)
        def _():
            dw_ref[...] = jnp.zeros_like(dw_ref)
            dsc_ref[...] = jnp.zeros_like(dsc_ref)
            dg_ref[...] = jnp.zeros_like(dg_ref)

        gv = g_ref[...]
        xv = x_ref[...]
        xn, r = _rms(xv)
        h = xn * gv
        hbuf[0:HALO, :] = jnp.where(i > 0, _rms(xh_ref[...])[0] * gv, 0.0)
        hbuf[HALO:, :] = h
        dyv = dy_ref[...]
        dz = dyv * sc_ref[...]
        dzh = jnp.where(i < n - 1, dyh_ref[...], 0.0) * sc_ref[...]
        row = i * T + lax.broadcasted_iota(jnp.int32, (T, 1), 0)
        rowh = (i + 1) * T + lax.broadcasted_iota(jnp.int32, (HALO, 1), 0)
        for gi, (p, cnt) in enumerate(_pool_windows(hbuf, h, row, T)):
            win = POOL_WINDOWS[gi]
            cs = slice(gi * GC, (gi + 1) * GC)
            pb = p.astype(_MXU)
            wg = w_ref[gi]
            dsc_ref[:, cs] += _colsum(dyv[:, cs] * _dot(pb, wg))
            dzb = dz[:, cs].astype(_MXU)
            dw_ref[gi] += _dot(pb, dzb, _TN)
            dp = _dot(dzb, wg, _NT)
            dph = _dot(dzh[:, cs].astype(_MXU), wg, _NT)
            qbuf[0:T, cs] = dp / cnt
            qbuf[T:T + HALO, cs] = dph / jnp.minimum(rowh + 1, win).astype(_F32)
            dhbuf[:, cs] = _window_sum(qbuf[:, cs], win, False)[0:T, :] - dp
        dh = dhbuf[...]
        dg_ref[...] += _colsum(dh * xn)
        dx_ref[...] = dyv + _rms_bwd(dh, xn, r, gv)

    return pl.pallas_call(
        body, name="pool_bwd", grid=(n,),
        in_specs=[pl.BlockSpec((T, D), lambda i: (i, 0)),
                  pl.BlockSpec((HALO, D), lambda i: (jnp.maximum(i * hb - 1, 0), 0)),
                  pl.BlockSpec((T, D), lambda i: (i, 0)),
                  pl.BlockSpec((HALO, D), lambda i: (jnp.minimum((i + 1) * hb, S // HALO - 1), 0)),
                  _full((1, D)), _full((4, GC, GC)), _full((1, D))],
        out_specs=[pl.BlockSpec((T, D), lambda i: (i, 0)), _full((4, GC, GC)), _full((1, D)), _full((1, D))],
        out_shape=[jax.ShapeDtypeStruct((S, D), _F32), jax.ShapeDtypeStruct((4, GC, GC), _F32),
                   jax.ShapeDtypeStruct((1, D), _F32), jax.ShapeDtypeStruct((1, D), _F32)],
        scratch_shapes=[pltpu.VMEM((T + HALO, D), _F32), pltpu.VMEM((T + HALO, D), _F32), pltpu.VMEM((T, D), _F32)],
        compiler_params=_seq(),
    )(x, x, dy, dy, g, w, sc)


FFN_FWD_TILE, FFN_FWD_CHUNKS = 256, 1
FFN_BWD_TILE, FFN_BWD_CHUNKS = 256, 1
EDGE = 8


def _shift_down(v, k, prev):
    r = pltpu.roll(v, k, axis=0)
    i8 = lax.broadcasted_iota(jnp.int32, (EDGE, v.shape[1]), 0)
    head = jnp.where(i8 >= k, r[0:EDGE, :], pltpu.roll(prev, k, axis=0))
    return jnp.concatenate([head, r[EDGE:, :]], axis=0)


def _shift_up(v, k, nxt):
    T = v.shape[0]
    r = pltpu.roll(v, T - k, axis=0)
    i8 = lax.broadcasted_iota(jnp.int32, (EDGE, v.shape[1]), 0)
    tail = jnp.where(i8 < EDGE - k, r[T - EDGE:, :], pltpu.roll(nxt, EDGE - k, axis=0))
    return jnp.concatenate([r[:T - EDGE, :], tail], axis=0)


def _load_weights(i, pairs, sems):
    @pl.when(i == 0)
    def _():
        cps = [pltpu.make_async_copy(src, dst, sems.at[k]) for k, (src, dst) in enumerate(pairs)]
        for cp in cps:
            cp.start()
        for cp in cps:
            cp.wait()


def _ffn_fwd(x, g, wup_t, cw, cb, wdn):
    S = x.shape[0]
    F2 = wup_t.shape[0]
    F = F2 // 2
    C = F // FFN_FWD_CHUNKS
    T = _tile(S, FFN_FWD_TILE, 16)
    n = S // T

    def body(x_ref, g_ref, wup_hbm, cw_ref, cb_ref, wdn_hbm, o_ref, u_ref, c_ref, wup, wdnv, carry, sems):
        i = pl.program_id(0)
        _load_weights(i, [(wup_hbm, wup), (wdn_hbm, wdnv)], sems)

        @pl.when(i == 0)
        def _():
            carry[...] = jnp.zeros_like(carry)

        xv = x_ref[...]
        hb = (_rms(xv)[0] * g_ref[...]).astype(_MXU)
        acc = jnp.zeros((T, D), _F32)
        for j in range(FFN_FWD_CHUNKS):
            halves = []
            for cs in (slice(j * C, (j + 1) * C), slice(F + j * C, F + (j + 1) * C)):
                u = _dot(hb, wup[cs, :], _NT)
                u_ref[:, cs] = u.astype(u_ref.dtype)
                prev = carry[:, cs]
                carry[:, cs] = u[T - EDGE:, :]
                c = (cw_ref[0:1, cs] * _shift_down(u, 2, prev) + cw_ref[1:2, cs] * _shift_down(u, 1, prev)
                     + cw_ref[2:3, cs] * u + cb_ref[:, cs])
                c_ref[:, cs] = c.astype(c_ref.dtype)
                halves.append(c)
            cg, cv = halves
            a = (cg * jax.nn.sigmoid(cg)) * cv
            acc = acc + _dot(a.astype(_MXU), wdnv[j * C:(j + 1) * C, :])
        o_ref[...] = xv + acc

    any_ = pl.BlockSpec(memory_space=pl.ANY)
    wide = pl.BlockSpec((T, F2), lambda i: (i, 0))
    return pl.pallas_call(
        body, name="ffn_fwd", grid=(n,),
        in_specs=[pl.BlockSpec((T, D), lambda i: (i, 0)), _full((1, D)), any_, _full((CONV_W, F2)), _full((1, F2)), any_],
        out_specs=[pl.BlockSpec((T, D), lambda i: (i, 0)), wide, wide],
        out_shape=[jax.ShapeDtypeStruct((S, D), _F32), jax.ShapeDtypeStruct((S, F2), _MXU),
                   jax.ShapeDtypeStruct((S, F2), _MXU)],
        scratch_shapes=[pltpu.VMEM((F2, D), _MXU), pltpu.VMEM((F, D), _MXU),
                        pltpu.VMEM((EDGE, F2), _F32), pltpu.SemaphoreType.DMA((2,))],
        compiler_params=_seq(),
    )(x, g, wup_t, cw, cb, wdn)


def _ffn_bwd(x, dy, u, c, g, wup_t, cw, wdn):
    S = x.shape[0]
    F2 = wup_t.shape[0]
    F = F2 // 2
    C = F // FFN_BWD_CHUNKS
    T = _tile(S, FFN_BWD_TILE, 16)
    n = S // T

    def body(x_ref, dy_ref, u_ref, c_ref, g_ref, wup_hbm, cw_ref, wdn_hbm,
             dx_ref, du_ref, a_ref, h_ref, dcw_ref, dcb_ref, dg_ref, wup, wdnv, carry, sems):
        i = pl.program_id(0)
        _load_weights(i, [(wup_hbm, wup), (wdn_hbm, wdnv)], sems)

        @pl.when(i == 0)
        def _():
            carry[...] = jnp.zeros_like(carry)
            dcw_ref[...] = jnp.zeros_like(dcw_ref)
            dcb_ref[...] = jnp.zeros_like(dcb_ref)
            dg_ref[...] = jnp.zeros_like(dg_ref)

        gv = g_ref[...]
        xv = x_ref[...]
        xn, r = _rms(xv)
        hbf = (xn * gv).astype(_MXU)
        h_ref[...] = hbf
        dyv = dy_ref[...]
        dyb = dyv.astype(_MXU)
        dh = jnp.zeros((T, D), _F32)
        for j in range(FFN_BWD_CHUNKS):
            gs, vs = slice(j * C, (j + 1) * C), slice(F + j * C, F + (j + 1) * C)
            cg, cv = c_ref[:, gs].astype(_F32), c_ref[:, vs].astype(_F32)
            sg = jax.nn.sigmoid(cg)
            sl = cg * sg
            a_ref[:, gs] = (sl * cv).astype(a_ref.dtype)
            da = _dot(dyb, wdnv[gs, :], _NT)
            for cs, dc in ((gs, da * cv * (sg * (1.0 + cg * (1.0 - sg)))), (vs, da * sl)):
                nxt = carry[:, cs]
                carry[:, cs] = dc[0:EDGE, :]
                dc1, dc2 = _shift_up(dc, 1, nxt), _shift_up(dc, 2, nxt)
                uf = u_ref[:, cs].astype(_F32)
                dcb_ref[:, cs] += _colsum(dc)
                for k, d in enumerate((dc2, dc1, dc)):
                    dcw_ref[k:k + 1, cs] += _colsum(d * uf)
                du = cw_ref[2:3, cs] * dc + cw_ref[1:2, cs] * dc1 + cw_ref[0:1, cs] * dc2
                dub = du.astype(_MXU)
                du_ref[:, cs] = dub
                dh = dh + _dot(dub, wup[cs, :])
        dg_ref[...] += _colsum(dh * xn)
        dx_ref[...] = dyv + _rms_bwd(dh, xn, r, gv)

    any_ = pl.BlockSpec(memory_space=pl.ANY)
    rev = lambda i: (n - 1 - i, 0)
    return pl.pallas_call(
        body, name="ffn_bwd", grid=(n,),
        in_specs=[pl.BlockSpec((T, D), rev), pl.BlockSpec((T, D), rev), pl.BlockSpec((T, F2), rev),
                  pl.BlockSpec((T, F2), rev), _full((1, D)), any_, _full((CONV_W, F2)), any_],
        out_specs=[pl.BlockSpec((T, D), rev), pl.BlockSpec((T, F2), rev), pl.BlockSpec((T, F), rev),
                   pl.BlockSpec((T, D), rev), _full((CONV_W, F2)), _full((1, F2)), _full((1, D))],
        out_shape=[jax.ShapeDtypeStruct((S, D), _F32), jax.ShapeDtypeStruct((S, F2), _MXU),
                   jax.ShapeDtypeStruct((S, F), _MXU), jax.ShapeDtypeStruct((S, D), _MXU),
                   jax.ShapeDtypeStruct((CONV_W, F2), _F32), jax.ShapeDtypeStruct((1, F2), _F32),
                   jax.ShapeDtypeStruct((1, D), _F32)],
        scratch_shapes=[pltpu.VMEM((F2, D), _MXU), pltpu.VMEM((F, D), _MXU),
                        pltpu.VMEM((EDGE, F2), _F32), pltpu.SemaphoreType.DMA((2,))],
        compiler_params=_seq(),
    )(x, dy, u, c, g, wup_t, cw, wdn)


def _tn_matmul(a, b, name, token=None):
    S, M = a.shape
    N = b.shape[1]
    bm = _tile(M, 1408, LANES)
    tk = _tile(S, 2048, 16)
    nk = S // tk
    tokens = [] if token is None else [token]

    def body(a_ref, b_ref, *rest):
        o_ref, acc = rest[-2:]
        k = pl.program_id(1)

        @pl.when(k == 0)
        def _():
            acc[...] = jnp.zeros_like(acc)

        acc[...] += _dot(a_ref[...].astype(_MXU), b_ref[...].astype(_MXU), _TN)

        @pl.when(k == nk - 1)
        def _():
            o_ref[...] = acc[...].astype(o_ref.dtype)

    return pl.pallas_call(
        body, name=name, grid=(M // bm, nk),
        in_specs=[pl.BlockSpec((tk, bm), lambda i, k: (k, i)), pl.BlockSpec((tk, N), lambda i, k: (k, 0))]
        + [_full((8, LANES))] * len(tokens),
        out_specs=pl.BlockSpec((bm, N), lambda i, k: (i, 0)),
        out_shape=jax.ShapeDtypeStruct((M, N), _MXU),
        scratch_shapes=[pltpu.VMEM((bm, N), _F32)],
        compiler_params=_seq(2),
    )(a, b, *tokens)


def _kv_fwd(x, g, wkv, bkv):
    S = x.shape[0]
    T = _tile(S, 512, 16)

    def body(x_ref, g_ref, w_ref, b_ref, o_ref):
        hb = (_rms(x_ref[...])[0] * g_ref[...]).astype(_MXU)
        o_ref[...] = (_dot(hb, w_ref[...]) + b_ref[...]).astype(o_ref.dtype)

    return pl.pallas_call(
        body, name="kv_fwd", grid=(S // T,),
        in_specs=[pl.BlockSpec((T, D), lambda i: (i, 0)), _full((1, D)), _full((D, KVD)), _full((1, KVD))],
        out_specs=pl.BlockSpec((T, KVD), lambda i: (i, 0)),
        out_shape=jax.ShapeDtypeStruct((S, KVD), _MXU),
        compiler_params=_seq(),
    )(x, g, wkv, bkv)


def _kv_bwd(x, dx_in, g, wkv, cur_a, prev_a, cur_b, prev_b):
    S = x.shape[0]
    T = _tile(S, 512, BLK)
    n, per = S // T, T // BLK

    def body(x_ref, dxi_ref, g_ref, w_ref, ca, pa, na, cb, pb, nb, dx_ref, dw_ref, db_ref, dg_ref):
        i = pl.program_id(0)

        @pl.when(i == 0)
        def _():
            dw_ref[...] = jnp.zeros_like(dw_ref)
            db_ref[...] = jnp.zeros_like(db_ref)
            dg_ref[...] = jnp.zeros_like(dg_ref)

        gv = g_ref[...]
        xn, r = _rms(x_ref[...])
        nxt = jnp.where(i < n - 1, na[...] + nb[...], 0.0)
        prev = jnp.concatenate([pa[BLK:, :] + pb[BLK:, :], nxt], axis=0) if per > 1 else nxt
        dkv = ca[...] + cb[...] + prev
        db_ref[...] += _colsum(dkv)
        dkb = dkv.astype(_MXU)
        dw_ref[...] += _dot((xn * gv).astype(_MXU), dkb, _TN)
        dh = _dot(dkb, w_ref[...], _NT)
        dg_ref[...] += _colsum(dh * xn)
        dx_ref[...] = dxi_ref[...] + _rms_bwd(dh, xn, r, gv)

    blk = lambda w: pl.BlockSpec((T, w), lambda i: (i, 0))
    nxt = pl.BlockSpec((BLK, KVD), lambda i: (jnp.minimum((i + 1) * per, S // BLK - 1), 0))
    return pl.pallas_call(
        body, name="kv_bwd", grid=(n,),
        in_specs=[blk(D), blk(D), _full((1, D)), _full((D, KVD)), blk(KVD), blk(KVD), nxt, blk(KVD), blk(KVD), nxt],
        out_specs=[blk(D), _full((D, KVD)), _full((1, KVD)), _full((1, D))],
        out_shape=[jax.ShapeDtypeStruct((S, D), _F32), jax.ShapeDtypeStruct((D, KVD), _F32),
                   jax.ShapeDtypeStruct((1, KVD), _F32), jax.ShapeDtypeStruct((1, D), _F32)],
        compiler_params=_seq(),
    )(x, dx_in, g, wkv, cur_a, prev_a, prev_a, cur_b, prev_b, prev_b)


STACK = GROUP * BLK
ATTN_FWD_BLOCKS = 4


def _attn_mask(i, rows):
    qi = lax.broadcasted_iota(jnp.int32, (rows, 2 * BLK), 0) & (BLK - 1)
    si = lax.broadcasted_iota(jnp.int32, (rows, 2 * BLK), 1)
    return (si > qi) & (si <= qi + BLK) & jnp.logical_or(i > 0, si >= BLK)


def _low_half():
    return lax.broadcasted_iota(jnp.int32, (BLK, PAIR), 1) < HEAD_DIM


def _stack_heads(ref, kh, dst):
    low = _low_half()
    for pp in range(GROUP // 2):
        pr = kh * (GROUP // 2) + pp
        v2 = ref[:, pr * PAIR:(pr + 1) * PAIR]
        zero = jnp.zeros_like(v2)
        dst[2 * pp * BLK:(2 * pp + 1) * BLK, :] = jnp.where(low, v2, zero)
        dst[(2 * pp + 1) * BLK:(2 * pp + 2) * BLK, :] = jnp.where(low, zero, v2)


def _unstack_heads(st, pp):
    return jnp.where(_low_half(), st[2 * pp * BLK:(2 * pp + 1) * BLK, :], st[(2 * pp + 1) * BLK:(2 * pp + 2) * BLK, :])


def _sink_col(sk_ref, kh):
    return jnp.concatenate([jnp.full((BLK, 1), sk_ref[kh * GROUP + h], _F32) for h in range(GROUP)], axis=0)


def _head_probs(qm, kd, mask, sink):
    s = jnp.where(mask, _dot(qm, kd, _NT) * (HEAD_DIM ** -0.5), NEG)
    m = jnp.maximum(jnp.max(s, axis=-1, keepdims=True), sink)
    p = jnp.exp(s - m)
    es = jnp.exp(sink - m)
    inv = 1.0 / (jnp.sum(p, axis=-1, keepdims=True) + es)
    return p * inv, es * inv


def _attn_fwd(x, g, wq, bq, sinks, kvd, wo, bo):
    S = x.shape[0]
    nb = min(ATTN_FWD_BLOCKS, S // BLK)
    T = nb * BLK
    n = S // T

    def body(x_ref, g_ref, wq_ref, bq_ref, sk_ref, kp_ref, kc_ref, wo_ref, bo_ref, xo_ref, q_ref, o_ref, win):
        i = pl.program_id(0)
        xv = x_ref[...]
        hb = (_rms(xv)[0] * g_ref[...]).astype(_MXU)
        q_ref[...] = (_dot(hb, wq_ref[...]) + bq_ref[...]).astype(q_ref.dtype)
        win[0:BLK, :] = kp_ref[...]
        win[BLK:, :] = kc_ref[...]
        low = _low_half()
        for b in range(nb):
            rows, keys = slice(b * BLK, (b + 1) * BLK), slice(b * BLK, (b + 2) * BLK)
            mask = _attn_mask(i if b == 0 else 1, BLK)
            for pr in range(N_HEADS // 2):
                kh = (2 * pr) // GROUP
                kd = win[keys, kh * PAIR:(kh + 1) * PAIR]
                vd = win[keys, (N_KV + kh) * PAIR:(N_KV + kh + 1) * PAIR]
                q2 = q_ref[rows, pr * PAIR:(pr + 1) * PAIR]
                outs = []
                for half in range(2):
                    qm = jnp.where(low if half == 0 else ~low, q2, jnp.zeros_like(q2))
                    pbs, _ = _head_probs(qm, kd, mask, sk_ref[2 * pr + half])
                    outs.append(_dot(pbs.astype(_MXU), vd))
                o_ref[rows, pr * PAIR:(pr + 1) * PAIR] = jnp.where(low, outs[0], outs[1]).astype(o_ref.dtype)
        xo_ref[...] = xv + _dot(o_ref[...], wo_ref[...]) + bo_ref[...]

    blk = lambda w: pl.BlockSpec((T, w), lambda i: (i, 0))
    return pl.pallas_call(
        body, name="attn_fwd", grid=(n,),
        in_specs=[blk(D), _full((1, D)), _full((D, D)), _full((1, D)),
                  pl.BlockSpec(memory_space=pltpu.SMEM),
                  pl.BlockSpec((BLK, KVD), lambda i: (jnp.maximum(i * nb - 1, 0), 0)), blk(KVD),
                  _full((D, D)), _full((1, D))],
        out_specs=[blk(D), blk(D), blk(D)],
        out_shape=[jax.ShapeDtypeStruct((S, D), _F32), jax.ShapeDtypeStruct((S, D), _MXU),
                   jax.ShapeDtypeStruct((S, D), _MXU)],
        scratch_shapes=[pltpu.VMEM((T + BLK, KVD), _MXU)],
        compiler_params=_seq(),
    )(x, g, wq, bq, sinks, kvd, kvd, wo, bo)


def _attn_bwd(x, dy, q, o, g, wq, sinks, kvd, wo):
    S = x.shape[0]
    n = S // BLK
    all_rows = N_HEADS * BLK

    def body(x_ref, dy_ref, q_ref, o_ref, g_ref, wq_ref, sk_ref, kp_ref, kc_ref, wo_ref,
             dx_ref, dq_ref, h_ref, dc_ref, dp_ref, dbq_ref, dbo_ref, dg_ref, dsk_ref, win, dob, qs, dos, pall, dsall):
        i = pl.program_id(0)

        @pl.when(i == 0)
        def _():
            dbq_ref[...] = jnp.zeros_like(dbq_ref)
            dbo_ref[...] = jnp.zeros_like(dbo_ref)
            dg_ref[...] = jnp.zeros_like(dg_ref)
            dsk_ref[...] = jnp.zeros_like(dsk_ref)

        gv = g_ref[...]
        xv = x_ref[...]
        xn, r = _rms(xv)
        h_ref[...] = (xn * gv).astype(h_ref.dtype)
        dyv = dy_ref[...]
        dbo_ref[...] += _colsum(dyv)
        dob[...] = _dot(dyv.astype(_MXU), wo_ref[...], _NT).astype(dob.dtype)
        win[0:BLK, :] = kp_ref[...]
        win[BLK:, :] = kc_ref[...]
        mask = _attn_mask(i, BLK)
        low = _low_half()
        lane = lax.broadcasted_iota(jnp.int32, (1, LANES), 1)
        for pr in range(N_HEADS // 2):
            kh = (2 * pr) // GROUP
            kd = win[:, kh * PAIR:(kh + 1) * PAIR]
            vd = win[:, (N_KV + kh) * PAIR:(N_KV + kh + 1) * PAIR]
            q2 = q_ref[:, pr * PAIR:(pr + 1) * PAIR]
            do2 = dob[:, pr * PAIR:(pr + 1) * PAIR]
            od = do2.astype(_F32) * o_ref[:, pr * PAIR:(pr + 1) * PAIR].astype(_F32)
            for half in range(2):
                hd = 2 * pr + half
                rows = slice(hd * BLK, (hd + 1) * BLK)
                sel = low if half == 0 else ~low
                qm = jnp.where(sel, q2, jnp.zeros_like(q2))
                dom = jnp.where(sel, do2, jnp.zeros_like(do2))
                qs[rows, :] = qm
                dos[rows, :] = dom
                pbs, ps = _head_probs(qm, kd, mask, sk_ref[hd])
                pall[rows, :] = pbs.astype(_MXU)
                delta = jnp.sum(jnp.where(sel, od, 0.0), axis=-1, keepdims=True)
                dsall[rows, :] = (pbs * (_dot(dom, vd, _NT) - delta) * (HEAD_DIM ** -0.5)).astype(_MXU)
                dsk_ref[...] -= jnp.where(lane == hd, _colsum(ps * delta), 0.0)
        dq_all = []
        for kh in range(N_KV):
            ks = slice(kh * PAIR, (kh + 1) * PAIR)
            vs = slice((N_KV + kh) * PAIR, (N_KV + kh + 1) * PAIR)
            rows = slice(kh * STACK, (kh + 1) * STACK)
            dqst = _dot(dsall[rows, :], win[:, ks])
            dk = _dot(dsall[rows, :], qs[rows, :], _TN)
            dv = _dot(pall[rows, :], dos[rows, :], _TN)
            dp_ref[:, ks], dc_ref[:, ks] = dk[0:BLK, :], dk[BLK:, :]
            dp_ref[:, vs], dc_ref[:, vs] = dv[0:BLK, :], dv[BLK:, :]
            dq_all += [_unstack_heads(dqst, pp) for pp in range(GROUP // 2)]
        dq = jnp.concatenate(dq_all, axis=1)
        dbq_ref[...] += _colsum(dq)
        dqb = dq.astype(_MXU)
        dq_ref[...] = dqb
        dh = _dot(dqb, wq_ref[...], _NT)
        dg_ref[...] += _colsum(dh * xn)
        dx_ref[...] = dyv + _rms_bwd(dh, xn, r, gv)

    blk = lambda w: pl.BlockSpec((BLK, w), lambda i: (i, 0))
    return pl.pallas_call(
        body, name="attn_bwd", grid=(n,),
        in_specs=[blk(D), blk(D), blk(D), blk(D), _full((1, D)), _full((D, D)),
                  pl.BlockSpec(memory_space=pltpu.SMEM),
                  pl.BlockSpec((BLK, KVD), lambda i: (jnp.maximum(i - 1, 0), 0)), blk(KVD), _full((D, D))],
        out_specs=[blk(D), blk(D), blk(D), blk(KVD), blk(KVD),
                   _full((1, D)), _full((1, D)), _full((1, D)), _full((1, LANES))],
        out_shape=[jax.ShapeDtypeStruct((S, D), _F32), jax.ShapeDtypeStruct((S, D), _MXU),
                   jax.ShapeDtypeStruct((S, D), _MXU), jax.ShapeDtypeStruct((S, KVD), _F32),
                   jax.ShapeDtypeStruct((S, KVD), _F32), jax.ShapeDtypeStruct((1, D), _F32),
                   jax.ShapeDtypeStruct((1, D), _F32), jax.ShapeDtypeStruct((1, D), _F32),
                   jax.ShapeDtypeStruct((1, LANES), _F32)],
        scratch_shapes=[pltpu.VMEM((2 * BLK, KVD), _MXU), pltpu.VMEM((BLK, D), _MXU),
                        pltpu.VMEM((all_rows, PAIR), _MXU), pltpu.VMEM((all_rows, PAIR), _MXU),
                        pltpu.VMEM((all_rows, 2 * BLK), _MXU), pltpu.VMEM((all_rows, 2 * BLK), _MXU)],
        compiler_params=_seq(),
    )(x, dy, q, o, g, wq, sinks, kvd, kvd, wo)


def _loss_bwd(x, g, tgt):
    S = x.shape[0]
    T = _tile(S, 512, 8)

    def body(x_ref, g_ref, t_ref, dx_ref, ls_ref, dg_ref):
        @pl.when(pl.program_id(0) == 0)
        def _():
            ls_ref[...] = jnp.zeros_like(ls_ref)
            dg_ref[...] = jnp.zeros_like(dg_ref)

        gv = g_ref[...]
        xn, r = _rms(x_ref[...])
        err = xn * gv - t_ref[...]
        ls_ref[...] += 0.5 * jnp.sum(jnp.mean(err * err, axis=-1, keepdims=True))
        dyv = err * (1.0 / D)
        dg_ref[...] += _colsum(dyv * xn)
        dx_ref[...] = _rms_bwd(dyv, xn, r, gv)

    return pl.pallas_call(
        body, name="loss_bwd", grid=(S // T,),
        in_specs=[pl.BlockSpec((T, D), lambda i: (i, 0)), _full((1, D)), pl.BlockSpec((T, D), lambda i: (i, 0))],
        out_specs=[pl.BlockSpec((T, D), lambda i: (i, 0)), _full((8, LANES)), _full((1, D))],
        out_shape=[jax.ShapeDtypeStruct((S, D), _F32), jax.ShapeDtypeStruct((8, LANES), _F32),
                   jax.ShapeDtypeStruct((1, D), _F32)],
        compiler_params=_seq(),
    )(x, g, tgt)


def _me():
    return 4 * lax.axis_index("x") + 2 * lax.axis_index("y") + lax.axis_index("c")


def _peer(j):
    x, y, c = lax.axis_index("x"), lax.axis_index("y"), lax.axis_index("c")
    px = 1 - x if j & 4 else x
    py = 1 - y if j & 2 else y
    pc = 1 - c if j & 1 else c
    return (px, py, pc), 4 * px + 2 * py + pc


_HBM = pl.BlockSpec(memory_space=pltpu.HBM)
_SEMS = pl.BlockSpec(memory_space=pltpu.SEMAPHORE)
_EFFECT = pltpu.SideEffectType.DATAFLOW_SIDE_EFFECTING


def _in_hbm(a):
    return pltpu.with_memory_space_constraint(a, pltpu.HBM)


def _start_copies(name, groups):
    flat = []
    for srcs, zones, _ in groups:
        flat += [_in_hbm(a) for a in srcs] + [_in_hbm(lax.empty(z.shape, z.dtype)) for z in zones]
    n_in, n_g = len(flat), len(groups)

    def body(*refs):
        sems = refs[n_in:n_in + 2 * n_g]
        me, k = _me(), 0
        for gi, (srcs, zones, plan) in enumerate(groups):
            src_refs, zone_refs = refs[k:k + len(srcs)], refs[k + len(srcs):k + len(srcs) + len(zones)]
            k += len(srcs) + len(zones)
            for t, (si, zi, src_of, dst_of) in enumerate(plan):
                for j in range(1, N_DEV):
                    dev, pk = _peer(j)
                    pltpu.make_async_remote_copy(
                        src_ref=src_of(src_refs[si], pk), dst_ref=dst_of(zone_refs[zi], me),
                        send_sem=sems[2 * gi].at[t * (N_DEV - 1) + j - 1], recv_sem=sems[2 * gi + 1].at[t * (N_DEV - 1) + j - 1],
                        device_id=dev, device_id_type=pl.DeviceIdType.MESH).start()
                pltpu.make_async_copy(src_of(src_refs[si], me), dst_of(zone_refs[zi], me),
                                      sems[2 * gi].at[len(plan) * (N_DEV - 1) + t]).start()
        refs[-1][...] = jnp.zeros_like(refs[-1])

    sem_shapes = []
    for _, _, plan in groups:
        sem_shapes += [pltpu.SemaphoreType.DMA((len(plan) * N_DEV,)), pltpu.SemaphoreType.DMA((len(plan) * (N_DEV - 1),))]
    outs = pl.pallas_call(
        body, name=name,
        out_shape=(*sem_shapes, *[pltpu.HBM(a.shape, a.dtype) for a in flat], jax.ShapeDtypeStruct((8, LANES), _F32)),
        in_specs=[_HBM] * n_in,
        out_specs=(*[_SEMS] * (2 * n_g), *[_HBM] * n_in, pl.BlockSpec(memory_space=pltpu.VMEM)),
        input_output_aliases={k: 2 * n_g + k for k in range(n_in)},
        compiler_params=pltpu.CompilerParams(has_side_effects=_EFFECT),
    )(*flat)
    handles, k = [], 2 * n_g
    for gi, (srcs, zones, plan) in enumerate(groups):
        ns, nz = len(srcs), len(zones)
        handles.append((outs[2 * gi], outs[2 * gi + 1], list(outs[k:k + ns]), list(outs[k + ns:k + ns + nz]), plan))
        k += ns + nz
    return handles, outs[-1]


def _wait_copies(name, handles, after):
    flat = []
    for _, _, srcs, zones, _ in handles:
        flat += srcs + zones
    n_in, n_g = len(flat), len(handles)

    def body(*refs):
        sems = refs[n_in:n_in + 2 * n_g]
        me, k, local, remote = _me(), 0, [], []
        for gi, (_, _, srcs, zones, plan) in enumerate(handles):
            ns, nz = len(srcs), len(zones)
            src_refs, zone_refs = refs[k:k + ns], refs[k + ns:k + ns + nz]
            k += ns + nz
            for t, (si, zi, src_of, dst_of) in enumerate(plan):
                local.append(pltpu.make_async_copy(src_of(src_refs[si], me), dst_of(zone_refs[zi], me),
                                                   sems[2 * gi].at[len(plan) * (N_DEV - 1) + t]))
                for j in range(1, N_DEV):
                    dev, pk = _peer(j)
                    remote.append(pltpu.make_async_remote_copy(
                        src_ref=src_of(src_refs[si], pk), dst_ref=dst_of(zone_refs[zi], pk),
                        send_sem=sems[2 * gi].at[t * (N_DEV - 1) + j - 1], recv_sem=sems[2 * gi + 1].at[t * (N_DEV - 1) + j - 1],
                        device_id=dev, device_id_type=pl.DeviceIdType.MESH))
        for cp in remote:
            cp.wait_send()
            cp.wait_recv()
        for cp in local:
            cp.wait()

    sem_args = []
    for send, recv, _, _, _ in handles:
        sem_args += [send, recv]
    outs = pl.pallas_call(
        body, name=name, out_shape=tuple(pltpu.HBM(a.shape, a.dtype) for a in flat),
        in_specs=[_HBM] * n_in + [_SEMS] * (2 * n_g) + [pl.BlockSpec(memory_space=pl.ANY)],
        out_specs=tuple([_HBM] * n_in), input_output_aliases={k: k for k in range(n_in)},
        compiler_params=pltpu.CompilerParams(has_side_effects=_EFFECT),
    )(*flat, *sem_args, after)
    res, k = [], 0
    for _, _, srcs, zones, _ in handles:
        res.append(list(outs[k + len(srcs):k + len(srcs) + len(zones)]))
        k += len(srcs) + len(zones)
    return res


def _rows(axis, size):
    def of(ref, b):
        start = b * size
        if size % 8 == 0:
            start = pl.multiple_of(start, 8)
        return ref.at[(slice(None),) * axis + (pl.ds(start, size),)]
    return of


def _whole(ref, b):
    return ref


def _slot(ref, b):
    return ref.at[b]


def _gather_group(shards):
    zones, plan = [], []
    for k, (a, axis) in enumerate(shards):
        zones.append(jax.ShapeDtypeStruct(a.shape[:axis] + (N_DEV * a.shape[axis],) + a.shape[axis + 1:], a.dtype))
        plan.append((k, k, _whole, _rows(axis, a.shape[axis])))
    return [a for a, _ in shards], zones, plan


def _scatter_group(grads):
    zones, plan = [], []
    for k, (a, axis) in enumerate(grads):
        size = a.shape[axis] // N_DEV
        zones.append(jax.ShapeDtypeStruct((N_DEV,) + a.shape[:axis] + (size,) + a.shape[axis + 1:], a.dtype))
        plan.append((k, k, _rows(axis, size), _slot))
    return [a for a, _ in grads], zones, plan


def _sum_landed(land):
    g = land[0].astype(_F32)
    for b in range(1, N_DEV):
        g = g + land[b].astype(_F32)
    return g


def _landed_specs(n_layers, tr, C, nr):
    def spec(k):
        return pl.BlockSpec((N_DEV, tr, C), lambda l, i: (0, jnp.where(l == k, i, jnp.where(l < k, 0, nr - 1)), 0))
    return [spec(k) for k in range(n_layers)]


def _per_layer(l, zone_refs, fn):
    for k, ref in enumerate(zone_refs):
        @pl.when(l == k)
        def _(ref=ref):
            fn(_sum_landed(ref))


def _sum8(zones):
    L = len(zones)
    _, R, C = zones[0].shape
    tr = _tile(R, 352, 16)
    nr = R // tr

    def body(*refs):
        o_ref = refs[L]

        def put(g):
            o_ref[...] = g

        _per_layer(pl.program_id(0), refs[:L], put)

    return pl.pallas_call(
        body, name="sum8", grid=(L, nr), in_specs=_landed_specs(L, tr, C, nr),
        out_specs=pl.BlockSpec((tr, C), lambda l, i: (l * nr + i, 0)),
        out_shape=jax.ShapeDtypeStruct((L * R, C), _F32), compiler_params=_seq(2),
    )(*zones)


def _adam_update(gv, w_ref, m_ref, v_ref, d_ref, mo_ref, vo_ref):
    mn = ADAM_B1 * m_ref[...] + (1.0 - ADAM_B1) * gv
    vn = ADAM_B2 * v_ref[...] + (1.0 - ADAM_B2) * (gv * gv)
    mo_ref[...] = mn
    vo_ref[...] = vn
    d_ref[...] = -ADAM_LR * ((mn / (1.0 - ADAM_B1 ** ADAM_STEP)) / (jnp.sqrt(vn / (1.0 - ADAM_B2 ** ADAM_STEP)) + ADAM_EPS)
                             + ADAM_WD * w_ref[...])


def _earlier(outs):
    outs = list(outs or [])
    return outs, [pl.BlockSpec(memory_space=pl.ANY)] * len(outs)


def _adamw(g, w, m, v, name, first_row=0, earlier=None):
    Rg, C = g.shape
    R = w.shape[0]
    tr = _tile(Rg if first_row == 0 else min(Rg, first_row), 256, 16)
    off = first_row // tr
    more, more_specs = _earlier(earlier)

    def body(g_ref, w_ref, m_ref, v_ref, *rest):
        go_ref, d_ref, mo_ref, vo_ref = rest[-4:]
        gv = g_ref[...]
        go_ref[...] = gv
        _adam_update(gv, w_ref, m_ref, v_ref, d_ref, mo_ref, vo_ref)

    row = pl.BlockSpec((tr, C), lambda i: (i + off, 0))
    return pl.pallas_call(
        body, name=name, grid=(Rg // tr,), in_specs=[pl.BlockSpec((tr, C), lambda i: (i, 0))] + [row] * 3 + more_specs,
        out_specs=[row] * 4, out_shape=[jax.ShapeDtypeStruct((R, C), _F32)] * 4,
        input_output_aliases={4 + k: k for k in range(len(more))}, compiler_params=_seq(),
    )(g, w, m, v, *more)


def _adamw_landed(zones, w, m, v, name, first_layer=0, earlier=None):
    L = len(zones)
    _, R, C = zones[0].shape
    tr = _tile(R, 176, 16)
    nr = R // tr
    more, more_specs = _earlier(earlier)

    def body(*refs):
        w_ref, m_ref, v_ref = refs[L:L + 3]
        g_ref, d_ref, mo_ref, vo_ref = refs[-4:]

        def update(g):
            g_ref[...] = g
            _adam_update(g, w_ref, m_ref, v_ref, d_ref, mo_ref, vo_ref)

        _per_layer(pl.program_id(0), refs[:L], update)

    row = pl.BlockSpec((tr, C), lambda l, i: ((l + first_layer) * nr + i, 0))
    return pl.pallas_call(
        body, name=name, grid=(L, nr), in_specs=_landed_specs(L, tr, C, nr) + [row] * 3 + more_specs,
        out_specs=[row] * 4, out_shape=[jax.ShapeDtypeStruct(w.shape, _F32)] * 4,
        input_output_aliases={L + 3 + k: k for k in range(len(more))}, compiler_params=_seq(2),
    )(*zones, w, m, v, *more)


def _pack(parts):
    flat = jnp.concatenate([p.reshape(-1).astype(_F32) for p in parts])
    n = flat.shape[0]
    rows = -(-n // (8 * LANES)) * 8
    return jnp.pad(flat, (0, rows * LANES - n)).reshape(rows, LANES)


def _unpack(packed, shapes):
    flat, out, k = packed.reshape(-1), [], 0
    for s in shapes:
        n = 1
        for d in s:
            n *= d
        out.append(flat[k:k + n].reshape(s))
        k += n
    return out


def kernel(x, norm1_g, norm2_g, pool_w, pool_scale, kv_norm_g, w_kv, b_kv, w_q, b_q, sinks, w_o, b_o, ffn_up, ffn_conv_w, ffn_conv_b, ffn_down, final_g, loss_target, m_norm1_g, m_norm2_g, m_pool_w, m_pool_scale, m_kv_norm_g, m_w_kv, m_b_kv, m_w_q, m_b_q, m_sinks, m_w_o, m_b_o, m_ffn_up, m_ffn_conv_w, m_ffn_conv_b, m_ffn_down, m_final_g, v_norm1_g, v_norm2_g, v_pool_w, v_pool_scale, v_kv_norm_g, v_w_kv, v_b_kv, v_w_q, v_b_q, v_sinks, v_w_o, v_b_o, v_ffn_up, v_ffn_conv_w, v_ffn_conv_b, v_ffn_down, v_final_g):
    S = x.shape[1]
    F2s = ffn_up.shape[2]
    F2 = N_DEV * F2s
    me = _me()
    x0 = x.reshape(S, D)
    tgt = loss_target.reshape(S, D)
    row = lambda a: a.reshape(1, -1)

    small = _pack([pool_scale, ffn_conv_w])
    wire = lambda a: a.astype(_MXU)
    ffn_w = lambda l: [(wire(ffn_up[l]).T, 0), (wire(ffn_down[l]), 0)]
    attn_w = lambda j: [(wire(w_q[j]), 0), (wire(w_o[j]), 0)]
    gathers, token = _start_copies("gather_start", [_gather_group(g) for g in (
        [(wire(pool_w[0]), 1), (small[None], 0)], ffn_w(0), [(wire(pool_w[1]), 1)] + ffn_w(1),
        [(wire(w_kv), 0)] + attn_w(0), ffn_w(2), attn_w(1), ffn_w(3))])

    def gathered(k, after):
        return _wait_copies("gather_wait_%d" % k, [gathers[k]], after)[0]

    pw, up_t, down, wq, wo = [None] * N_A, [None] * DEPTH, [None] * DEPTH, [None] * 2, [None] * 2
    pw[0], small_all = gathered(0, token)
    n_ps = pool_scale.size
    small_all = small_all.reshape(N_DEV, -1)
    pscale = jnp.transpose(small_all[:, :n_ps].reshape(N_DEV, N_A, D // N_DEV), (1, 0, 2)).reshape(N_A, D)
    conv_w = jnp.transpose(small_all[:, n_ps:n_ps + ffn_conv_w.size].reshape(N_DEV, DEPTH, CONV_W, F2s),
                           (1, 2, 0, 3)).reshape(DEPTH, CONV_W, F2)

    def dup(a):
        a4 = a.reshape(a.shape[:-1] + (2 * N_KV, 1, HEAD_DIM))
        return jnp.broadcast_to(a4, a.shape[:-1] + (2 * N_KV, 2, HEAD_DIM)).reshape(a.shape[:-1] + (KVD,))

    def fold(a):
        return a.reshape(a.shape[:-1] + (2 * N_KV, 2, HEAD_DIM)).sum(axis=-2).reshape(a.shape[:-1] + (2 * N_KV * HEAD_DIM,))

    xs, us, qs, os_ = [x0], [], [], []
    xc = x0
    kvd = None
    for l in range(DEPTH):
        if l == 1:
            pw[1], up_t[1], down[1] = gathered(2, xc)
        if l == 3:
            wq[1], wo[1] = gathered(5, xc)
        if l < N_A:
            xc = _pool_fwd(xc, row(norm1_g[l]), pw[l], row(pscale[l]))
        else:
            j = l - N_A
            xc, q, o = _attn_fwd(xc, row(norm1_g[l]), wq[j], row(b_q[j]), sinks[j], kvd, wo[j], row(b_o[j]))
            qs.append(q)
            os_.append(o)
        xs.append(xc)
        if l != 1:
            up_t[l], down[l] = gathered((1, None, 4, 6)[l], xc)
        xc, u, c = _ffn_fwd(xc, row(norm2_g[l]), up_t[l], conv_w[l], row(ffn_conv_b[l]), down[l])
        us.append((u, c))
        xs.append(xc)
        if l == N_A - 1:
            wkv, wq[0], wo[0] = gathered(3, xc)
            wkv_d, bkv_d = dup(wkv), dup(row(b_kv))
            kvd = _kv_fwd(xc, row(kv_norm_g), wkv_d, bkv_d)

    dx, loss_p, d_final = _loss_bwd(xc, row(final_g), tgt)
    d_n1, d_n2, d_cw, d_cb = [None] * DEPTH, [None] * DEPTH, [None] * DEPTH, [None] * DEPTH
    d_bq, d_bo, d_sk, d_ps, dkv_parts = [None] * 2, [None] * 2, [None] * 2, [None] * N_A, []
    up_z, down_z, mix_z = [None] * DEPTH, [None] * DEPTH, [None] * DEPTH
    token = None

    def after(gain):
        return gain if token is None else lax.optimization_barrier((gain, token))[0]

    rep_names = ["norm1_g", "norm2_g", "kv_norm_g", "b_kv", "b_q", "sinks", "b_o", "ffn_conv_b", "final_g"]

    def small_parts(n1_rest, ps_rest):
        zero = jnp.zeros((1, D), _F32)
        return [jnp.concatenate([zero] + n1_rest), jnp.concatenate(d_n2), d_kvg, fold(d_bkv), jnp.concatenate(d_bq),
                jnp.concatenate([s[:, :N_HEADS] for s in d_sk]), jnp.concatenate(d_bo), jnp.concatenate(d_cb), d_final,
                jnp.concatenate([zero] + ps_rest), jnp.stack(d_cw), loss_p[0:1, 0:1]]

    for l in reversed(range(DEPTH)):
        x_in, x_mid, x_out = xs[2 * l], xs[2 * l + 1], xs[2 * l + 2]
        mixer_grads = []
        if l == N_A - 1:
            dx, d_wkv, d_bkv, d_kvg = _kv_bwd(x_out, dx, after(row(kv_norm_g)), wkv_d, *dkv_parts)
            mixer_grads.append((fold(d_wkv).astype(_MXU), 0))
        dy = dx
        dx, du, a, h, d_cw[l], d_cb[l], d_n2[l] = _ffn_bwd(
            x_mid, dy, *us[l], after(row(norm2_g[l])), up_t[l], conv_w[l], down[l])
        g_up = _tn_matmul(du, h, "tn_up")
        if l > 0:
            (up_z[l], down_z[l]), token = _start_copies("scatter_ffn_%d" % l, [
                _scatter_group([(g_up, 0)]), _scatter_group([(_tn_matmul(a, dy, "tn_down"), 0)])])
        else:
            (small_handle, up_z[l]), token = _start_copies("scatter_up_0", [
                _gather_group([(_pack(small_parts(d_n1[1:], d_ps[1:]))[None], 0)]), _scatter_group([(g_up, 0)])])
            (down_z[l],), token = _start_copies("scatter_down_0", [
                _scatter_group([(_tn_matmul(a, dy, "tn_down", token), 0)])])
        dy = dx
        if l < N_A:
            dx, d_pw, d_ps[l], d_n1[l] = _pool_bwd(x_in, dy, after(row(norm1_g[l])), pw[l], row(pscale[l]))
            mixer_grads.append((d_pw.astype(_MXU), 1))
        else:
            j = l - N_A
            dx, dq, h, d_cur, d_prev, d_bq[j], d_bo[j], d_n1[l], d_sk[j] = _attn_bwd(
                x_in, dy, qs[j], os_[j], after(row(norm1_g[l])), wq[j], sinks[j], kvd, wo[j])
            mixer_grads += [(_tn_matmul(h, dq, "tn_q"), 0), (_tn_matmul(os_[j], dy, "tn_o"), 0)]
            dkv_parts += [d_cur, d_prev]
        groups = [_scatter_group(mixer_grads)]
        if l == 0:
            groups.append(_gather_group([(_pack([d_n1[0], d_ps[0]])[None], 0)]))
        handles, token = _start_copies("scatter_mixer_%d" % l, groups)
        mix_z[l], late_handle = handles[0], handles[-1]

    moms = dict(norm1_g=(m_norm1_g, v_norm1_g), norm2_g=(m_norm2_g, v_norm2_g), pool_w=(m_pool_w, v_pool_w),
                pool_scale=(m_pool_scale, v_pool_scale), kv_norm_g=(m_kv_norm_g, v_kv_norm_g), w_kv=(m_w_kv, v_w_kv),
                b_kv=(m_b_kv, v_b_kv), w_q=(m_w_q, v_w_q), b_q=(m_b_q, v_b_q), sinks=(m_sinks, v_sinks),
                w_o=(m_w_o, v_w_o), b_o=(m_b_o, v_b_o), ffn_up=(m_ffn_up, v_ffn_up),
                ffn_conv_w=(m_ffn_conv_w, v_ffn_conv_w), ffn_conv_b=(m_ffn_conv_b, v_ffn_conv_b),
                ffn_down=(m_ffn_down, v_ffn_down), final_g=(m_final_g, v_final_g))
    given = dict(norm1_g=norm1_g, norm2_g=norm2_g, kv_norm_g=kv_norm_g, b_kv=b_kv, b_q=b_q, sinks=sinks, b_o=b_o,
                 ffn_conv_b=ffn_conv_b, final_g=final_g, pool_scale=pool_scale, ffn_conv_w=ffn_conv_w,
                 pool_w=pool_w, w_kv=w_kv, w_q=w_q, w_o=w_o, ffn_up=ffn_up, ffn_down=ffn_down)
    grad, delta, new_m, new_v = {}, {}, {}, {}

    def update(name, zones, cols, first_layer=0, earlier=None, last=True):
        w = given[name]
        two_d = lambda a: a.reshape(-1, cols)
        wmv = (two_d(w), two_d(moms[name][0]), two_d(moms[name][1]))
        if name == "ffn_up":
            g = jnp.swapaxes(_sum8(zones).reshape(len(zones), F2s, D), 1, 2).reshape(-1, F2s)
            outs = _adamw(g, *wmv, "adamw_" + name, first_layer * D, earlier)
        else:
            landed = [z.reshape(N_DEV, -1, cols) for z in zones]
            outs = _adamw_landed(landed, *wmv, "adamw_" + name, first_layer, earlier)
        if last:
            grad[name], delta[name], new_m[name], new_v[name] = (o.reshape(w.shape) for o in outs)
        return outs

    rest = _wait_copies("scatter_wait_rest", up_z[1:] + down_z[1:] + mix_z[1:], token)
    up_r, down_r, mix_r = rest[:DEPTH - 1], rest[DEPTH - 1:2 * (DEPTH - 1)], rest[2 * (DEPTH - 1):]
    up_1 = update("ffn_up", [z[0] for z in up_r], F2s, 1, last=False)
    down_1 = update("ffn_down", [z[0] for z in down_r], D, 1, last=False)
    update("w_q", [mix_r[N_A - 1][0], mix_r[N_A][0]], D)
    update("w_o", [mix_r[N_A - 1][1], mix_r[N_A][1]], D)
    update("w_kv", [mix_r[N_A - 2][0]], w_kv.shape[1])
    pw_1 = update("pool_w", [mix_r[N_A - 2][1]], GC, 1, last=False)

    (up_0,), (down_0,), (pw_0,), (early,), (late,) = _wait_copies(
        "scatter_wait_0", [up_z[0], down_z[0], mix_z[0], small_handle, late_handle],
        lax.optimization_barrier((up_1[1], down_1[1], pw_1[1]))[0])
    late = _sum8([late]).reshape(-1)
    tot = _unpack(_sum8([early]), [given[k].shape for k in rep_names] + [(N_A, D), (DEPTH, CONV_W, F2), ()])
    tot[0] = tot[0].at[0].add(late[:D])
    tot[-3] = tot[-3].at[0].add(late[D:2 * D])
    grad.update(zip(rep_names, tot))
    loss = tot[-1]
    grad["pool_scale"] = lax.dynamic_slice_in_dim(tot[-3], me * (D // N_DEV), D // N_DEV, axis=1)
    grad["ffn_conv_w"] = lax.dynamic_slice_in_dim(tot[-2], me * F2s, F2s, axis=2)
    small_names = rep_names + ["pool_scale", "ffn_conv_w"]
    shapes = [given[k].shape for k in small_names]
    outs = _adamw(_pack([grad[k] for k in small_names]), _pack([given[k] for k in small_names]),
                  _pack([moms[k][0] for k in small_names]), _pack([moms[k][1] for k in small_names]), "adamw_small")
    for dst, packed in zip((delta, new_m, new_v), outs[1:]):
        dst.update(zip(small_names, _unpack(packed, shapes)))

    update("ffn_up", [up_0], F2s, 0, up_1)
    update("ffn_down", [down_0], D, 0, down_1)
    update("pool_w", [pw_0], GC, 0, pw_1)

    names = ["norm1_g", "norm2_g", "pool_w", "pool_scale", "kv_norm_g", "w_kv", "b_kv", "w_q", "b_q", "sinks", "w_o",
             "b_o", "ffn_up", "ffn_conv_w", "ffn_conv_b", "ffn_down", "final_g"]
    return (loss, dx.reshape(x.shape), *[grad[k] for k in names], *[delta[k] for k in names],
            *[new_m[k] for k in names], *[new_v[k] for k in names])
```
